```python
import jax, jax.numpy as jnp
from jax import lax
import numpy as np

D_MODEL = 1024
BATCH = 2
SEQ = 8192
DEPTH = 1

D_MIX = D_MODEL
CONF_WIDTH = D_MIX // 2
CONF_HEADS = 8
SCONV_WIDTH = D_MIX - CONF_WIDTH
SCONV_HEADS = 8
CONF_KERNEL = 31
SCONV_KERNEL = 3
D_IN = 2 * CONF_WIDTH + 3 * SCONV_WIDTH
N_GROUPS = 4
EXPERTS_PER_GROUP = 4
N_EXPERTS = N_GROUPS * EXPERTS_PER_GROUP
TOP_K_IN_GROUP = 2
D_EXPERT = 256
D_PLE = 256
LN_EPS = 1e-5
ALPHA = (2.0 * DEPTH) ** 0.25
BETA = (8.0 * DEPTH) ** -0.25

kernel_name = "hybrid_conformer_shortconv_hiermoe_deepnorm"


def layer_norm(x, g, b):
    xf = x.astype(jnp.float32)
    mu = jnp.mean(xf, axis=-1, keepdims=True)
    var = jnp.mean(jnp.square(xf - mu), axis=-1, keepdims=True)
    y = (xf - mu) * lax.rsqrt(var + LN_EPS) * g.astype(jnp.float32) + b.astype(jnp.float32)
    return y.astype(x.dtype)


def causal_depthwise_conv(u, w, b):
    k = w.shape[0]
    y = lax.conv_general_dilated(
        u, w[:, None, :].astype(u.dtype), window_strides=(1,), padding=[(k - 1, 0)],
        dimension_numbers=("NWC", "WIO", "NWC"), feature_group_count=u.shape[-1])
    return y + b


def hybrid_mixer(x, w_in, b_in, conf_dw_w, conf_dw_b, conf_ln_g, conf_ln_b, sc_w, sc_b, w_out, b_out):
    h = x @ w_in + b_in
    splits = [CONF_WIDTH, 2 * CONF_WIDTH, 2 * CONF_WIDTH + SCONV_WIDTH, 2 * CONF_WIDTH + 2 * SCONV_WIDTH]
    a_val, a_gate, b_gate, c_gate, v = jnp.split(h, splits, axis=-1)
    a = a_val * jax.nn.sigmoid(a_gate)
    a = causal_depthwise_conv(a, conf_dw_w, conf_dw_b)
    a = jax.nn.silu(layer_norm(a, conf_ln_g, conf_ln_b))
    s = b_gate * causal_depthwise_conv(c_gate * v, sc_w, sc_b)
    return jnp.concatenate([a, s], axis=-1) @ w_out + b_out


def hierarchical_moe(x, w_rg, b_rg, w_re, b_re, w_gate, w_up, w_down):
    bsz, t, d = x.shape
    xt = x.reshape(bsz * t, d)
    g_probs = jax.nn.softmax((xt @ w_rg + b_rg).astype(jnp.float32), axis=-1)
    g_p, g_idx = lax.top_k(g_probs, 1)
    e_logits_all = (xt @ w_re + b_re).astype(jnp.float32).reshape(-1, N_GROUPS, EXPERTS_PER_GROUP)
    e_logits = jnp.take_along_axis(e_logits_all, g_idx[:, :, None], axis=1)[:, 0]
    e_p, e_idx = lax.top_k(jax.nn.softmax(e_logits, axis=-1), TOP_K_IN_GROUP)
    e_p = e_p / jnp.sum(e_p, axis=-1, keepdims=True)
    weights = g_p * e_p
    expert_ids = g_idx * EXPERTS_PER_GROUP + e_idx
    combine = jnp.sum(jax.nn.one_hot(expert_ids, N_EXPERTS, dtype=jnp.float32) * weights[..., None], axis=1)
    combine = combine.astype(x.dtype)
    hg = jnp.einsum("nd,edf->nef", xt, w_gate)
    hu = jnp.einsum("nd,edf->nef", xt, w_up)
    hidden = jax.nn.silu(hg) * hu * combine[:, :, None]
    y = jnp.einsum("nef,efd->nd", hidden, w_down)
    return y.reshape(bsz, t, d)


def setup_inputs(seed: int = 0) -> dict:
    key = jax.random.key(seed)
    ks = jax.random.split(key, 32)
    f32 = jnp.float32
    nrm = lambda k, shape, s: jax.random.normal(k, shape, f32) * s
    L = DEPTH
    return {
        "x": nrm(ks[0], (BATCH, SEQ, D_MODEL), 1.0),
        "p": nrm(ks[1], (DEPTH, BATCH, SEQ, D_PLE), 1.0),
        "ln_in_g": 1.0 + nrm(ks[2], (D_MODEL,), 0.02),
        "ln_in_b": nrm(ks[3], (D_MODEL,), 0.02),
        "w_in": nrm(ks[4], (L, D_MODEL, D_IN), D_MODEL ** -0.5),
        "b_in": nrm(ks[5], (L, D_IN), 0.02),
        "conf_dw_w": nrm(ks[6], (L, CONF_KERNEL, CONF_WIDTH), CONF_KERNEL ** -0.5),
        "conf_dw_b": nrm(ks[7], (L, CONF_WIDTH), 0.02),
        "conf_ln_g": 1.0 + nrm(ks[8], (L, CONF_WIDTH), 0.02),
        "conf_ln_b": nrm(ks[9], (L, CONF_WIDTH), 0.02),
        "sc_w": nrm(ks[10], (L, SCONV_KERNEL, SCONV_WIDTH), SCONV_KERNEL ** -0.5),
        "sc_b": nrm(ks[11], (L, SCONV_WIDTH), 0.02),
        "w_out": nrm(ks[12], (L, D_MIX, D_MODEL), BETA * D_MIX ** -0.5),
        "b_out": nrm(ks[13], (L, D_MODEL), 0.02),
        "ln1_g": 1.0 + nrm(ks[14], (L, D_MODEL), 0.02),
        "ln1_b": nrm(ks[15], (L, D_MODEL), 0.02),
        "w_rg": nrm(ks[16], (L, D_MODEL, N_GROUPS), D_MODEL ** -0.5),
        "b_rg": nrm(ks[17], (L, N_GROUPS), 0.01),
        "w_re": nrm(ks[18], (L, D_MODEL, N_EXPERTS), D_MODEL ** -0.5),
        "b_re": nrm(ks[19], (L, N_EXPERTS), 0.01),
        "w_gate": nrm(ks[20], (L, N_EXPERTS, D_MODEL, D_EXPERT), D_MODEL ** -0.5),
        "w_up": nrm(ks[21], (L, N_EXPERTS, D_MODEL, D_EXPERT), D_MODEL ** -0.5),
        "w_down": nrm(ks[22], (L, N_EXPERTS, D_EXPERT, D_MODEL), BETA * D_EXPERT ** -0.5),
        "w_pg": nrm(ks[23], (L, D_MODEL, D_MODEL), D_MODEL ** -0.5),
        "b_pg": nrm(ks[24], (L, D_MODEL), 0.02),
        "w_pp": nrm(ks[25], (L, D_PLE, D_MODEL), BETA * D_PLE ** -0.5),
        "ln2_g": 1.0 + nrm(ks[26], (L, D_MODEL), 0.02),
        "ln2_b": nrm(ks[27], (L, D_MODEL), 0.02),
    }


def reference(x, p, ln_in_g, ln_in_b, w_in, b_in, conf_dw_w, conf_dw_b, conf_ln_g, conf_ln_b,
              sc_w, sc_b, w_out, b_out, ln1_g, ln1_b, w_rg, b_rg, w_re, b_re,
              w_gate, w_up, w_down, w_pg, b_pg, w_pp, ln2_g, ln2_b):
    x = layer_norm(x, ln_in_g, ln_in_b)
    for i in range(DEPTH):
        mix = hybrid_mixer(x, w_in[i], b_in[i], conf_dw_w[i], conf_dw_b[i], conf_ln_g[i], conf_ln_b[i],
                           sc_w[i], sc_b[i], w_out[i], b_out[i])
        x = layer_norm(ALPHA * x + mix, ln1_g[i], ln1_b[i])
        r = ALPHA * x + hierarchical_moe(x, w_rg[i], b_rg[i], w_re[i], b_re[i], w_gate[i], w_up[i], w_down[i])
        gate = jax.nn.sigmoid(r @ w_pg[i] + b_pg[i])
        x = layer_norm(r + gate * (p[i] @ w_pp[i]), ln2_g[i], ln2_b[i])
    return x
```

```python
import functools

import jax
import jax.numpy as jnp
from jax import lax
from jax.experimental import pallas as pl
from jax.experimental.pallas import tpu as pltpu

LN_EPS = 1e-5
N_GROUPS = 4
EXPERTS_PER_GROUP = 4
SUBLANES = 8
ROUTER_ROWS = 48
VMEM_LIMIT_BYTES = 56 * 1024 * 1024
NEG_BIG = -1e30

_F32 = jnp.float32
_BF16 = jnp.bfloat16


def _layer_norm(x, g, b):
    mu = jnp.mean(x, axis=-1, keepdims=True)
    xc = x - mu
    var = jnp.mean(xc * xc, axis=-1, keepdims=True)
    return xc * lax.rsqrt(var + LN_EPS) * g + b


def _sigmoid(x):
    return 1.0 / (1.0 + jnp.exp(-x))


def _dot(a, b):
    return jnp.dot(a, b, preferred_element_type=_F32)


def _route(logits):
    row = lax.broadcasted_iota(jnp.int32, (SUBLANES, logits.shape[1]), 0)
    real = row < N_GROUPS
    gl = jnp.where(real, logits[0:SUBLANES], NEG_BIG)
    gm = jnp.max(gl, axis=0, keepdims=True)
    ge = jnp.exp(gl - gm)
    gp = ge / jnp.sum(ge, axis=0, keepdims=True)
    gp_top = jnp.max(gp, axis=0, keepdims=True)
    gidx = jnp.min(jnp.where(gp == gp_top, row, SUBLANES), axis=0, keepdims=True)

    el = jnp.zeros_like(gl)
    for g in range(N_GROUPS):
        blk = logits[SUBLANES * (g + 1):SUBLANES * (g + 2)]
        el = jnp.where(gidx == g, blk, el)
    el = jnp.where(real, el, NEG_BIG)
    em = jnp.max(el, axis=0, keepdims=True)
    ee = jnp.exp(el - em)
    ep = ee / jnp.sum(ee, axis=0, keepdims=True)
    ep = jnp.where(real, ep, -1.0)
    p1 = jnp.max(ep, axis=0, keepdims=True)
    i1 = jnp.min(jnp.where(ep == p1, row, SUBLANES), axis=0, keepdims=True)
    ep2 = jnp.where(row == i1, -1.0, ep)
    p2 = jnp.max(ep2, axis=0, keepdims=True)
    i2 = jnp.min(jnp.where(ep2 == p2, row, SUBLANES), axis=0, keepdims=True)
    denom = p1 + p2
    w = jnp.where(row == i1, p1 / denom, jnp.where(row == i2, p2 / denom, 0.0))
    return gidx, w * gp_top


def _mixer_kernel(x_ref, lng_ref, lnb_ref, w_in_ref, b_in_ref, cw_ref, cb_ref, clg_ref, clb_ref,
                  sw_ref, sb_ref, w_out_ref, b_out_ref, l1g_ref, l1b_ref, wr_ref, br_ref,
                  x1_ref, gidx_ref, cwt_ref, abuf, ubuf, *, alpha, tile, conf_w, sc_w_, conf_k, sc_k,
                  a_halo, u_halo):
    j = pl.program_id(1)

    @pl.when(j == 0)
    def _():
        abuf[0:a_halo, :] = jnp.zeros((a_halo, conf_w), _F32)
        ubuf[0:u_halo, :] = jnp.zeros((u_halo, sc_w_), _F32)

    x0 = _layer_norm(x_ref[0], lng_ref[...], lnb_ref[...])
    h = _dot(x0.astype(_BF16), w_in_ref[...]) + b_in_ref[...]
    c0, c1, c2, c3 = conf_w, 2 * conf_w, 2 * conf_w + sc_w_, 2 * conf_w + 2 * sc_w_

    a = h[:, 0:c0] * _sigmoid(h[:, c0:c1])
    abuf[a_halo:a_halo + tile, :] = a
    acc = jnp.broadcast_to(cb_ref[...], (tile, conf_w))
    for k in range(conf_k):
        off = a_halo - (conf_k - 1) + k
        acc = acc + cw_ref[k:k + 1, :] * abuf[off:off + tile, :]
    abuf[0:a_halo, :] = abuf[tile:tile + a_halo, :]
    an = _layer_norm(acc, clg_ref[...], clb_ref[...])
    a2 = an * _sigmoid(an)

    u = h[:, c2:c3] * h[:, c3:]
    ubuf[u_halo:u_halo + tile, :] = u
    sc = jnp.broadcast_to(sb_ref[...], (tile, sc_w_))
    for k in range(sc_k):
        off = u_halo - (sc_k - 1) + k
        sc = sc + sw_ref[k:k + 1, :] * ubuf[off:off + tile, :]
    ubuf[0:u_halo, :] = ubuf[tile:tile + u_halo, :]
    s = h[:, c1:c2] * sc

    mix = (_dot(a2.astype(_BF16), w_out_ref[0:conf_w, :])
           + _dot(s.astype(_BF16), w_out_ref[conf_w:, :]) + b_out_ref[...])
    x1 = _layer_norm(alpha * x0 + mix, l1g_ref[...], l1b_ref[...])
    x1_ref[0] = x1

    logits = lax.dot_general(wr_ref[...], x1.astype(_BF16), (((1,), (1,)), ((), ())),
                             preferred_element_type=_F32) + br_ref[...]
    gidx, w = _route(logits)
    gidx_ref[0] = gidx
    cwt_ref[0] = w


def _moe_kernel(x1_ref, gidx_ref, cwt_ref, p_ref, wg_ref, wu_ref, wd_ref, ex_ref, wpg_ref, bpg_ref,
                wpp_ref, l2g_ref, l2b_ref, o_ref, acc_ref, *, alpha, d_expert):
    g = pl.program_id(1)

    @pl.when(g == 0)
    def _():
        acc_ref[...] = jnp.zeros_like(acc_ref)

    xb = x1_ref[...].astype(_BF16)
    w = jnp.where(gidx_ref[0] == g, cwt_ref[0], 0.0)
    w_hi = w.astype(_BF16).astype(_F32)
    w_lo = (w - w_hi).astype(_BF16).astype(_F32)
    row = lax.broadcasted_iota(jnp.int32, w.shape, 0)
    w_hl = jnp.where(row < EXPERTS_PER_GROUP, w_hi, pltpu.roll(w_lo, EXPERTS_PER_GROUP, 0))
    c = lax.dot_general(w_hl, ex_ref[...], (((0,), (0,)), ((), ())), preferred_element_type=_F32)

    y = acc_ref[...]
    for e in range(EXPERTS_PER_GROUP):
        hg = _dot(xb, wg_ref[e])
        hu = _dot(xb, wu_ref[e])
        hid = hg * _sigmoid(hg) * hu * c[:, e * d_expert:(e + 1) * d_expert]
        y = y + _dot(hid.astype(_BF16), wd_ref[e])
    acc_ref[...] = y

    @pl.when(g == N_GROUPS - 1)
    def _():
        r = alpha * x1_ref[...] + acc_ref[...]
        gate = _sigmoid(_dot(r.astype(_BF16), wpg_ref[...]) + bpg_ref[...])
        pp = _dot(p_ref[...].astype(_BF16), wpp_ref[...])
        o_ref[...] = _layer_norm(r + gate * pp, l2g_ref[...], l2b_ref[...])


def _full(shape):
    return pl.BlockSpec(shape, lambda *_: (0,) * len(shape))


def _row(v):
    return v.reshape(1, -1).astype(_F32)


def kernel(x, p, ln_in_g, ln_in_b, w_in, b_in, conf_dw_w, conf_dw_b, conf_ln_g, conf_ln_b, sc_w, sc_b, w_out, b_out, ln1_g, ln1_b, w_rg, b_rg, w_re, b_re, w_gate, w_up, w_down, w_pg, b_pg, w_pp, ln2_g, ln2_b):
    depth = w_in.shape[0]
    assert depth == 1, "single-layer block"
    alpha = (2.0 * depth) ** 0.25
    bsz, seq, d = x.shape
    n = bsz * seq
    conf_k, conf_w = conf_dw_w.shape[1:]
    sc_k, sc_w_ = sc_w.shape[1:]
    d_in = w_in.shape[2]
    n_exp, _, d_expert = w_gate.shape[1:]
    assert n_exp == N_GROUPS * EXPERTS_PER_GROUP
    assert d_in == 2 * conf_w + 3 * sc_w_ and w_out.shape[1] == conf_w + sc_w_
    d_ple = p.shape[-1]

    tile = min(512, seq)
    assert seq % tile == 0
    nt = seq // tile
    a_halo = -(-(conf_k - 1) // SUBLANES) * SUBLANES
    u_halo = -(-(sc_k - 1) // SUBLANES) * SUBLANES

    wr = jnp.zeros((ROUTER_ROWS, d), _F32).at[0:N_GROUPS].set(w_rg[0].T)
    br = jnp.zeros((ROUTER_ROWS,), _F32).at[0:N_GROUPS].set(b_rg[0])
    for g in range(N_GROUPS):
        lo = SUBLANES * (g + 1)
        sl = slice(g * EXPERTS_PER_GROUP, (g + 1) * EXPERTS_PER_GROUP)
        wr = wr.at[lo:lo + EXPERTS_PER_GROUP].set(w_re[0][:, sl].T)
        br = br.at[lo:lo + EXPERTS_PER_GROUP].set(b_re[0][sl])

    mixer = pl.pallas_call(
        functools.partial(_mixer_kernel, alpha=alpha, tile=tile, conf_w=conf_w, sc_w_=sc_w_,
                          conf_k=conf_k, sc_k=sc_k, a_halo=a_halo, u_halo=u_halo),
        grid=(bsz, nt),
        in_specs=[
            pl.BlockSpec((1, tile, d), lambda b, j: (b, j, 0)),
            _full((1, d)), _full((1, d)),
            _full((d, d_in)), _full((1, d_in)),
            _full((conf_k, conf_w)), _full((1, conf_w)), _full((1, conf_w)), _full((1, conf_w)),
            _full((sc_k, sc_w_)), _full((1, sc_w_)),
            _full((conf_w + sc_w_, d)), _full((1, d)),
            _full((1, d)), _full((1, d)),
            _full((ROUTER_ROWS, d)), _full((ROUTER_ROWS, 1)),
        ],
        out_specs=[
            pl.BlockSpec((1, tile, d), lambda b, j: (b, j, 0)),
            pl.BlockSpec((1, 1, tile), lambda b, j: (b * nt + j, 0, 0)),
            pl.BlockSpec((1, SUBLANES, tile), lambda b, j: (b * nt + j, 0, 0)),
        ],
        out_shape=[
            jax.ShapeDtypeStruct((bsz, seq, d), _F32),
            jax.ShapeDtypeStruct((bsz * nt, 1, tile), jnp.int32),
            jax.ShapeDtypeStruct((bsz * nt, SUBLANES, tile), _F32),
        ],
        scratch_shapes=[
            pltpu.VMEM((a_halo + tile, conf_w), _F32),
            pltpu.VMEM((u_halo + tile, sc_w_), _F32),
        ],
        compiler_params=pltpu.CompilerParams(
            dimension_semantics=("arbitrary", "arbitrary"), vmem_limit_bytes=VMEM_LIMIT_BYTES),
        name="mixer",
    )
    x1, gidx, cwt = mixer(
        x, _row(ln_in_g), _row(ln_in_b), w_in[0].astype(_BF16), _row(b_in[0]),
        conf_dw_w[0], _row(conf_dw_b[0]), _row(conf_ln_g[0]), _row(conf_ln_b[0]),
        sc_w[0], _row(sc_b[0]), w_out[0].astype(_BF16), _row(b_out[0]),
        _row(ln1_g[0]), _row(ln1_b[0]), wr.astype(_BF16), br.reshape(ROUTER_ROWS, 1))

    col_e = jnp.arange(EXPERTS_PER_GROUP * d_expert) // d_expert
    expand = (col_e[None, :] == (jnp.arange(SUBLANES) % EXPERTS_PER_GROUP)[:, None]).astype(_F32)

    moe = pl.pallas_call(
        functools.partial(_moe_kernel, alpha=alpha, d_expert=d_expert),
        grid=(n // tile, N_GROUPS),
        in_specs=[
            pl.BlockSpec((tile, d), lambda i, g: (i, 0)),
            pl.BlockSpec((1, 1, tile), lambda i, g: (i, 0, 0)),
            pl.BlockSpec((1, SUBLANES, tile), lambda i, g: (i, 0, 0)),
            pl.BlockSpec((tile, d_ple), lambda i, g: (i, 0)),
            pl.BlockSpec((EXPERTS_PER_GROUP, d, d_expert), lambda i, g: (g, 0, 0)),
            pl.BlockSpec((EXPERTS_PER_GROUP, d, d_expert), lambda i, g: (g, 0, 0)),
            pl.BlockSpec((EXPERTS_PER_GROUP, d_expert, d), lambda i, g: (g, 0, 0)),
            _full((SUBLANES, EXPERTS_PER_GROUP * d_expert)),
            _full((d, d)), _full((1, d)), _full((d_ple, d)), _full((1, d)), _full((1, d)),
        ],
        out_specs=pl.BlockSpec((tile, d), lambda i, g: (i, 0)),
        out_shape=jax.ShapeDtypeStruct((n, d), _F32),
        scratch_shapes=[pltpu.VMEM((tile, d), _F32)],
        compiler_params=pltpu.CompilerParams(
            dimension_semantics=("arbitrary", "arbitrary"), vmem_limit_bytes=VMEM_LIMIT_BYTES),
        name="moe",
    )
    out = moe(
        x1.reshape(n, d), gidx, cwt, p[0].reshape(n, d_ple),
        w_gate[0].astype(_BF16), w_up[0].astype(_BF16), w_down[0].astype(_BF16), expand,
        w_pg[0].astype(_BF16), _row(b_pg[0]), w_pp[0].astype(_BF16), _row(ln2_g[0]), _row(ln2_b[0]))
    return out.reshape(bsz, seq, d)
```

```python
import functools

import jax
import jax.numpy as jnp
from jax import lax
from jax.experimental import pallas as pl
from jax.experimental.pallas import tpu as pltpu

LN_EPS = 1e-5
N_GROUPS = 4
EXPERTS_PER_GROUP = 4
SUBLANES = 8
ROUTER_ROWS = 48
VMEM_LIMIT_BYTES = 56 * 1024 * 1024
NEG_BIG = -1e30

_F32 = jnp.float32
_BF16 = jnp.bfloat16


def _layer_norm(x, g, b):
    mu = jnp.mean(x, axis=-1, keepdims=True)
    xc = x - mu
    var = jnp.mean(xc * xc, axis=-1, keepdims=True)
    return xc * lax.rsqrt(var + LN_EPS) * g + b


def _sigmoid(x):
    return 0.5 * jnp.tanh(0.5 * x) + 0.5


def _dot(a, b):
    return jnp.dot(a, b, preferred_element_type=_F32)


def _route(logits):
    row = lax.broadcasted_iota(jnp.int32, (SUBLANES, logits.shape[1]), 0)
    real = row < N_GROUPS
    gl = jnp.where(real, logits[0:SUBLANES], NEG_BIG)
    gm = jnp.max(gl, axis=0, keepdims=True)
    ge = jnp.exp(gl - gm)
    gp = ge / jnp.sum(ge, axis=0, keepdims=True)
    gp_top = jnp.max(gp, axis=0, keepdims=True)
    gidx = jnp.min(jnp.where(gp == gp_top, row, SUBLANES), axis=0, keepdims=True)

    el = jnp.zeros_like(gl)
    for g in range(N_GROUPS):
        blk = logits[SUBLANES * (g + 1):SUBLANES * (g + 2)]
        el = jnp.where(gidx == g, blk, el)
    el = jnp.where(real, el, NEG_BIG)
    em = jnp.max(el, axis=0, keepdims=True)
    ee = jnp.exp(el - em)
    ep = ee / jnp.sum(ee, axis=0, keepdims=True)
    ep = jnp.where(real, ep, -1.0)
    p1 = jnp.max(ep, axis=0, keepdims=True)
    i1 = jnp.min(jnp.where(ep == p1, row, SUBLANES), axis=0, keepdims=True)
    ep2 = jnp.where(row == i1, -1.0, ep)
    p2 = jnp.max(ep2, axis=0, keepdims=True)
    i2 = jnp.min(jnp.where(ep2 == p2, row, SUBLANES), axis=0, keepdims=True)
    denom = p1 + p2
    w = jnp.where(row == i1, p1 / denom, jnp.where(row == i2, p2 / denom, 0.0))
    return gidx, w * gp_top


def _mixer_kernel(x_ref, lng_ref, lnb_ref, w_in_ref, b_in_ref, cw_ref, cb_ref, clg_ref, clb_ref,
                  sw_ref, sb_ref, w_out_ref, b_out_ref, l1g_ref, l1b_ref, wr_ref, br_ref,
                  x1_ref, gidx_ref, cwt_ref, abuf, ashift, ubuf, w_in_bf, w_out_bf, *, alpha, tile,
                  conf_w, sc_w_, conf_k, sc_k, a_halo, u_halo):
    j = pl.program_id(1)

    @pl.when((pl.program_id(0) == 0) & (j == 0))
    def _():
        w_in_bf[...] = w_in_ref[...].astype(_BF16)
        w_out_bf[...] = w_out_ref[...].astype(_BF16)

    @pl.when(j == 0)
    def _():
        abuf[0:a_halo, :] = jnp.zeros((a_halo, conf_w), _F32)
        ubuf[0:u_halo, :] = jnp.zeros((u_halo, sc_w_), _F32)

    x0 = _layer_norm(x_ref[0], lng_ref[...], lnb_ref[...])
    h = _dot(x0.astype(_BF16), w_in_bf[...]) + b_in_ref[...]
    c0, c1, c2, c3 = conf_w, 2 * conf_w, 2 * conf_w + sc_w_, 2 * conf_w + 2 * sc_w_

    a = h[:, 0:c0] * _sigmoid(h[:, c0:c1])
    abuf[a_halo:a_halo + tile, :] = a
    for i in range(1, SUBLANES):
        ashift[i - 1] = abuf[i:i + tile + a_halo - SUBLANES, :]
    acc = jnp.broadcast_to(cb_ref[...], (tile, conf_w))
    for k in range(conf_k):
        q, i = divmod(a_halo - (conf_k - 1) + k, SUBLANES)
        if i == 0:
            src = abuf[SUBLANES * q:SUBLANES * q + tile, :]
        else:
            src = ashift[i - 1, SUBLANES * q:SUBLANES * q + tile, :]
        acc = acc + cw_ref[k:k + 1, :] * src
    abuf[0:a_halo, :] = abuf[tile:tile + a_halo, :]
    an = _layer_norm(acc, clg_ref[...], clb_ref[...])
    a2 = an * _sigmoid(an)

    u = h[:, c2:c3] * h[:, c3:]
    ubuf[u_halo:u_halo + tile, :] = u
    sc = jnp.broadcast_to(sb_ref[...], (tile, sc_w_))
    for k in range(sc_k):
        off = u_halo - (sc_k - 1) + k
        sc = sc + sw_ref[k:k + 1, :] * ubuf[off:off + tile, :]
    ubuf[0:u_halo, :] = ubuf[tile:tile + u_halo, :]
    s = h[:, c1:c2] * sc

    mix = (_dot(a2.astype(_BF16), w_out_bf[0:conf_w, :])
           + _dot(s.astype(_BF16), w_out_bf[conf_w:, :]) + b_out_ref[...])
    x1 = _layer_norm(alpha * x0 + mix, l1g_ref[...], l1b_ref[...])
    x1_ref[0] = x1

    logits = lax.dot_general(wr_ref[...], x1.astype(_BF16), (((1,), (1,)), ((), ())),
                             preferred_element_type=_F32) + br_ref[...]
    gidx, w = _route(logits)
    gidx_ref[0] = gidx
    cwt_ref[0] = w


def _moe_kernel(x1_ref, gidx_ref, cwt_ref, p_ref, wg_ref, wu_ref, wd_ref, ex_ref, wpg_ref, bpg_ref,
                wpp_ref, l2g_ref, l2b_ref, o_ref, acc_ref, *, alpha, d_expert):
    g = pl.program_id(1)

    @pl.when(g == 0)
    def _():
        acc_ref[...] = jnp.zeros_like(acc_ref)

    xb = x1_ref[...].astype(_BF16)
    w = jnp.where(gidx_ref[0] == g, cwt_ref[0], 0.0)
    w_hi = w.astype(_BF16).astype(_F32)
    w_lo = (w - w_hi).astype(_BF16).astype(_F32)
    row = lax.broadcasted_iota(jnp.int32, w.shape, 0)
    w_hl = jnp.where(row < EXPERTS_PER_GROUP, w_hi, pltpu.roll(w_lo, EXPERTS_PER_GROUP, 0))
    c = lax.dot_general(w_hl, ex_ref[...], (((0,), (0,)), ((), ())), preferred_element_type=_F32)

    y = acc_ref[...]
    for e in range(EXPERTS_PER_GROUP):
        hg = _dot(xb, wg_ref[e])
        hu = _dot(xb, wu_ref[e])
        hid = hg * _sigmoid(hg) * hu * c[:, e * d_expert:(e + 1) * d_expert]
        y = y + _dot(hid.astype(_BF16), wd_ref[e])
    acc_ref[...] = y

    @pl.when(g == N_GROUPS - 1)
    def _():
        r = alpha * x1_ref[...] + acc_ref[...]
        gate = _sigmoid(_dot(r.astype(_BF16), wpg_ref[...]) + bpg_ref[...])
        pp = _dot(p_ref[...].astype(_BF16), wpp_ref[...])
        o_ref[...] = _layer_norm(r + gate * pp, l2g_ref[...], l2b_ref[...])


def _full(shape):
    return pl.BlockSpec(shape, lambda *_: (0,) * len(shape))


def _resident(shape):
    return pl.BlockSpec(shape, lambda *_: (0,) * len(shape), pipeline_mode=pl.Buffered(1))


def _row(v):
    return v.reshape(1, -1).astype(_F32)


def kernel(x, p, ln_in_g, ln_in_b, w_in, b_in, conf_dw_w, conf_dw_b, conf_ln_g, conf_ln_b, sc_w, sc_b, w_out, b_out, ln1_g, ln1_b, w_rg, b_rg, w_re, b_re, w_gate, w_up, w_down, w_pg, b_pg, w_pp, ln2_g, ln2_b):
    depth = w_in.shape[0]
    assert depth == 1, "single-layer block"
    alpha = (2.0 * depth) ** 0.25
    bsz, seq, d = x.shape
    n = bsz * seq
    conf_k, conf_w = conf_dw_w.shape[1:]
    sc_k, sc_w_ = sc_w.shape[1:]
    d_in = w_in.shape[2]
    n_exp, _, d_expert = w_gate.shape[1:]
    assert n_exp == N_GROUPS * EXPERTS_PER_GROUP
    assert d_in == 2 * conf_w + 3 * sc_w_ and w_out.shape[1] == conf_w + sc_w_
    d_ple = p.shape[-1]

    tile = min(512, seq)
    assert seq % tile == 0
    nt = seq // tile
    a_halo = -(-(conf_k - 1) // SUBLANES) * SUBLANES
    u_halo = -(-(sc_k - 1) // SUBLANES) * SUBLANES

    wr = jnp.zeros((ROUTER_ROWS, d), _F32).at[0:N_GROUPS].set(w_rg[0].T)
    br = jnp.zeros((ROUTER_ROWS,), _F32).at[0:N_GROUPS].set(b_rg[0])
    for g in range(N_GROUPS):
        lo = SUBLANES * (g + 1)
        sl = slice(g * EXPERTS_PER_GROUP, (g + 1) * EXPERTS_PER_GROUP)
        wr = wr.at[lo:lo + EXPERTS_PER_GROUP].set(w_re[0][:, sl].T)
        br = br.at[lo:lo + EXPERTS_PER_GROUP].set(b_re[0][sl])

    mixer = pl.pallas_call(
        functools.partial(_mixer_kernel, alpha=alpha, tile=tile, conf_w=conf_w, sc_w_=sc_w_,
                          conf_k=conf_k, sc_k=sc_k, a_halo=a_halo, u_halo=u_halo),
        grid=(bsz, nt),
        in_specs=[
            pl.BlockSpec((1, tile, d), lambda b, j: (b, j, 0)),
            _full((1, d)), _full((1, d)),
            _resident((d, d_in)), _full((1, d_in)),
            _full((conf_k, conf_w)), _full((1, conf_w)), _full((1, conf_w)), _full((1, conf_w)),
            _full((sc_k, sc_w_)), _full((1, sc_w_)),
            _resident((conf_w + sc_w_, d)), _full((1, d)),
            _full((1, d)), _full((1, d)),
            _full((ROUTER_ROWS, d)), _full((ROUTER_ROWS, 1)),
        ],
        out_specs=[
            pl.BlockSpec((1, tile, d), lambda b, j: (b, j, 0)),
            pl.BlockSpec((1, 1, tile), lambda b, j: (b * nt + j, 0, 0)),
            pl.BlockSpec((1, SUBLANES, tile), lambda b, j: (b * nt + j, 0, 0)),
        ],
        out_shape=[
            jax.ShapeDtypeStruct((bsz, seq, d), _F32),
            jax.ShapeDtypeStruct((bsz * nt, 1, tile), jnp.int32),
            jax.ShapeDtypeStruct((bsz * nt, SUBLANES, tile), _F32),
        ],
        scratch_shapes=[
            pltpu.VMEM((a_halo + tile, conf_w), _F32),
            pltpu.VMEM((SUBLANES - 1, a_halo + tile - SUBLANES, conf_w), _F32),
            pltpu.VMEM((u_halo + tile, sc_w_), _F32),
            pltpu.VMEM((d, d_in), _BF16),
            pltpu.VMEM((conf_w + sc_w_, d), _BF16),
        ],
        compiler_params=pltpu.CompilerParams(
            dimension_semantics=("arbitrary", "arbitrary"), vmem_limit_bytes=VMEM_LIMIT_BYTES),
        name="mixer",
    )
    x1, gidx, cwt = mixer(
        x, _row(ln_in_g), _row(ln_in_b), w_in[0], _row(b_in[0]),
        conf_dw_w[0], _row(conf_dw_b[0]), _row(conf_ln_g[0]), _row(conf_ln_b[0]),
        sc_w[0], _row(sc_b[0]), w_out[0], _row(b_out[0]),
        _row(ln1_g[0]), _row(ln1_b[0]), wr.astype(_BF16), br.reshape(ROUTER_ROWS, 1))

    col_e = jnp.arange(EXPERTS_PER_GROUP * d_expert) // d_expert
    expand = (col_e[None, :] == (jnp.arange(SUBLANES) % EXPERTS_PER_GROUP)[:, None]).astype(_F32)

    moe = pl.pallas_call(
        functools.partial(_moe_kernel, alpha=alpha, d_expert=d_expert),
        grid=(n // tile, N_GROUPS),
        in_specs=[
            pl.BlockSpec((tile, d), lambda i, g: (i, 0)),
            pl.BlockSpec((1, 1, tile), lambda i, g: (i, 0, 0)),
            pl.BlockSpec((1, SUBLANES, tile), lambda i, g: (i, 0, 0)),
            pl.BlockSpec((tile, d_ple), lambda i, g: (i, 0)),
            pl.BlockSpec((EXPERTS_PER_GROUP, d, d_expert), lambda i, g: (g, 0, 0)),
            pl.BlockSpec((EXPERTS_PER_GROUP, d, d_expert), lambda i, g: (g, 0, 0)),
            pl.BlockSpec((EXPERTS_PER_GROUP, d_expert, d), lambda i, g: (g, 0, 0)),
            _full((SUBLANES, EXPERTS_PER_GROUP * d_expert)),
            _full((d, d)), _full((1, d)), _full((d_ple, d)), _full((1, d)), _full((1, d)),
        ],
        out_specs=pl.BlockSpec((tile, d), lambda i, g: (i, 0)),
        out_shape=jax.ShapeDtypeStruct((n, d), _F32),
        scratch_shapes=[pltpu.VMEM((tile, d), _F32)],
        compiler_params=pltpu.CompilerParams(
            dimension_semantics=("arbitrary", "arbitrary"), vmem_limit_bytes=VMEM_LIMIT_BYTES),
        name="moe",
    )
    out = moe(
        x1.reshape(n, d), gidx, cwt, p[0].reshape(n, d_ple),
        w_gate[0].astype(_BF16), w_up[0].astype(_BF16), w_down[0].astype(_BF16), expand,
        w_pg[0].astype(_BF16), _row(b_pg[0]), w_pp[0].astype(_BF16), _row(ln2_g[0]), _row(ln2_b[0]))
    return out.reshape(bsz, seq, d)
```

```python
import functools

import jax
import jax.numpy as jnp
from jax import lax
from jax.experimental import pallas as pl
from jax.experimental.pallas import tpu as pltpu

LN_EPS = 1e-5
N_GROUPS = 4
EXPERTS_PER_GROUP = 4
SUBLANES = 8
LANES = 128
ROUTER_ROWS = 48
TILE = 512
VMEM_LIMIT_BYTES = 56 * 1024 * 1024
NEG_BIG = -1e30
META_FIELDS = 5

_F32 = jnp.float32
_BF16 = jnp.bfloat16
_NT = (((1,), (1,)), ((), ()))
_TN = (((0,), (0,)), ((), ()))


def _layer_norm(x, g, b):
    mu = jnp.mean(x, axis=-1, keepdims=True)
    xc = x - mu
    var = jnp.mean(xc * xc, axis=-1, keepdims=True)
    return xc * lax.rsqrt(var + LN_EPS) * g + b


def _sigmoid(x):
    return 0.5 * jnp.tanh(0.5 * x) + 0.5


def _dot(a, b):
    return jnp.dot(a, b, preferred_element_type=_F32)


def _aligned(v):
    return v if isinstance(v, int) else pl.multiple_of(v, SUBLANES)


def _rows(src, s0, dst, d0, m, sem):
    return pltpu.make_async_copy(src.at[pl.ds(_aligned(s0), _aligned(m))],
                                 dst.at[pl.ds(_aligned(d0), _aligned(m))], sem)


def _route(logits):
    row = lax.broadcasted_iota(jnp.int32, (SUBLANES, logits.shape[1]), 0)
    real = row < N_GROUPS
    gl = jnp.where(real, logits[0:SUBLANES], NEG_BIG)
    gm = jnp.max(gl, axis=0, keepdims=True)
    ge = jnp.exp(gl - gm)
    gp = ge / jnp.sum(ge, axis=0, keepdims=True)
    gp_top = jnp.max(gp, axis=0, keepdims=True)
    gidx = jnp.min(jnp.where(gp == gp_top, row, SUBLANES), axis=0, keepdims=True)

    el = jnp.zeros_like(gl)
    for g in range(N_GROUPS):
        blk = logits[SUBLANES * (g + 1):SUBLANES * (g + 2)]
        el = jnp.where(gidx == g, blk, el)
    el = jnp.where(real, el, NEG_BIG)
    em = jnp.max(el, axis=0, keepdims=True)
    ee = jnp.exp(el - em)
    ep = ee / jnp.sum(ee, axis=0, keepdims=True)
    ep = jnp.where(real, ep, -1.0)
    p1 = jnp.max(ep, axis=0, keepdims=True)
    i1 = jnp.min(jnp.where(ep == p1, row, SUBLANES), axis=0, keepdims=True)
    ep2 = jnp.where(row == i1, -1.0, ep)
    p2 = jnp.max(ep2, axis=0, keepdims=True)
    i2 = jnp.min(jnp.where(ep2 == p2, row, SUBLANES), axis=0, keepdims=True)
    denom = p1 + p2
    w = jnp.where(row == i1, p1 / denom, jnp.where(row == i2, p2 / denom, 0.0))
    return gidx, w * gp_top


def _mixer_kernel(x_ref, lng_ref, lnb_ref, w_in_ref, b_in_ref, cw_ref, cb_ref, clg_ref, clb_ref,
                  sw_ref, sb_ref, w_out_ref, b_out_ref, l1g_ref, l1b_ref, wr_ref, br_ref, tri_ref,
                  x1_ref, pos_ref, xs_ref, meta_ref, info_ref,
                  abuf, ashift, ubuf, w_in_bf, w_out_bf, stage, zbuf, cur_ref, fill_ref, nfree_ref,
                  ring_ref, sem, zsem, *, alpha, tile, t_pad, chunk, n_chunks, n_tiles, conf_w, sc_w_,
                  conf_k, sc_k, a_halo, u_halo):
    j = pl.program_id(1)
    k = pl.program_id(0) * pl.num_programs(1) + j
    slot = k % 2
    d = x1_ref.shape[-1]

    @pl.when(k == 0)
    def _():
        w_in_bf[...] = w_in_ref[...].astype(_BF16)
        w_out_bf[...] = w_out_ref[...].astype(_BF16)
        nfree_ref[0] = 0
        for g in range(N_GROUPS):
            cur_ref[g] = 0
            fill_ref[g] = chunk
        for c in range(info_ref.shape[0]):
            info_ref[c] = N_GROUPS - 1

    @pl.when(j == 0)
    def _():
        abuf[0:a_halo, :] = jnp.zeros((a_halo, conf_w), _F32)
        ubuf[0:u_halo, :] = jnp.zeros((u_halo, sc_w_), _F32)

    x0 = _layer_norm(x_ref[0], lng_ref[...], lnb_ref[...])
    h = _dot(x0.astype(_BF16), w_in_bf[...]) + b_in_ref[...]
    c0, c1, c2, c3 = conf_w, 2 * conf_w, 2 * conf_w + sc_w_, 2 * conf_w + 2 * sc_w_

    a = h[:, 0:c0] * _sigmoid(h[:, c0:c1])
    abuf[a_halo:a_halo + tile, :] = a
    for i in range(1, SUBLANES):
        ashift[i - 1] = abuf[i:i + tile + a_halo - SUBLANES, :]
    acc = jnp.broadcast_to(cb_ref[...], (tile, conf_w))
    for t in range(conf_k):
        q, i = divmod(a_halo - (conf_k - 1) + t, SUBLANES)
        if i == 0:
            src = abuf[SUBLANES * q:SUBLANES * q + tile, :]
        else:
            src = ashift[i - 1, SUBLANES * q:SUBLANES * q + tile, :]
        acc = acc + cw_ref[t:t + 1, :] * src
    abuf[0:a_halo, :] = abuf[tile:tile + a_halo, :]
    an = _layer_norm(acc, clg_ref[...], clb_ref[...])
    a2 = an * _sigmoid(an)

    u = h[:, c2:c3] * h[:, c3:]
    ubuf[u_halo:u_halo + tile, :] = u
    sc = jnp.broadcast_to(sb_ref[...], (tile, sc_w_))
    for t in range(sc_k):
        off = u_halo - (sc_k - 1) + t
        sc = sc + sw_ref[t:t + 1, :] * ubuf[off:off + tile, :]
    ubuf[0:u_halo, :] = ubuf[tile:tile + u_halo, :]
    s = h[:, c1:c2] * sc

    mix = (_dot(a2.astype(_BF16), w_out_bf[0:conf_w, :])
           + _dot(s.astype(_BF16), w_out_bf[conf_w:, :]) + b_out_ref[...])
    x1 = _layer_norm(alpha * x0 + mix, l1g_ref[...], l1b_ref[...])
    x1_ref[0] = x1
    x1b = x1.astype(_BF16)

    logits = lax.dot_general(wr_ref[...], x1b, _NT, preferred_element_type=_F32) + br_ref[...]
    gidx, w = _route(logits)

    row8 = lax.broadcasted_iota(jnp.int32, (SUBLANES, tile), 0)
    onehot = jnp.where(gidx == row8, 1.0, 0.0)
    cnt = jnp.sum(onehot, axis=1, keepdims=True)
    sizes, starts = [], []
    start = jnp.int32(0)
    startv = jnp.zeros((SUBLANES, tile), jnp.int32)
    for g in range(N_GROUPS):
        n8 = (cnt[g, 0].astype(jnp.int32) + (SUBLANES - 1)) & (-SUBLANES)
        sizes.append(n8)
        starts.append(start)
        startv = jnp.where(row8 == g, start, startv)
        start = start + n8
    cum = _dot(onehot.astype(_BF16), tri_ref[...])
    pos = jnp.sum(onehot * (startv.astype(_F32) + cum), axis=0, keepdims=True).astype(jnp.int32)
    pos_ref[0] = pos
    perm = jnp.where(lax.broadcasted_iota(jnp.int32, (t_pad, tile), 0) == pos, 1.0, 0.0).astype(_BF16)

    w_hi = w.astype(_BF16).astype(_F32)
    w_lo = (w - w_hi).astype(_BF16).astype(_F32)
    w_hl = jnp.where(row8 < EXPERTS_PER_GROUP, w_hi, pltpu.roll(w_lo, EXPERTS_PER_GROUP, 0))
    w_rows = jnp.concatenate([w_hl, jnp.zeros((LANES - SUBLANES, tile), _F32)], axis=0).astype(_BF16)

    def wait_slot(sl):
        for g in range(N_GROUPS):
            for piece in range(2):
                m = ring_ref[sl, 2 * g + piece]

                @pl.when(m > 0)
                def _():
                    _rows(stage.at[sl], 0, xs_ref, 0, m, sem.at[sl]).wait()

    @pl.when(k >= 2)
    def _():
        wait_slot(slot)

    stage[slot, :, 0:d] = _dot(perm, x1b)
    stage[slot, :, d:] = lax.dot_general(perm, w_rows, _NT, preferred_element_type=_F32)

    for g in range(N_GROUPS):
        n8, sg = sizes[g], starts[g]
        fill, cur, newc = fill_ref[g], cur_ref[g], nfree_ref[0]
        m1 = jnp.minimum(n8, chunk - fill)
        m2 = n8 - m1
        r1 = cur * chunk + fill
        r2 = newc * chunk
        ring_ref[slot, 2 * g] = m1
        ring_ref[slot, 2 * g + 1] = m2
        for f, v in enumerate((r1, m1, r2, m2, sg)):
            meta_ref[k, g * META_FIELDS + f] = v

        @pl.when(m1 > 0)
        def _():
            _rows(stage.at[slot], sg, xs_ref, r1, m1, sem.at[slot]).start()

        @pl.when(m2 > 0)
        def _():
            _rows(stage.at[slot], sg + m1, xs_ref, r2, m2, sem.at[slot]).start()
            info_ref[newc] = g
            cur_ref[g] = newc
            nfree_ref[0] = newc + 1

        fill_ref[g] = jnp.where(m2 > 0, m2, fill + m1)

    @pl.when(k == n_tiles - 1)
    def _():
        zbuf[...] = jnp.zeros_like(zbuf)
        nfree = nfree_ref[0]
        info_ref[n_chunks] = nfree
        tails = []
        for g in range(N_GROUPS):
            rem = chunk - fill_ref[g]
            tails.append((rem, _rows(zbuf, 0, xs_ref, cur_ref[g] * chunk + fill_ref[g], rem, zsem)))
        spare = [(c, pltpu.make_async_copy(zbuf, xs_ref.at[pl.ds(c * chunk, chunk)], zsem))
                 for c in range(n_tiles * tile // chunk, n_chunks)]
        for rem, cp in tails:
            pl.when(rem > 0)(cp.start)
        for c, cp in spare:
            pl.when(c >= nfree)(cp.start)

        @pl.when(k >= 1)
        def _():
            wait_slot(1 - slot)

        wait_slot(slot)
        for rem, cp in tails:
            pl.when(rem > 0)(cp.wait)
        for c, cp in spare:
            pl.when(c >= nfree)(cp.wait)


def _moe_kernel(order_ref, info_ref, xs_ref, wg_ref, wu_ref, wd_ref, ex_ref, o_ref,
                wg_bf, wu_bf, wd_bf, *, d, d_expert, n_chunks):
    i = pl.program_id(0)
    c = order_ref[i]
    grp = info_ref[c]
    prev = info_ref[order_ref[jnp.maximum(i - 1, 0)]]

    @pl.when((i == 0) | (grp != prev))
    def _():
        wg_bf[...] = wg_ref[...].astype(_BF16)
        wu_bf[...] = wu_ref[...].astype(_BF16)
        wd_bf[...] = wd_ref[...].astype(_BF16)

    used = c < info_ref[n_chunks]

    @pl.when(used)
    def _():
        xb = xs_ref[:, 0:d].astype(_BF16)
        cw = _dot(xs_ref[:, d:].astype(_BF16), ex_ref[...])
        y = jnp.zeros(o_ref.shape, _F32)
        for e in range(EXPERTS_PER_GROUP):
            hg = _dot(xb, wg_bf[e])
            hu = _dot(xb, wu_bf[e])
            hid = hg * _sigmoid(hg) * hu * cw[:, e * d_expert:(e + 1) * d_expert]
            y = y + _dot(hid.astype(_BF16), wd_bf[e])
        o_ref[...] = y

    @pl.when(jnp.logical_not(used))
    def _():
        o_ref[...] = jnp.zeros_like(o_ref)


def _ple_kernel(meta_ref, x1_ref, p_ref, pos_ref, ys_ref, wpg_ref, bpg_ref, wpp_ref, l2g_ref, l2b_ref,
                o_ref, ybuf, sem, wpg_bf, wpp_bf, *, alpha, tile, t_pad, n_tiles):
    k = pl.program_id(0)
    slot = k % 2

    def pieces(kk, sl):
        out = []
        for g in range(N_GROUPS):
            r1, m1, r2, m2, sg = (meta_ref[kk, g * META_FIELDS + f] for f in range(META_FIELDS))
            out.append((m1, _rows(ys_ref, r1, ybuf.at[sl], sg, m1, sem.at[sl])))
            out.append((m2, _rows(ys_ref, r2, ybuf.at[sl], sg + m1, m2, sem.at[sl])))
        return out

    def fetch(kk, sl):
        for m, cp in pieces(kk, sl):
            pl.when(m > 0)(cp.start)

    @pl.when(k == 0)
    def _():
        wpg_bf[...] = wpg_ref[...].astype(_BF16)
        wpp_bf[...] = wpp_ref[...].astype(_BF16)
        ybuf[...] = jnp.zeros_like(ybuf)
        fetch(0, 0)

    @pl.when(k + 1 < n_tiles)
    def _():
        fetch(k + 1, 1 - slot)

    for m, cp in pieces(k, slot):
        pl.when(m > 0)(cp.wait)

    perm = jnp.where(lax.broadcasted_iota(jnp.int32, (t_pad, tile), 0) == pos_ref[0], 1.0, 0.0).astype(_BF16)
    y = lax.dot_general(perm, ybuf[slot].astype(_BF16), _TN, preferred_element_type=_F32)
    r = alpha * x1_ref[...] + y
    gate = _sigmoid(_dot(r.astype(_BF16), wpg_bf[...]) + bpg_ref[...])
    pp = _dot(p_ref[...].astype(_BF16), wpp_bf[...])
    o_ref[...] = _layer_norm(r + gate * pp, l2g_ref[...], l2b_ref[...])


def _full(shape):
    return pl.BlockSpec(shape, lambda *_: (0,) * len(shape))


def _resident(shape):
    return pl.BlockSpec(shape, lambda *_: (0,) * len(shape), pipeline_mode=pl.Buffered(1))


def _row(v):
    return v.reshape(1, -1).astype(_F32)


def kernel(x, p, ln_in_g, ln_in_b, w_in, b_in, conf_dw_w, conf_dw_b, conf_ln_g, conf_ln_b, sc_w, sc_b, w_out, b_out, ln1_g, ln1_b, w_rg, b_rg, w_re, b_re, w_gate, w_up, w_down, w_pg, b_pg, w_pp, ln2_g, ln2_b):
    depth = w_in.shape[0]
    assert depth == 1, "single-layer block"
    alpha = (2.0 * depth) ** 0.25
    bsz, seq, d = x.shape
    n = bsz * seq
    conf_k, conf_w = conf_dw_w.shape[1:]
    sc_k, sc_w_ = sc_w.shape[1:]
    d_in = w_in.shape[2]
    n_exp, _, d_expert = w_gate.shape[1:]
    assert n_exp == N_GROUPS * EXPERTS_PER_GROUP
    assert d_in == 2 * conf_w + 3 * sc_w_ and w_out.shape[1] == conf_w + sc_w_
    d_ple = p.shape[-1]

    tile = chunk = TILE
    assert seq % tile == 0
    nt = seq // tile
    n_tiles = bsz * nt
    a_halo = -(-(conf_k - 1) // SUBLANES) * SUBLANES
    u_halo = -(-(sc_k - 1) // SUBLANES) * SUBLANES
    t_pad = tile + N_GROUPS * SUBLANES
    n_chunks = -(-(n + (SUBLANES - 1) * N_GROUPS * n_tiles) // chunk) + N_GROUPS
    d_pay = d + LANES

    wr = jnp.zeros((ROUTER_ROWS, d), _F32).at[0:N_GROUPS].set(w_rg[0].T)
    br = jnp.zeros((ROUTER_ROWS,), _F32).at[0:N_GROUPS].set(b_rg[0])
    for g in range(N_GROUPS):
        lo = SUBLANES * (g + 1)
        sl = slice(g * EXPERTS_PER_GROUP, (g + 1) * EXPERTS_PER_GROUP)
        wr = wr.at[lo:lo + EXPERTS_PER_GROUP].set(w_re[0][:, sl].T)
        br = br.at[lo:lo + EXPERTS_PER_GROUP].set(b_re[0][sl])
    tri = (jnp.arange(tile)[:, None] < jnp.arange(tile)[None, :]).astype(_BF16)

    mixer = pl.pallas_call(
        functools.partial(_mixer_kernel, alpha=alpha, tile=tile, t_pad=t_pad, chunk=chunk,
                          n_chunks=n_chunks, n_tiles=n_tiles, conf_w=conf_w, sc_w_=sc_w_,
                          conf_k=conf_k, sc_k=sc_k, a_halo=a_halo, u_halo=u_halo),
        grid=(bsz, nt),
        in_specs=[
            pl.BlockSpec((1, tile, d), lambda b, j: (b, j, 0)),
            _full((1, d)), _full((1, d)),
            _resident((d, d_in)), _full((1, d_in)),
            _full((conf_k, conf_w)), _full((1, conf_w)), _full((1, conf_w)), _full((1, conf_w)),
            _full((sc_k, sc_w_)), _full((1, sc_w_)),
            _resident((conf_w + sc_w_, d)), _full((1, d)),
            _full((1, d)), _full((1, d)),
            _full((ROUTER_ROWS, d)), _full((ROUTER_ROWS, 1)),
            _resident((tile, tile)),
        ],
        out_specs=[
            pl.BlockSpec((1, tile, d), lambda b, j: (b, j, 0)),
            pl.BlockSpec((1, 1, tile), lambda b, j: (b * nt + j, 0, 0)),
            pl.BlockSpec(memory_space=pl.ANY),
            pl.BlockSpec(memory_space=pltpu.SMEM),
            pl.BlockSpec(memory_space=pltpu.SMEM),
        ],
        out_shape=[
            jax.ShapeDtypeStruct((bsz, seq, d), _F32),
            jax.ShapeDtypeStruct((n_tiles, 1, tile), jnp.int32),
            jax.ShapeDtypeStruct((n_chunks * chunk, d_pay), _F32),
            jax.ShapeDtypeStruct((n_tiles, N_GROUPS * META_FIELDS), jnp.int32),
            jax.ShapeDtypeStruct((n_chunks + 1,), jnp.int32),
        ],
        scratch_shapes=[
            pltpu.VMEM((a_halo + tile, conf_w), _F32),
            pltpu.VMEM((SUBLANES - 1, a_halo + tile - SUBLANES, conf_w), _F32),
            pltpu.VMEM((u_halo + tile, sc_w_), _F32),
            pltpu.VMEM((d, d_in), _BF16),
            pltpu.VMEM((conf_w + sc_w_, d), _BF16),
            pltpu.VMEM((2, t_pad, d_pay), _F32),
            pltpu.VMEM((chunk, d_pay), _F32),
            pltpu.SMEM((N_GROUPS,), jnp.int32),
            pltpu.SMEM((N_GROUPS,), jnp.int32),
            pltpu.SMEM((1,), jnp.int32),
            pltpu.SMEM((2, 2 * N_GROUPS), jnp.int32),
            pltpu.SemaphoreType.DMA((2,)),
            pltpu.SemaphoreType.DMA(()),
        ],
        compiler_params=pltpu.CompilerParams(
            dimension_semantics=("arbitrary", "arbitrary"), vmem_limit_bytes=VMEM_LIMIT_BYTES),
        name="mixer",
    )
    x1, pos, xs, meta, info = mixer(
        x, _row(ln_in_g), _row(ln_in_b), w_in[0], _row(b_in[0]),
        conf_dw_w[0], _row(conf_dw_b[0]), _row(conf_ln_g[0]), _row(conf_ln_b[0]),
        sc_w[0], _row(sc_b[0]), w_out[0], _row(b_out[0]),
        _row(ln1_g[0]), _row(ln1_b[0]), wr.astype(_BF16), br.reshape(ROUTER_ROWS, 1), tri)

    order = jnp.argsort(info[:n_chunks], stable=True).astype(jnp.int32)
    col_e = jnp.arange(EXPERTS_PER_GROUP * d_expert) // d_expert
    lane = jnp.arange(LANES)
    expand = ((col_e[None, :] == (lane % EXPERTS_PER_GROUP)[:, None])
              & (lane < 2 * EXPERTS_PER_GROUP)[:, None]).astype(_BF16)

    moe = pl.pallas_call(
        functools.partial(_moe_kernel, d=d, d_expert=d_expert, n_chunks=n_chunks),
        grid_spec=pltpu.PrefetchScalarGridSpec(
            num_scalar_prefetch=2,
            grid=(n_chunks,),
            in_specs=[
                pl.BlockSpec((chunk, d_pay), lambda i, order, info: (order[i], 0)),
                pl.BlockSpec((EXPERTS_PER_GROUP, d, d_expert), lambda i, order, info: (info[order[i]], 0, 0)),
                pl.BlockSpec((EXPERTS_PER_GROUP, d, d_expert), lambda i, order, info: (info[order[i]], 0, 0)),
                pl.BlockSpec((EXPERTS_PER_GROUP, d_expert, d), lambda i, order, info: (info[order[i]], 0, 0)),
                _full((LANES, EXPERTS_PER_GROUP * d_expert)),
            ],
            out_specs=pl.BlockSpec((chunk, d), lambda i, order, info: (order[i], 0)),
            scratch_shapes=[
                pltpu.VMEM((EXPERTS_PER_GROUP, d, d_expert), _BF16),
                pltpu.VMEM((EXPERTS_PER_GROUP, d, d_expert), _BF16),
                pltpu.VMEM((EXPERTS_PER_GROUP, d_expert, d), _BF16),
            ],
        ),
        out_shape=jax.ShapeDtypeStruct((n_chunks * chunk, d), _F32),
        compiler_params=pltpu.CompilerParams(
            dimension_semantics=("arbitrary",), vmem_limit_bytes=VMEM_LIMIT_BYTES),
        name="moe",
    )
    ys = moe(order, info, xs, w_gate[0], w_up[0], w_down[0], expand)

    ple = pl.pallas_call(
        functools.partial(_ple_kernel, alpha=alpha, tile=tile, t_pad=t_pad, n_tiles=n_tiles),
        grid_spec=pltpu.PrefetchScalarGridSpec(
            num_scalar_prefetch=1,
            grid=(n_tiles,),
            in_specs=[
                pl.BlockSpec((tile, d), lambda k, meta: (k, 0)),
                pl.BlockSpec((tile, d_ple), lambda k, meta: (k, 0)),
                pl.BlockSpec((1, 1, tile), lambda k, meta: (k, 0, 0)),
                pl.BlockSpec(memory_space=pl.ANY),
                _resident((d, d)), _full((1, d)), _resident((d_ple, d)), _full((1, d)), _full((1, d)),
            ],
            out_specs=pl.BlockSpec((tile, d), lambda k, meta: (k, 0)),
            scratch_shapes=[
                pltpu.VMEM((2, t_pad, d), _F32),
                pltpu.SemaphoreType.DMA((2,)),
                pltpu.VMEM((d, d), _BF16),
                pltpu.VMEM((d_ple, d), _BF16),
            ],
        ),
        out_shape=jax.ShapeDtypeStruct((n, d), _F32),
        compiler_params=pltpu.CompilerParams(
            dimension_semantics=("arbitrary",), vmem_limit_bytes=VMEM_LIMIT_BYTES),
        name="ple",
    )
    out = ple(meta, x1.reshape(n, d), p[0].reshape(n, d_ple), pos, ys,
              w_pg[0], _row(b_pg[0]), w_pp[0], _row(ln2_g[0]), _row(ln2_b[0]))
    return out.reshape(bsz, seq, d)
```

```python
import functools

import jax
import jax.numpy as jnp
from jax import lax
from jax.experimental import pallas as pl
from jax.experimental.pallas import tpu as pltpu

LN_EPS = 1e-5
N_GROUPS = 4
EXPERTS_PER_GROUP = 4
SUBLANES = 8
LANES = 128
ROUTER_ROWS = 48
TILE = 512
N_SUB = 2
VMEM_LIMIT_BYTES = 56 * 1024 * 1024
NEG_BIG = -1e30
META_FIELDS = 5

_F32 = jnp.float32
_BF16 = jnp.bfloat16
_NT = (((1,), (1,)), ((), ()))
_TN = (((0,), (0,)), ((), ()))


def _layer_norm(x, g, b):
    mu = jnp.mean(x, axis=-1, keepdims=True)
    xc = x - mu
    var = jnp.mean(xc * xc, axis=-1, keepdims=True)
    return xc * lax.rsqrt(var + LN_EPS) * g + b


def _sigmoid(x):
    return 0.5 * jnp.tanh(0.5 * x) + 0.5


def _dot(a, b):
    return jnp.dot(a, b, preferred_element_type=_F32)


def _aligned(v):
    return v if isinstance(v, int) else pl.multiple_of(v, SUBLANES)


def _rows(src, s0, dst, d0, m, sem):
    return pltpu.make_async_copy(src.at[pl.ds(_aligned(s0), _aligned(m))],
                                 dst.at[pl.ds(_aligned(d0), _aligned(m))], sem)


def _route(logits):
    row = lax.broadcasted_iota(jnp.int32, (SUBLANES, logits.shape[1]), 0)
    real = row < N_GROUPS
    gl = jnp.where(real, logits[0:SUBLANES], NEG_BIG)
    gm = jnp.max(gl, axis=0, keepdims=True)
    ge = jnp.exp(gl - gm)
    gp = ge / jnp.sum(ge, axis=0, keepdims=True)
    gp_top = jnp.max(gp, axis=0, keepdims=True)
    gidx = jnp.min(jnp.where(gp == gp_top, row, SUBLANES), axis=0, keepdims=True)

    el = jnp.zeros_like(gl)
    for g in range(N_GROUPS):
        blk = logits[SUBLANES * (g + 1):SUBLANES * (g + 2)]
        el = jnp.where(gidx == g, blk, el)
    el = jnp.where(real, el, NEG_BIG)
    em = jnp.max(el, axis=0, keepdims=True)
    ee = jnp.exp(el - em)
    ep = ee / jnp.sum(ee, axis=0, keepdims=True)
    ep = jnp.where(real, ep, -1.0)
    p1 = jnp.max(ep, axis=0, keepdims=True)
    i1 = jnp.min(jnp.where(ep == p1, row, SUBLANES), axis=0, keepdims=True)
    ep2 = jnp.where(row == i1, -1.0, ep)
    p2 = jnp.max(ep2, axis=0, keepdims=True)
    i2 = jnp.min(jnp.where(ep2 == p2, row, SUBLANES), axis=0, keepdims=True)
    denom = p1 + p2
    w = jnp.where(row == i1, p1 / denom, jnp.where(row == i2, p2 / denom, 0.0))
    return gidx, w * gp_top


def _mixer_kernel(x_ref, lng_ref, lnb_ref, w_in_ref, b_in_ref, cw_ref, cb_ref, clg_ref, clb_ref,
                  sw_ref, sb_ref, w_out_ref, b_out_ref, l1g_ref, l1b_ref, wr_ref, br_ref, tri_ref,
                  x1_ref, pos_ref, xs_ref, meta_ref, info_ref,
                  abuf, ashift, ubuf, w_in_bf, w_out_bf, stage, zbuf, cur_ref, fill_ref, nfree_ref,
                  ring_ref, sem, zsem, *, alpha, tile, t_pad, chunk, n_chunks, n_tiles, conf_w, sc_w_,
                  conf_k, sc_k, a_halo, u_halo):
    j = pl.program_id(1)
    k = pl.program_id(0) * pl.num_programs(1) + j
    slot = k % 2
    d = x1_ref.shape[-1]

    @pl.when(k == 0)
    def _():
        w_in_bf[...] = w_in_ref[...].astype(_BF16)
        w_out_bf[...] = w_out_ref[...].astype(_BF16)
        nfree_ref[0] = 0
        for g in range(N_GROUPS):
            cur_ref[g] = 0
            fill_ref[g] = chunk
        for c in range(info_ref.shape[0]):
            info_ref[c] = N_GROUPS - 1

    @pl.when(j == 0)
    def _():
        abuf[0:a_halo, :] = jnp.zeros((a_halo, conf_w), _F32)
        ubuf[0:u_halo, :] = jnp.zeros((u_halo, sc_w_), _F32)

    x0 = _layer_norm(x_ref[0], lng_ref[...], lnb_ref[...])
    h = _dot(x0.astype(_BF16), w_in_bf[...]) + b_in_ref[...]
    c0, c1, c2, c3 = conf_w, 2 * conf_w, 2 * conf_w + sc_w_, 2 * conf_w + 2 * sc_w_

    a = h[:, 0:c0] * _sigmoid(h[:, c0:c1])
    abuf[a_halo:a_halo + tile, :] = a
    for i in range(1, SUBLANES):
        ashift[i - 1] = abuf[i:i + tile + a_halo - SUBLANES, :]
    acc = jnp.broadcast_to(cb_ref[...], (tile, conf_w))
    for t in range(conf_k):
        q, i = divmod(a_halo - (conf_k - 1) + t, SUBLANES)
        if i == 0:
            src = abuf[SUBLANES * q:SUBLANES * q + tile, :]
        else:
            src = ashift[i - 1, SUBLANES * q:SUBLANES * q + tile, :]
        acc = acc + cw_ref[t:t + 1, :] * src
    abuf[0:a_halo, :] = abuf[tile:tile + a_halo, :]
    an = _layer_norm(acc, clg_ref[...], clb_ref[...])
    a2 = an * _sigmoid(an)

    u = h[:, c2:c3] * h[:, c3:]
    ubuf[u_halo:u_halo + tile, :] = u
    sc = jnp.broadcast_to(sb_ref[...], (tile, sc_w_))
    for t in range(sc_k):
        off = u_halo - (sc_k - 1) + t
        sc = sc + sw_ref[t:t + 1, :] * ubuf[off:off + tile, :]
    ubuf[0:u_halo, :] = ubuf[tile:tile + u_halo, :]
    s = h[:, c1:c2] * sc

    mix = (_dot(a2.astype(_BF16), w_out_bf[0:conf_w, :])
           + _dot(s.astype(_BF16), w_out_bf[conf_w:, :]) + b_out_ref[...])
    x1 = _layer_norm(alpha * x0 + mix, l1g_ref[...], l1b_ref[...])
    x1_ref[0] = x1
    x1b = x1.astype(_BF16)

    logits = lax.dot_general(wr_ref[...], x1b, _NT, preferred_element_type=_F32) + br_ref[...]
    gidx, w = _route(logits)

    row8 = lax.broadcasted_iota(jnp.int32, (SUBLANES, tile), 0)
    onehot = jnp.where(gidx == row8, 1.0, 0.0)
    cnt = jnp.sum(onehot, axis=1, keepdims=True)
    sizes, starts = [], []
    start = jnp.int32(0)
    startv = jnp.zeros((SUBLANES, tile), jnp.int32)
    for g in range(N_GROUPS):
        n8 = (cnt[g, 0].astype(jnp.int32) + (SUBLANES - 1)) & (-SUBLANES)
        sizes.append(n8)
        starts.append(start)
        startv = jnp.where(row8 == g, start, startv)
        start = start + n8
    cum = _dot(onehot.astype(_BF16), tri_ref[...])
    pos = jnp.sum(onehot * (startv.astype(_F32) + cum), axis=0, keepdims=True).astype(jnp.int32)
    pos_ref[0] = pos
    perm = jnp.where(lax.broadcasted_iota(jnp.int32, (t_pad, tile), 0) == pos, 1.0, 0.0).astype(_BF16)

    w_hi = w.astype(_BF16).astype(_F32)
    w_lo = (w - w_hi).astype(_BF16).astype(_F32)
    w_hl = jnp.where(row8 < EXPERTS_PER_GROUP, w_hi, pltpu.roll(w_lo, EXPERTS_PER_GROUP, 0))
    w_rows = jnp.concatenate([w_hl, jnp.zeros((LANES - SUBLANES, tile), _F32)], axis=0).astype(_BF16)

    def wait_slot(sl):
        for g in range(N_GROUPS):
            for piece in range(2):
                m = ring_ref[sl, 2 * g + piece]

                @pl.when(m > 0)
                def _():
                    _rows(stage.at[sl], 0, xs_ref, 0, m, sem.at[sl]).wait()

    @pl.when(k >= 2)
    def _():
        wait_slot(slot)

    stage[slot, :, 0:d] = _dot(perm, x1b)
    stage[slot, :, d:] = lax.dot_general(perm, w_rows, _NT, preferred_element_type=_F32)

    for g in range(N_GROUPS):
        n8, sg = sizes[g], starts[g]
        fill, cur, newc = fill_ref[g], cur_ref[g], nfree_ref[0]
        m1 = jnp.minimum(n8, chunk - fill)
        m2 = n8 - m1
        r1 = cur * chunk + fill
        r2 = newc * chunk
        ring_ref[slot, 2 * g] = m1
        ring_ref[slot, 2 * g + 1] = m2
        for f, v in enumerate((r1, m1, r2, m2, sg)):
            meta_ref[k, g * META_FIELDS + f] = v

        @pl.when(m1 > 0)
        def _():
            _rows(stage.at[slot], sg, xs_ref, r1, m1, sem.at[slot]).start()

        @pl.when(m2 > 0)
        def _():
            _rows(stage.at[slot], sg + m1, xs_ref, r2, m2, sem.at[slot]).start()
            info_ref[newc] = g
            cur_ref[g] = newc
            nfree_ref[0] = newc + 1

        fill_ref[g] = jnp.where(m2 > 0, m2, fill + m1)

    @pl.when(k == n_tiles - 1)
    def _():
        zbuf[...] = jnp.zeros_like(zbuf)
        nfree = nfree_ref[0]
        info_ref[n_chunks] = nfree
        tails = []
        for g in range(N_GROUPS):
            rem = chunk - fill_ref[g]
            tails.append((rem, _rows(zbuf, 0, xs_ref, cur_ref[g] * chunk + fill_ref[g], rem, zsem)))
        spare = [(c, pltpu.make_async_copy(zbuf, xs_ref.at[pl.ds(c * chunk, chunk)], zsem))
                 for c in range(n_tiles * tile // chunk, n_chunks)]
        for rem, cp in tails:
            pl.when(rem > 0)(cp.start)
        for c, cp in spare:
            pl.when(c >= nfree)(cp.start)

        @pl.when(k >= 1)
        def _():
            wait_slot(1 - slot)

        wait_slot(slot)
        for rem, cp in tails:
            pl.when(rem > 0)(cp.wait)
        for c, cp in spare:
            pl.when(c >= nfree)(cp.wait)


def _moe_kernel(order_ref, info_ref, xs_ref, wg_ref, wu_ref, wd_ref, o_ref,
                wg_bf, wu_bf, wd_bf, *, d, d_expert, n_chunks):
    i = pl.program_id(0)
    c = order_ref[i]
    grp = info_ref[c]
    prev = info_ref[order_ref[jnp.maximum(i - 1, 0)]]

    @pl.when((i == 0) | (grp != prev))
    def _():
        wg_bf[...] = wg_ref[...].astype(_BF16)
        wu_bf[...] = wu_ref[...].astype(_BF16)
        wd_bf[...] = wd_ref[...].astype(_BF16)

    used = c < info_ref[n_chunks]

    @pl.when(used)
    def _():
        xb = xs_ref[:, 0:d].astype(_BF16)
        aux = xs_ref[:, d:]
        y = jnp.zeros(o_ref.shape, _F32)
        for e in range(EXPERTS_PER_GROUP):
            hg = _dot(xb, wg_bf[e])
            hu = _dot(xb, wu_bf[e])
            we = aux[:, e:e + 1] + aux[:, EXPERTS_PER_GROUP + e:EXPERTS_PER_GROUP + e + 1]
            hid = hg * _sigmoid(hg) * hu * we
            y = y + _dot(hid.astype(_BF16), wd_bf[e])
        o_ref[...] = y

    @pl.when(jnp.logical_not(used))
    def _():
        o_ref[...] = jnp.zeros_like(o_ref)


def _ple_kernel(meta_ref, x1_ref, p_ref, pos_ref, ys_ref, wpg_ref, bpg_ref, wpp_ref, l2g_ref, l2b_ref,
                o_ref, ybuf, sem, wpg_bf, wpp_bf, *, alpha, tile, n_sub, t_pad, n_tiles):
    k = pl.program_id(0)
    slot = k % 2

    def pieces(kk, sl):
        out = []
        for g in range(N_GROUPS):
            r1, m1, r2, m2, sg = (meta_ref[kk, g * META_FIELDS + f] for f in range(META_FIELDS))
            out.append((m1, _rows(ys_ref, r1, ybuf.at[sl], sg, m1, sem.at[sl])))
            out.append((m2, _rows(ys_ref, r2, ybuf.at[sl], sg + m1, m2, sem.at[sl])))
        return out

    def fetch(kk, sl):
        for m, cp in pieces(kk, sl):
            pl.when(m > 0)(cp.start)

    @pl.when(k == 0)
    def _():
        wpg_bf[...] = wpg_ref[...].astype(_BF16)
        wpp_bf[...] = wpp_ref[...].astype(_BF16)
        ybuf[...] = jnp.zeros_like(ybuf)
        fetch(0, 0)

    @pl.when(k + 1 < n_tiles)
    def _():
        fetch(k + 1, 1 - slot)

    for m, cp in pieces(k, slot):
        pl.when(m > 0)(cp.wait)

    yb = ybuf[slot].astype(_BF16)
    pos = pos_ref[0]
    sub = tile // n_sub
    for blk in range(n_sub):
        rows = slice(blk * sub, (blk + 1) * sub)
        perm = jnp.where(lax.broadcasted_iota(jnp.int32, (t_pad, sub), 0) == pos[:, rows],
                         1.0, 0.0).astype(_BF16)
        y = lax.dot_general(perm, yb, _TN, preferred_element_type=_F32)
        r = alpha * x1_ref[rows, :] + y
        gate = _sigmoid(_dot(r.astype(_BF16), wpg_bf[...]) + bpg_ref[...])
        pp = _dot(p_ref[rows, :].astype(_BF16), wpp_bf[...])
        o_ref[rows, :] = _layer_norm(r + gate * pp, l2g_ref[...], l2b_ref[...])


def _full(shape):
    return pl.BlockSpec(shape, lambda *_: (0,) * len(shape))


def _resident(shape):
    return pl.BlockSpec(shape, lambda *_: (0,) * len(shape), pipeline_mode=pl.Buffered(1))


def _row(v):
    return v.reshape(1, -1).astype(_F32)


def kernel(x, p, ln_in_g, ln_in_b, w_in, b_in, conf_dw_w, conf_dw_b, conf_ln_g, conf_ln_b, sc_w, sc_b, w_out, b_out, ln1_g, ln1_b, w_rg, b_rg, w_re, b_re, w_gate, w_up, w_down, w_pg, b_pg, w_pp, ln2_g, ln2_b):
    depth = w_in.shape[0]
    assert depth == 1, "single-layer block"
    alpha = (2.0 * depth) ** 0.25
    bsz, seq, d = x.shape
    n = bsz * seq
    conf_k, conf_w = conf_dw_w.shape[1:]
    sc_k, sc_w_ = sc_w.shape[1:]
    d_in = w_in.shape[2]
    n_exp, _, d_expert = w_gate.shape[1:]
    assert n_exp == N_GROUPS * EXPERTS_PER_GROUP
    assert d_in == 2 * conf_w + 3 * sc_w_ and w_out.shape[1] == conf_w + sc_w_
    d_ple = p.shape[-1]

    tile = chunk = TILE
    assert seq % tile == 0
    nt = seq // tile
    n_tiles = bsz * nt
    a_halo = -(-(conf_k - 1) // SUBLANES) * SUBLANES
    u_halo = -(-(sc_k - 1) // SUBLANES) * SUBLANES
    t_pad = tile + N_GROUPS * SUBLANES
    n_chunks = -(-(n + (SUBLANES - 1) * N_GROUPS * n_tiles) // chunk) + N_GROUPS
    d_pay = d + LANES

    wr = jnp.zeros((ROUTER_ROWS, d), _F32).at[0:N_GROUPS].set(w_rg[0].T)
    br = jnp.zeros((ROUTER_ROWS,), _F32).at[0:N_GROUPS].set(b_rg[0])
    for g in range(N_GROUPS):
        lo = SUBLANES * (g + 1)
        sl = slice(g * EXPERTS_PER_GROUP, (g + 1) * EXPERTS_PER_GROUP)
        wr = wr.at[lo:lo + EXPERTS_PER_GROUP].set(w_re[0][:, sl].T)
        br = br.at[lo:lo + EXPERTS_PER_GROUP].set(b_re[0][sl])
    tri = (jnp.arange(tile)[:, None] < jnp.arange(tile)[None, :]).astype(_BF16)

    mixer = pl.pallas_call(
        functools.partial(_mixer_kernel, alpha=alpha, tile=tile, t_pad=t_pad, chunk=chunk,
                          n_chunks=n_chunks, n_tiles=n_tiles, conf_w=conf_w, sc_w_=sc_w_,
                          conf_k=conf_k, sc_k=sc_k, a_halo=a_halo, u_halo=u_halo),
        grid=(bsz, nt),
        in_specs=[
            pl.BlockSpec((1, tile, d), lambda b, j: (b, j, 0)),
            _full((1, d)), _full((1, d)),
            _resident((d, d_in)), _full((1, d_in)),
            _full((conf_k, conf_w)), _full((1, conf_w)), _full((1, conf_w)), _full((1, conf_w)),
            _full((sc_k, sc_w_)), _full((1, sc_w_)),
            _resident((conf_w + sc_w_, d)), _full((1, d)),
            _full((1, d)), _full((1, d)),
            _full((ROUTER_ROWS, d)), _full((ROUTER_ROWS, 1)),
            _resident((tile, tile)),
        ],
        out_specs=[
            pl.BlockSpec((1, tile, d), lambda b, j: (b, j, 0)),
            pl.BlockSpec((1, 1, tile), lambda b, j: (b * nt + j, 0, 0)),
            pl.BlockSpec(memory_space=pl.ANY),
            pl.BlockSpec(memory_space=pltpu.SMEM),
            pl.BlockSpec(memory_space=pltpu.SMEM),
        ],
        out_shape=[
            jax.ShapeDtypeStruct((bsz, seq, d), _F32),
            jax.ShapeDtypeStruct((n_tiles, 1, tile), jnp.int32),
            jax.ShapeDtypeStruct((n_chunks * chunk, d_pay), _F32),
            jax.ShapeDtypeStruct((n_tiles, N_GROUPS * META_FIELDS), jnp.int32),
            jax.ShapeDtypeStruct((n_chunks + 1,), jnp.int32),
        ],
        scratch_shapes=[
            pltpu.VMEM((a_halo + tile, conf_w), _F32),
            pltpu.VMEM((SUBLANES - 1, a_halo + tile - SUBLANES, conf_w), _F32),
            pltpu.VMEM((u_halo + tile, sc_w_), _F32),
            pltpu.VMEM((d, d_in), _BF16),
            pltpu.VMEM((conf_w + sc_w_, d), _BF16),
            pltpu.VMEM((2, t_pad, d_pay), _F32),
            pltpu.VMEM((chunk, d_pay), _F32),
            pltpu.SMEM((N_GROUPS,), jnp.int32),
            pltpu.SMEM((N_GROUPS,), jnp.int32),
            pltpu.SMEM((1,), jnp.int32),
            pltpu.SMEM((2, 2 * N_GROUPS), jnp.int32),
            pltpu.SemaphoreType.DMA((2,)),
            pltpu.SemaphoreType.DMA(()),
        ],
        compiler_params=pltpu.CompilerParams(
            dimension_semantics=("arbitrary", "arbitrary"), vmem_limit_bytes=VMEM_LIMIT_BYTES),
        name="mixer",
    )
    x1, pos, xs, meta, info = mixer(
        x, _row(ln_in_g), _row(ln_in_b), w_in[0], _row(b_in[0]),
        conf_dw_w[0], _row(conf_dw_b[0]), _row(conf_ln_g[0]), _row(conf_ln_b[0]),
        sc_w[0], _row(sc_b[0]), w_out[0], _row(b_out[0]),
        _row(ln1_g[0]), _row(ln1_b[0]), wr.astype(_BF16), br.reshape(ROUTER_ROWS, 1), tri)

    order = jnp.argsort(info[:n_chunks], stable=True).astype(jnp.int32)

    moe = pl.pallas_call(
        functools.partial(_moe_kernel, d=d, d_expert=d_expert, n_chunks=n_chunks),
        grid_spec=pltpu.PrefetchScalarGridSpec(
            num_scalar_prefetch=2,
            grid=(n_chunks,),
            in_specs=[
                pl.BlockSpec((chunk, d_pay), lambda i, order, info: (order[i], 0)),
                pl.BlockSpec((EXPERTS_PER_GROUP, d, d_expert), lambda i, order, info: (info[order[i]], 0, 0)),
                pl.BlockSpec((EXPERTS_PER_GROUP, d, d_expert), lambda i, order, info: (info[order[i]], 0, 0)),
                pl.BlockSpec((EXPERTS_PER_GROUP, d_expert, d), lambda i, order, info: (info[order[i]], 0, 0)),
            ],
            out_specs=pl.BlockSpec((chunk, d), lambda i, order, info: (order[i], 0)),
            scratch_shapes=[
                pltpu.VMEM((EXPERTS_PER_GROUP, d, d_expert), _BF16),
                pltpu.VMEM((EXPERTS_PER_GROUP, d, d_expert), _BF16),
                pltpu.VMEM((EXPERTS_PER_GROUP, d_expert, d), _BF16),
            ],
        ),
        out_shape=jax.ShapeDtypeStruct((n_chunks * chunk, d), _F32),
        compiler_params=pltpu.CompilerParams(
            dimension_semantics=("arbitrary",), vmem_limit_bytes=VMEM_LIMIT_BYTES),
        name="moe",
    )
    ys = moe(order, info, xs, w_gate[0], w_up[0], w_down[0])

    ple = pl.pallas_call(
        functools.partial(_ple_kernel, alpha=alpha, tile=tile, n_sub=N_SUB, t_pad=t_pad, n_tiles=n_tiles),
        grid_spec=pltpu.PrefetchScalarGridSpec(
            num_scalar_prefetch=1,
            grid=(n_tiles,),
            in_specs=[
                pl.BlockSpec((tile, d), lambda k, meta: (k, 0)),
                pl.BlockSpec((tile, d_ple), lambda k, meta: (k, 0)),
                pl.BlockSpec((1, 1, tile), lambda k, meta: (k, 0, 0)),
                pl.BlockSpec(memory_space=pl.ANY),
                _resident((d, d)), _full((1, d)), _resident((d_ple, d)), _full((1, d)), _full((1, d)),
            ],
            out_specs=pl.BlockSpec((tile, d), lambda k, meta: (k, 0)),
            scratch_shapes=[
                pltpu.VMEM((2, t_pad, d), _F32),
                pltpu.SemaphoreType.DMA((2,)),
                pltpu.VMEM((d, d), _BF16),
                pltpu.VMEM((d_ple, d), _BF16),
            ],
        ),
        out_shape=jax.ShapeDtypeStruct((n, d), _F32),
        compiler_params=pltpu.CompilerParams(
            dimension_semantics=("arbitrary",), vmem_limit_bytes=VMEM_LIMIT_BYTES),
        name="ple",
    )
    out = ple(meta, x1.reshape(n, d), p[0].reshape(n, d_ple), pos, ys,
              w_pg[0], _row(b_pg[0]), w_pp[0], _row(ln2_g[0]), _row(ln2_b[0]))
    return out.reshape(bsz, seq, d)
```

```python
import functools

import jax
import jax.numpy as jnp
from jax import lax
from jax.experimental import pallas as pl
from jax.experimental.pallas import tpu as pltpu

LN_EPS = 1e-5
N_GROUPS = 4
EXPERTS_PER_GROUP = 4
SUBLANES = 8
LANES = 128
ROUTER_ROWS = 48
TILE = 512
N_SUB = 2
VMEM_LIMIT_BYTES = 56 * 1024 * 1024
NEG_BIG = -1e30
META_FIELDS = 5

_F32 = jnp.float32
_BF16 = jnp.bfloat16
_NT = (((1,), (1,)), ((), ()))
_TN = (((0,), (0,)), ((), ()))


def _layer_norm(x, g, b):
    mu = jnp.mean(x, axis=-1, keepdims=True)
    xc = x - mu
    var = jnp.mean(xc * xc, axis=-1, keepdims=True)
    return xc * lax.rsqrt(var + LN_EPS) * g + b


def _sigmoid(x):
    return 0.5 * jnp.tanh(0.5 * x) + 0.5


def _dot(a, b):
    return jnp.dot(a, b, preferred_element_type=_F32)


def _aligned(v):
    return v if isinstance(v, int) else pl.multiple_of(v, SUBLANES)


def _rows(src, s0, dst, d0, m, sem):
    return pltpu.make_async_copy(src.at[pl.ds(_aligned(s0), _aligned(m))],
                                 dst.at[pl.ds(_aligned(d0), _aligned(m))], sem)


def _route(logits):
    row = lax.broadcasted_iota(jnp.int32, (SUBLANES, logits.shape[1]), 0)
    real = row < N_GROUPS
    gl = jnp.where(real, logits[0:SUBLANES], NEG_BIG)
    gm = jnp.max(gl, axis=0, keepdims=True)
    ge = jnp.exp(gl - gm)
    gp = ge / jnp.sum(ge, axis=0, keepdims=True)
    gp_top = jnp.max(gp, axis=0, keepdims=True)
    gidx = jnp.min(jnp.where(gp == gp_top, row, SUBLANES), axis=0, keepdims=True)

    el = jnp.zeros_like(gl)
    for g in range(N_GROUPS):
        blk = logits[SUBLANES * (g + 1):SUBLANES * (g + 2)]
        el = jnp.where(gidx == g, blk, el)
    el = jnp.where(real, el, NEG_BIG)
    em = jnp.max(el, axis=0, keepdims=True)
    ee = jnp.exp(el - em)
    ep = ee / jnp.sum(ee, axis=0, keepdims=True)
    ep = jnp.where(real, ep, -1.0)
    p1 = jnp.max(ep, axis=0, keepdims=True)
    i1 = jnp.min(jnp.where(ep == p1, row, SUBLANES), axis=0, keepdims=True)
    ep2 = jnp.where(row == i1, -1.0, ep)
    p2 = jnp.max(ep2, axis=0, keepdims=True)
    i2 = jnp.min(jnp.where(ep2 == p2, row, SUBLANES), axis=0, keepdims=True)
    denom = p1 + p2
    w = jnp.where(row == i1, p1 / denom, jnp.where(row == i2, p2 / denom, 0.0))
    return gidx, w * gp_top


def _mixer_kernel(x_ref, lng_ref, lnb_ref, w_in_ref, b_in_ref, cw_ref, cb_ref, clg_ref, clb_ref,
                  sw_ref, sb_ref, w_out_ref, b_out_ref, l1g_ref, l1b_ref, wr_ref, br_ref, tri_ref,
                  x1_ref, pos_ref, xs_ref, meta_ref, info_ref,
                  abuf, ashift, ubuf, w_in_bf, w_out_bf, stage, zbuf, cur_ref, fill_ref, nfree_ref,
                  ring_ref, sem, zsem, *, alpha, tile, t_pad, chunk, n_chunks, n_tiles, conf_w, sc_w_,
                  conf_k, sc_k, a_halo, u_halo):
    j = pl.program_id(1)
    k = pl.program_id(0) * pl.num_programs(1) + j
    slot = k % 2
    d = x1_ref.shape[-1]

    @pl.when(k == 0)
    def _():
        w_in_bf[...] = w_in_ref[...].astype(_BF16)
        w_out_bf[...] = w_out_ref[...].astype(_BF16)
        nfree_ref[0] = 0
        for g in range(N_GROUPS):
            cur_ref[g] = 0
            fill_ref[g] = chunk
        for c in range(info_ref.shape[0]):
            info_ref[c] = N_GROUPS - 1

    @pl.when(j == 0)
    def _():
        abuf[0:a_halo, :] = jnp.zeros((a_halo, conf_w), _F32)
        ubuf[0:u_halo, :] = jnp.zeros((u_halo, sc_w_), _F32)

    x0 = _layer_norm(x_ref[0], lng_ref[...], lnb_ref[...])
    h = _dot(x0.astype(_BF16), w_in_bf[...]) + b_in_ref[...]
    c0, c1, c2, c3 = conf_w, 2 * conf_w, 2 * conf_w + sc_w_, 2 * conf_w + 2 * sc_w_

    a = h[:, 0:c0] * _sigmoid(h[:, c0:c1])
    abuf[a_halo:a_halo + tile, :] = a
    for i in range(1, SUBLANES):
        ashift[i - 1] = abuf[i:i + tile + a_halo - SUBLANES, :]
    acc = jnp.broadcast_to(cb_ref[...], (tile, conf_w))
    for t in range(conf_k):
        q, i = divmod(a_halo - (conf_k - 1) + t, SUBLANES)
        if i == 0:
            src = abuf[SUBLANES * q:SUBLANES * q + tile, :]
        else:
            src = ashift[i - 1, SUBLANES * q:SUBLANES * q + tile, :]
        acc = acc + cw_ref[t:t + 1, :] * src
    abuf[0:a_halo, :] = abuf[tile:tile + a_halo, :]
    an = _layer_norm(acc, clg_ref[...], clb_ref[...])
    a2 = an * _sigmoid(an)

    u = h[:, c2:c3] * h[:, c3:]
    ubuf[u_halo:u_halo + tile, :] = u
    sc = jnp.broadcast_to(sb_ref[...], (tile, sc_w_))
    for t in range(sc_k):
        off = u_halo - (sc_k - 1) + t
        sc = sc + sw_ref[t:t + 1, :] * ubuf[off:off + tile, :]
    ubuf[0:u_halo, :] = ubuf[tile:tile + u_halo, :]
    s = h[:, c1:c2] * sc

    mix = (_dot(a2.astype(_BF16), w_out_bf[0:conf_w, :])
           + _dot(s.astype(_BF16), w_out_bf[conf_w:, :]) + b_out_ref[...])
    x1 = _layer_norm(alpha * x0 + mix, l1g_ref[...], l1b_ref[...])
    x1_ref[0] = x1
    x1b = x1.astype(_BF16)

    logits = lax.dot_general(wr_ref[...], x1b, _NT, preferred_element_type=_F32) + br_ref[...]
    gidx, w = _route(logits)

    row8 = lax.broadcasted_iota(jnp.int32, (SUBLANES, tile), 0)
    onehot = jnp.where(gidx == row8, 1.0, 0.0)
    cnt = jnp.sum(onehot, axis=1, keepdims=True)
    sizes, starts = [], []
    start = jnp.int32(0)
    startv = jnp.zeros((SUBLANES, tile), jnp.int32)
    for g in range(N_GROUPS):
        n8 = (cnt[g, 0].astype(jnp.int32) + (SUBLANES - 1)) & (-SUBLANES)
        sizes.append(n8)
        starts.append(start)
        startv = jnp.where(row8 == g, start, startv)
        start = start + n8
    cum = _dot(onehot.astype(_BF16), tri_ref[...])
    pos = jnp.sum(onehot * (startv.astype(_F32) + cum), axis=0, keepdims=True).astype(jnp.int32)
    pos_ref[0] = pos
    perm = jnp.where(lax.broadcasted_iota(jnp.int32, (t_pad, tile), 0) == pos, 1.0, 0.0).astype(_BF16)

    w_hi = w.astype(_BF16).astype(_F32)
    w_lo = (w - w_hi).astype(_BF16).astype(_F32)
    w_hl = jnp.where(row8 < EXPERTS_PER_GROUP, w_hi, pltpu.roll(w_lo, EXPERTS_PER_GROUP, 0))
    w_rows = jnp.concatenate([w_hl, jnp.zeros((LANES - SUBLANES, tile), _F32)], axis=0).astype(_BF16)

    def wait_slot(sl):
        for g in range(N_GROUPS):
            for piece in range(2):
                m = ring_ref[sl, 2 * g + piece]

                @pl.when(m > 0)
                def _():
                    _rows(stage.at[sl], 0, xs_ref, 0, m, sem.at[sl]).wait()

    @pl.when(k >= 2)
    def _():
        wait_slot(slot)

    stage[slot, :, 0:d] = _dot(perm, x1b)
    stage[slot, :, d:] = lax.dot_general(perm, w_rows, _NT, preferred_element_type=_F32)

    for g in range(N_GROUPS):
        n8, sg = sizes[g], starts[g]
        fill, cur, newc = fill_ref[g], cur_ref[g], nfree_ref[0]
        m1 = jnp.minimum(n8, chunk - fill)
        m2 = n8 - m1
        r1 = cur * chunk + fill
        r2 = newc * chunk
        ring_ref[slot, 2 * g] = m1
        ring_ref[slot, 2 * g + 1] = m2
        for f, v in enumerate((r1, m1, r2, m2, sg)):
            meta_ref[k, g * META_FIELDS + f] = v

        @pl.when(m1 > 0)
        def _():
            _rows(stage.at[slot], sg, xs_ref, r1, m1, sem.at[slot]).start()

        @pl.when(m2 > 0)
        def _():
            _rows(stage.at[slot], sg + m1, xs_ref, r2, m2, sem.at[slot]).start()
            info_ref[newc] = g
            cur_ref[g] = newc
            nfree_ref[0] = newc + 1

        fill_ref[g] = jnp.where(m2 > 0, m2, fill + m1)

    @pl.when(k == n_tiles - 1)
    def _():
        zbuf[...] = jnp.zeros_like(zbuf)
        nfree = nfree_ref[0]
        info_ref[n_chunks] = nfree
        tails = []
        for g in range(N_GROUPS):
            rem = chunk - fill_ref[g]
            tails.append((rem, _rows(zbuf, 0, xs_ref, cur_ref[g] * chunk + fill_ref[g], rem, zsem)))
        spare = [(c, pltpu.make_async_copy(zbuf, xs_ref.at[pl.ds(c * chunk, chunk)], zsem))
                 for c in range(n_tiles * tile // chunk, n_chunks)]
        for rem, cp in tails:
            pl.when(rem > 0)(cp.start)
        for c, cp in spare:
            pl.when(c >= nfree)(cp.start)

        @pl.when(k >= 1)
        def _():
            wait_slot(1 - slot)

        wait_slot(slot)
        for rem, cp in tails:
            pl.when(rem > 0)(cp.wait)
        for c, cp in spare:
            pl.when(c >= nfree)(cp.wait)


def _moe_kernel(order_ref, info_ref, nextg_ref, xs_ref, wg_hbm, wu_hbm, wd_hbm, o_ref,
                wg_st, wu_st, wd_st, wg_bf, wu_bf, wd_bf, wsem, *, d, d_expert, n_chunks):
    i = pl.program_id(0)
    c = order_ref[i]
    grp = info_ref[c]
    prev = info_ref[order_ref[jnp.maximum(i - 1, 0)]]

    def weight_copies(g):
        lo = g * EXPERTS_PER_GROUP
        return [pltpu.make_async_copy(src.at[pl.ds(lo, EXPERTS_PER_GROUP)], dst, wsem)
                for src, dst in ((wg_hbm, wg_st), (wu_hbm, wu_st), (wd_hbm, wd_st))]

    @pl.when(i == 0)
    def _():
        for cp in weight_copies(grp):
            cp.start()

    @pl.when((i == 0) | (grp != prev))
    def _():
        for cp in weight_copies(grp):
            cp.wait()
        wg_bf[...] = wg_st[...].astype(_BF16)
        wu_bf[...] = wu_st[...].astype(_BF16)
        wd_bf[...] = wd_st[...].astype(_BF16)
        nxt = nextg_ref[i]

        @pl.when(nxt >= 0)
        def _():
            for cp in weight_copies(nxt):
                cp.start()

    used = c < info_ref[n_chunks]

    @pl.when(used)
    def _():
        xb = xs_ref[:, 0:d].astype(_BF16)
        aux = xs_ref[:, d:]
        y = jnp.zeros(o_ref.shape, _F32)
        for e in range(EXPERTS_PER_GROUP):
            hg = _dot(xb, wg_bf[e])
            hu = _dot(xb, wu_bf[e])
            we = aux[:, e:e + 1] + aux[:, EXPERTS_PER_GROUP + e:EXPERTS_PER_GROUP + e + 1]
            hid = hg * _sigmoid(hg) * hu * we
            y = y + _dot(hid.astype(_BF16), wd_bf[e])
        o_ref[...] = y

    @pl.when(jnp.logical_not(used))
    def _():
        o_ref[...] = jnp.zeros_like(o_ref)


def _ple_kernel(meta_ref, x1_ref, p_ref, pos_ref, ys_ref, wpg_ref, bpg_ref, wpp_ref, l2g_ref, l2b_ref,
                o_ref, ybuf, sem, wpg_bf, wpp_bf, *, alpha, tile, n_sub, t_pad, n_tiles):
    k = pl.program_id(0)
    slot = k % 2

    def pieces(kk, sl):
        out = []
        for g in range(N_GROUPS):
            r1, m1, r2, m2, sg = (meta_ref[kk, g * META_FIELDS + f] for f in range(META_FIELDS))
            out.append((m1, _rows(ys_ref, r1, ybuf.at[sl], sg, m1, sem.at[sl])))
            out.append((m2, _rows(ys_ref, r2, ybuf.at[sl], sg + m1, m2, sem.at[sl])))
        return out

    def fetch(kk, sl):
        for m, cp in pieces(kk, sl):
            pl.when(m > 0)(cp.start)

    @pl.when(k == 0)
    def _():
        wpg_bf[...] = wpg_ref[...].astype(_BF16)
        wpp_bf[...] = wpp_ref[...].astype(_BF16)
        ybuf[...] = jnp.zeros_like(ybuf)
        fetch(0, 0)

    @pl.when(k + 1 < n_tiles)
    def _():
        fetch(k + 1, 1 - slot)

    for m, cp in pieces(k, slot):
        pl.when(m > 0)(cp.wait)

    yb = ybuf[slot].astype(_BF16)
    pos = pos_ref[0]
    sub = tile // n_sub
    for blk in range(n_sub):
        rows = slice(blk * sub, (blk + 1) * sub)
        perm = jnp.where(lax.broadcasted_iota(jnp.int32, (t_pad, sub), 0) == pos[:, rows],
                         1.0, 0.0).astype(_BF16)
        y = lax.dot_general(perm, yb, _TN, preferred_element_type=_F32)
        r = alpha * x1_ref[rows, :] + y
        gate = _sigmoid(_dot(r.astype(_BF16), wpg_bf[...]) + bpg_ref[...])
        pp = _dot(p_ref[rows, :].astype(_BF16), wpp_bf[...])
        o_ref[rows, :] = _layer_norm(r + gate * pp, l2g_ref[...], l2b_ref[...])


def _full(shape):
    return pl.BlockSpec(shape, lambda *_: (0,) * len(shape))


def _resident(shape):
    return pl.BlockSpec(shape, lambda *_: (0,) * len(shape), pipeline_mode=pl.Buffered(1))


def _row(v):
    return v.reshape(1, -1).astype(_F32)


def kernel(x, p, ln_in_g, ln_in_b, w_in, b_in, conf_dw_w, conf_dw_b, conf_ln_g, conf_ln_b, sc_w, sc_b, w_out, b_out, ln1_g, ln1_b, w_rg, b_rg, w_re, b_re, w_gate, w_up, w_down, w_pg, b_pg, w_pp, ln2_g, ln2_b):
    depth = w_in.shape[0]
    assert depth == 1, "single-layer block"
    alpha = (2.0 * depth) ** 0.25
    bsz, seq, d = x.shape
    n = bsz * seq
    conf_k, conf_w = conf_dw_w.shape[1:]
    sc_k, sc_w_ = sc_w.shape[1:]
    d_in = w_in.shape[2]
    n_exp, _, d_expert = w_gate.shape[1:]
    assert n_exp == N_GROUPS * EXPERTS_PER_GROUP
    assert d_in == 2 * conf_w + 3 * sc_w_ and w_out.shape[1] == conf_w + sc_w_
    d_ple = p.shape[-1]

    tile = chunk = TILE
    assert seq % tile == 0
    nt = seq // tile
    n_tiles = bsz * nt
    a_halo = -(-(conf_k - 1) // SUBLANES) * SUBLANES
    u_halo = -(-(sc_k - 1) // SUBLANES) * SUBLANES
    t_pad = tile + N_GROUPS * SUBLANES
    n_chunks = -(-(n + (SUBLANES - 1) * N_GROUPS * n_tiles) // chunk) + N_GROUPS
    d_pay = d + LANES

    blocks = [(w_rg[0], b_rg[0])] + [
        (w_re[0][:, g * EXPERTS_PER_GROUP:(g + 1) * EXPERTS_PER_GROUP],
         b_re[0][g * EXPERTS_PER_GROUP:(g + 1) * EXPERTS_PER_GROUP]) for g in range(N_GROUPS)]
    w_cols, b_cols = [], []
    for wb, bb in blocks:
        w_cols += [wb, jnp.zeros((d, SUBLANES - wb.shape[1]), _F32)]
        b_cols += [bb, jnp.zeros((SUBLANES - bb.shape[0],), _F32)]
    tail = ROUTER_ROWS - SUBLANES * len(blocks)
    wr = jnp.concatenate(w_cols + [jnp.zeros((d, tail), _F32)], axis=1).T
    br = jnp.concatenate(b_cols + [jnp.zeros((tail,), _F32)])
    tri = (jnp.arange(tile)[:, None] < jnp.arange(tile)[None, :]).astype(_BF16)

    mixer = pl.pallas_call(
        functools.partial(_mixer_kernel, alpha=alpha, tile=tile, t_pad=t_pad, chunk=chunk,
                          n_chunks=n_chunks, n_tiles=n_tiles, conf_w=conf_w, sc_w_=sc_w_,
                          conf_k=conf_k, sc_k=sc_k, a_halo=a_halo, u_halo=u_halo),
        grid=(bsz, nt),
        in_specs=[
            pl.BlockSpec((1, tile, d), lambda b, j: (b, j, 0)),
            _full((1, d)), _full((1, d)),
            _resident((d, d_in)), _full((1, d_in)),
            _full((conf_k, conf_w)), _full((1, conf_w)), _full((1, conf_w)), _full((1, conf_w)),
            _full((sc_k, sc_w_)), _full((1, sc_w_)),
            _resident((conf_w + sc_w_, d)), _full((1, d)),
            _full((1, d)), _full((1, d)),
            _full((ROUTER_ROWS, d)), _full((ROUTER_ROWS, 1)),
            _resident((tile, tile)),
        ],
        out_specs=[
            pl.BlockSpec((1, tile, d), lambda b, j: (b, j, 0)),
            pl.BlockSpec((1, 1, tile), lambda b, j: (b * nt + j, 0, 0)),
            pl.BlockSpec(memory_space=pl.ANY),
            pl.BlockSpec(memory_space=pltpu.SMEM),
            pl.BlockSpec(memory_space=pltpu.SMEM),
        ],
        out_shape=[
            jax.ShapeDtypeStruct((bsz, seq, d), _F32),
            jax.ShapeDtypeStruct((n_tiles, 1, tile), jnp.int32),
            jax.ShapeDtypeStruct((n_chunks * chunk, d_pay), _F32),
            jax.ShapeDtypeStruct((n_tiles, N_GROUPS * META_FIELDS), jnp.int32),
            jax.ShapeDtypeStruct((n_chunks + 1,), jnp.int32),
        ],
        scratch_shapes=[
            pltpu.VMEM((a_halo + tile, conf_w), _F32),
            pltpu.VMEM((SUBLANES - 1, a_halo + tile - SUBLANES, conf_w), _F32),
            pltpu.VMEM((u_halo + tile, sc_w_), _F32),
            pltpu.VMEM((d, d_in), _BF16),
            pltpu.VMEM((conf_w + sc_w_, d), _BF16),
            pltpu.VMEM((2, t_pad, d_pay), _F32),
            pltpu.VMEM((chunk, d_pay), _F32),
            pltpu.SMEM((N_GROUPS,), jnp.int32),
            pltpu.SMEM((N_GROUPS,), jnp.int32),
            pltpu.SMEM((1,), jnp.int32),
            pltpu.SMEM((2, 2 * N_GROUPS), jnp.int32),
            pltpu.SemaphoreType.DMA((2,)),
            pltpu.SemaphoreType.DMA(()),
        ],
        compiler_params=pltpu.CompilerParams(
            dimension_semantics=("arbitrary", "arbitrary"), vmem_limit_bytes=VMEM_LIMIT_BYTES),
        name="mixer",
    )
    x1, pos, xs, meta, info = mixer(
        x, _row(ln_in_g), _row(ln_in_b), w_in[0], _row(b_in[0]),
        conf_dw_w[0], _row(conf_dw_b[0]), _row(conf_ln_g[0]), _row(conf_ln_b[0]),
        sc_w[0], _row(sc_b[0]), w_out[0], _row(b_out[0]),
        _row(ln1_g[0]), _row(ln1_b[0]), wr.astype(_BF16), br.reshape(ROUTER_ROWS, 1), tri)

    order = jnp.argsort(info[:n_chunks], stable=True).astype(jnp.int32)
    gseq = info[:n_chunks][order]
    later = jnp.arange(n_chunks)[None, :] > jnp.arange(n_chunks)[:, None]
    cand = jnp.where(later & (gseq[None, :] != gseq[:, None]), jnp.arange(n_chunks)[None, :], n_chunks)
    first = jnp.min(cand, axis=1)
    nextg = jnp.where(first < n_chunks, gseq[jnp.minimum(first, n_chunks - 1)], -1).astype(jnp.int32)

    moe = pl.pallas_call(
        functools.partial(_moe_kernel, d=d, d_expert=d_expert, n_chunks=n_chunks),
        grid_spec=pltpu.PrefetchScalarGridSpec(
            num_scalar_prefetch=3,
            grid=(n_chunks,),
            in_specs=[
                pl.BlockSpec((chunk, d_pay), lambda i, order, info, nextg: (order[i], 0)),
                pl.BlockSpec(memory_space=pl.ANY),
                pl.BlockSpec(memory_space=pl.ANY),
                pl.BlockSpec(memory_space=pl.ANY),
            ],
            out_specs=pl.BlockSpec((chunk, d), lambda i, order, info, nextg: (order[i], 0)),
            scratch_shapes=[
                pltpu.VMEM((EXPERTS_PER_GROUP, d, d_expert), _F32),
                pltpu.VMEM((EXPERTS_PER_GROUP, d, d_expert), _F32),
                pltpu.VMEM((EXPERTS_PER_GROUP, d_expert, d), _F32),
                pltpu.VMEM((EXPERTS_PER_GROUP, d, d_expert), _BF16),
                pltpu.VMEM((EXPERTS_PER_GROUP, d, d_expert), _BF16),
                pltpu.VMEM((EXPERTS_PER_GROUP, d_expert, d), _BF16),
                pltpu.SemaphoreType.DMA(()),
            ],
        ),
        out_shape=jax.ShapeDtypeStruct((n_chunks * chunk, d), _F32),
        compiler_params=pltpu.CompilerParams(
            dimension_semantics=("arbitrary",), vmem_limit_bytes=VMEM_LIMIT_BYTES),
        name="moe",
    )
    ys = moe(order, info, nextg, xs, w_gate[0], w_up[0], w_down[0])

    ple = pl.pallas_call(
        functools.partial(_ple_kernel, alpha=alpha, tile=tile, n_sub=N_SUB, t_pad=t_pad, n_tiles=n_tiles),
        grid_spec=pltpu.PrefetchScalarGridSpec(
            num_scalar_prefetch=1,
            grid=(n_tiles,),
            in_specs=[
                pl.BlockSpec((tile, d), lambda k, meta: (k, 0)),
                pl.BlockSpec((tile, d_ple), lambda k, meta: (k, 0)),
                pl.BlockSpec((1, 1, tile), lambda k, meta: (k, 0, 0)),
                pl.BlockSpec(memory_space=pl.ANY),
                _resident((d, d)), _full((1, d)), _resident((d_ple, d)), _full((1, d)), _full((1, d)),
            ],
            out_specs=pl.BlockSpec((tile, d), lambda k, meta: (k, 0)),
            scratch_shapes=[
                pltpu.VMEM((2, t_pad, d), _F32),
                pltpu.SemaphoreType.DMA((2,)),
                pltpu.VMEM((d, d), _BF16),
                pltpu.VMEM((d_ple, d), _BF16),
            ],
        ),
        out_shape=jax.ShapeDtypeStruct((n, d), _F32),
        compiler_params=pltpu.CompilerParams(
            dimension_semantics=("arbitrary",), vmem_limit_bytes=VMEM_LIMIT_BYTES),
        name="ple",
    )
    out = ple(meta, x1.reshape(n, d), p[0].reshape(n, d_ple), pos, ys,
              w_pg[0], _row(b_pg[0]), w_pp[0], _row(ln2_g[0]), _row(ln2_b[0]))
    return out.reshape(bsz, seq, d)
```

```python
import functools

import jax
import jax.numpy as jnp
from jax import lax
from jax.experimental import pallas as pl
from jax.experimental.pallas import tpu as pltpu

LN_EPS = 1e-5
N_GROUPS = 4
EXPERTS_PER_GROUP = 4
SUBLANES = 8
LANES = 128
ROUTER_ROWS = 32
TILE = 512
N_SUB = 2
VMEM_LIMIT_BYTES = 56 * 1024 * 1024
NEG_BIG = -1e30
META_FIELDS = 5

_F32 = jnp.float32
_BF16 = jnp.bfloat16
_NT = (((1,), (1,)), ((), ()))
_TN = (((0,), (0,)), ((), ()))


def _layer_norm(x, g, b):
    mu = jnp.mean(x, axis=-1, keepdims=True)
    xc = x - mu
    var = jnp.mean(xc * xc, axis=-1, keepdims=True)
    return xc * lax.rsqrt(var + LN_EPS) * g + b


def _sigmoid(x):
    return 0.5 * jnp.tanh(0.5 * x) + 0.5


def _dot(a, b):
    return jnp.dot(a, b, preferred_element_type=_F32)


def _aligned(v):
    return v if isinstance(v, int) else pl.multiple_of(v, SUBLANES)


def _rows(src, s0, dst, d0, m, sem):
    return pltpu.make_async_copy(src.at[pl.ds(_aligned(s0), _aligned(m))],
                                 dst.at[pl.ds(_aligned(d0), _aligned(m))], sem)


def _route(logits):
    row = lax.broadcasted_iota(jnp.int32, (SUBLANES, logits.shape[1]), 0)
    real = row < N_GROUPS
    gl = jnp.where(real, logits[0:SUBLANES], NEG_BIG)
    gm = jnp.max(gl, axis=0, keepdims=True)
    ge = jnp.exp(gl - gm)
    gp = ge / jnp.sum(ge, axis=0, keepdims=True)
    gp_top = jnp.max(gp, axis=0, keepdims=True)
    gidx = jnp.min(jnp.where(gp == gp_top, row, SUBLANES), axis=0, keepdims=True)

    el = jnp.zeros_like(gl)
    for g in range(N_GROUPS):
        lo = N_GROUPS + g * EXPERTS_PER_GROUP
        blk = logits[lo - lo % SUBLANES:lo - lo % SUBLANES + SUBLANES]
        if lo % SUBLANES:
            blk = pltpu.roll(blk, SUBLANES - lo % SUBLANES, 0)
        el = jnp.where(gidx == g, blk, el)
    el = jnp.where(real, el, NEG_BIG)
    em = jnp.max(el, axis=0, keepdims=True)
    ee = jnp.exp(el - em)
    ep = ee / jnp.sum(ee, axis=0, keepdims=True)
    ep = jnp.where(real, ep, -1.0)
    p1 = jnp.max(ep, axis=0, keepdims=True)
    i1 = jnp.min(jnp.where(ep == p1, row, SUBLANES), axis=0, keepdims=True)
    ep2 = jnp.where(row == i1, -1.0, ep)
    p2 = jnp.max(ep2, axis=0, keepdims=True)
    i2 = jnp.min(jnp.where(ep2 == p2, row, SUBLANES), axis=0, keepdims=True)
    denom = p1 + p2
    w = jnp.where(row == i1, p1 / denom, jnp.where(row == i2, p2 / denom, 0.0))
    return gidx, w * gp_top


def _mixer_kernel(x_ref, lng_ref, lnb_ref, w_in_ref, b_in_ref, cw_ref, cb_ref, clg_ref, clb_ref,
                  sw_ref, sb_ref, w_out_ref, b_out_ref, l1g_ref, l1b_ref, wr_ref, br_ref, tri_ref,
                  x1_ref, pos_ref, xs_ref, meta_ref, info_ref, order_ref,
                  abuf, ashift, ubuf, w_in_bf, w_out_bf, stage, zbuf, cur_ref, fill_ref, nfree_ref,
                  ring_ref, sem, zsem, *, alpha, tile, t_pad, chunk, n_chunks, n_tiles, conf_w, sc_w_,
                  conf_k, sc_k, a_halo, u_halo):
    j = pl.program_id(1)
    k = pl.program_id(0) * pl.num_programs(1) + j
    slot = k % 2
    d = x1_ref.shape[-1]

    @pl.when(k == 0)
    def _():
        w_in_bf[...] = w_in_ref[...].astype(_BF16)
        w_out_bf[...] = w_out_ref[...].astype(_BF16)
        nfree_ref[0] = 0
        for g in range(N_GROUPS):
            cur_ref[g] = 0
            fill_ref[g] = chunk
        for c in range(info_ref.shape[0]):
            info_ref[c] = N_GROUPS - 1

    @pl.when(j == 0)
    def _():
        abuf[0:a_halo, :] = jnp.zeros((a_halo, conf_w), _F32)
        ubuf[0:u_halo, :] = jnp.zeros((u_halo, sc_w_), _F32)

    x0 = _layer_norm(x_ref[0], lng_ref[...], lnb_ref[...])
    h = _dot(x0.astype(_BF16), w_in_bf[...]) + b_in_ref[...]
    c0, c1, c2, c3 = conf_w, 2 * conf_w, 2 * conf_w + sc_w_, 2 * conf_w + 2 * sc_w_

    a = h[:, 0:c0] * _sigmoid(h[:, c0:c1])
    abuf[a_halo:a_halo + tile, :] = a
    for i in range(1, SUBLANES):
        ashift[i - 1] = abuf[i:i + tile + a_halo - SUBLANES, :]
    acc = jnp.broadcast_to(cb_ref[...], (tile, conf_w))
    for t in range(conf_k):
        q, i = divmod(a_halo - (conf_k - 1) + t, SUBLANES)
        if i == 0:
            src = abuf[SUBLANES * q:SUBLANES * q + tile, :]
        else:
            src = ashift[i - 1, SUBLANES * q:SUBLANES * q + tile, :]
        acc = acc + cw_ref[t:t + 1, :] * src
    abuf[0:a_halo, :] = abuf[tile:tile + a_halo, :]
    an = _layer_norm(acc, clg_ref[...], clb_ref[...])
    a2 = an * _sigmoid(an)

    u = h[:, c2:c3] * h[:, c3:]
    ubuf[u_halo:u_halo + tile, :] = u
    sc = jnp.broadcast_to(sb_ref[...], (tile, sc_w_))
    for t in range(sc_k):
        off = u_halo - (sc_k - 1) + t
        sc = sc + sw_ref[t:t + 1, :] * ubuf[off:off + tile, :]
    ubuf[0:u_halo, :] = ubuf[tile:tile + u_halo, :]
    s = h[:, c1:c2] * sc

    mix = (_dot(a2.astype(_BF16), w_out_bf[0:conf_w, :])
           + _dot(s.astype(_BF16), w_out_bf[conf_w:, :]) + b_out_ref[...])
    x1 = _layer_norm(alpha * x0 + mix, l1g_ref[...], l1b_ref[...])
    x1_ref[0] = x1
    x1b = x1.astype(_BF16)

    logits = lax.dot_general(wr_ref[...], x1b, _NT, preferred_element_type=_F32) + br_ref[...]
    gidx, w = _route(logits)

    row8 = lax.broadcasted_iota(jnp.int32, (SUBLANES, tile), 0)
    onehot = jnp.where(gidx == row8, 1.0, 0.0)
    cnt = jnp.sum(onehot, axis=1, keepdims=True)
    sizes, starts = [], []
    start = jnp.int32(0)
    startv = jnp.zeros((SUBLANES, tile), jnp.int32)
    for g in range(N_GROUPS):
        n8 = (cnt[g, 0].astype(jnp.int32) + (SUBLANES - 1)) & (-SUBLANES)
        sizes.append(n8)
        starts.append(start)
        startv = jnp.where(row8 == g, start, startv)
        start = start + n8
    cum = _dot(onehot.astype(_BF16), tri_ref[...])
    pos = jnp.sum(onehot * (startv.astype(_F32) + cum), axis=0, keepdims=True).astype(jnp.int32)
    pos_ref[0] = pos
    perm = jnp.where(lax.broadcasted_iota(jnp.int32, (t_pad, tile), 0) == pos, 1.0, 0.0).astype(_BF16)

    w_hi = w.astype(_BF16).astype(_F32)
    w_lo = (w - w_hi).astype(_BF16).astype(_F32)
    w_hl = jnp.where(row8 < EXPERTS_PER_GROUP, w_hi, pltpu.roll(w_lo, EXPERTS_PER_GROUP, 0))
    w_rows = jnp.concatenate([w_hl, jnp.zeros((LANES - SUBLANES, tile), _F32)], axis=0).astype(_BF16)

    def wait_slot(sl):
        for g in range(N_GROUPS):
            for piece in range(2):
                m = ring_ref[sl, 2 * g + piece]

                @pl.when(m > 0)
                def _():
                    _rows(stage.at[sl], 0, xs_ref, 0, m, sem.at[sl]).wait()

    @pl.when(k >= 2)
    def _():
        wait_slot(slot)

    stage[slot, :, 0:d] = _dot(perm, x1b)
    stage[slot, :, d:] = lax.dot_general(perm, w_rows, _NT, preferred_element_type=_F32)

    for g in range(N_GROUPS):
        n8, sg = sizes[g], starts[g]
        fill, cur, newc = fill_ref[g], cur_ref[g], nfree_ref[0]
        m1 = jnp.minimum(n8, chunk - fill)
        m2 = n8 - m1
        r1 = cur * chunk + fill
        r2 = newc * chunk
        ring_ref[slot, 2 * g] = m1
        ring_ref[slot, 2 * g + 1] = m2
        for f, v in enumerate((r1, m1, r2, m2, sg)):
            meta_ref[k, g * META_FIELDS + f] = v

        @pl.when(m1 > 0)
        def _():
            _rows(stage.at[slot], sg, xs_ref, r1, m1, sem.at[slot]).start()

        @pl.when(m2 > 0)
        def _():
            _rows(stage.at[slot], sg + m1, xs_ref, r2, m2, sem.at[slot]).start()
            info_ref[newc] = g
            cur_ref[g] = newc
            nfree_ref[0] = newc + 1

        fill_ref[g] = jnp.where(m2 > 0, m2, fill + m1)

    @pl.when(k == n_tiles - 1)
    def _():
        zbuf[...] = jnp.zeros_like(zbuf)
        nfree = nfree_ref[0]
        info_ref[n_chunks] = nfree
        tails = []
        for g in range(N_GROUPS):
            rem = chunk - fill_ref[g]
            tails.append((rem, _rows(zbuf, 0, xs_ref, cur_ref[g] * chunk + fill_ref[g], rem, zsem)))
        spare = [(c, pltpu.make_async_copy(zbuf, xs_ref.at[pl.ds(c * chunk, chunk)], zsem))
                 for c in range(n_tiles * tile // chunk, n_chunks)]
        for rem, cp in tails:
            pl.when(rem > 0)(cp.start)
        for c, cp in spare:
            pl.when(c >= nfree)(cp.start)

        slot_out = jnp.int32(0)
        for g in range(N_GROUPS):
            def place(c, nxt, g=g):
                hit = info_ref[c] == g

                @pl.when(hit)
                def _():
                    order_ref[nxt] = c

                return nxt + hit.astype(jnp.int32)

            slot_out = lax.fori_loop(0, n_chunks, place, slot_out)

        @pl.when(k >= 1)
        def _():
            wait_slot(1 - slot)

        wait_slot(slot)
        for rem, cp in tails:
            pl.when(rem > 0)(cp.wait)
        for c, cp in spare:
            pl.when(c >= nfree)(cp.wait)


def _moe_kernel(order_ref, info_ref, xs_ref, wg_hbm, wu_hbm, wd_hbm, o_ref,
                wg_st, wu_st, wd_st, wg_bf, wu_bf, wd_bf, wsem, *, d, d_expert, n_chunks):
    i = pl.program_id(0)
    c = order_ref[i]
    grp = info_ref[c]
    prev = info_ref[order_ref[jnp.maximum(i - 1, 0)]]

    def weight_copies(g):
        lo = g * EXPERTS_PER_GROUP
        return [pltpu.make_async_copy(src.at[pl.ds(lo, EXPERTS_PER_GROUP)], dst, wsem)
                for src, dst in ((wg_hbm, wg_st), (wu_hbm, wu_st), (wd_hbm, wd_st))]

    @pl.when(i == 0)
    def _():
        for cp in weight_copies(grp):
            cp.start()

    @pl.when((i == 0) | (grp != prev))
    def _():
        for cp in weight_copies(grp):
            cp.wait()
        wg_bf[...] = wg_st[...].astype(_BF16)
        wu_bf[...] = wu_st[...].astype(_BF16)
        wd_bf[...] = wd_st[...].astype(_BF16)

        def later_group(j, found):
            gj = info_ref[order_ref[j]]
            return jnp.where((found < 0) & (j > i) & (gj != grp), gj, found)

        nxt = lax.fori_loop(0, n_chunks, later_group, jnp.int32(-1))

        @pl.when(nxt >= 0)
        def _():
            for cp in weight_copies(nxt):
                cp.start()

    used = c < info_ref[n_chunks]

    @pl.when(used)
    def _():
        xb = xs_ref[:, 0:d].astype(_BF16)
        aux = xs_ref[:, d:]
        y = jnp.zeros(o_ref.shape, _F32)
        for e in range(EXPERTS_PER_GROUP):
            hg = _dot(xb, wg_bf[e])
            hu = _dot(xb, wu_bf[e])
            we = aux[:, e:e + 1] + aux[:, EXPERTS_PER_GROUP + e:EXPERTS_PER_GROUP + e + 1]
            hid = hg * _sigmoid(hg) * hu * we
            y = y + _dot(hid.astype(_BF16), wd_bf[e])
        o_ref[...] = y

    @pl.when(jnp.logical_not(used))
    def _():
        o_ref[...] = jnp.zeros_like(o_ref)


def _ple_kernel(meta_ref, x1_ref, p_ref, pos_ref, ys_ref, wpg_ref, bpg_ref, wpp_ref, l2g_ref, l2b_ref,
                o_ref, ybuf, sem, wpg_bf, wpp_bf, *, alpha, tile, n_sub, t_pad, n_tiles):
    k = pl.program_id(0)
    slot = k % 2

    def pieces(kk, sl):
        out = []
        for g in range(N_GROUPS):
            r1, m1, r2, m2, sg = (meta_ref[kk, g * META_FIELDS + f] for f in range(META_FIELDS))
            out.append((m1, _rows(ys_ref, r1, ybuf.at[sl], sg, m1, sem.at[sl])))
            out.append((m2, _rows(ys_ref, r2, ybuf.at[sl], sg + m1, m2, sem.at[sl])))
        return out

    def fetch(kk, sl):
        for m, cp in pieces(kk, sl):
            pl.when(m > 0)(cp.start)

    @pl.when(k == 0)
    def _():
        wpg_bf[...] = wpg_ref[...].astype(_BF16)
        wpp_bf[...] = wpp_ref[...].astype(_BF16)
        ybuf[...] = jnp.zeros_like(ybuf)
        fetch(0, 0)

    @pl.when(k + 1 < n_tiles)
    def _():
        fetch(k + 1, 1 - slot)

    for m, cp in pieces(k, slot):
        pl.when(m > 0)(cp.wait)

    yb = ybuf[slot].astype(_BF16)
    pos = pos_ref[0]
    sub = tile // n_sub
    for blk in range(n_sub):
        rows = slice(blk * sub, (blk + 1) * sub)
        perm = jnp.where(lax.broadcasted_iota(jnp.int32, (t_pad, sub), 0) == pos[:, rows],
                         1.0, 0.0).astype(_BF16)
        y = lax.dot_general(perm, yb, _TN, preferred_element_type=_F32)
        r = alpha * x1_ref[rows, :] + y
        gate = _sigmoid(_dot(r.astype(_BF16), wpg_bf[...]) + bpg_ref[...])
        pp = _dot(p_ref[rows, :].astype(_BF16), wpp_bf[...])
        o_ref[rows, :] = _layer_norm(r + gate * pp, l2g_ref[...], l2b_ref[...])


def _full(shape):
    return pl.BlockSpec(shape, lambda *_: (0,) * len(shape))


def _resident(shape):
    return pl.BlockSpec(shape, lambda *_: (0,) * len(shape), pipeline_mode=pl.Buffered(1))


def _row(v):
    return v.reshape(1, -1).astype(_F32)


def kernel(x, p, ln_in_g, ln_in_b, w_in, b_in, conf_dw_w, conf_dw_b, conf_ln_g, conf_ln_b, sc_w, sc_b, w_out, b_out, ln1_g, ln1_b, w_rg, b_rg, w_re, b_re, w_gate, w_up, w_down, w_pg, b_pg, w_pp, ln2_g, ln2_b):
    depth = w_in.shape[0]
    assert depth == 1, "single-layer block"
    alpha = (2.0 * depth) ** 0.25
    bsz, seq, d = x.shape
    n = bsz * seq
    conf_k, conf_w = conf_dw_w.shape[1:]
    sc_k, sc_w_ = sc_w.shape[1:]
    d_in = w_in.shape[2]
    n_exp, _, d_expert = w_gate.shape[1:]
    assert n_exp == N_GROUPS * EXPERTS_PER_GROUP
    assert d_in == 2 * conf_w + 3 * sc_w_ and w_out.shape[1] == conf_w + sc_w_
    d_ple = p.shape[-1]

    tile = chunk = TILE
    assert seq % tile == 0
    nt = seq // tile
    n_tiles = bsz * nt
    a_halo = -(-(conf_k - 1) // SUBLANES) * SUBLANES
    u_halo = -(-(sc_k - 1) // SUBLANES) * SUBLANES
    t_pad = tile + N_GROUPS * SUBLANES
    n_chunks = -(-(n + (SUBLANES - 1) * N_GROUPS * n_tiles) // chunk) + N_GROUPS
    d_pay = d + LANES

    n_logits = N_GROUPS + n_exp
    wr = jnp.pad(jnp.concatenate([w_rg[0], w_re[0]], axis=1).T, ((0, ROUTER_ROWS - n_logits), (0, 0)))
    br = jnp.pad(jnp.concatenate([b_rg[0], b_re[0]]), (0, ROUTER_ROWS - n_logits))
    tri = (jnp.arange(tile)[:, None] < jnp.arange(tile)[None, :]).astype(_BF16)

    mixer = pl.pallas_call(
        functools.partial(_mixer_kernel, alpha=alpha, tile=tile, t_pad=t_pad, chunk=chunk,
                          n_chunks=n_chunks, n_tiles=n_tiles, conf_w=conf_w, sc_w_=sc_w_,
                          conf_k=conf_k, sc_k=sc_k, a_halo=a_halo, u_halo=u_halo),
        grid=(bsz, nt),
        in_specs=[
            pl.BlockSpec((1, tile, d), lambda b, j: (b, j, 0)),
            _full((1, d)), _full((1, d)),
            _resident((d, d_in)), _full((1, d_in)),
            _full((conf_k, conf_w)), _full((1, conf_w)), _full((1, conf_w)), _full((1, conf_w)),
            _full((sc_k, sc_w_)), _full((1, sc_w_)),
            _resident((conf_w + sc_w_, d)), _full((1, d)),
            _full((1, d)), _full((1, d)),
            _full((ROUTER_ROWS, d)), _full((ROUTER_ROWS, 1)),
            _resident((tile, tile)),
        ],
        out_specs=[
            pl.BlockSpec((1, tile, d), lambda b, j: (b, j, 0)),
            pl.BlockSpec((1, 1, tile), lambda b, j: (b * nt + j, 0, 0)),
            pl.BlockSpec(memory_space=pl.ANY),
            pl.BlockSpec(memory_space=pltpu.SMEM),
            pl.BlockSpec(memory_space=pltpu.SMEM),
            pl.BlockSpec(memory_space=pltpu.SMEM),
        ],
        out_shape=[
            jax.ShapeDtypeStruct((bsz, seq, d), _F32),
            jax.ShapeDtypeStruct((n_tiles, 1, tile), jnp.int32),
            jax.ShapeDtypeStruct((n_chunks * chunk, d_pay), _F32),
            jax.ShapeDtypeStruct((n_tiles, N_GROUPS * META_FIELDS), jnp.int32),
            jax.ShapeDtypeStruct((n_chunks + 1,), jnp.int32),
            jax.ShapeDtypeStruct((n_chunks,), jnp.int32),
        ],
        scratch_shapes=[
            pltpu.VMEM((a_halo + tile, conf_w), _F32),
            pltpu.VMEM((SUBLANES - 1, a_halo + tile - SUBLANES, conf_w), _F32),
            pltpu.VMEM((u_halo + tile, sc_w_), _F32),
            pltpu.VMEM((d, d_in), _BF16),
            pltpu.VMEM((conf_w + sc_w_, d), _BF16),
            pltpu.VMEM((2, t_pad, d_pay), _F32),
            pltpu.VMEM((chunk, d_pay), _F32),
            pltpu.SMEM((N_GROUPS,), jnp.int32),
            pltpu.SMEM((N_GROUPS,), jnp.int32),
            pltpu.SMEM((1,), jnp.int32),
            pltpu.SMEM((2, 2 * N_GROUPS), jnp.int32),
            pltpu.SemaphoreType.DMA((2,)),
            pltpu.SemaphoreType.DMA(()),
        ],
        compiler_params=pltpu.CompilerParams(
            dimension_semantics=("arbitrary", "arbitrary"), vmem_limit_bytes=VMEM_LIMIT_BYTES),
        name="mixer",
    )
    x1, pos, xs, meta, info, order = mixer(
        x, _row(ln_in_g), _row(ln_in_b), w_in[0], _row(b_in[0]),
        conf_dw_w[0], _row(conf_dw_b[0]), _row(conf_ln_g[0]), _row(conf_ln_b[0]),
        sc_w[0], _row(sc_b[0]), w_out[0], _row(b_out[0]),
        _row(ln1_g[0]), _row(ln1_b[0]), wr.astype(_BF16), br.reshape(ROUTER_ROWS, 1), tri)


    moe = pl.pallas_call(
        functools.partial(_moe_kernel, d=d, d_expert=d_expert, n_chunks=n_chunks),
        grid_spec=pltpu.PrefetchScalarGridSpec(
            num_scalar_prefetch=2,
            grid=(n_chunks,),
            in_specs=[
                pl.BlockSpec((chunk, d_pay), lambda i, order, info: (order[i], 0)),
                pl.BlockSpec(memory_space=pl.ANY),
                pl.BlockSpec(memory_space=pl.ANY),
                pl.BlockSpec(memory_space=pl.ANY),
            ],
            out_specs=pl.BlockSpec((chunk, d), lambda i, order, info: (order[i], 0)),
            scratch_shapes=[
                pltpu.VMEM((EXPERTS_PER_GROUP, d, d_expert), _F32),
                pltpu.VMEM((EXPERTS_PER_GROUP, d, d_expert), _F32),
                pltpu.VMEM((EXPERTS_PER_GROUP, d_expert, d), _F32),
                pltpu.VMEM((EXPERTS_PER_GROUP, d, d_expert), _BF16),
                pltpu.VMEM((EXPERTS_PER_GROUP, d, d_expert), _BF16),
                pltpu.VMEM((EXPERTS_PER_GROUP, d_expert, d), _BF16),
                pltpu.SemaphoreType.DMA(()),
            ],
        ),
        out_shape=jax.ShapeDtypeStruct((n_chunks * chunk, d), _F32),
        compiler_params=pltpu.CompilerParams(
            dimension_semantics=("arbitrary",), vmem_limit_bytes=VMEM_LIMIT_BYTES),
        name="moe",
    )
    ys = moe(order, info, xs, w_gate[0], w_up[0], w_down[0])

    ple = pl.pallas_call(
        functools.partial(_ple_kernel, alpha=alpha, tile=tile, n_sub=N_SUB, t_pad=t_pad, n_tiles=n_tiles),
        grid_spec=pltpu.PrefetchScalarGridSpec(
            num_scalar_prefetch=1,
            grid=(n_tiles,),
            in_specs=[
                pl.BlockSpec((tile, d), lambda k, meta: (k, 0)),
                pl.BlockSpec((tile, d_ple), lambda k, meta: (k, 0)),
                pl.BlockSpec((1, 1, tile), lambda k, meta: (k, 0, 0)),
                pl.BlockSpec(memory_space=pl.ANY),
                _resident((d, d)), _full((1, d)), _resident((d_ple, d)), _full((1, d)), _full((1, d)),
            ],
            out_specs=pl.BlockSpec((tile, d), lambda k, meta: (k, 0)),
            scratch_shapes=[
                pltpu.VMEM((2, t_pad, d), _F32),
                pltpu.SemaphoreType.DMA((2,)),
                pltpu.VMEM((d, d), _BF16),
                pltpu.VMEM((d_ple, d), _BF16),
            ],
        ),
        out_shape=jax.ShapeDtypeStruct((n, d), _F32),
        compiler_params=pltpu.CompilerParams(
            dimension_semantics=("arbitrary",), vmem_limit_bytes=VMEM_LIMIT_BYTES),
        name="ple",
    )
    out = ple(meta, x1.reshape(n, d), p[0].reshape(n, d_ple), pos, ys,
              w_pg[0], _row(b_pg[0]), w_pp[0], _row(ln2_g[0]), _row(ln2_b[0]))
    return out.reshape(bsz, seq, d)
```

```python
import functools

import jax
import jax.numpy as jnp
from jax import lax
from jax.experimental import pallas as pl
from jax.experimental.pallas import tpu as pltpu

LN_EPS = 1e-5
N_GROUPS = 4
EXPERTS_PER_GROUP = 4
SUBLANES = 8
LANES = 128
ROUTER_ROWS = 32
TILE = 512
N_SUB = 2
VMEM_LIMIT_BYTES = 56 * 1024 * 1024
NEG_BIG = -1e30
META_FIELDS = 5

_F32 = jnp.float32
_BF16 = jnp.bfloat16
_NT = (((1,), (1,)), ((), ()))
_TN = (((0,), (0,)), ((), ()))


def _layer_norm(x, g, b):
    mu = jnp.mean(x, axis=-1, keepdims=True)
    xc = x - mu
    var = jnp.mean(xc * xc, axis=-1, keepdims=True)
    return xc * lax.rsqrt(var + LN_EPS) * g + b


def _sigmoid(x):
    return 0.5 * jnp.tanh(0.5 * x) + 0.5


def _dot(a, b):
    return jnp.dot(a, b, preferred_element_type=_F32)


def _aligned(v):
    return v if isinstance(v, int) else pl.multiple_of(v, SUBLANES)


def _rows(src, s0, dst, d0, m, sem):
    return pltpu.make_async_copy(src.at[pl.ds(_aligned(s0), _aligned(m))],
                                 dst.at[pl.ds(_aligned(d0), _aligned(m))], sem)


def _route(logits):
    row = lax.broadcasted_iota(jnp.int32, (SUBLANES, logits.shape[1]), 0)
    real = row < N_GROUPS
    gl = jnp.where(real, logits[0:SUBLANES], NEG_BIG)
    gm = jnp.max(gl, axis=0, keepdims=True)
    ge = jnp.exp(gl - gm)
    gp = ge / jnp.sum(ge, axis=0, keepdims=True)
    gp_top = jnp.max(gp, axis=0, keepdims=True)
    gidx = jnp.min(jnp.where(gp == gp_top, row, SUBLANES), axis=0, keepdims=True)

    el = jnp.zeros_like(gl)
    for g in range(N_GROUPS):
        lo = N_GROUPS + g * EXPERTS_PER_GROUP
        blk = logits[lo - lo % SUBLANES:lo - lo % SUBLANES + SUBLANES]
        if lo % SUBLANES:
            blk = pltpu.roll(blk, SUBLANES - lo % SUBLANES, 0)
        el = jnp.where(gidx == g, blk, el)
    el = jnp.where(real, el, NEG_BIG)
    em = jnp.max(el, axis=0, keepdims=True)
    ee = jnp.exp(el - em)
    ep = ee / jnp.sum(ee, axis=0, keepdims=True)
    ep = jnp.where(real, ep, -1.0)
    p1 = jnp.max(ep, axis=0, keepdims=True)
    i1 = jnp.min(jnp.where(ep == p1, row, SUBLANES), axis=0, keepdims=True)
    ep2 = jnp.where(row == i1, -1.0, ep)
    p2 = jnp.max(ep2, axis=0, keepdims=True)
    i2 = jnp.min(jnp.where(ep2 == p2, row, SUBLANES), axis=0, keepdims=True)
    denom = p1 + p2
    w = jnp.where(row == i1, p1 / denom, jnp.where(row == i2, p2 / denom, 0.0))
    return gidx, w * gp_top


def _mixer_kernel(x_ref, lng_ref, lnb_ref, w_in_ref, b_in_ref, cw_ref, cb_ref, clg_ref, clb_ref,
                  sw_ref, sb_ref, w_out_ref, b_out_ref, l1g_ref, l1b_ref, wr_ref, br_ref, tri_ref,
                  x1_ref, pos_ref, xs_ref, meta_ref, info_ref, order_ref,
                  abuf, ashift, ubuf, w_in_bf, w_out_bf, stage, zbuf, cur_ref, fill_ref, nfree_ref,
                  ring_ref, sem, zsem, *, alpha, tile, t_pad, chunk, n_chunks, n_tiles, conf_w, sc_w_,
                  conf_k, sc_k, a_halo, u_halo):
    j = pl.program_id(1)
    k = pl.program_id(0) * pl.num_programs(1) + j
    slot = k % 2
    d = x1_ref.shape[-1]

    @pl.when(k == 0)
    def _():
        w_in_bf[...] = w_in_ref[...].astype(_BF16)
        w_out_bf[...] = w_out_ref[...].astype(_BF16)
        nfree_ref[0] = 0
        for g in range(N_GROUPS):
            cur_ref[g] = 0
            fill_ref[g] = chunk
        for c in range(info_ref.shape[0]):
            info_ref[c] = N_GROUPS - 1

    @pl.when(j == 0)
    def _():
        abuf[0:a_halo, :] = jnp.zeros((a_halo, conf_w), _F32)
        ubuf[0:u_halo, :] = jnp.zeros((u_halo, sc_w_), _F32)

    x0 = _layer_norm(x_ref[0], lng_ref[...], lnb_ref[...])
    h = _dot(x0.astype(_BF16), w_in_bf[...]) + b_in_ref[...]
    c0, c1, c2, c3 = conf_w, 2 * conf_w, 2 * conf_w + sc_w_, 2 * conf_w + 2 * sc_w_

    a = h[:, 0:c0] * _sigmoid(h[:, c0:c1])
    abuf[a_halo:a_halo + tile, :] = a
    for i in range(1, SUBLANES):
        ashift[i - 1] = abuf[i:i + tile + a_halo - SUBLANES, :]
    acc = jnp.broadcast_to(cb_ref[...], (tile, conf_w))
    for t in range(conf_k):
        q, i = divmod(a_halo - (conf_k - 1) + t, SUBLANES)
        if i == 0:
            src = abuf[SUBLANES * q:SUBLANES * q + tile, :]
        else:
            src = ashift[i - 1, SUBLANES * q:SUBLANES * q + tile, :]
        acc = acc + cw_ref[t:t + 1, :] * src
    abuf[0:a_halo, :] = abuf[tile:tile + a_halo, :]
    an = _layer_norm(acc, clg_ref[...], clb_ref[...])
    a2 = an * _sigmoid(an)

    u = h[:, c2:c3] * h[:, c3:]
    ubuf[u_halo:u_halo + tile, :] = u
    sc = jnp.broadcast_to(sb_ref[...], (tile, sc_w_))
    for t in range(sc_k):
        off = u_halo - (sc_k - 1) + t
        sc = sc + sw_ref[t:t + 1, :] * ubuf[off:off + tile, :]
    ubuf[0:u_halo, :] = ubuf[tile:tile + u_halo, :]
    s = h[:, c1:c2] * sc

    mix = (_dot(a2.astype(_BF16), w_out_bf[0:conf_w, :])
           + _dot(s.astype(_BF16), w_out_bf[conf_w:, :]) + b_out_ref[...])
    x1 = _layer_norm(alpha * x0 + mix, l1g_ref[...], l1b_ref[...])
    x1_ref[0] = x1
    x1b = x1.astype(_BF16)

    logits = lax.dot_general(wr_ref[...], x1b, _NT, preferred_element_type=_F32) + br_ref[...]
    gidx, w = _route(logits)

    row8 = lax.broadcasted_iota(jnp.int32, (SUBLANES, tile), 0)
    onehot = jnp.where(gidx == row8, 1.0, 0.0)
    cnt = jnp.sum(onehot, axis=1, keepdims=True)
    sizes, starts = [], []
    start = jnp.int32(0)
    startv = jnp.zeros((SUBLANES, tile), jnp.int32)
    for g in range(N_GROUPS):
        n8 = (cnt[g, 0].astype(jnp.int32) + (SUBLANES - 1)) & (-SUBLANES)
        sizes.append(n8)
        starts.append(start)
        startv = jnp.where(row8 == g, start, startv)
        start = start + n8
    cum = _dot(onehot.astype(_BF16), tri_ref[...])
    pos = jnp.sum(onehot * (startv.astype(_F32) + cum), axis=0, keepdims=True).astype(jnp.int32)
    pos_ref[0] = pos
    perm = jnp.where(lax.broadcasted_iota(jnp.int32, (t_pad, tile), 0) == pos, 1.0, 0.0).astype(_BF16)

    w_hi = w.astype(_BF16).astype(_F32)
    w_lo = (w - w_hi).astype(_BF16).astype(_F32)
    w_hl = jnp.where(row8 < EXPERTS_PER_GROUP, w_hi, pltpu.roll(w_lo, EXPERTS_PER_GROUP, 0))
    w_rows = jnp.concatenate([w_hl, jnp.zeros((LANES - SUBLANES, tile), _F32)], axis=0).astype(_BF16)

    def wait_slot(sl):
        for g in range(N_GROUPS):
            for piece in range(2):
                m = ring_ref[sl, 2 * g + piece]

                @pl.when(m > 0)
                def _():
                    _rows(stage.at[sl], 0, xs_ref, 0, m, sem.at[sl]).wait()

    @pl.when(k >= 2)
    def _():
        wait_slot(slot)

    stage[slot, :, 0:d] = _dot(perm, x1b)
    stage[slot, :, d:] = lax.dot_general(perm, w_rows, _NT, preferred_element_type=_F32)

    for g in range(N_GROUPS):
        n8, sg = sizes[g], starts[g]
        fill, cur, newc = fill_ref[g], cur_ref[g], nfree_ref[0]
        m1 = jnp.minimum(n8, chunk - fill)
        m2 = n8 - m1
        r1 = cur * chunk + fill
        r2 = newc * chunk
        ring_ref[slot, 2 * g] = m1
        ring_ref[slot, 2 * g + 1] = m2
        for f, v in enumerate((r1, m1, r2, m2, sg)):
            meta_ref[k, g * META_FIELDS + f] = v

        @pl.when(m1 > 0)
        def _():
            _rows(stage.at[slot], sg, xs_ref, r1, m1, sem.at[slot]).start()

        @pl.when(m2 > 0)
        def _():
            _rows(stage.at[slot], sg + m1, xs_ref, r2, m2, sem.at[slot]).start()
            info_ref[newc] = g
            cur_ref[g] = newc
            nfree_ref[0] = newc + 1

        fill_ref[g] = jnp.where(m2 > 0, m2, fill + m1)

    @pl.when(k == n_tiles - 1)
    def _():
        zbuf[...] = jnp.zeros_like(zbuf)
        nfree = nfree_ref[0]
        info_ref[n_chunks] = nfree
        tails = []
        for g in range(N_GROUPS):
            rem = chunk - fill_ref[g]
            tails.append((rem, _rows(zbuf, 0, xs_ref, cur_ref[g] * chunk + fill_ref[g], rem, zsem)))
        spare = [(c, pltpu.make_async_copy(zbuf, xs_ref.at[pl.ds(c * chunk, chunk)], zsem))
                 for c in range(n_tiles * tile // chunk, n_chunks)]
        for rem, cp in tails:
            pl.when(rem > 0)(cp.start)
        for c, cp in spare:
            pl.when(c >= nfree)(cp.start)

        slot_out = jnp.int32(0)
        for g in range(N_GROUPS):
            def place(c, nxt, g=g):
                hit = info_ref[c] == g

                @pl.when(hit)
                def _():
                    order_ref[nxt] = c

                return nxt + hit.astype(jnp.int32)

            slot_out = lax.fori_loop(0, n_chunks, place, slot_out)

        @pl.when(k >= 1)
        def _():
            wait_slot(1 - slot)

        wait_slot(slot)
        for rem, cp in tails:
            pl.when(rem > 0)(cp.wait)
        for c, cp in spare:
            pl.when(c >= nfree)(cp.wait)


def _moe_kernel(order_ref, info_ref, xs_ref, wg_hbm, wu_hbm, wd_hbm, o_ref,
                wg_buf, wu_buf, wd_buf, slot_ref, wsem, *, d, d_expert, n_chunks):
    i = pl.program_id(0)
    c = order_ref[i]
    grp = info_ref[c]
    prev = info_ref[order_ref[jnp.maximum(i - 1, 0)]]

    def weight_copies(g, sl):
        lo = g * EXPERTS_PER_GROUP
        return [pltpu.make_async_copy(src.at[pl.ds(lo, EXPERTS_PER_GROUP)], dst.at[sl], wsem)
                for src, dst in ((wg_hbm, wg_buf), (wu_hbm, wu_buf), (wd_hbm, wd_buf))]

    @pl.when(i == 0)
    def _():
        slot_ref[0] = 0
        for cp in weight_copies(grp, 0):
            cp.start()

    @pl.when((i == 0) | (grp != prev))
    def _():
        @pl.when(i > 0)
        def _():
            slot_ref[0] = 1 - slot_ref[0]

        cur = slot_ref[0]
        for cp in weight_copies(grp, cur):
            cp.wait()

        def later_group(j, found):
            gj = info_ref[order_ref[j]]
            return jnp.where((found < 0) & (j > i) & (gj != grp), gj, found)

        nxt = lax.fori_loop(0, n_chunks, later_group, jnp.int32(-1))

        @pl.when(nxt >= 0)
        def _():
            for cp in weight_copies(nxt, 1 - cur):
                cp.start()

    used = c < info_ref[n_chunks]

    @pl.when(used)
    def _():
        xb = xs_ref[:, 0:d]
        ws = slot_ref[0]
        aux = xs_ref[:, d:]
        y = jnp.zeros(o_ref.shape, _F32)
        for e in range(EXPERTS_PER_GROUP):
            hg = _dot(xb, wg_buf[ws, e])
            hu = _dot(xb, wu_buf[ws, e])
            we = aux[:, e:e + 1] + aux[:, EXPERTS_PER_GROUP + e:EXPERTS_PER_GROUP + e + 1]
            hid = hg * _sigmoid(hg) * hu * we
            y = y + _dot(hid, wd_buf[ws, e])
        o_ref[...] = y

    @pl.when(jnp.logical_not(used))
    def _():
        o_ref[...] = jnp.zeros_like(o_ref)


def _ple_kernel(meta_ref, x1_ref, p_ref, pos_ref, ys_ref, wpg_ref, bpg_ref, wpp_ref, l2g_ref, l2b_ref,
                o_ref, ybuf, sem, *, alpha, tile, n_sub, t_pad, n_tiles):
    k = pl.program_id(0)
    slot = k % 2

    def pieces(kk, sl):
        out = []
        for g in range(N_GROUPS):
            r1, m1, r2, m2, sg = (meta_ref[kk, g * META_FIELDS + f] for f in range(META_FIELDS))
            out.append((m1, _rows(ys_ref, r1, ybuf.at[sl], sg, m1, sem.at[sl])))
            out.append((m2, _rows(ys_ref, r2, ybuf.at[sl], sg + m1, m2, sem.at[sl])))
        return out

    def fetch(kk, sl):
        for m, cp in pieces(kk, sl):
            pl.when(m > 0)(cp.start)

    @pl.when(k == 0)
    def _():
        ybuf[...] = jnp.zeros_like(ybuf)
        fetch(0, 0)

    @pl.when(k + 1 < n_tiles)
    def _():
        fetch(k + 1, 1 - slot)

    for m, cp in pieces(k, slot):
        pl.when(m > 0)(cp.wait)

    yb = ybuf[slot]
    pos = pos_ref[0]
    sub = tile // n_sub
    for blk in range(n_sub):
        rows = slice(blk * sub, (blk + 1) * sub)
        perm = jnp.where(lax.broadcasted_iota(jnp.int32, (t_pad, sub), 0) == pos[:, rows], 1.0, 0.0)
        y = lax.dot_general(perm, yb, _TN, preferred_element_type=_F32)
        r = alpha * x1_ref[rows, :] + y
        gate = _sigmoid(_dot(r, wpg_ref[...]) + bpg_ref[...])
        pp = _dot(p_ref[rows, :], wpp_ref[...])
        o_ref[rows, :] = _layer_norm(r + gate * pp, l2g_ref[...], l2b_ref[...])


def _full(shape):
    return pl.BlockSpec(shape, lambda *_: (0,) * len(shape))


def _resident(shape):
    return pl.BlockSpec(shape, lambda *_: (0,) * len(shape), pipeline_mode=pl.Buffered(1))


def _row(v):
    return v.reshape(1, -1).astype(_F32)


def kernel(x, p, ln_in_g, ln_in_b, w_in, b_in, conf_dw_w, conf_dw_b, conf_ln_g, conf_ln_b, sc_w, sc_b, w_out, b_out, ln1_g, ln1_b, w_rg, b_rg, w_re, b_re, w_gate, w_up, w_down, w_pg, b_pg, w_pp, ln2_g, ln2_b):
    depth = w_in.shape[0]
    assert depth == 1, "single-layer block"
    alpha = (2.0 * depth) ** 0.25
    bsz, seq, d = x.shape
    n = bsz * seq
    conf_k, conf_w = conf_dw_w.shape[1:]
    sc_k, sc_w_ = sc_w.shape[1:]
    d_in = w_in.shape[2]
    n_exp, _, d_expert = w_gate.shape[1:]
    assert n_exp == N_GROUPS * EXPERTS_PER_GROUP
    assert d_in == 2 * conf_w + 3 * sc_w_ and w_out.shape[1] == conf_w + sc_w_
    d_ple = p.shape[-1]

    tile = chunk = TILE
    assert seq % tile == 0
    nt = seq // tile
    n_tiles = bsz * nt
    a_halo = -(-(conf_k - 1) // SUBLANES) * SUBLANES
    u_halo = -(-(sc_k - 1) // SUBLANES) * SUBLANES
    t_pad = tile + N_GROUPS * SUBLANES
    n_chunks = -(-(n + (SUBLANES - 1) * N_GROUPS * n_tiles) // chunk) + N_GROUPS
    d_pay = d + LANES

    n_logits = N_GROUPS + n_exp
    wr = jnp.pad(jnp.concatenate([w_rg[0], w_re[0]], axis=1).T, ((0, ROUTER_ROWS - n_logits), (0, 0)))
    br = jnp.pad(jnp.concatenate([b_rg[0], b_re[0]]), (0, ROUTER_ROWS - n_logits))
    tri = (jnp.arange(tile)[:, None] < jnp.arange(tile)[None, :]).astype(_BF16)

    mixer = pl.pallas_call(
        functools.partial(_mixer_kernel, alpha=alpha, tile=tile, t_pad=t_pad, chunk=chunk,
                          n_chunks=n_chunks, n_tiles=n_tiles, conf_w=conf_w, sc_w_=sc_w_,
                          conf_k=conf_k, sc_k=sc_k, a_halo=a_halo, u_halo=u_halo),
        grid=(bsz, nt),
        in_specs=[
            pl.BlockSpec((1, tile, d), lambda b, j: (b, j, 0)),
            _full((1, d)), _full((1, d)),
            _resident((d, d_in)), _full((1, d_in)),
            _full((conf_k, conf_w)), _full((1, conf_w)), _full((1, conf_w)), _full((1, conf_w)),
            _full((sc_k, sc_w_)), _full((1, sc_w_)),
            _resident((conf_w + sc_w_, d)), _full((1, d)),
            _full((1, d)), _full((1, d)),
            _full((ROUTER_ROWS, d)), _full((ROUTER_ROWS, 1)),
            _resident((tile, tile)),
        ],
        out_specs=[
            pl.BlockSpec((1, tile, d), lambda b, j: (b, j, 0)),
            pl.BlockSpec((1, 1, tile), lambda b, j: (b * nt + j, 0, 0)),
            pl.BlockSpec(memory_space=pl.ANY),
            pl.BlockSpec(memory_space=pltpu.SMEM),
            pl.BlockSpec(memory_space=pltpu.SMEM),
            pl.BlockSpec(memory_space=pltpu.SMEM),
        ],
        out_shape=[
            jax.ShapeDtypeStruct((bsz, seq, d), _F32),
            jax.ShapeDtypeStruct((n_tiles, 1, tile), jnp.int32),
            jax.ShapeDtypeStruct((n_chunks * chunk, d_pay), _F32),
            jax.ShapeDtypeStruct((n_tiles, N_GROUPS * META_FIELDS), jnp.int32),
            jax.ShapeDtypeStruct((n_chunks + 1,), jnp.int32),
            jax.ShapeDtypeStruct((n_chunks,), jnp.int32),
        ],
        scratch_shapes=[
            pltpu.VMEM((a_halo + tile, conf_w), _F32),
            pltpu.VMEM((SUBLANES - 1, a_halo + tile - SUBLANES, conf_w), _F32),
            pltpu.VMEM((u_halo + tile, sc_w_), _F32),
            pltpu.VMEM((d, d_in), _BF16),
            pltpu.VMEM((conf_w + sc_w_, d), _BF16),
            pltpu.VMEM((2, t_pad, d_pay), _F32),
            pltpu.VMEM((chunk, d_pay), _F32),
            pltpu.SMEM((N_GROUPS,), jnp.int32),
            pltpu.SMEM((N_GROUPS,), jnp.int32),
            pltpu.SMEM((1,), jnp.int32),
            pltpu.SMEM((2, 2 * N_GROUPS), jnp.int32),
            pltpu.SemaphoreType.DMA((2,)),
            pltpu.SemaphoreType.DMA(()),
        ],
        compiler_params=pltpu.CompilerParams(
            dimension_semantics=("arbitrary", "arbitrary"), vmem_limit_bytes=VMEM_LIMIT_BYTES),
        name="mixer",
    )
    x1, pos, xs, meta, info, order = mixer(
        x, _row(ln_in_g), _row(ln_in_b), w_in[0], _row(b_in[0]),
        conf_dw_w[0], _row(conf_dw_b[0]), _row(conf_ln_g[0]), _row(conf_ln_b[0]),
        sc_w[0], _row(sc_b[0]), w_out[0], _row(b_out[0]),
        _row(ln1_g[0]), _row(ln1_b[0]), wr.astype(_BF16), br.reshape(ROUTER_ROWS, 1), tri)


    moe = pl.pallas_call(
        functools.partial(_moe_kernel, d=d, d_expert=d_expert, n_chunks=n_chunks),
        grid_spec=pltpu.PrefetchScalarGridSpec(
            num_scalar_prefetch=2,
            grid=(n_chunks,),
            in_specs=[
                pl.BlockSpec((chunk, d_pay), lambda i, order, info: (order[i], 0)),
                pl.BlockSpec(memory_space=pl.ANY),
                pl.BlockSpec(memory_space=pl.ANY),
                pl.BlockSpec(memory_space=pl.ANY),
            ],
            out_specs=pl.BlockSpec((chunk, d), lambda i, order, info: (order[i], 0)),
            scratch_shapes=[
                pltpu.VMEM((2, EXPERTS_PER_GROUP, d, d_expert), _F32),
                pltpu.VMEM((2, EXPERTS_PER_GROUP, d, d_expert), _F32),
                pltpu.VMEM((2, EXPERTS_PER_GROUP, d_expert, d), _F32),
                pltpu.SMEM((1,), jnp.int32),
                pltpu.SemaphoreType.DMA(()),
            ],
        ),
        out_shape=jax.ShapeDtypeStruct((n_chunks * chunk, d), _F32),
        compiler_params=pltpu.CompilerParams(
            dimension_semantics=("arbitrary",), vmem_limit_bytes=VMEM_LIMIT_BYTES),
        name="moe",
    )
    ys = moe(order, info, xs, w_gate[0], w_up[0], w_down[0])

    ple = pl.pallas_call(
        functools.partial(_ple_kernel, alpha=alpha, tile=tile, n_sub=N_SUB, t_pad=t_pad, n_tiles=n_tiles),
        grid_spec=pltpu.PrefetchScalarGridSpec(
            num_scalar_prefetch=1,
            grid=(n_tiles,),
            in_specs=[
                pl.BlockSpec((tile, d), lambda k, meta: (k, 0)),
                pl.BlockSpec((tile, d_ple), lambda k, meta: (k, 0)),
                pl.BlockSpec((1, 1, tile), lambda k, meta: (k, 0, 0)),
                pl.BlockSpec(memory_space=pl.ANY),
                _resident((d, d)), _full((1, d)), _resident((d_ple, d)), _full((1, d)), _full((1, d)),
            ],
            out_specs=pl.BlockSpec((tile, d), lambda k, meta: (k, 0)),
            scratch_shapes=[
                pltpu.VMEM((2, t_pad, d), _F32),
                pltpu.SemaphoreType.DMA((2,)),
            ],
        ),
        out_shape=jax.ShapeDtypeStruct((n, d), _F32),
        compiler_params=pltpu.CompilerParams(
            dimension_semantics=("arbitrary",), vmem_limit_bytes=VMEM_LIMIT_BYTES),
        name="ple",
    )
    out = ple(meta, x1.reshape(n, d), p[0].reshape(n, d_ple), pos, ys,
              w_pg[0], _row(b_pg[0]), w_pp[0], _row(ln2_g[0]), _row(ln2_b[0]))
    return out.reshape(bsz, seq, d)
```

```python
import functools

import jax
import jax.numpy as jnp
from jax import lax
from jax.experimental import pallas as pl
from jax.experimental.pallas import tpu as pltpu

LN_EPS = 1e-5
N_GROUPS = 4
EXPERTS_PER_GROUP = 4
SUBLANES = 8
LANES = 128
ROUTER_ROWS = 32
TILE = 512
POS_HALF_BITS = 5
N_SUB = 2
VMEM_LIMIT_BYTES = 56 * 1024 * 1024
NEG_BIG = -1e30
META_FIELDS = 5

_F32 = jnp.float32
_BF16 = jnp.bfloat16
_NT = (((1,), (1,)), ((), ()))
_TN = (((0,), (0,)), ((), ()))


def _layer_norm(x, g, b):
    mu = jnp.mean(x, axis=-1, keepdims=True)
    xc = x - mu
    var = jnp.mean(xc * xc, axis=-1, keepdims=True)
    return xc * lax.rsqrt(var + LN_EPS) * g + b


def _sigmoid(x):
    return 0.5 * jnp.tanh(0.5 * x) + 0.5


def _dot(a, b):
    return jnp.dot(a, b, preferred_element_type=_F32)


def _aligned(v):
    return v if isinstance(v, int) else pl.multiple_of(v, SUBLANES)


def _rows(src, s0, dst, d0, m, sem):
    return pltpu.make_async_copy(src.at[pl.ds(_aligned(s0), _aligned(m))],
                                 dst.at[pl.ds(_aligned(d0), _aligned(m))], sem)


def _route(logits):
    row = lax.broadcasted_iota(jnp.int32, (SUBLANES, logits.shape[1]), 0)
    real = row < N_GROUPS
    gl = jnp.where(real, logits[0:SUBLANES], NEG_BIG)
    gm = jnp.max(gl, axis=0, keepdims=True)
    ge = jnp.exp(gl - gm)
    gp = ge / jnp.sum(ge, axis=0, keepdims=True)
    gp_top = jnp.max(gp, axis=0, keepdims=True)
    gidx = jnp.min(jnp.where(gp == gp_top, row, SUBLANES), axis=0, keepdims=True)

    el = jnp.zeros_like(gl)
    for g in range(N_GROUPS):
        lo = N_GROUPS + g * EXPERTS_PER_GROUP
        blk = logits[lo - lo % SUBLANES:lo - lo % SUBLANES + SUBLANES]
        if lo % SUBLANES:
            blk = pltpu.roll(blk, SUBLANES - lo % SUBLANES, 0)
        el = jnp.where(gidx == g, blk, el)
    el = jnp.where(real, el, NEG_BIG)
    em = jnp.max(el, axis=0, keepdims=True)
    ee = jnp.exp(el - em)
    ep = ee / jnp.sum(ee, axis=0, keepdims=True)
    ep = jnp.where(real, ep, -1.0)
    p1 = jnp.max(ep, axis=0, keepdims=True)
    i1 = jnp.min(jnp.where(ep == p1, row, SUBLANES), axis=0, keepdims=True)
    ep2 = jnp.where(row == i1, -1.0, ep)
    p2 = jnp.max(ep2, axis=0, keepdims=True)
    i2 = jnp.min(jnp.where(ep2 == p2, row, SUBLANES), axis=0, keepdims=True)
    denom = p1 + p2
    w = jnp.where(row == i1, p1 / denom, jnp.where(row == i2, p2 / denom, 0.0))
    return gidx, w * gp_top


def _history(tail, prev_tail):
    n = tail.shape[0]
    q = lax.broadcasted_iota(jnp.int32, tail.shape, 0) % SUBLANES
    return jnp.where(q == 0, pltpu.roll(prev_tail, n - (SUBLANES - 1), 0), pltpu.roll(tail, 1, 0))


def _mixer_kernel(x_hbm, lng_ref, lnb_ref, w_in_ref, b_in_ref, cw_ref, cb_ref, clg_ref, clb_ref,
                  sw_ref, sb_ref, w_out_ref, b_out_ref, l1g_ref, l1b_ref, wr_ref, br_ref, tri_ref, unperm_ref,
                  x1_hbm, pos_ref, xs_ref, meta_ref, info_ref, order_ref,
                  xbuf, x1buf, abuf, aprev, ubuf, uprev, w_in_bf, w_out_bf, stage, zbuf,
                  cur_ref, fill_ref, nfree_ref, ring_ref, semx, semo, sem, zsem,
                  *, alpha, tile, nt, t_pad, chunk, n_chunks, n_tiles, conf_w, sc_w_, conf_k, sc_k):
    k = pl.program_id(0)
    slot = k % 2
    d = x_hbm.shape[-1]
    groups = tile // SUBLANES
    a_hist = SUBLANES * (conf_k - 1)
    u_hist = SUBLANES * (sc_k - 1)

    def x_copies(kk, sl):
        b, j = kk // nt, kk % nt
        return [pltpu.make_async_copy(
            x_hbm.at[b, pl.ds(pl.multiple_of(j * tile + groups * q, SUBLANES), groups), :],
            xbuf.at[sl, :, q, :], semx.at[sl]) for q in range(SUBLANES)]

    def x1_copies(kk, sl):
        b, j = kk // nt, kk % nt
        return [pltpu.make_async_copy(
            x1buf.at[sl, :, q, :],
            x1_hbm.at[b, pl.ds(pl.multiple_of(j * tile + groups * q, SUBLANES), groups), :],
            semo.at[sl]) for q in range(SUBLANES)]

    @pl.when(k == 0)
    def _():
        for cp in x_copies(0, 0):
            cp.start()
        w_in_bf[...] = w_in_ref[...].astype(_BF16)
        w_out_bf[...] = w_out_ref[...].astype(_BF16)
        nfree_ref[0] = 0
        for g in range(N_GROUPS):
            cur_ref[g] = 0
            fill_ref[g] = chunk
        for c in range(info_ref.shape[0]):
            info_ref[c] = N_GROUPS - 1

    @pl.when(k + 1 < n_tiles)
    def _():
        for cp in x_copies(k + 1, 1 - slot):
            cp.start()

    @pl.when(k % nt == 0)
    def _():
        aprev[...] = jnp.zeros_like(aprev)
        uprev[...] = jnp.zeros_like(uprev)

    for cp in x_copies(k, slot):
        cp.wait()

    x0 = _layer_norm(xbuf[slot].reshape(tile, d), lng_ref[...], lnb_ref[...])
    h = _dot(x0.astype(_BF16), w_in_bf[...]) + b_in_ref[...]
    c0, c1, c2, c3 = conf_w, 2 * conf_w, 2 * conf_w + sc_w_, 2 * conf_w + 2 * sc_w_

    a = h[:, 0:c0] * _sigmoid(h[:, c0:c1])
    abuf[0:a_hist, :] = _history(a[tile - a_hist:, :], aprev[...])
    abuf[a_hist:, :] = a
    aprev[...] = a[tile - a_hist:, :]
    acc = jnp.broadcast_to(cb_ref[...], (tile, conf_w))
    for t in range(conf_k):
        acc = acc + cw_ref[t:t + 1, :] * abuf[SUBLANES * t:SUBLANES * t + tile, :]
    an = _layer_norm(acc, clg_ref[...], clb_ref[...])
    a2 = an * _sigmoid(an)

    u = h[:, c2:c3] * h[:, c3:]
    ubuf[0:u_hist, :] = _history(u[tile - u_hist:, :], uprev[...])
    ubuf[u_hist:, :] = u
    uprev[...] = u[tile - u_hist:, :]
    sc = jnp.broadcast_to(sb_ref[...], (tile, sc_w_))
    for t in range(sc_k):
        sc = sc + sw_ref[t:t + 1, :] * ubuf[SUBLANES * t:SUBLANES * t + tile, :]
    s = h[:, c1:c2] * sc

    mix = (_dot(a2.astype(_BF16), w_out_bf[0:conf_w, :])
           + _dot(s.astype(_BF16), w_out_bf[conf_w:, :]) + b_out_ref[...])
    x1 = _layer_norm(alpha * x0 + mix, l1g_ref[...], l1b_ref[...])

    @pl.when(k >= 2)
    def _():
        for cp in x1_copies(k - 2, slot):
            cp.wait()

    x1buf[slot] = x1.reshape(groups, SUBLANES, d)
    for cp in x1_copies(k, slot):
        cp.start()
    x1b = x1.astype(_BF16)

    logits = lax.dot_general(wr_ref[...], x1b, _NT, preferred_element_type=_F32) + br_ref[...]
    gidx, w = _route(logits)

    row8 = lax.broadcasted_iota(jnp.int32, (SUBLANES, tile), 0)
    onehot = jnp.where(gidx == row8, 1.0, 0.0)
    cnt = jnp.sum(onehot, axis=1, keepdims=True)
    sizes, starts = [], []
    start = jnp.int32(0)
    startv = jnp.zeros((SUBLANES, tile), jnp.int32)
    for g in range(N_GROUPS):
        n8 = (cnt[g, 0].astype(jnp.int32) + (SUBLANES - 1)) & (-SUBLANES)
        sizes.append(n8)
        starts.append(start)
        startv = jnp.where(row8 == g, start, startv)
        start = start + n8
    cum = _dot(onehot.astype(_BF16), tri_ref[...])
    pos = jnp.sum(onehot * (startv.astype(_F32) + cum), axis=0, keepdims=True).astype(jnp.int32)
    row_id = lax.broadcasted_iota(jnp.int32, (SUBLANES, tile), 0)
    halves = jnp.where(row_id == 0, pos >> POS_HALF_BITS,
                       jnp.where(row_id == 1, pos & ((1 << POS_HALF_BITS) - 1), 0)).astype(_F32)
    halves = _dot(halves.astype(_BF16), unperm_ref[...])
    pos_ref[0] = (float(1 << POS_HALF_BITS) * halves[0:1] + halves[1:2]).astype(jnp.int32)
    perm = jnp.where(lax.broadcasted_iota(jnp.int32, (t_pad, tile), 0) == pos, 1.0, 0.0).astype(_BF16)

    w_hi = w.astype(_BF16).astype(_F32)
    w_lo = (w - w_hi).astype(_BF16).astype(_F32)
    w_hl = jnp.where(row8 < EXPERTS_PER_GROUP, w_hi, pltpu.roll(w_lo, EXPERTS_PER_GROUP, 0))
    w_rows = jnp.concatenate([w_hl, jnp.zeros((LANES - SUBLANES, tile), _F32)], axis=0).astype(_BF16)

    def wait_slot(sl):
        for g in range(N_GROUPS):
            for piece in range(2):
                m = ring_ref[sl, 2 * g + piece]

                @pl.when(m > 0)
                def _():
                    _rows(stage.at[sl], 0, xs_ref, 0, m, sem.at[sl]).wait()

    @pl.when(k >= 2)
    def _():
        wait_slot(slot)

    stage[slot, :, 0:d] = _dot(perm, x1b)
    stage[slot, :, d:] = lax.dot_general(perm, w_rows, _NT, preferred_element_type=_F32)

    for g in range(N_GROUPS):
        n8, sg = sizes[g], starts[g]
        fill, cur, newc = fill_ref[g], cur_ref[g], nfree_ref[0]
        m1 = jnp.minimum(n8, chunk - fill)
        m2 = n8 - m1
        r1 = cur * chunk + fill
        r2 = newc * chunk
        ring_ref[slot, 2 * g] = m1
        ring_ref[slot, 2 * g + 1] = m2
        for f, v in enumerate((r1, m1, r2, m2, sg)):
            meta_ref[k, g * META_FIELDS + f] = v

        @pl.when(m1 > 0)
        def _():
            _rows(stage.at[slot], sg, xs_ref, r1, m1, sem.at[slot]).start()

        @pl.when(m2 > 0)
        def _():
            _rows(stage.at[slot], sg + m1, xs_ref, r2, m2, sem.at[slot]).start()
            info_ref[newc] = g
            cur_ref[g] = newc
            nfree_ref[0] = newc + 1

        fill_ref[g] = jnp.where(m2 > 0, m2, fill + m1)

    @pl.when(k == n_tiles - 1)
    def _():
        zbuf[...] = jnp.zeros_like(zbuf)
        nfree = nfree_ref[0]
        info_ref[n_chunks] = nfree
        tails = []
        for g in range(N_GROUPS):
            rem = chunk - fill_ref[g]
            tails.append((rem, _rows(zbuf, 0, xs_ref, cur_ref[g] * chunk + fill_ref[g], rem, zsem)))
        spare = [(c, pltpu.make_async_copy(zbuf, xs_ref.at[pl.ds(c * chunk, chunk)], zsem))
                 for c in range(n_tiles * tile // chunk, n_chunks)]
        for rem, cp in tails:
            pl.when(rem > 0)(cp.start)
        for c, cp in spare:
            pl.when(c >= nfree)(cp.start)

        slot_out = jnp.int32(0)
        for g in range(N_GROUPS):
            def place(c, nxt, g=g):
                hit = info_ref[c] == g

                @pl.when(hit)
                def _():
                    order_ref[nxt] = c

                return nxt + hit.astype(jnp.int32)

            slot_out = lax.fori_loop(0, n_chunks, place, slot_out)

        @pl.when(k >= 1)
        def _():
            wait_slot(1 - slot)
            for cp in x1_copies(k - 1, 1 - slot):
                cp.wait()

        wait_slot(slot)
        for cp in x1_copies(k, slot):
            cp.wait()
        for rem, cp in tails:
            pl.when(rem > 0)(cp.wait)
        for c, cp in spare:
            pl.when(c >= nfree)(cp.wait)


def _moe_kernel(order_ref, info_ref, xs_ref, wg_hbm, wu_hbm, wd_hbm, o_ref,
                wg_buf, wu_buf, wd_buf, slot_ref, wsem, *, d, d_expert, n_chunks):
    i = pl.program_id(0)
    c = order_ref[i]
    grp = info_ref[c]
    prev = info_ref[order_ref[jnp.maximum(i - 1, 0)]]

    def weight_copies(g, sl):
        lo = g * EXPERTS_PER_GROUP
        return [pltpu.make_async_copy(src.at[pl.ds(lo, EXPERTS_PER_GROUP)], dst.at[sl], wsem)
                for src, dst in ((wg_hbm, wg_buf), (wu_hbm, wu_buf), (wd_hbm, wd_buf))]

    @pl.when(i == 0)
    def _():
        slot_ref[0] = 0
        for cp in weight_copies(grp, 0):
            cp.start()

    @pl.when((i == 0) | (grp != prev))
    def _():
        @pl.when(i > 0)
        def _():
            slot_ref[0] = 1 - slot_ref[0]

        cur = slot_ref[0]
        for cp in weight_copies(grp, cur):
            cp.wait()

        def later_group(j, found):
            gj = info_ref[order_ref[j]]
            return jnp.where((found < 0) & (j > i) & (gj != grp), gj, found)

        nxt = lax.fori_loop(0, n_chunks, later_group, jnp.int32(-1))

        @pl.when(nxt >= 0)
        def _():
            for cp in weight_copies(nxt, 1 - cur):
                cp.start()

    used = c < info_ref[n_chunks]

    @pl.when(used)
    def _():
        xb = xs_ref[:, 0:d]
        ws = slot_ref[0]
        aux = xs_ref[:, d:]
        y = jnp.zeros(o_ref.shape, _F32)
        for e in range(EXPERTS_PER_GROUP):
            hg = _dot(xb, wg_buf[ws, e])
            hu = _dot(xb, wu_buf[ws, e])
            we = aux[:, e:e + 1] + aux[:, EXPERTS_PER_GROUP + e:EXPERTS_PER_GROUP + e + 1]
            hid = hg * _sigmoid(hg) * hu * we
            y = y + _dot(hid, wd_buf[ws, e])
        o_ref[...] = y

    @pl.when(jnp.logical_not(used))
    def _():
        o_ref[...] = jnp.zeros_like(o_ref)


def _ple_kernel(meta_ref, x1_ref, p_ref, pos_ref, ys_ref, wpg_ref, bpg_ref, wpp_ref, l2g_ref, l2b_ref,
                o_ref, ybuf, sem, *, alpha, tile, n_sub, t_pad, n_tiles):
    k = pl.program_id(0)
    slot = k % 2

    def pieces(kk, sl):
        out = []
        for g in range(N_GROUPS):
            r1, m1, r2, m2, sg = (meta_ref[kk, g * META_FIELDS + f] for f in range(META_FIELDS))
            out.append((m1, _rows(ys_ref, r1, ybuf.at[sl], sg, m1, sem.at[sl])))
            out.append((m2, _rows(ys_ref, r2, ybuf.at[sl], sg + m1, m2, sem.at[sl])))
        return out

    def fetch(kk, sl):
        for m, cp in pieces(kk, sl):
            pl.when(m > 0)(cp.start)

    @pl.when(k == 0)
    def _():
        ybuf[...] = jnp.zeros_like(ybuf)
        fetch(0, 0)

    @pl.when(k + 1 < n_tiles)
    def _():
        fetch(k + 1, 1 - slot)

    for m, cp in pieces(k, slot):
        pl.when(m > 0)(cp.wait)

    yb = ybuf[slot]
    pos = pos_ref[0]
    sub = tile // n_sub
    for blk in range(n_sub):
        rows = slice(blk * sub, (blk + 1) * sub)
        perm = jnp.where(lax.broadcasted_iota(jnp.int32, (t_pad, sub), 0) == pos[:, rows], 1.0, 0.0)
        y = lax.dot_general(perm, yb, _TN, preferred_element_type=_F32)
        r = alpha * x1_ref[rows, :] + y
        gate = _sigmoid(_dot(r, wpg_ref[...]) + bpg_ref[...])
        pp = _dot(p_ref[rows, :], wpp_ref[...])
        o_ref[rows, :] = _layer_norm(r + gate * pp, l2g_ref[...], l2b_ref[...])


def _full(shape):
    return pl.BlockSpec(shape, lambda *_: (0,) * len(shape))


def _resident(shape):
    return pl.BlockSpec(shape, lambda *_: (0,) * len(shape), pipeline_mode=pl.Buffered(1))


def _row(v):
    return v.reshape(1, -1).astype(_F32)


def kernel(x, p, ln_in_g, ln_in_b, w_in, b_in, conf_dw_w, conf_dw_b, conf_ln_g, conf_ln_b, sc_w, sc_b, w_out, b_out, ln1_g, ln1_b, w_rg, b_rg, w_re, b_re, w_gate, w_up, w_down, w_pg, b_pg, w_pp, ln2_g, ln2_b):
    depth = w_in.shape[0]
    assert depth == 1, "single-layer block"
    alpha = (2.0 * depth) ** 0.25
    bsz, seq, d = x.shape
    n = bsz * seq
    conf_k, conf_w = conf_dw_w.shape[1:]
    sc_k, sc_w_ = sc_w.shape[1:]
    d_in = w_in.shape[2]
    n_exp, _, d_expert = w_gate.shape[1:]
    assert n_exp == N_GROUPS * EXPERTS_PER_GROUP
    assert d_in == 2 * conf_w + 3 * sc_w_ and w_out.shape[1] == conf_w + sc_w_
    d_ple = p.shape[-1]

    tile = chunk = TILE
    assert seq % tile == 0
    nt = seq // tile
    n_tiles = bsz * nt
    t_pad = tile + N_GROUPS * SUBLANES
    n_chunks = -(-(n + (SUBLANES - 1) * N_GROUPS * n_tiles) // chunk) + N_GROUPS
    d_pay = d + LANES

    n_logits = N_GROUPS + n_exp
    wr = jnp.pad(jnp.concatenate([w_rg[0], w_re[0]], axis=1).T, ((0, ROUTER_ROWS - n_logits), (0, 0)))
    br = jnp.pad(jnp.concatenate([b_rg[0], b_re[0]]), (0, ROUTER_ROWS - n_logits))
    tri = (jnp.arange(tile)[:, None] < jnp.arange(tile)[None, :]).astype(_BF16)
    tok_of_row = (tile // SUBLANES) * (jnp.arange(tile) % SUBLANES) + jnp.arange(tile) // SUBLANES
    unperm = (tok_of_row[:, None] == jnp.arange(tile)[None, :]).astype(_BF16)

    mixer = pl.pallas_call(
        functools.partial(_mixer_kernel, alpha=alpha, tile=tile, nt=nt, t_pad=t_pad, chunk=chunk,
                          n_chunks=n_chunks, n_tiles=n_tiles, conf_w=conf_w, sc_w_=sc_w_,
                          conf_k=conf_k, sc_k=sc_k),
        grid=(n_tiles,),
        in_specs=[
            pl.BlockSpec(memory_space=pl.ANY),
            _full((1, d)), _full((1, d)),
            _resident((d, d_in)), _full((1, d_in)),
            _full((conf_k, conf_w)), _full((1, conf_w)), _full((1, conf_w)), _full((1, conf_w)),
            _full((sc_k, sc_w_)), _full((1, sc_w_)),
            _resident((conf_w + sc_w_, d)), _full((1, d)),
            _full((1, d)), _full((1, d)),
            _full((ROUTER_ROWS, d)), _full((ROUTER_ROWS, 1)),
            _resident((tile, tile)), _resident((tile, tile)),
        ],
        out_specs=[
            pl.BlockSpec(memory_space=pl.ANY),
            pl.BlockSpec((1, 1, tile), lambda k: (k, 0, 0)),
            pl.BlockSpec(memory_space=pl.ANY),
            pl.BlockSpec(memory_space=pltpu.SMEM),
            pl.BlockSpec(memory_space=pltpu.SMEM),
            pl.BlockSpec(memory_space=pltpu.SMEM),
        ],
        out_shape=[
            jax.ShapeDtypeStruct((bsz, seq, d), _F32),
            jax.ShapeDtypeStruct((n_tiles, 1, tile), jnp.int32),
            jax.ShapeDtypeStruct((n_chunks * chunk, d_pay), _F32),
            jax.ShapeDtypeStruct((n_tiles, N_GROUPS * META_FIELDS), jnp.int32),
            jax.ShapeDtypeStruct((n_chunks + 1,), jnp.int32),
            jax.ShapeDtypeStruct((n_chunks,), jnp.int32),
        ],
        scratch_shapes=[
            pltpu.VMEM((2, tile // SUBLANES, SUBLANES, d), _F32),
            pltpu.VMEM((2, tile // SUBLANES, SUBLANES, d), _F32),
            pltpu.VMEM((SUBLANES * (conf_k - 1) + tile, conf_w), _F32),
            pltpu.VMEM((SUBLANES * (conf_k - 1), conf_w), _F32),
            pltpu.VMEM((SUBLANES * (sc_k - 1) + tile, sc_w_), _F32),
            pltpu.VMEM((SUBLANES * (sc_k - 1), sc_w_), _F32),
            pltpu.VMEM((d, d_in), _BF16),
            pltpu.VMEM((conf_w + sc_w_, d), _BF16),
            pltpu.VMEM((2, t_pad, d_pay), _F32),
            pltpu.VMEM((chunk, d_pay), _F32),
            pltpu.SMEM((N_GROUPS,), jnp.int32),
            pltpu.SMEM((N_GROUPS,), jnp.int32),
            pltpu.SMEM((1,), jnp.int32),
            pltpu.SMEM((2, 2 * N_GROUPS), jnp.int32),
            pltpu.SemaphoreType.DMA((2,)),
            pltpu.SemaphoreType.DMA((2,)),
            pltpu.SemaphoreType.DMA((2,)),
            pltpu.SemaphoreType.DMA(()),
        ],
        compiler_params=pltpu.CompilerParams(
            dimension_semantics=("arbitrary",), vmem_limit_bytes=VMEM_LIMIT_BYTES),
        name="mixer",
    )
    x1, pos, xs, meta, info, order = mixer(
        x, _row(ln_in_g), _row(ln_in_b), w_in[0], _row(b_in[0]),
        conf_dw_w[0], _row(conf_dw_b[0]), _row(conf_ln_g[0]), _row(conf_ln_b[0]),
        sc_w[0], _row(sc_b[0]), w_out[0], _row(b_out[0]),
        _row(ln1_g[0]), _row(ln1_b[0]), wr.astype(_BF16), br.reshape(ROUTER_ROWS, 1), tri, unperm)


    moe = pl.pallas_call(
        functools.partial(_moe_kernel, d=d, d_expert=d_expert, n_chunks=n_chunks),
        grid_spec=pltpu.PrefetchScalarGridSpec(
            num_scalar_prefetch=2,
            grid=(n_chunks,),
            in_specs=[
                pl.BlockSpec((chunk, d_pay), lambda i, order, info: (order[i], 0)),
                pl.BlockSpec(memory_space=pl.ANY),
                pl.BlockSpec(memory_space=pl.ANY),
                pl.BlockSpec(memory_space=pl.ANY),
            ],
            out_specs=pl.BlockSpec((chunk, d), lambda i, order, info: (order[i], 0)),
            scratch_shapes=[
                pltpu.VMEM((2, EXPERTS_PER_GROUP, d, d_expert), _F32),
                pltpu.VMEM((2, EXPERTS_PER_GROUP, d, d_expert), _F32),
                pltpu.VMEM((2, EXPERTS_PER_GROUP, d_expert, d), _F32),
                pltpu.SMEM((1,), jnp.int32),
                pltpu.SemaphoreType.DMA(()),
            ],
        ),
        out_shape=jax.ShapeDtypeStruct((n_chunks * chunk, d), _F32),
        compiler_params=pltpu.CompilerParams(
            dimension_semantics=("arbitrary",), vmem_limit_bytes=VMEM_LIMIT_BYTES),
        name="moe",
    )
    ys = moe(order, info, xs, w_gate[0], w_up[0], w_down[0])

    ple = pl.pallas_call(
        functools.partial(_ple_kernel, alpha=alpha, tile=tile, n_sub=N_SUB, t_pad=t_pad, n_tiles=n_tiles),
        grid_spec=pltpu.PrefetchScalarGridSpec(
            num_scalar_prefetch=1,
            grid=(n_tiles,),
            in_specs=[
                pl.BlockSpec((tile, d), lambda k, meta: (k, 0)),
                pl.BlockSpec((tile, d_ple), lambda k, meta: (k, 0)),
                pl.BlockSpec((1, 1, tile), lambda k, meta: (k, 0, 0)),
                pl.BlockSpec(memory_space=pl.ANY),
                _resident((d, d)), _full((1, d)), _resident((d_ple, d)), _full((1, d)), _full((1, d)),
            ],
            out_specs=pl.BlockSpec((tile, d), lambda k, meta: (k, 0)),
            scratch_shapes=[
                pltpu.VMEM((2, t_pad, d), _F32),
                pltpu.SemaphoreType.DMA((2,)),
            ],
        ),
        out_shape=jax.ShapeDtypeStruct((n, d), _F32),
        compiler_params=pltpu.CompilerParams(
            dimension_semantics=("arbitrary",), vmem_limit_bytes=VMEM_LIMIT_BYTES),
        name="ple",
    )
    out = ple(meta, x1.reshape(n, d), p[0].reshape(n, d_ple), pos, ys,
              w_pg[0], _row(b_pg[0]), w_pp[0], _row(ln2_g[0]), _row(ln2_b[0]))
    return out.reshape(bsz, seq, d)
```

```python
import functools

import jax
import jax.numpy as jnp
from jax import lax
from jax.experimental import pallas as pl
from jax.experimental.pallas import tpu as pltpu

LN_EPS = 1e-5
N_GROUPS = 4
EXPERTS_PER_GROUP = 4
SUBLANES = 8
LANES = 128
ROUTER_ROWS = 32
TILE = 512
PLE_TILES = 2
N_SUB = 2
VMEM_LIMIT_BYTES = 56 * 1024 * 1024
NEG_BIG = -1e30
META_FIELDS = 5

_F32 = jnp.float32
_BF16 = jnp.bfloat16
_NT = (((1,), (1,)), ((), ()))
_TN = (((0,), (0,)), ((), ()))


def _layer_norm(x, g, b):
    mu = jnp.mean(x, axis=-1, keepdims=True)
    xc = x - mu
    var = jnp.mean(xc * xc, axis=-1, keepdims=True)
    return xc * lax.rsqrt(var + LN_EPS) * g + b


def _sigmoid(x):
    return 0.5 * jnp.tanh(0.5 * x) + 0.5


def _dot(a, b):
    return jnp.dot(a, b, preferred_element_type=_F32)


def _aligned(v):
    return v if isinstance(v, int) else pl.multiple_of(v, SUBLANES)


def _rows(src, s0, dst, d0, m, sem):
    return pltpu.make_async_copy(src.at[pl.ds(_aligned(s0), _aligned(m))],
                                 dst.at[pl.ds(_aligned(d0), _aligned(m))], sem)


def _route(logits):
    row = lax.broadcasted_iota(jnp.int32, (SUBLANES, logits.shape[1]), 0)
    real = row < N_GROUPS
    gl = jnp.where(real, logits[0:SUBLANES], NEG_BIG)
    gm = jnp.max(gl, axis=0, keepdims=True)
    ge = jnp.exp(gl - gm)
    gp = ge / jnp.sum(ge, axis=0, keepdims=True)
    gp_top = jnp.max(gp, axis=0, keepdims=True)
    gidx = jnp.min(jnp.where(gp == gp_top, row, SUBLANES), axis=0, keepdims=True)

    el = jnp.zeros_like(gl)
    for g in range(N_GROUPS):
        lo = N_GROUPS + g * EXPERTS_PER_GROUP
        blk = logits[lo - lo % SUBLANES:lo - lo % SUBLANES + SUBLANES]
        if lo % SUBLANES:
            blk = pltpu.roll(blk, SUBLANES - lo % SUBLANES, 0)
        el = jnp.where(gidx == g, blk, el)
    el = jnp.where(real, el, NEG_BIG)
    em = jnp.max(el, axis=0, keepdims=True)
    ee = jnp.exp(el - em)
    ep = ee / jnp.sum(ee, axis=0, keepdims=True)
    ep = jnp.where(real, ep, -1.0)
    p1 = jnp.max(ep, axis=0, keepdims=True)
    i1 = jnp.min(jnp.where(ep == p1, row, SUBLANES), axis=0, keepdims=True)
    ep2 = jnp.where(row == i1, -1.0, ep)
    p2 = jnp.max(ep2, axis=0, keepdims=True)
    i2 = jnp.min(jnp.where(ep2 == p2, row, SUBLANES), axis=0, keepdims=True)
    denom = p1 + p2
    w = jnp.where(row == i1, p1 / denom, jnp.where(row == i2, p2 / denom, 0.0))
    return gidx, w * gp_top


def _mixer_kernel(x_ref, lng_ref, lnb_ref, w_in_ref, b_in_ref, cw_ref, cb_ref, clg_ref, clb_ref,
                  sw_ref, sb_ref, w_out_ref, b_out_ref, l1g_ref, l1b_ref, wr_ref, br_ref, tri_ref,
                  x1_ref, pos_ref, xs_ref, meta_ref, info_ref, order_ref,
                  abuf, ashift, ubuf, w_in_bf, w_out_bf, stage, zbuf, cur_ref, fill_ref, nfree_ref,
                  ring_ref, sem, zsem, *, alpha, tile, t_pad, chunk, n_chunks, n_tiles, conf_w, sc_w_,
                  conf_k, sc_k, a_halo, u_halo):
    j = pl.program_id(1)
    k = pl.program_id(0) * pl.num_programs(1) + j
    slot = k % 2
    d = x1_ref.shape[-1]

    @pl.when(k == 0)
    def _():
        w_in_bf[...] = w_in_ref[...].astype(_BF16)
        w_out_bf[...] = w_out_ref[...].astype(_BF16)
        nfree_ref[0] = 0
        for g in range(N_GROUPS):
            cur_ref[g] = 0
            fill_ref[g] = chunk
        for c in range(info_ref.shape[0]):
            info_ref[c] = N_GROUPS - 1

    @pl.when(j == 0)
    def _():
        abuf[0:a_halo, :] = jnp.zeros((a_halo, conf_w), _F32)
        ubuf[0:u_halo, :] = jnp.zeros((u_halo, sc_w_), _F32)

    x0 = _layer_norm(x_ref[0], lng_ref[...], lnb_ref[...])
    h = _dot(x0.astype(_BF16), w_in_bf[...]) + b_in_ref[...]
    c0, c1, c2, c3 = conf_w, 2 * conf_w, 2 * conf_w + sc_w_, 2 * conf_w + 2 * sc_w_

    a = h[:, 0:c0] * _sigmoid(h[:, c0:c1])
    abuf[a_halo:a_halo + tile, :] = a
    for i in range(1, SUBLANES):
        ashift[i - 1] = abuf[i:i + tile + a_halo - SUBLANES, :]
    acc = jnp.broadcast_to(cb_ref[...], (tile, conf_w))
    for t in range(conf_k):
        q, i = divmod(a_halo - (conf_k - 1) + t, SUBLANES)
        if i == 0:
            src = abuf[SUBLANES * q:SUBLANES * q + tile, :]
        else:
            src = ashift[i - 1, SUBLANES * q:SUBLANES * q + tile, :]
        acc = acc + cw_ref[t:t + 1, :] * src
    abuf[0:a_halo, :] = abuf[tile:tile + a_halo, :]
    an = _layer_norm(acc, clg_ref[...], clb_ref[...])
    a2 = an * _sigmoid(an)

    u = h[:, c2:c3] * h[:, c3:]
    ubuf[u_halo:u_halo + tile, :] = u
    sc = jnp.broadcast_to(sb_ref[...], (tile, sc_w_))
    for t in range(sc_k):
        off = u_halo - (sc_k - 1) + t
        sc = sc + sw_ref[t:t + 1, :] * ubuf[off:off + tile, :]
    ubuf[0:u_halo, :] = ubuf[tile:tile + u_halo, :]
    s = h[:, c1:c2] * sc

    mix = (_dot(a2.astype(_BF16), w_out_bf[0:conf_w, :])
           + _dot(s.astype(_BF16), w_out_bf[conf_w:, :]) + b_out_ref[...])
    x1 = _layer_norm(alpha * x0 + mix, l1g_ref[...], l1b_ref[...])
    x1_ref[0] = x1
    x1b = x1.astype(_BF16)

    logits = lax.dot_general(wr_ref[...], x1b, _NT, preferred_element_type=_F32) + br_ref[...]
    gidx, w = _route(logits)

    row8 = lax.broadcasted_iota(jnp.int32, (SUBLANES, tile), 0)
    onehot = jnp.where(gidx == row8, 1.0, 0.0)
    cnt = jnp.sum(onehot, axis=1, keepdims=True)
    sizes, starts = [], []
    start = jnp.int32(0)
    startv = jnp.zeros((SUBLANES, tile), jnp.int32)
    for g in range(N_GROUPS):
        n8 = (cnt[g, 0].astype(jnp.int32) + (SUBLANES - 1)) & (-SUBLANES)
        sizes.append(n8)
        starts.append(start)
        startv = jnp.where(row8 == g, start, startv)
        start = start + n8
    cum = _dot(onehot.astype(_BF16), tri_ref[...])
    pos = jnp.sum(onehot * (startv.astype(_F32) + cum), axis=0, keepdims=True).astype(jnp.int32)
    pos_ref[0] = pos
    perm = jnp.where(lax.broadcasted_iota(jnp.int32, (t_pad, tile), 0) == pos, 1.0, 0.0).astype(_BF16)

    w_hi = w.astype(_BF16).astype(_F32)
    w_lo = (w - w_hi).astype(_BF16).astype(_F32)
    w_hl = jnp.where(row8 < EXPERTS_PER_GROUP, w_hi, pltpu.roll(w_lo, EXPERTS_PER_GROUP, 0))
    w_rows = jnp.concatenate([w_hl, jnp.zeros((LANES - SUBLANES, tile), _F32)], axis=0).astype(_BF16)

    def wait_slot(sl):
        for g in range(N_GROUPS):
            for piece in range(2):
                m = ring_ref[sl, 2 * g + piece]

                @pl.when(m > 0)
                def _():
                    _rows(stage.at[sl], 0, xs_ref, 0, m, sem.at[sl]).wait()

    @pl.when(k >= 2)
    def _():
        wait_slot(slot)

    stage[slot, :, 0:d] = _dot(perm, x1b)
    stage[slot, :, d:] = lax.dot_general(perm, w_rows, _NT, preferred_element_type=_F32)

    for g in range(N_GROUPS):
        n8, sg = sizes[g], starts[g]
        fill, cur, newc = fill_ref[g], cur_ref[g], nfree_ref[0]
        m1 = jnp.minimum(n8, chunk - fill)
        m2 = n8 - m1
        r1 = cur * chunk + fill
        r2 = newc * chunk
        ring_ref[slot, 2 * g] = m1
        ring_ref[slot, 2 * g + 1] = m2
        for f, v in enumerate((r1, m1, r2, m2, sg)):
            meta_ref[k, g * META_FIELDS + f] = v

        @pl.when(m1 > 0)
        def _():
            _rows(stage.at[slot], sg, xs_ref, r1, m1, sem.at[slot]).start()

        @pl.when(m2 > 0)
        def _():
            _rows(stage.at[slot], sg + m1, xs_ref, r2, m2, sem.at[slot]).start()
            info_ref[newc] = g
            cur_ref[g] = newc
            nfree_ref[0] = newc + 1

        fill_ref[g] = jnp.where(m2 > 0, m2, fill + m1)

    @pl.when(k == n_tiles - 1)
    def _():
        zbuf[...] = jnp.zeros_like(zbuf)
        nfree = nfree_ref[0]
        info_ref[n_chunks] = nfree
        tails = []
        for g in range(N_GROUPS):
            rem = chunk - fill_ref[g]
            tails.append((rem, _rows(zbuf, 0, xs_ref, cur_ref[g] * chunk + fill_ref[g], rem, zsem)))
        spare = [(c, pltpu.make_async_copy(zbuf, xs_ref.at[pl.ds(c * chunk, chunk)], zsem))
                 for c in range(n_tiles * tile // chunk, n_chunks)]
        for rem, cp in tails:
            pl.when(rem > 0)(cp.start)
        for c, cp in spare:
            pl.when(c >= nfree)(cp.start)

        slot_out = jnp.int32(0)
        for g in range(N_GROUPS):
            def place(c, nxt, g=g):
                hit = info_ref[c] == g

                @pl.when(hit)
                def _():
                    order_ref[nxt] = c

                return nxt + hit.astype(jnp.int32)

            slot_out = lax.fori_loop(0, n_chunks, place, slot_out)

        @pl.when(k >= 1)
        def _():
            wait_slot(1 - slot)

        wait_slot(slot)
        for rem, cp in tails:
            pl.when(rem > 0)(cp.wait)
        for c, cp in spare:
            pl.when(c >= nfree)(cp.wait)


def _moe_kernel(order_ref, info_ref, xs_ref, wg_hbm, wu_hbm, wd_hbm, o_ref,
                wg_buf, wu_buf, wd_buf, slot_ref, wsem, *, d, d_expert, n_chunks):
    i = pl.program_id(0)
    c = order_ref[i]
    grp = info_ref[c]
    prev = info_ref[order_ref[jnp.maximum(i - 1, 0)]]

    def weight_copies(g, sl):
        lo = g * EXPERTS_PER_GROUP
        return [pltpu.make_async_copy(src.at[pl.ds(lo, EXPERTS_PER_GROUP)], dst.at[sl], wsem)
                for src, dst in ((wg_hbm, wg_buf), (wu_hbm, wu_buf), (wd_hbm, wd_buf))]

    @pl.when(i == 0)
    def _():
        slot_ref[0] = 0
        for cp in weight_copies(grp, 0):
            cp.start()

    @pl.when((i == 0) | (grp != prev))
    def _():
        @pl.when(i > 0)
        def _():
            slot_ref[0] = 1 - slot_ref[0]

        cur = slot_ref[0]
        for cp in weight_copies(grp, cur):
            cp.wait()

        def later_group(j, found):
            gj = info_ref[order_ref[j]]
            return jnp.where((found < 0) & (j > i) & (gj != grp), gj, found)

        nxt = lax.fori_loop(0, n_chunks, later_group, jnp.int32(-1))

        @pl.when(nxt >= 0)
        def _():
            for cp in weight_copies(nxt, 1 - cur):
                cp.start()

    used = c < info_ref[n_chunks]

    @pl.when(used)
    def _():
        xb = xs_ref[:, 0:d]
        ws = slot_ref[0]
        aux = xs_ref[:, d:]
        y = jnp.zeros(o_ref.shape, _F32)
        for e in range(EXPERTS_PER_GROUP):
            hg = _dot(xb, wg_buf[ws, e])
            hu = _dot(xb, wu_buf[ws, e])
            we = aux[:, e:e + 1] + aux[:, EXPERTS_PER_GROUP + e:EXPERTS_PER_GROUP + e + 1]
            hid = hg * _sigmoid(hg) * hu * we
            y = y + _dot(hid, wd_buf[ws, e])
        o_ref[...] = y

    @pl.when(jnp.logical_not(used))
    def _():
        o_ref[...] = jnp.zeros_like(o_ref)


def _ple_kernel(meta_ref, x1_ref, p_ref, pos_ref, ys_ref, wpg_ref, bpg_ref, wpp_ref, l2g_ref, l2b_ref,
                o_ref, ybuf, sem, *, alpha, tile, n_sub, tiles_per_step, t_pad, n_steps):
    k = pl.program_id(0)
    slot = k % 2

    def pieces(step, sl):
        out = []
        for ti in range(tiles_per_step):
            kk = step * tiles_per_step + ti
            for g in range(N_GROUPS):
                r1, m1, r2, m2, sg = (meta_ref[kk, g * META_FIELDS + f] for f in range(META_FIELDS))
                out.append((m1, _rows(ys_ref, r1, ybuf.at[sl, ti], sg, m1, sem.at[sl])))
                out.append((m2, _rows(ys_ref, r2, ybuf.at[sl, ti], sg + m1, m2, sem.at[sl])))
        return out

    def fetch(kk, sl):
        for m, cp in pieces(kk, sl):
            pl.when(m > 0)(cp.start)

    @pl.when(k == 0)
    def _():
        ybuf[...] = jnp.zeros_like(ybuf)
        fetch(0, 0)

    @pl.when(k + 1 < n_steps)
    def _():
        fetch(k + 1, 1 - slot)

    for m, cp in pieces(k, slot):
        pl.when(m > 0)(cp.wait)

    sub = tile // n_sub
    for ti in range(tiles_per_step):
        yb = ybuf[slot, ti]
        pos = pos_ref[ti]
        for blk in range(n_sub):
            cols = slice(blk * sub, (blk + 1) * sub)
            rows = slice(ti * tile + blk * sub, ti * tile + (blk + 1) * sub)
            perm = jnp.where(lax.broadcasted_iota(jnp.int32, (t_pad, sub), 0) == pos[:, cols], 1.0, 0.0)
            y = lax.dot_general(perm, yb, _TN, preferred_element_type=_F32)
            r = alpha * x1_ref[rows, :] + y
            gate = _sigmoid(_dot(r, wpg_ref[...]) + bpg_ref[...])
            pp = _dot(p_ref[rows, :], wpp_ref[...])
            o_ref[rows, :] = _layer_norm(r + gate * pp, l2g_ref[...], l2b_ref[...])


def _full(shape):
    return pl.BlockSpec(shape, lambda *_: (0,) * len(shape))


def _resident(shape):
    return pl.BlockSpec(shape, lambda *_: (0,) * len(shape), pipeline_mode=pl.Buffered(1))


def _row(v):
    return v.reshape(1, -1).astype(_F32)


def kernel(x, p, ln_in_g, ln_in_b, w_in, b_in, conf_dw_w, conf_dw_b, conf_ln_g, conf_ln_b, sc_w, sc_b, w_out, b_out, ln1_g, ln1_b, w_rg, b_rg, w_re, b_re, w_gate, w_up, w_down, w_pg, b_pg, w_pp, ln2_g, ln2_b):
    depth = w_in.shape[0]
    assert depth == 1, "single-layer block"
    alpha = (2.0 * depth) ** 0.25
    bsz, seq, d = x.shape
    n = bsz * seq
    conf_k, conf_w = conf_dw_w.shape[1:]
    sc_k, sc_w_ = sc_w.shape[1:]
    d_in = w_in.shape[2]
    n_exp, _, d_expert = w_gate.shape[1:]
    assert n_exp == N_GROUPS * EXPERTS_PER_GROUP
    assert d_in == 2 * conf_w + 3 * sc_w_ and w_out.shape[1] == conf_w + sc_w_
    d_ple = p.shape[-1]

    tile = chunk = TILE
    assert seq % tile == 0 and (bsz * (seq // tile)) % PLE_TILES == 0
    nt = seq // tile
    n_tiles = bsz * nt
    a_halo = -(-(conf_k - 1) // SUBLANES) * SUBLANES
    u_halo = -(-(sc_k - 1) // SUBLANES) * SUBLANES
    t_pad = tile + N_GROUPS * SUBLANES
    n_chunks = -(-(n + (SUBLANES - 1) * N_GROUPS * n_tiles) // chunk) + N_GROUPS
    d_pay = d + LANES

    n_logits = N_GROUPS + n_exp
    wr = jnp.pad(jnp.concatenate([w_rg[0], w_re[0]], axis=1).T, ((0, ROUTER_ROWS - n_logits), (0, 0)))
    br = jnp.pad(jnp.concatenate([b_rg[0], b_re[0]]), (0, ROUTER_ROWS - n_logits))
    tri = (jnp.arange(tile)[:, None] < jnp.arange(tile)[None, :]).astype(_BF16)

    mixer = pl.pallas_call(
        functools.partial(_mixer_kernel, alpha=alpha, tile=tile, t_pad=t_pad, chunk=chunk,
                          n_chunks=n_chunks, n_tiles=n_tiles, conf_w=conf_w, sc_w_=sc_w_,
                          conf_k=conf_k, sc_k=sc_k, a_halo=a_halo, u_halo=u_halo),
        grid=(bsz, nt),
        in_specs=[
            pl.BlockSpec((1, tile, d), lambda b, j: (b, j, 0)),
            _full((1, d)), _full((1, d)),
            _resident((d, d_in)), _full((1, d_in)),
            _full((conf_k, conf_w)), _full((1, conf_w)), _full((1, conf_w)), _full((1, conf_w)),
            _full((sc_k, sc_w_)), _full((1, sc_w_)),
            _resident((conf_w + sc_w_, d)), _full((1, d)),
            _full((1, d)), _full((1, d)),
            _full((ROUTER_ROWS, d)), _full((ROUTER_ROWS, 1)),
            _resident((tile, tile)),
        ],
        out_specs=[
            pl.BlockSpec((1, tile, d), lambda b, j: (b, j, 0)),
            pl.BlockSpec((1, 1, tile), lambda b, j: (b * nt + j, 0, 0)),
            pl.BlockSpec(memory_space=pl.ANY),
            pl.BlockSpec(memory_space=pltpu.SMEM),
            pl.BlockSpec(memory_space=pltpu.SMEM),
            pl.BlockSpec(memory_space=pltpu.SMEM),
        ],
        out_shape=[
            jax.ShapeDtypeStruct((bsz, seq, d), _F32),
            jax.ShapeDtypeStruct((n_tiles, 1, tile), jnp.int32),
            jax.ShapeDtypeStruct((n_chunks * chunk, d_pay), _F32),
            jax.ShapeDtypeStruct((n_tiles, N_GROUPS * META_FIELDS), jnp.int32),
            jax.ShapeDtypeStruct((n_chunks + 1,), jnp.int32),
            jax.ShapeDtypeStruct((n_chunks,), jnp.int32),
        ],
        scratch_shapes=[
            pltpu.VMEM((a_halo + tile, conf_w), _F32),
            pltpu.VMEM((SUBLANES - 1, a_halo + tile - SUBLANES, conf_w), _F32),
            pltpu.VMEM((u_halo + tile, sc_w_), _F32),
            pltpu.VMEM((d, d_in), _BF16),
            pltpu.VMEM((conf_w + sc_w_, d), _BF16),
            pltpu.VMEM((2, t_pad, d_pay), _F32),
            pltpu.VMEM((chunk, d_pay), _F32),
            pltpu.SMEM((N_GROUPS,), jnp.int32),
            pltpu.SMEM((N_GROUPS,), jnp.int32),
            pltpu.SMEM((1,), jnp.int32),
            pltpu.SMEM((2, 2 * N_GROUPS), jnp.int32),
            pltpu.SemaphoreType.DMA((2,)),
            pltpu.SemaphoreType.DMA(()),
        ],
        compiler_params=pltpu.CompilerParams(
            dimension_semantics=("arbitrary", "arbitrary"), vmem_limit_bytes=VMEM_LIMIT_BYTES),
        name="mixer",
    )
    x1, pos, xs, meta, info, order = mixer(
        x, _row(ln_in_g), _row(ln_in_b), w_in[0], _row(b_in[0]),
        conf_dw_w[0], _row(conf_dw_b[0]), _row(conf_ln_g[0]), _row(conf_ln_b[0]),
        sc_w[0], _row(sc_b[0]), w_out[0], _row(b_out[0]),
        _row(ln1_g[0]), _row(ln1_b[0]), wr.astype(_BF16), br.reshape(ROUTER_ROWS, 1), tri)

    moe = pl.pallas_call(
        functools.partial(_moe_kernel, d=d, d_expert=d_expert, n_chunks=n_chunks),
        grid_spec=pltpu.PrefetchScalarGridSpec(
            num_scalar_prefetch=2,
            grid=(n_chunks,),
            in_specs=[
                pl.BlockSpec((chunk, d_pay), lambda i, order, info: (order[i], 0)),
                pl.BlockSpec(memory_space=pl.ANY),
                pl.BlockSpec(memory_space=pl.ANY),
                pl.BlockSpec(memory_space=pl.ANY),
            ],
            out_specs=pl.BlockSpec((chunk, d), lambda i, order, info: (order[i], 0)),
            scratch_shapes=[
                pltpu.VMEM((2, EXPERTS_PER_GROUP, d, d_expert), _F32),
                pltpu.VMEM((2, EXPERTS_PER_GROUP, d, d_expert), _F32),
                pltpu.VMEM((2, EXPERTS_PER_GROUP, d_expert, d), _F32),
                pltpu.SMEM((1,), jnp.int32),
                pltpu.SemaphoreType.DMA(()),
            ],
        ),
        out_shape=jax.ShapeDtypeStruct((n_chunks * chunk, d), _F32),
        compiler_params=pltpu.CompilerParams(
            dimension_semantics=("arbitrary",), vmem_limit_bytes=VMEM_LIMIT_BYTES),
        name="moe",
    )
    ys = moe(order, info, xs, w_gate[0], w_up[0], w_down[0])

    ple = pl.pallas_call(
        functools.partial(_ple_kernel, alpha=alpha, tile=tile, n_sub=N_SUB, tiles_per_step=PLE_TILES,
                          t_pad=t_pad, n_steps=n_tiles // PLE_TILES),
        grid_spec=pltpu.PrefetchScalarGridSpec(
            num_scalar_prefetch=1,
            grid=(n_tiles // PLE_TILES,),
            in_specs=[
                pl.BlockSpec((PLE_TILES * tile, d), lambda k, meta: (k, 0)),
                pl.BlockSpec((PLE_TILES * tile, d_ple), lambda k, meta: (k, 0)),
                pl.BlockSpec((PLE_TILES, 1, tile), lambda k, meta: (k, 0, 0)),
                pl.BlockSpec(memory_space=pl.ANY),
                _resident((d, d)), _full((1, d)), _resident((d_ple, d)), _full((1, d)), _full((1, d)),
            ],
            out_specs=pl.BlockSpec((PLE_TILES * tile, d), lambda k, meta: (k, 0)),
            scratch_shapes=[
                pltpu.VMEM((2, PLE_TILES, t_pad, d), _F32),
                pltpu.SemaphoreType.DMA((2,)),
            ],
        ),
        out_shape=jax.ShapeDtypeStruct((n, d), _F32),
        compiler_params=pltpu.CompilerParams(
            dimension_semantics=("arbitrary",), vmem_limit_bytes=VMEM_LIMIT_BYTES),
        name="ple",
    )
    out = ple(meta, x1.reshape(n, d), p[0].reshape(n, d_ple), pos, ys,
              w_pg[0], _row(b_pg[0]), w_pp[0], _row(ln2_g[0]), _row(ln2_b[0]))
    return out.reshape(bsz, seq, d)
```

```python
import functools

import jax
import jax.numpy as jnp
from jax import lax
from jax.experimental import pallas as pl
from jax.experimental.pallas import tpu as pltpu

LN_EPS = 1e-5
N_GROUPS = 4
EXPERTS_PER_GROUP = 4
SUBLANES = 8
LANES = 128
ROUTER_ROWS = 32
TILE = 512
MOE_CHUNKS = 2
PLE_TILES = 2
N_SUB = 2
VMEM_LIMIT_BYTES = 56 * 1024 * 1024
NEG_BIG = -1e30
META_FIELDS = 5

_F32 = jnp.float32
_BF16 = jnp.bfloat16
_NT = (((1,), (1,)), ((), ()))
_TN = (((0,), (0,)), ((), ()))


def _layer_norm(x, g, b):
    mu = jnp.mean(x, axis=-1, keepdims=True)
    xc = x - mu
    var = jnp.mean(xc * xc, axis=-1, keepdims=True)
    return xc * lax.rsqrt(var + LN_EPS) * g + b


def _sigmoid(x):
    return 0.5 * jnp.tanh(0.5 * x) + 0.5


def _dot(a, b):
    return jnp.dot(a, b, preferred_element_type=_F32)


def _aligned(v):
    return v if isinstance(v, int) else pl.multiple_of(v, SUBLANES)


def _rows(src, s0, dst, d0, m, sem):
    return pltpu.make_async_copy(src.at[pl.ds(_aligned(s0), _aligned(m))],
                                 dst.at[pl.ds(_aligned(d0), _aligned(m))], sem)


def _route(logits):
    row = lax.broadcasted_iota(jnp.int32, (SUBLANES, logits.shape[1]), 0)
    real = row < N_GROUPS
    gl = jnp.where(real, logits[0:SUBLANES], NEG_BIG)
    gm = jnp.max(gl, axis=0, keepdims=True)
    ge = jnp.exp(gl - gm)
    gp = ge / jnp.sum(ge, axis=0, keepdims=True)
    gp_top = jnp.max(gp, axis=0, keepdims=True)
    gidx = jnp.min(jnp.where(gp == gp_top, row, SUBLANES), axis=0, keepdims=True)

    el = jnp.zeros_like(gl)
    for g in range(N_GROUPS):
        lo = N_GROUPS + g * EXPERTS_PER_GROUP
        blk = logits[lo - lo % SUBLANES:lo - lo % SUBLANES + SUBLANES]
        if lo % SUBLANES:
            blk = pltpu.roll(blk, SUBLANES - lo % SUBLANES, 0)
        el = jnp.where(gidx == g, blk, el)
    el = jnp.where(real, el, NEG_BIG)
    em = jnp.max(el, axis=0, keepdims=True)
    ee = jnp.exp(el - em)
    ep = ee / jnp.sum(ee, axis=0, keepdims=True)
    ep = jnp.where(real, ep, -1.0)
    p1 = jnp.max(ep, axis=0, keepdims=True)
    i1 = jnp.min(jnp.where(ep == p1, row, SUBLANES), axis=0, keepdims=True)
    ep2 = jnp.where(row == i1, -1.0, ep)
    p2 = jnp.max(ep2, axis=0, keepdims=True)
    i2 = jnp.min(jnp.where(ep2 == p2, row, SUBLANES), axis=0, keepdims=True)
    denom = p1 + p2
    w = jnp.where(row == i1, p1 / denom, jnp.where(row == i2, p2 / denom, 0.0))
    return gidx, w * gp_top


def _mixer_kernel(x_ref, lng_ref, lnb_ref, w_in_ref, b_in_ref, cw_ref, cb_ref, clg_ref, clb_ref,
                  sw_ref, sb_ref, w_out_ref, b_out_ref, l1g_ref, l1b_ref, wr_ref, br_ref, tri_ref,
                  x1_ref, pos_ref, xs_ref, meta_ref, info_ref, order_ref, rank_ref,
                  abuf, ashift, ubuf, w_in_bf, w_out_bf, stage, zbuf, cur_ref, fill_ref, nfree_ref,
                  ring_ref, sem, zsem, *, alpha, tile, t_pad, chunk, n_chunks, n_tiles, conf_w, sc_w_,
                  conf_k, sc_k, a_halo, u_halo):
    j = pl.program_id(1)
    k = pl.program_id(0) * pl.num_programs(1) + j
    slot = k % 2
    d = x1_ref.shape[-1]

    @pl.when(k == 0)
    def _():
        w_in_bf[...] = w_in_ref[...].astype(_BF16)
        w_out_bf[...] = w_out_ref[...].astype(_BF16)
        nfree_ref[0] = 0
        for g in range(N_GROUPS):
            cur_ref[g] = 0
            fill_ref[g] = chunk
        for c in range(info_ref.shape[0]):
            info_ref[c] = N_GROUPS - 1

    @pl.when(j == 0)
    def _():
        abuf[0:a_halo, :] = jnp.zeros((a_halo, conf_w), _F32)
        ubuf[0:u_halo, :] = jnp.zeros((u_halo, sc_w_), _F32)

    x0 = _layer_norm(x_ref[0], lng_ref[...], lnb_ref[...])
    h = _dot(x0.astype(_BF16), w_in_bf[...]) + b_in_ref[...]
    c0, c1, c2, c3 = conf_w, 2 * conf_w, 2 * conf_w + sc_w_, 2 * conf_w + 2 * sc_w_

    a = h[:, 0:c0] * _sigmoid(h[:, c0:c1])
    abuf[a_halo:a_halo + tile, :] = a
    for i in range(1, SUBLANES):
        ashift[i - 1] = abuf[i:i + tile + a_halo - SUBLANES, :]
    acc = jnp.broadcast_to(cb_ref[...], (tile, conf_w))
    for t in range(conf_k):
        q, i = divmod(a_halo - (conf_k - 1) + t, SUBLANES)
        if i == 0:
            src = abuf[SUBLANES * q:SUBLANES * q + tile, :]
        else:
            src = ashift[i - 1, SUBLANES * q:SUBLANES * q + tile, :]
        acc = acc + cw_ref[t:t + 1, :] * src
    abuf[0:a_halo, :] = abuf[tile:tile + a_halo, :]
    an = _layer_norm(acc, clg_ref[...], clb_ref[...])
    a2 = an * _sigmoid(an)

    u = h[:, c2:c3] * h[:, c3:]
    ubuf[u_halo:u_halo + tile, :] = u
    sc = jnp.broadcast_to(sb_ref[...], (tile, sc_w_))
    for t in range(sc_k):
        off = u_halo - (sc_k - 1) + t
        sc = sc + sw_ref[t:t + 1, :] * ubuf[off:off + tile, :]
    ubuf[0:u_halo, :] = ubuf[tile:tile + u_halo, :]
    s = h[:, c1:c2] * sc

    mix = (_dot(a2.astype(_BF16), w_out_bf[0:conf_w, :])
           + _dot(s.astype(_BF16), w_out_bf[conf_w:, :]) + b_out_ref[...])
    x1 = _layer_norm(alpha * x0 + mix, l1g_ref[...], l1b_ref[...])
    x1_ref[0] = x1
    x1b = x1.astype(_BF16)

    logits = lax.dot_general(wr_ref[...], x1b, _NT, preferred_element_type=_F32) + br_ref[...]
    gidx, w = _route(logits)

    row8 = lax.broadcasted_iota(jnp.int32, (SUBLANES, tile), 0)
    onehot = jnp.where(gidx == row8, 1.0, 0.0)
    cnt = jnp.sum(onehot, axis=1, keepdims=True)
    sizes, starts = [], []
    start = jnp.int32(0)
    startv = jnp.zeros((SUBLANES, tile), jnp.int32)
    for g in range(N_GROUPS):
        n8 = (cnt[g, 0].astype(jnp.int32) + (SUBLANES - 1)) & (-SUBLANES)
        sizes.append(n8)
        starts.append(start)
        startv = jnp.where(row8 == g, start, startv)
        start = start + n8
    cum = _dot(onehot.astype(_BF16), tri_ref[...])
    pos = jnp.sum(onehot * (startv.astype(_F32) + cum), axis=0, keepdims=True).astype(jnp.int32)
    pos_ref[0] = pos
    perm = jnp.where(lax.broadcasted_iota(jnp.int32, (t_pad, tile), 0) == pos, 1.0, 0.0).astype(_BF16)

    w_hi = w.astype(_BF16).astype(_F32)
    w_lo = (w - w_hi).astype(_BF16).astype(_F32)
    w_hl = jnp.where(row8 < EXPERTS_PER_GROUP, w_hi, pltpu.roll(w_lo, EXPERTS_PER_GROUP, 0))
    w_rows = jnp.concatenate([w_hl, jnp.zeros((LANES - SUBLANES, tile), _F32)], axis=0).astype(_BF16)

    def wait_slot(sl):
        for g in range(N_GROUPS):
            for piece in range(2):
                m = ring_ref[sl, 2 * g + piece]

                @pl.when(m > 0)
                def _():
                    _rows(stage.at[sl], 0, xs_ref, 0, m, sem.at[sl]).wait()

    @pl.when(k >= 2)
    def _():
        wait_slot(slot)

    stage[slot, :, 0:d] = _dot(perm, x1b)
    stage[slot, :, d:] = lax.dot_general(perm, w_rows, _NT, preferred_element_type=_F32)

    for g in range(N_GROUPS):
        n8, sg = sizes[g], starts[g]
        fill, cur, newc = fill_ref[g], cur_ref[g], nfree_ref[0]
        m1 = jnp.minimum(n8, chunk - fill)
        m2 = n8 - m1
        r1 = cur * chunk + fill
        r2 = newc * chunk
        ring_ref[slot, 2 * g] = m1
        ring_ref[slot, 2 * g + 1] = m2
        for f, v in enumerate((r1, m1, r2, m2, sg)):
            meta_ref[k, g * META_FIELDS + f] = v

        @pl.when(m1 > 0)
        def _():
            _rows(stage.at[slot], sg, xs_ref, r1, m1, sem.at[slot]).start()

        @pl.when(m2 > 0)
        def _():
            _rows(stage.at[slot], sg + m1, xs_ref, r2, m2, sem.at[slot]).start()
            info_ref[newc] = g
            cur_ref[g] = newc
            nfree_ref[0] = newc + 1

        fill_ref[g] = jnp.where(m2 > 0, m2, fill + m1)

    @pl.when(k == n_tiles - 1)
    def _():
        zbuf[...] = jnp.zeros_like(zbuf)
        nfree = nfree_ref[0]
        info_ref[n_chunks] = nfree
        tails = []
        for g in range(N_GROUPS):
            rem = chunk - fill_ref[g]
            tails.append((rem, _rows(zbuf, 0, xs_ref, cur_ref[g] * chunk + fill_ref[g], rem, zsem)))
        spare = [(c, pltpu.make_async_copy(zbuf, xs_ref.at[pl.ds(c * chunk, chunk)], zsem))
                 for c in range(n_tiles * tile // chunk, n_chunks)]
        for rem, cp in tails:
            pl.when(rem > 0)(cp.start)
        for c, cp in spare:
            pl.when(c >= nfree)(cp.start)

        slot_out = jnp.int32(0)
        for g in range(N_GROUPS):
            def place(c, nxt, g=g):
                hit = info_ref[c] == g

                @pl.when(hit)
                def _():
                    order_ref[nxt] = c
                    rank_ref[c] = nxt

                return nxt + hit.astype(jnp.int32)

            slot_out = lax.fori_loop(0, n_chunks, place, slot_out)

        @pl.when(k >= 1)
        def _():
            wait_slot(1 - slot)

        wait_slot(slot)
        for rem, cp in tails:
            pl.when(rem > 0)(cp.wait)
        for c, cp in spare:
            pl.when(c >= nfree)(cp.wait)


def _moe_kernel(order_ref, info_ref, *refs, d, d_expert, n_chunks, per_step):
    xs_refs = refs[:per_step]
    wg_hbm, wu_hbm, wd_hbm, o_ref, wg_buf, wu_buf, wd_buf, slot_ref, wsem = refs[per_step:]
    chunk = xs_refs[0].shape[0]

    def weight_copies(g, sl):
        lo = g * EXPERTS_PER_GROUP
        return [pltpu.make_async_copy(src.at[pl.ds(lo, EXPERTS_PER_GROUP)], dst.at[sl], wsem)
                for src, dst in ((wg_hbm, wg_buf), (wu_hbm, wu_buf), (wd_hbm, wd_buf))]

    for ci, xs_ref in enumerate(xs_refs):
        i = pl.program_id(0) * per_step + ci
        rows = slice(ci * chunk, (ci + 1) * chunk)
        c = order_ref[i]
        grp = info_ref[c]
        prev = info_ref[order_ref[jnp.maximum(i - 1, 0)]]

        @pl.when(i == 0)
        def _():
            slot_ref[0] = 0
            for cp in weight_copies(grp, 0):
                cp.start()

        @pl.when((i == 0) | (grp != prev))
        def _():
            @pl.when(i > 0)
            def _():
                slot_ref[0] = 1 - slot_ref[0]

            cur = slot_ref[0]
            for cp in weight_copies(grp, cur):
                cp.wait()

            def later_group(j, found):
                gj = info_ref[order_ref[j]]
                return jnp.where((found < 0) & (j > i) & (gj != grp), gj, found)

            nxt = lax.fori_loop(0, n_chunks, later_group, jnp.int32(-1))

            @pl.when(nxt >= 0)
            def _():
                for cp in weight_copies(nxt, 1 - cur):
                    cp.start()

        used = c < info_ref[n_chunks]

        @pl.when(used)
        def _():
            xb = xs_ref[:, 0:d]
            ws = slot_ref[0]
            aux = xs_ref[:, d:]
            y = jnp.zeros((chunk, o_ref.shape[1]), _F32)
            for e in range(EXPERTS_PER_GROUP):
                hg = _dot(xb, wg_buf[ws, e])
                hu = _dot(xb, wu_buf[ws, e])
                we = aux[:, e:e + 1] + aux[:, EXPERTS_PER_GROUP + e:EXPERTS_PER_GROUP + e + 1]
                hid = hg * _sigmoid(hg) * hu * we
                y = y + _dot(hid, wd_buf[ws, e])
            o_ref[rows, :] = y

        @pl.when(jnp.logical_not(used))
        def _():
            o_ref[rows, :] = jnp.zeros((chunk, o_ref.shape[1]), _F32)


def _ple_kernel(meta_ref, rank_ref, x1_ref, p_ref, pos_ref, ys_ref, wpg_ref, bpg_ref, wpp_ref, l2g_ref, l2b_ref,
                o_ref, ybuf, sem, *, alpha, tile, chunk, n_sub, tiles_per_step, t_pad, n_steps):
    k = pl.program_id(0)
    slot = k % 2

    def moved(r):
        return rank_ref[r // chunk] * chunk + r % chunk

    def pieces(step, sl):
        out = []
        for ti in range(tiles_per_step):
            kk = step * tiles_per_step + ti
            for g in range(N_GROUPS):
                r1, m1, r2, m2, sg = (meta_ref[kk, g * META_FIELDS + f] for f in range(META_FIELDS))
                out.append((m1, _rows(ys_ref, moved(r1), ybuf.at[sl, ti], sg, m1, sem.at[sl])))
                out.append((m2, _rows(ys_ref, moved(r2), ybuf.at[sl, ti], sg + m1, m2, sem.at[sl])))
        return out

    def fetch(kk, sl):
        for m, cp in pieces(kk, sl):
            pl.when(m > 0)(cp.start)

    @pl.when(k == 0)
    def _():
        ybuf[...] = jnp.zeros_like(ybuf)
        fetch(0, 0)

    @pl.when(k + 1 < n_steps)
    def _():
        fetch(k + 1, 1 - slot)

    for m, cp in pieces(k, slot):
        pl.when(m > 0)(cp.wait)

    sub = tile // n_sub
    for ti in range(tiles_per_step):
        yb = ybuf[slot, ti]
        pos = pos_ref[ti]
        for blk in range(n_sub):
            cols = slice(blk * sub, (blk + 1) * sub)
            rows = slice(ti * tile + blk * sub, ti * tile + (blk + 1) * sub)
            perm = jnp.where(lax.broadcasted_iota(jnp.int32, (t_pad, sub), 0) == pos[:, cols], 1.0, 0.0)
            y = lax.dot_general(perm, yb, _TN, preferred_element_type=_F32)
            r = alpha * x1_ref[rows, :] + y
            gate = _sigmoid(_dot(r, wpg_ref[...]) + bpg_ref[...])
            pp = _dot(p_ref[rows, :], wpp_ref[...])
            o_ref[rows, :] = _layer_norm(r + gate * pp, l2g_ref[...], l2b_ref[...])


def _full(shape):
    return pl.BlockSpec(shape, lambda *_: (0,) * len(shape))


def _resident(shape):
    return pl.BlockSpec(shape, lambda *_: (0,) * len(shape), pipeline_mode=pl.Buffered(1))


def _row(v):
    return v.reshape(1, -1).astype(_F32)


def kernel(x, p, ln_in_g, ln_in_b, w_in, b_in, conf_dw_w, conf_dw_b, conf_ln_g, conf_ln_b, sc_w, sc_b, w_out, b_out, ln1_g, ln1_b, w_rg, b_rg, w_re, b_re, w_gate, w_up, w_down, w_pg, b_pg, w_pp, ln2_g, ln2_b):
    depth = w_in.shape[0]
    assert depth == 1, "single-layer block"
    alpha = (2.0 * depth) ** 0.25
    bsz, seq, d = x.shape
    n = bsz * seq
    conf_k, conf_w = conf_dw_w.shape[1:]
    sc_k, sc_w_ = sc_w.shape[1:]
    d_in = w_in.shape[2]
    n_exp, _, d_expert = w_gate.shape[1:]
    assert n_exp == N_GROUPS * EXPERTS_PER_GROUP
    assert d_in == 2 * conf_w + 3 * sc_w_ and w_out.shape[1] == conf_w + sc_w_
    d_ple = p.shape[-1]

    tile = chunk = TILE
    assert seq % tile == 0 and (bsz * (seq // tile)) % PLE_TILES == 0
    nt = seq // tile
    n_tiles = bsz * nt
    a_halo = -(-(conf_k - 1) // SUBLANES) * SUBLANES
    u_halo = -(-(sc_k - 1) // SUBLANES) * SUBLANES
    t_pad = tile + N_GROUPS * SUBLANES
    n_chunks = -(-(n + (SUBLANES - 1) * N_GROUPS * n_tiles) // chunk) + N_GROUPS
    n_chunks = -(-n_chunks // MOE_CHUNKS) * MOE_CHUNKS
    d_pay = d + LANES

    n_logits = N_GROUPS + n_exp
    wr = jnp.pad(jnp.concatenate([w_rg[0], w_re[0]], axis=1).T, ((0, ROUTER_ROWS - n_logits), (0, 0)))
    br = jnp.pad(jnp.concatenate([b_rg[0], b_re[0]]), (0, ROUTER_ROWS - n_logits))
    tri = (jnp.arange(tile)[:, None] < jnp.arange(tile)[None, :]).astype(_BF16)

    mixer = pl.pallas_call(
        functools.partial(_mixer_kernel, alpha=alpha, tile=tile, t_pad=t_pad, chunk=chunk,
                          n_chunks=n_chunks, n_tiles=n_tiles, conf_w=conf_w, sc_w_=sc_w_,
                          conf_k=conf_k, sc_k=sc_k, a_halo=a_halo, u_halo=u_halo),
        grid=(bsz, nt),
        in_specs=[
            pl.BlockSpec((1, tile, d), lambda b, j: (b, j, 0)),
            _full((1, d)), _full((1, d)),
            _resident((d, d_in)), _full((1, d_in)),
            _full((conf_k, conf_w)), _full((1, conf_w)), _full((1, conf_w)), _full((1, conf_w)),
            _full((sc_k, sc_w_)), _full((1, sc_w_)),
            _resident((conf_w + sc_w_, d)), _full((1, d)),
            _full((1, d)), _full((1, d)),
            _full((ROUTER_ROWS, d)), _full((ROUTER_ROWS, 1)),
            _resident((tile, tile)),
        ],
        out_specs=[
            pl.BlockSpec((1, tile, d), lambda b, j: (b, j, 0)),
            pl.BlockSpec((1, 1, tile), lambda b, j: (b * nt + j, 0, 0)),
            pl.BlockSpec(memory_space=pl.ANY),
            pl.BlockSpec(memory_space=pltpu.SMEM),
            pl.BlockSpec(memory_space=pltpu.SMEM),
            pl.BlockSpec(memory_space=pltpu.SMEM),
            pl.BlockSpec(memory_space=pltpu.SMEM),
        ],
        out_shape=[
            jax.ShapeDtypeStruct((bsz, seq, d), _F32),
            jax.ShapeDtypeStruct((n_tiles, 1, tile), jnp.int32),
            jax.ShapeDtypeStruct((n_chunks * chunk, d_pay), _F32),
            jax.ShapeDtypeStruct((n_tiles, N_GROUPS * META_FIELDS), jnp.int32),
            jax.ShapeDtypeStruct((n_chunks + 1,), jnp.int32),
            jax.ShapeDtypeStruct((n_chunks,), jnp.int32),
            jax.ShapeDtypeStruct((n_chunks,), jnp.int32),
        ],
        scratch_shapes=[
            pltpu.VMEM((a_halo + tile, conf_w), _F32),
            pltpu.VMEM((SUBLANES - 1, a_halo + tile - SUBLANES, conf_w), _F32),
            pltpu.VMEM((u_halo + tile, sc_w_), _F32),
            pltpu.VMEM((d, d_in), _BF16),
            pltpu.VMEM((conf_w + sc_w_, d), _BF16),
            pltpu.VMEM((2, t_pad, d_pay), _F32),
            pltpu.VMEM((chunk, d_pay), _F32),
            pltpu.SMEM((N_GROUPS,), jnp.int32),
            pltpu.SMEM((N_GROUPS,), jnp.int32),
            pltpu.SMEM((1,), jnp.int32),
            pltpu.SMEM((2, 2 * N_GROUPS), jnp.int32),
            pltpu.SemaphoreType.DMA((2,)),
            pltpu.SemaphoreType.DMA(()),
        ],
        compiler_params=pltpu.CompilerParams(
            dimension_semantics=("arbitrary", "arbitrary"), vmem_limit_bytes=VMEM_LIMIT_BYTES),
        name="mixer",
    )
    x1, pos, xs, meta, info, order, rank = mixer(
        x, _row(ln_in_g), _row(ln_in_b), w_in[0], _row(b_in[0]),
        conf_dw_w[0], _row(conf_dw_b[0]), _row(conf_ln_g[0]), _row(conf_ln_b[0]),
        sc_w[0], _row(sc_b[0]), w_out[0], _row(b_out[0]),
        _row(ln1_g[0]), _row(ln1_b[0]), wr.astype(_BF16), br.reshape(ROUTER_ROWS, 1), tri)

    moe = pl.pallas_call(
        functools.partial(_moe_kernel, d=d, d_expert=d_expert, n_chunks=n_chunks, per_step=MOE_CHUNKS),
        grid_spec=pltpu.PrefetchScalarGridSpec(
            num_scalar_prefetch=2,
            grid=(n_chunks // MOE_CHUNKS,),
            in_specs=[
                pl.BlockSpec((chunk, d_pay), functools.partial(
                    lambda i, order, info, ci: (order[i * MOE_CHUNKS + ci], 0), ci=ci))
                for ci in range(MOE_CHUNKS)
            ] + [
                pl.BlockSpec(memory_space=pl.ANY),
                pl.BlockSpec(memory_space=pl.ANY),
                pl.BlockSpec(memory_space=pl.ANY),
            ],
            out_specs=pl.BlockSpec((MOE_CHUNKS * chunk, d), lambda i, order, info: (i, 0)),
            scratch_shapes=[
                pltpu.VMEM((2, EXPERTS_PER_GROUP, d, d_expert), _F32),
                pltpu.VMEM((2, EXPERTS_PER_GROUP, d, d_expert), _F32),
                pltpu.VMEM((2, EXPERTS_PER_GROUP, d_expert, d), _F32),
                pltpu.SMEM((1,), jnp.int32),
                pltpu.SemaphoreType.DMA(()),
            ],
        ),
        out_shape=jax.ShapeDtypeStruct((n_chunks * chunk, d), _F32),
        compiler_params=pltpu.CompilerParams(
            dimension_semantics=("arbitrary",), vmem_limit_bytes=VMEM_LIMIT_BYTES),
        name="moe",
    )
    ys = moe(order, info, *([xs] * MOE_CHUNKS), w_gate[0], w_up[0], w_down[0])

    ple = pl.pallas_call(
        functools.partial(_ple_kernel, alpha=alpha, tile=tile, chunk=chunk, n_sub=N_SUB, tiles_per_step=PLE_TILES,
                          t_pad=t_pad, n_steps=n_tiles // PLE_TILES),
        grid_spec=pltpu.PrefetchScalarGridSpec(
            num_scalar_prefetch=2,
            grid=(n_tiles // PLE_TILES,),
            in_specs=[
                pl.BlockSpec((PLE_TILES * tile, d), lambda k, meta, rank: (k, 0)),
                pl.BlockSpec((PLE_TILES * tile, d_ple), lambda k, meta, rank: (k, 0)),
                pl.BlockSpec((PLE_TILES, 1, tile), lambda k, meta, rank: (k, 0, 0)),
                pl.BlockSpec(memory_space=pl.ANY),
                _resident((d, d)), _full((1, d)), _resident((d_ple, d)), _full((1, d)), _full((1, d)),
            ],
            out_specs=pl.BlockSpec((PLE_TILES * tile, d), lambda k, meta, rank: (k, 0)),
            scratch_shapes=[
                pltpu.VMEM((2, PLE_TILES, t_pad, d), _F32),
                pltpu.SemaphoreType.DMA((2,)),
            ],
        ),
        out_shape=jax.ShapeDtypeStruct((n, d), _F32),
        compiler_params=pltpu.CompilerParams(
            dimension_semantics=("arbitrary",), vmem_limit_bytes=VMEM_LIMIT_BYTES),
        name="ple",
    )
    out = ple(meta, rank, x1.reshape(n, d), p[0].reshape(n, d_ple), pos, ys,
              w_pg[0], _row(b_pg[0]), w_pp[0], _row(ln2_g[0]), _row(ln2_b[0]))
    return out.reshape(bsz, seq, d)
```

```python
import functools

import jax
import jax.numpy as jnp
from jax import lax
from jax.experimental import pallas as pl
from jax.experimental.pallas import tpu as pltpu

LN_EPS = 1e-5
N_GROUPS = 4
EXPERTS_PER_GROUP = 4
SUBLANES = 8
LANES = 128
ROUTER_ROWS = 32
TILE = 512
MOE_CHUNKS = 2
PLE_TILES = 2
N_SUB = 2
VMEM_LIMIT_BYTES = 56 * 1024 * 1024
NEG_BIG = -1e30
META_FIELDS = 5

_F32 = jnp.float32
_BF16 = jnp.bfloat16
_NT = (((1,), (1,)), ((), ()))
_TN = (((0,), (0,)), ((), ()))


def _layer_norm(x, g, b):
    mu = jnp.mean(x, axis=-1, keepdims=True)
    xc = x - mu
    var = jnp.mean(xc * xc, axis=-1, keepdims=True)
    return xc * lax.rsqrt(var + LN_EPS) * g + b


def _sigmoid(x):
    return 0.5 * jnp.tanh(0.5 * x) + 0.5


def _dot(a, b):
    return jnp.dot(a, b, preferred_element_type=_F32)


def _aligned(v):
    return v if isinstance(v, int) else pl.multiple_of(v, SUBLANES)


def _rows(src, s0, dst, d0, m, sem):
    return pltpu.make_async_copy(src.at[pl.ds(_aligned(s0), _aligned(m))],
                                 dst.at[pl.ds(_aligned(d0), _aligned(m))], sem)


def _route(logits):
    row = lax.broadcasted_iota(jnp.int32, (SUBLANES, logits.shape[1]), 0)
    real = row < N_GROUPS
    gl = jnp.where(real, logits[0:SUBLANES], NEG_BIG)
    gm = jnp.max(gl, axis=0, keepdims=True)
    ge = jnp.exp(gl - gm)
    gp = ge / jnp.sum(ge, axis=0, keepdims=True)
    gp_top = jnp.max(gp, axis=0, keepdims=True)
    gidx = jnp.min(jnp.where(gp == gp_top, row, SUBLANES), axis=0, keepdims=True)

    el = jnp.zeros_like(gl)
    for g in range(N_GROUPS):
        lo = N_GROUPS + g * EXPERTS_PER_GROUP
        blk = logits[lo - lo % SUBLANES:lo - lo % SUBLANES + SUBLANES]
        if lo % SUBLANES:
            blk = pltpu.roll(blk, SUBLANES - lo % SUBLANES, 0)
        el = jnp.where(gidx == g, blk, el)
    el = jnp.where(real, el, NEG_BIG)
    em = jnp.max(el, axis=0, keepdims=True)
    ee = jnp.exp(el - em)
    ep = ee / jnp.sum(ee, axis=0, keepdims=True)
    ep = jnp.where(real, ep, -1.0)
    p1 = jnp.max(ep, axis=0, keepdims=True)
    i1 = jnp.min(jnp.where(ep == p1, row, SUBLANES), axis=0, keepdims=True)
    ep2 = jnp.where(row == i1, -1.0, ep)
    p2 = jnp.max(ep2, axis=0, keepdims=True)
    i2 = jnp.min(jnp.where(ep2 == p2, row, SUBLANES), axis=0, keepdims=True)
    denom = p1 + p2
    w = jnp.where(row == i1, p1 / denom, jnp.where(row == i2, p2 / denom, 0.0))
    return gidx, w * gp_top


def _mixer_kernel(x_ref, lng_ref, lnb_ref, w_in_ref, b_in_ref, cw_ref, cb_ref, clg_ref, clb_ref,
                  sw_ref, sb_ref, w_out_ref, b_out_ref, l1g_ref, l1b_ref, wr_ref, br_ref, tri_ref,
                  x1_ref, pos_ref, xs_ref, meta_ref, info_ref, order_ref, rank_ref,
                  abuf, ashift, ubuf, w_in_bf, w_out_bf, stage, zbuf, cur_ref, fill_ref, nfree_ref,
                  ring_ref, sem, zsem, *, alpha, tile, t_pad, chunk, n_chunks, n_tiles, conf_w, sc_w_,
                  conf_k, sc_k, a_halo, u_halo):
    j = pl.program_id(1)
    k = pl.program_id(0) * pl.num_programs(1) + j
    slot = k % 2
    d = x1_ref.shape[-1]

    @pl.when(k == 0)
    def _():
        w_in_bf[...] = w_in_ref[...].astype(_BF16)
        w_out_bf[...] = w_out_ref[...].astype(_BF16)
        nfree_ref[0] = 0
        for g in range(N_GROUPS):
            cur_ref[g] = 0
            fill_ref[g] = chunk
        for c in range(info_ref.shape[0]):
            info_ref[c] = N_GROUPS - 1

    @pl.when(j == 0)
    def _():
        abuf[0:a_halo, :] = jnp.zeros((a_halo, conf_w), _F32)
        ubuf[0:u_halo, :] = jnp.zeros((u_halo, sc_w_), _F32)

    x0 = _layer_norm(x_ref[0], lng_ref[...], lnb_ref[...])
    h = _dot(x0.astype(_BF16), w_in_bf[...]) + b_in_ref[...]
    c0, c1, c2, c3 = conf_w, 2 * conf_w, 2 * conf_w + sc_w_, 2 * conf_w + 2 * sc_w_

    a = h[:, 0:c0] * _sigmoid(h[:, c0:c1])
    abuf[a_halo:a_halo + tile, :] = a
    for i in range(1, SUBLANES):
        ashift[i - 1] = abuf[i:i + tile + a_halo - SUBLANES, :]
    acc = jnp.broadcast_to(cb_ref[...], (tile, conf_w))
    for t in range(conf_k):
        q, i = divmod(a_halo - (conf_k - 1) + t, SUBLANES)
        if i == 0:
            src = abuf[SUBLANES * q:SUBLANES * q + tile, :]
        else:
            src = ashift[i - 1, SUBLANES * q:SUBLANES * q + tile, :]
        acc = acc + cw_ref[t:t + 1, :] * src
    abuf[0:a_halo, :] = abuf[tile:tile + a_halo, :]
    an = _layer_norm(acc, clg_ref[...], clb_ref[...])
    a2 = an * _sigmoid(an)

    u = h[:, c2:c3] * h[:, c3:]
    ubuf[u_halo:u_halo + tile, :] = u
    sc = jnp.broadcast_to(sb_ref[...], (tile, sc_w_))
    for t in range(sc_k):
        off = u_halo - (sc_k - 1) + t
        sc = sc + sw_ref[t:t + 1, :] * ubuf[off:off + tile, :]
    ubuf[0:u_halo, :] = ubuf[tile:tile + u_halo, :]
    s = h[:, c1:c2] * sc

    mix = (_dot(a2.astype(_BF16), w_out_bf[0:conf_w, :])
           + _dot(s.astype(_BF16), w_out_bf[conf_w:, :]) + b_out_ref[...])
    x1 = _layer_norm(alpha * x0 + mix, l1g_ref[...], l1b_ref[...])
    x1_ref[0] = x1
    x1b = x1.astype(_BF16)

    logits = lax.dot_general(wr_ref[...], x1b, _NT, preferred_element_type=_F32) + br_ref[...]
    gidx, w = _route(logits)

    row8 = lax.broadcasted_iota(jnp.int32, (SUBLANES, tile), 0)
    onehot = jnp.where(gidx == row8, 1.0, 0.0)
    cnt = jnp.sum(onehot, axis=1, keepdims=True)
    sizes, starts = [], []
    start = jnp.int32(0)
    startv = jnp.zeros((SUBLANES, tile), jnp.int32)
    for g in range(N_GROUPS):
        n8 = (cnt[g, 0].astype(jnp.int32) + (SUBLANES - 1)) & (-SUBLANES)
        sizes.append(n8)
        starts.append(start)
        startv = jnp.where(row8 == g, start, startv)
        start = start + n8
    cum = _dot(onehot.astype(_BF16), tri_ref[...])
    pos = jnp.sum(onehot * (startv.astype(_F32) + cum), axis=0, keepdims=True).astype(jnp.int32)
    pos_ref[0] = pos
    perm = jnp.where(lax.broadcasted_iota(jnp.int32, (t_pad, tile), 0) == pos, 1.0, 0.0).astype(_BF16)

    w_hi = w.astype(_BF16).astype(_F32)
    w_lo = (w - w_hi).astype(_BF16).astype(_F32)
    w_hl = jnp.where(row8 < EXPERTS_PER_GROUP, w_hi, pltpu.roll(w_lo, EXPERTS_PER_GROUP, 0))
    w_rows = jnp.concatenate([w_hl, jnp.zeros((LANES - SUBLANES, tile), _F32)], axis=0).astype(_BF16)

    def wait_slot(sl):
        for g in range(N_GROUPS):
            for piece in range(2):
                m = ring_ref[sl, 2 * g + piece]

                @pl.when(m > 0)
                def _():
                    _rows(stage.at[sl], 0, xs_ref, 0, m, sem.at[sl]).wait()

    @pl.when(k >= 2)
    def _():
        wait_slot(slot)

    stage[slot, :, 0:d] = _dot(perm, x1b)
    stage[slot, :, d:] = lax.dot_general(perm, w_rows, _NT, preferred_element_type=_F32)

    for g in range(N_GROUPS):
        n8, sg = sizes[g], starts[g]
        fill, cur, newc = fill_ref[g], cur_ref[g], nfree_ref[0]
        m1 = jnp.minimum(n8, chunk - fill)
        m2 = n8 - m1
        r1 = cur * chunk + fill
        r2 = newc * chunk
        ring_ref[slot, 2 * g] = m1
        ring_ref[slot, 2 * g + 1] = m2
        for f, v in enumerate((r1, m1, r2, m2, sg)):
            meta_ref[k, g * META_FIELDS + f] = v

        @pl.when(m1 > 0)
        def _():
            _rows(stage.at[slot], sg, xs_ref, r1, m1, sem.at[slot]).start()

        @pl.when(m2 > 0)
        def _():
            _rows(stage.at[slot], sg + m1, xs_ref, r2, m2, sem.at[slot]).start()
            info_ref[newc] = g
            cur_ref[g] = newc
            nfree_ref[0] = newc + 1

        fill_ref[g] = jnp.where(m2 > 0, m2, fill + m1)

    @pl.when(k == n_tiles - 1)
    def _():
        zbuf[...] = jnp.zeros_like(zbuf)
        nfree = nfree_ref[0]
        info_ref[n_chunks] = nfree
        tails = []
        for g in range(N_GROUPS):
            rem = chunk - fill_ref[g]
            tails.append((rem, _rows(zbuf, 0, xs_ref, cur_ref[g] * chunk + fill_ref[g], rem, zsem)))
        spare = [(c, pltpu.make_async_copy(zbuf, xs_ref.at[pl.ds(c * chunk, chunk)], zsem))
                 for c in range(n_tiles * tile // chunk, n_chunks)]
        for rem, cp in tails:
            pl.when(rem > 0)(cp.start)
        for c, cp in spare:
            pl.when(c >= nfree)(cp.start)

        slot_out = jnp.int32(0)
        for g in range(N_GROUPS):
            def place(c, nxt, g=g):
                hit = info_ref[c] == g

                @pl.when(hit)
                def _():
                    order_ref[nxt] = c
                    rank_ref[c] = nxt

                return nxt + hit.astype(jnp.int32)

            slot_out = lax.fori_loop(0, n_chunks, place, slot_out)

        @pl.when(k >= 1)
        def _():
            wait_slot(1 - slot)

        wait_slot(slot)
        for rem, cp in tails:
            pl.when(rem > 0)(cp.wait)
        for c, cp in spare:
            pl.when(c >= nfree)(cp.wait)


def _moe_kernel(order_ref, info_ref, *refs, d, d_expert, n_chunks, per_step):
    xs_refs = refs[:per_step]
    wg_hbm, wu_hbm, wd_hbm, o_ref, wg_buf, wu_buf, wd_buf, slot_ref, wsem = refs[per_step:]
    chunk = xs_refs[0].shape[0]

    def weight_copies(g, sl):
        lo = g * EXPERTS_PER_GROUP
        return [pltpu.make_async_copy(src.at[pl.ds(lo, EXPERTS_PER_GROUP)], dst.at[sl], wsem)
                for src, dst in ((wg_hbm, wg_buf), (wu_hbm, wu_buf), (wd_hbm, wd_buf))]

    for ci, xs_ref in enumerate(xs_refs):
        i = pl.program_id(0) * per_step + ci
        rows = slice(ci * chunk, (ci + 1) * chunk)
        c = order_ref[i]
        grp = info_ref[c]
        prev = info_ref[order_ref[jnp.maximum(i - 1, 0)]]

        @pl.when(i == 0)
        def _():
            slot_ref[0] = 0
            for cp in weight_copies(grp, 0):
                cp.start()

        @pl.when((i == 0) | (grp != prev))
        def _():
            @pl.when(i > 0)
            def _():
                slot_ref[0] = 1 - slot_ref[0]

            cur = slot_ref[0]
            for cp in weight_copies(grp, cur):
                cp.wait()

            def later_group(j, found):
                gj = info_ref[order_ref[j]]
                return jnp.where((found < 0) & (j > i) & (gj != grp), gj, found)

            nxt = lax.fori_loop(0, n_chunks, later_group, jnp.int32(-1))

            @pl.when(nxt >= 0)
            def _():
                for cp in weight_copies(nxt, 1 - cur):
                    cp.start()

        used = c < info_ref[n_chunks]

        @pl.when(used)
        def _():
            xb = xs_ref[:, 0:d]
            ws = slot_ref[0]
            aux = xs_ref[:, d:]
            y = jnp.zeros((chunk, o_ref.shape[1]), _F32)
            for e in range(EXPERTS_PER_GROUP):
                hg = _dot(xb, wg_buf[ws, e])
                hu = _dot(xb, wu_buf[ws, e])
                we = aux[:, e:e + 1] + aux[:, EXPERTS_PER_GROUP + e:EXPERTS_PER_GROUP + e + 1]
                hid = hg * _sigmoid(hg) * hu * we
                y = y + _dot(hid, wd_buf[ws, e])
            o_ref[rows, :] = y

        @pl.when(jnp.logical_not(used))
        def _():
            o_ref[rows, :] = jnp.zeros((chunk, o_ref.shape[1]), _F32)


def _ple_kernel(meta_ref, rank_ref, x1_ref, p_ref, pos_ref, ys_ref, wpg_ref, bpg_ref, wpp_ref, l2g_ref, l2b_ref,
                o_ref, ybuf, sem, *, alpha, tile, chunk, n_sub, tiles_per_step, t_pad, n_steps):
    k = pl.program_id(0)
    slot = k % 2

    def moved(r):
        bits = chunk.bit_length() - 1
        return (rank_ref[lax.shift_right_logical(r, bits)] << bits) + (r & (chunk - 1))

    def pieces(step, sl):
        out = []
        for ti in range(tiles_per_step):
            kk = step * tiles_per_step + ti
            for g in range(N_GROUPS):
                r1, m1, r2, m2, sg = (meta_ref[kk, g * META_FIELDS + f] for f in range(META_FIELDS))
                out.append((m1, _rows(ys_ref, moved(r1), ybuf.at[sl, ti], sg, m1, sem.at[sl])))
                out.append((m2, _rows(ys_ref, moved(r2), ybuf.at[sl, ti], sg + m1, m2, sem.at[sl])))
        return out

    def fetch(kk, sl):
        for m, cp in pieces(kk, sl):
            pl.when(m > 0)(cp.start)

    @pl.when(k == 0)
    def _():
        ybuf[...] = jnp.zeros_like(ybuf)
        fetch(0, 0)

    @pl.when(k + 1 < n_steps)
    def _():
        fetch(k + 1, 1 - slot)

    for m, cp in pieces(k, slot):
        pl.when(m > 0)(cp.wait)

    sub = tile // n_sub
    for ti in range(tiles_per_step):
        yb = ybuf[slot, ti]
        pos = pos_ref[ti]
        for blk in range(n_sub):
            cols = slice(blk * sub, (blk + 1) * sub)
            rows = slice(ti * tile + blk * sub, ti * tile + (blk + 1) * sub)
            perm = jnp.where(lax.broadcasted_iota(jnp.int32, (t_pad, sub), 0) == pos[:, cols], 1.0, 0.0)
            y = lax.dot_general(perm, yb, _TN, preferred_element_type=_F32)
            r = alpha * x1_ref[rows, :] + y
            gate = _sigmoid(_dot(r, wpg_ref[...]) + bpg_ref[...])
            pp = _dot(p_ref[rows, :], wpp_ref[...])
            o_ref[rows, :] = _layer_norm(r + gate * pp, l2g_ref[...], l2b_ref[...])


def _full(shape):
    return pl.BlockSpec(shape, lambda *_: (0,) * len(shape))


def _resident(shape):
    return pl.BlockSpec(shape, lambda *_: (0,) * len(shape), pipeline_mode=pl.Buffered(1))


def _row(v):
    return v.reshape(1, -1).astype(_F32)


def kernel(x, p, ln_in_g, ln_in_b, w_in, b_in, conf_dw_w, conf_dw_b, conf_ln_g, conf_ln_b, sc_w, sc_b, w_out, b_out, ln1_g, ln1_b, w_rg, b_rg, w_re, b_re, w_gate, w_up, w_down, w_pg, b_pg, w_pp, ln2_g, ln2_b):
    depth = w_in.shape[0]
    assert depth == 1, "single-layer block"
    alpha = (2.0 * depth) ** 0.25
    bsz, seq, d = x.shape
    n = bsz * seq
    conf_k, conf_w = conf_dw_w.shape[1:]
    sc_k, sc_w_ = sc_w.shape[1:]
    d_in = w_in.shape[2]
    n_exp, _, d_expert = w_gate.shape[1:]
    assert n_exp == N_GROUPS * EXPERTS_PER_GROUP
    assert d_in == 2 * conf_w + 3 * sc_w_ and w_out.shape[1] == conf_w + sc_w_
    d_ple = p.shape[-1]

    tile = chunk = TILE
    assert chunk & (chunk - 1) == 0, "chunk rows must be a power of two"
    assert seq % tile == 0 and (bsz * (seq // tile)) % PLE_TILES == 0
    nt = seq // tile
    n_tiles = bsz * nt
    a_halo = -(-(conf_k - 1) // SUBLANES) * SUBLANES
    u_halo = -(-(sc_k - 1) // SUBLANES) * SUBLANES
    t_pad = tile + N_GROUPS * SUBLANES
    n_chunks = -(-(n + (SUBLANES - 1) * N_GROUPS * n_tiles) // chunk) + N_GROUPS
    n_chunks = -(-n_chunks // MOE_CHUNKS) * MOE_CHUNKS
    d_pay = d + LANES

    n_logits = N_GROUPS + n_exp
    wr = jnp.pad(jnp.concatenate([w_rg[0], w_re[0]], axis=1).T, ((0, ROUTER_ROWS - n_logits), (0, 0)))
    br = jnp.pad(jnp.concatenate([b_rg[0], b_re[0]]), (0, ROUTER_ROWS - n_logits))
    tri = (jnp.arange(tile)[:, None] < jnp.arange(tile)[None, :]).astype(_BF16)

    mixer = pl.pallas_call(
        functools.partial(_mixer_kernel, alpha=alpha, tile=tile, t_pad=t_pad, chunk=chunk,
                          n_chunks=n_chunks, n_tiles=n_tiles, conf_w=conf_w, sc_w_=sc_w_,
                          conf_k=conf_k, sc_k=sc_k, a_halo=a_halo, u_halo=u_halo),
        grid=(bsz, nt),
        in_specs=[
            pl.BlockSpec((1, tile, d), lambda b, j: (b, j, 0)),
            _full((1, d)), _full((1, d)),
            _resident((d, d_in)), _full((1, d_in)),
            _full((conf_k, conf_w)), _full((1, conf_w)), _full((1, conf_w)), _full((1, conf_w)),
            _full((sc_k, sc_w_)), _full((1, sc_w_)),
            _resident((conf_w + sc_w_, d)), _full((1, d)),
            _full((1, d)), _full((1, d)),
            _full((ROUTER_ROWS, d)), _full((ROUTER_ROWS, 1)),
            _resident((tile, tile)),
        ],
        out_specs=[
            pl.BlockSpec((1, tile, d), lambda b, j: (b, j, 0)),
            pl.BlockSpec((1, 1, tile), lambda b, j: (b * nt + j, 0, 0)),
            pl.BlockSpec(memory_space=pl.ANY),
            pl.BlockSpec(memory_space=pltpu.SMEM),
            pl.BlockSpec(memory_space=pltpu.SMEM),
            pl.BlockSpec(memory_space=pltpu.SMEM),
            pl.BlockSpec(memory_space=pltpu.SMEM),
        ],
        out_shape=[
            jax.ShapeDtypeStruct((bsz, seq, d), _F32),
            jax.ShapeDtypeStruct((n_tiles, 1, tile), jnp.int32),
            jax.ShapeDtypeStruct((n_chunks * chunk, d_pay), _F32),
            jax.ShapeDtypeStruct((n_tiles, N_GROUPS * META_FIELDS), jnp.int32),
            jax.ShapeDtypeStruct((n_chunks + 1,), jnp.int32),
            jax.ShapeDtypeStruct((n_chunks,), jnp.int32),
            jax.ShapeDtypeStruct((n_chunks,), jnp.int32),
        ],
        scratch_shapes=[
            pltpu.VMEM((a_halo + tile, conf_w), _F32),
            pltpu.VMEM((SUBLANES - 1, a_halo + tile - SUBLANES, conf_w), _F32),
            pltpu.VMEM((u_halo + tile, sc_w_), _F32),
            pltpu.VMEM((d, d_in), _BF16),
            pltpu.VMEM((conf_w + sc_w_, d), _BF16),
            pltpu.VMEM((2, t_pad, d_pay), _F32),
            pltpu.VMEM((chunk, d_pay), _F32),
            pltpu.SMEM((N_GROUPS,), jnp.int32),
            pltpu.SMEM((N_GROUPS,), jnp.int32),
            pltpu.SMEM((1,), jnp.int32),
            pltpu.SMEM((2, 2 * N_GROUPS), jnp.int32),
            pltpu.SemaphoreType.DMA((2,)),
            pltpu.SemaphoreType.DMA(()),
        ],
        compiler_params=pltpu.CompilerParams(
            dimension_semantics=("arbitrary", "arbitrary"), vmem_limit_bytes=VMEM_LIMIT_BYTES),
        name="mixer",
    )
    x1, pos, xs, meta, info, order, rank = mixer(
        x, _row(ln_in_g), _row(ln_in_b), w_in[0], _row(b_in[0]),
        conf_dw_w[0], _row(conf_dw_b[0]), _row(conf_ln_g[0]), _row(conf_ln_b[0]),
        sc_w[0], _row(sc_b[0]), w_out[0], _row(b_out[0]),
        _row(ln1_g[0]), _row(ln1_b[0]), wr.astype(_BF16), br.reshape(ROUTER_ROWS, 1), tri)

    moe = pl.pallas_call(
        functools.partial(_moe_kernel, d=d, d_expert=d_expert, n_chunks=n_chunks, per_step=MOE_CHUNKS),
        grid_spec=pltpu.PrefetchScalarGridSpec(
            num_scalar_prefetch=2,
            grid=(n_chunks // MOE_CHUNKS,),
            in_specs=[
                pl.BlockSpec((chunk, d_pay), functools.partial(
                    lambda i, order, info, ci: (order[jnp.minimum(i * MOE_CHUNKS + ci, info[n_chunks] - 1)], 0),
                    ci=ci))
                for ci in range(MOE_CHUNKS)
            ] + [
                pl.BlockSpec(memory_space=pl.ANY),
                pl.BlockSpec(memory_space=pl.ANY),
                pl.BlockSpec(memory_space=pl.ANY),
            ],
            out_specs=pl.BlockSpec((MOE_CHUNKS * chunk, d), lambda i, order, info: (i, 0)),
            scratch_shapes=[
                pltpu.VMEM((2, EXPERTS_PER_GROUP, d, d_expert), _F32),
                pltpu.VMEM((2, EXPERTS_PER_GROUP, d, d_expert), _F32),
                pltpu.VMEM((2, EXPERTS_PER_GROUP, d_expert, d), _F32),
                pltpu.SMEM((1,), jnp.int32),
                pltpu.SemaphoreType.DMA(()),
            ],
        ),
        out_shape=jax.ShapeDtypeStruct((n_chunks * chunk, d), _F32),
        compiler_params=pltpu.CompilerParams(
            dimension_semantics=("arbitrary",), vmem_limit_bytes=VMEM_LIMIT_BYTES),
        name="moe",
    )
    ys = moe(order, info, *([xs] * MOE_CHUNKS), w_gate[0], w_up[0], w_down[0])

    ple = pl.pallas_call(
        functools.partial(_ple_kernel, alpha=alpha, tile=tile, chunk=chunk, n_sub=N_SUB, tiles_per_step=PLE_TILES,
                          t_pad=t_pad, n_steps=n_tiles // PLE_TILES),
        grid_spec=pltpu.PrefetchScalarGridSpec(
            num_scalar_prefetch=2,
            grid=(n_tiles // PLE_TILES,),
            in_specs=[
                pl.BlockSpec((PLE_TILES * tile, d), lambda k, meta, rank: (k, 0)),
                pl.BlockSpec((PLE_TILES * tile, d_ple), lambda k, meta, rank: (k, 0)),
                pl.BlockSpec((PLE_TILES, 1, tile), lambda k, meta, rank: (k, 0, 0)),
                pl.BlockSpec(memory_space=pl.ANY),
                _resident((d, d)), _full((1, d)), _resident((d_ple, d)), _full((1, d)), _full((1, d)),
            ],
            out_specs=pl.BlockSpec((PLE_TILES * tile, d), lambda k, meta, rank: (k, 0)),
            scratch_shapes=[
                pltpu.VMEM((2, PLE_TILES, t_pad, d), _F32),
                pltpu.SemaphoreType.DMA((2,)),
            ],
        ),
        out_shape=jax.ShapeDtypeStruct((n, d), _F32),
        compiler_params=pltpu.CompilerParams(
            dimension_semantics=("arbitrary",), vmem_limit_bytes=VMEM_LIMIT_BYTES),
        name="ple",
    )
    out = ple(meta, rank, x1.reshape(n, d), p[0].reshape(n, d_ple), pos, ys,
              w_pg[0], _row(b_pg[0]), w_pp[0], _row(ln2_g[0]), _row(ln2_b[0]))
    return out.reshape(bsz, seq, d)
```

```python
import functools

import jax
import jax.numpy as jnp
from jax import lax
from jax.experimental import pallas as pl
from jax.experimental.pallas import tpu as pltpu

LN_EPS = 1e-5
N_GROUPS = 4
EXPERTS_PER_GROUP = 4
SUBLANES = 8
LANES = 128
ROUTER_ROWS = 32
TILE = 512
MOE_CHUNKS = 2
PLE_TILES = 2
N_SUB = 2
VMEM_LIMIT_BYTES = 56 * 1024 * 1024
NEG_BIG = -1e30
META_FIELDS = 5

_F32 = jnp.float32
_BF16 = jnp.bfloat16
_NT = (((1,), (1,)), ((), ()))
_TN = (((0,), (0,)), ((), ()))


def _layer_norm(x, g, b):
    mu = jnp.mean(x, axis=-1, keepdims=True)
    xc = x - mu
    var = jnp.mean(xc * xc, axis=-1, keepdims=True)
    return xc * lax.rsqrt(var + LN_EPS) * g + b


def _sigmoid(x):
    return 0.5 * jnp.tanh(0.5 * x) + 0.5


def _dot(a, b):
    return jnp.dot(a, b, preferred_element_type=_F32)


def _aligned(v):
    return v if isinstance(v, int) else pl.multiple_of(v, SUBLANES)


def _rows(src, s0, dst, d0, m, sem):
    return pltpu.make_async_copy(src.at[pl.ds(_aligned(s0), _aligned(m))],
                                 dst.at[pl.ds(_aligned(d0), _aligned(m))], sem)


def _route(logits):
    row = lax.broadcasted_iota(jnp.int32, (SUBLANES, logits.shape[1]), 0)
    real = row < N_GROUPS
    gl = jnp.where(real, logits[0:SUBLANES], NEG_BIG)
    gm = jnp.max(gl, axis=0, keepdims=True)
    ge = jnp.exp(gl - gm)
    gp = ge / jnp.sum(ge, axis=0, keepdims=True)
    gp_top = jnp.max(gp, axis=0, keepdims=True)
    gidx = jnp.min(jnp.where(gp == gp_top, row, SUBLANES), axis=0, keepdims=True)

    el = jnp.zeros_like(gl)
    for g in range(N_GROUPS):
        lo = N_GROUPS + g * EXPERTS_PER_GROUP
        blk = logits[lo - lo % SUBLANES:lo - lo % SUBLANES + SUBLANES]
        if lo % SUBLANES:
            blk = pltpu.roll(blk, SUBLANES - lo % SUBLANES, 0)
        el = jnp.where(gidx == g, blk, el)
    el = jnp.where(real, el, NEG_BIG)
    em = jnp.max(el, axis=0, keepdims=True)
    ee = jnp.exp(el - em)
    ep = ee / jnp.sum(ee, axis=0, keepdims=True)
    ep = jnp.where(real, ep, -1.0)
    p1 = jnp.max(ep, axis=0, keepdims=True)
    i1 = jnp.min(jnp.where(ep == p1, row, SUBLANES), axis=0, keepdims=True)
    ep2 = jnp.where(row == i1, -1.0, ep)
    p2 = jnp.max(ep2, axis=0, keepdims=True)
    i2 = jnp.min(jnp.where(ep2 == p2, row, SUBLANES), axis=0, keepdims=True)
    denom = p1 + p2
    w = jnp.where(row == i1, p1 / denom, jnp.where(row == i2, p2 / denom, 0.0))
    return gidx, w * gp_top


def _mixer_kernel(x_ref, lng_ref, lnb_ref, w_in_ref, b_in_ref, cw_ref, cb_ref, clg_ref, clb_ref,
                  sw_ref, sb_ref, w_out_ref, b_out_ref, l1g_ref, l1b_ref, wr_ref, br_ref, tri_ref,
                  x1_ref, pos_ref, xs_ref, meta_ref, info_ref, order_ref, rank_ref,
                  abuf, ashift, ubuf, ushift, w_in_bf, w_out_bf, stage, zbuf, cur_ref, fill_ref, nfree_ref,
                  ring_ref, sem, zsem, *, alpha, tile, t_pad, chunk, n_chunks, n_tiles, conf_w, sc_w_,
                  conf_k, sc_k, a_halo, u_halo):
    j = pl.program_id(1)
    k = pl.program_id(0) * pl.num_programs(1) + j
    slot = k % 2
    d = x1_ref.shape[-1]

    @pl.when(k == 0)
    def _():
        w_in_bf[...] = w_in_ref[...].astype(_BF16)
        w_out_bf[...] = w_out_ref[...].astype(_BF16)
        nfree_ref[0] = 0
        for g in range(N_GROUPS):
            cur_ref[g] = 0
            fill_ref[g] = chunk
        for c in range(info_ref.shape[0]):
            info_ref[c] = N_GROUPS - 1

    @pl.when(j == 0)
    def _():
        abuf[0:a_halo, :] = jnp.zeros((a_halo, conf_w), _F32)
        ubuf[0:u_halo, :] = jnp.zeros((u_halo, sc_w_), _F32)

    x0 = _layer_norm(x_ref[0], lng_ref[...], lnb_ref[...])
    h = _dot(x0.astype(_BF16), w_in_bf[...]) + b_in_ref[...]
    c0, c1, c2, c3 = conf_w, 2 * conf_w, 2 * conf_w + sc_w_, 2 * conf_w + 2 * sc_w_

    a = h[:, 0:c0] * _sigmoid(h[:, c0:c1])
    abuf[a_halo:a_halo + tile, :] = a
    u = h[:, c2:c3] * h[:, c3:]
    ubuf[u_halo:u_halo + tile, :] = u

    @pl.when(k >= 0)
    def _():
        for i in range(1, SUBLANES):
            ashift[i - 1] = abuf[i:i + tile + a_halo - SUBLANES, :]
        for t in range(sc_k - 1):
            off = u_halo - (sc_k - 1) + t
            ushift[t] = ubuf[off:off + tile, :]

    acc = jnp.broadcast_to(cb_ref[...], (tile, conf_w))
    for t in range(conf_k):
        q, i = divmod(a_halo - (conf_k - 1) + t, SUBLANES)
        if i == 0:
            src = abuf[SUBLANES * q:SUBLANES * q + tile, :]
        else:
            src = ashift[i - 1, SUBLANES * q:SUBLANES * q + tile, :]
        acc = acc + cw_ref[t:t + 1, :] * src
    abuf[0:a_halo, :] = abuf[tile:tile + a_halo, :]
    an = _layer_norm(acc, clg_ref[...], clb_ref[...])
    a2 = an * _sigmoid(an)

    sc = jnp.broadcast_to(sb_ref[...], (tile, sc_w_))
    for t in range(sc_k - 1):
        sc = sc + sw_ref[t:t + 1, :] * ushift[t]
    sc = sc + sw_ref[sc_k - 1:sc_k, :] * u
    ubuf[0:u_halo, :] = ubuf[tile:tile + u_halo, :]
    s = h[:, c1:c2] * sc

    mix = (_dot(a2.astype(_BF16), w_out_bf[0:conf_w, :])
           + _dot(s.astype(_BF16), w_out_bf[conf_w:, :]) + b_out_ref[...])
    x1 = _layer_norm(alpha * x0 + mix, l1g_ref[...], l1b_ref[...])
    x1_ref[0] = x1
    x1b = x1.astype(_BF16)

    logits = lax.dot_general(wr_ref[...], x1b, _NT, preferred_element_type=_F32) + br_ref[...]
    gidx, w = _route(logits)

    row8 = lax.broadcasted_iota(jnp.int32, (SUBLANES, tile), 0)
    onehot = jnp.where(gidx == row8, 1.0, 0.0)
    cnt = jnp.sum(onehot, axis=1, keepdims=True)
    sizes, starts = [], []
    start = jnp.int32(0)
    startv = jnp.zeros((SUBLANES, tile), jnp.int32)
    for g in range(N_GROUPS):
        n8 = (cnt[g, 0].astype(jnp.int32) + (SUBLANES - 1)) & (-SUBLANES)
        sizes.append(n8)
        starts.append(start)
        startv = jnp.where(row8 == g, start, startv)
        start = start + n8
    cum = _dot(onehot.astype(_BF16), tri_ref[...])
    pos = jnp.sum(onehot * (startv.astype(_F32) + cum), axis=0, keepdims=True).astype(jnp.int32)
    pos_ref[0] = pos
    perm = jnp.where(lax.broadcasted_iota(jnp.int32, (t_pad, tile), 0) == pos, 1.0, 0.0).astype(_BF16)

    w_hi = w.astype(_BF16).astype(_F32)
    w_lo = (w - w_hi).astype(_BF16).astype(_F32)
    w_hl = jnp.where(row8 < EXPERTS_PER_GROUP, w_hi, pltpu.roll(w_lo, EXPERTS_PER_GROUP, 0))
    w_rows = jnp.concatenate([w_hl, jnp.zeros((LANES - SUBLANES, tile), _F32)], axis=0).astype(_BF16)

    def wait_slot(sl):
        for g in range(N_GROUPS):
            for piece in range(2):
                m = ring_ref[sl, 2 * g + piece]

                @pl.when(m > 0)
                def _():
                    _rows(stage.at[sl], 0, xs_ref, 0, m, sem.at[sl]).wait()

    @pl.when(k >= 2)
    def _():
        wait_slot(slot)

    stage[slot, :, 0:d] = _dot(perm, x1b)
    stage[slot, :, d:] = lax.dot_general(perm, w_rows, _NT, preferred_element_type=_F32)

    for g in range(N_GROUPS):
        n8, sg = sizes[g], starts[g]
        fill, cur, newc = fill_ref[g], cur_ref[g], nfree_ref[0]
        m1 = jnp.minimum(n8, chunk - fill)
        m2 = n8 - m1
        r1 = cur * chunk + fill
        r2 = newc * chunk
        ring_ref[slot, 2 * g] = m1
        ring_ref[slot, 2 * g + 1] = m2
        for f, v in enumerate((r1, m1, r2, m2, sg)):
            meta_ref[k, g * META_FIELDS + f] = v

        @pl.when(m1 > 0)
        def _():
            _rows(stage.at[slot], sg, xs_ref, r1, m1, sem.at[slot]).start()

        @pl.when(m2 > 0)
        def _():
            _rows(stage.at[slot], sg + m1, xs_ref, r2, m2, sem.at[slot]).start()
            info_ref[newc] = g
            cur_ref[g] = newc
            nfree_ref[0] = newc + 1

        fill_ref[g] = jnp.where(m2 > 0, m2, fill + m1)

    @pl.when(k == n_tiles - 1)
    def _():
        zbuf[...] = jnp.zeros_like(zbuf)
        nfree = nfree_ref[0]
        info_ref[n_chunks] = nfree
        tails = []
        for g in range(N_GROUPS):
            rem = chunk - fill_ref[g]
            tails.append((rem, _rows(zbuf, 0, xs_ref, cur_ref[g] * chunk + fill_ref[g], rem, zsem)))
        spare = [(c, pltpu.make_async_copy(zbuf, xs_ref.at[pl.ds(c * chunk, chunk)], zsem))
                 for c in range(n_tiles * tile // chunk, n_chunks)]
        for rem, cp in tails:
            pl.when(rem > 0)(cp.start)
        for c, cp in spare:
            pl.when(c >= nfree)(cp.start)

        slot_out = jnp.int32(0)
        for g in range(N_GROUPS):
            def place(c, nxt, g=g):
                hit = info_ref[c] == g

                @pl.when(hit)
                def _():
                    order_ref[nxt] = c
                    rank_ref[c] = nxt

                return nxt + hit.astype(jnp.int32)

            slot_out = lax.fori_loop(0, n_chunks, place, slot_out)

        @pl.when(k >= 1)
        def _():
            wait_slot(1 - slot)

        wait_slot(slot)
        for rem, cp in tails:
            pl.when(rem > 0)(cp.wait)
        for c, cp in spare:
            pl.when(c >= nfree)(cp.wait)


def _moe_kernel(order_ref, info_ref, *refs, d, d_expert, n_chunks, per_step):
    xs_refs = refs[:per_step]
    wg_hbm, wu_hbm, wd_hbm, o_ref, wg_buf, wu_buf, wd_buf, slot_ref, wsem = refs[per_step:]
    chunk = xs_refs[0].shape[0]

    def weight_copies(g, sl):
        lo = g * EXPERTS_PER_GROUP
        return [pltpu.make_async_copy(src.at[pl.ds(lo, EXPERTS_PER_GROUP)], dst.at[sl], wsem)
                for src, dst in ((wg_hbm, wg_buf), (wu_hbm, wu_buf), (wd_hbm, wd_buf))]

    for ci, xs_ref in enumerate(xs_refs):
        i = pl.program_id(0) * per_step + ci
        rows = slice(ci * chunk, (ci + 1) * chunk)
        c = order_ref[i]
        grp = info_ref[c]
        prev = info_ref[order_ref[jnp.maximum(i - 1, 0)]]

        @pl.when(i == 0)
        def _():
            slot_ref[0] = 0
            for cp in weight_copies(grp, 0):
                cp.start()

        @pl.when((i == 0) | (grp != prev))
        def _():
            @pl.when(i > 0)
            def _():
                slot_ref[0] = 1 - slot_ref[0]

            cur = slot_ref[0]
            for cp in weight_copies(grp, cur):
                cp.wait()

            def later_group(j, found):
                gj = info_ref[order_ref[j]]
                return jnp.where((found < 0) & (j > i) & (gj != grp), gj, found)

            nxt = lax.fori_loop(0, n_chunks, later_group, jnp.int32(-1))

            @pl.when(nxt >= 0)
            def _():
                for cp in weight_copies(nxt, 1 - cur):
                    cp.start()

        used = c < info_ref[n_chunks]

        @pl.when(used)
        def _():
            xb = xs_ref[:, 0:d]
            ws = slot_ref[0]
            aux = xs_ref[:, d:]
            y = jnp.zeros((chunk, o_ref.shape[1]), _F32)
            for e in range(EXPERTS_PER_GROUP):
                hg = _dot(xb, wg_buf[ws, e])
                hu = _dot(xb, wu_buf[ws, e])
                we = aux[:, e:e + 1] + aux[:, EXPERTS_PER_GROUP + e:EXPERTS_PER_GROUP + e + 1]
                hid = hg * _sigmoid(hg) * hu * we
                y = y + _dot(hid, wd_buf[ws, e])
            o_ref[rows, :] = y

        @pl.when(jnp.logical_not(used))
        def _():
            o_ref[rows, :] = jnp.zeros((chunk, o_ref.shape[1]), _F32)


def _ple_kernel(meta_ref, rank_ref, x1_ref, p_ref, pos_ref, ys_ref, wpg_ref, bpg_ref, wpp_ref, l2g_ref, l2b_ref,
                o_ref, ybuf, sem, *, alpha, tile, chunk, n_sub, tiles_per_step, t_pad, n_steps):
    k = pl.program_id(0)
    slot = k % 2

    def moved(r):
        return rank_ref[r // chunk] * chunk + r % chunk

    def pieces(step, sl):
        out = []
        for ti in range(tiles_per_step):
            kk = step * tiles_per_step + ti
            for g in range(N_GROUPS):
                r1, m1, r2, m2, sg = (meta_ref[kk, g * META_FIELDS + f] for f in range(META_FIELDS))
                out.append((m1, _rows(ys_ref, moved(r1), ybuf.at[sl, ti], sg, m1, sem.at[sl])))
                out.append((m2, _rows(ys_ref, moved(r2), ybuf.at[sl, ti], sg + m1, m2, sem.at[sl])))
        return out

    def fetch(kk, sl):
        for m, cp in pieces(kk, sl):
            pl.when(m > 0)(cp.start)

    @pl.when(k == 0)
    def _():
        ybuf[...] = jnp.zeros_like(ybuf)
        fetch(0, 0)

    @pl.when(k + 1 < n_steps)
    def _():
        fetch(k + 1, 1 - slot)

    for m, cp in pieces(k, slot):
        pl.when(m > 0)(cp.wait)

    sub = tile // n_sub
    for ti in range(tiles_per_step):
        yb = ybuf[slot, ti]
        pos = pos_ref[ti]
        for blk in range(n_sub):
            cols = slice(blk * sub, (blk + 1) * sub)
            rows = slice(ti * tile + blk * sub, ti * tile + (blk + 1) * sub)
            perm = jnp.where(lax.broadcasted_iota(jnp.int32, (t_pad, sub), 0) == pos[:, cols], 1.0, 0.0)
            y = lax.dot_general(perm, yb, _TN, preferred_element_type=_F32)
            r = alpha * x1_ref[rows, :] + y
            gate = _sigmoid(_dot(r, wpg_ref[...]) + bpg_ref[...])
            pp = _dot(p_ref[rows, :], wpp_ref[...])
            o_ref[rows, :] = _layer_norm(r + gate * pp, l2g_ref[...], l2b_ref[...])


def _full(shape):
    return pl.BlockSpec(shape, lambda *_: (0,) * len(shape))


def _resident(shape):
    return pl.BlockSpec(shape, lambda *_: (0,) * len(shape), pipeline_mode=pl.Buffered(1))


def _row(v):
    return v.reshape(1, -1).astype(_F32)


def kernel(x, p, ln_in_g, ln_in_b, w_in, b_in, conf_dw_w, conf_dw_b, conf_ln_g, conf_ln_b, sc_w, sc_b, w_out, b_out, ln1_g, ln1_b, w_rg, b_rg, w_re, b_re, w_gate, w_up, w_down, w_pg, b_pg, w_pp, ln2_g, ln2_b):
    depth = w_in.shape[0]
    assert depth == 1, "single-layer block"
    alpha = (2.0 * depth) ** 0.25
    bsz, seq, d = x.shape
    n = bsz * seq
    conf_k, conf_w = conf_dw_w.shape[1:]
    sc_k, sc_w_ = sc_w.shape[1:]
    d_in = w_in.shape[2]
    n_exp, _, d_expert = w_gate.shape[1:]
    assert n_exp == N_GROUPS * EXPERTS_PER_GROUP
    assert d_in == 2 * conf_w + 3 * sc_w_ and w_out.shape[1] == conf_w + sc_w_
    d_ple = p.shape[-1]

    tile = chunk = TILE
    assert seq % tile == 0 and (bsz * (seq // tile)) % PLE_TILES == 0
    nt = seq // tile
    n_tiles = bsz * nt
    a_halo = -(-(conf_k - 1) // SUBLANES) * SUBLANES
    u_halo = -(-(sc_k - 1) // SUBLANES) * SUBLANES
    t_pad = tile + N_GROUPS * SUBLANES
    n_chunks = -(-(n + (SUBLANES - 1) * N_GROUPS * n_tiles) // chunk) + N_GROUPS
    n_chunks = -(-n_chunks // MOE_CHUNKS) * MOE_CHUNKS
    d_pay = d + LANES

    n_logits = N_GROUPS + n_exp
    wr = jnp.pad(jnp.concatenate([w_rg[0], w_re[0]], axis=1).T, ((0, ROUTER_ROWS - n_logits), (0, 0)))
    br = jnp.pad(jnp.concatenate([b_rg[0], b_re[0]]), (0, ROUTER_ROWS - n_logits))
    tri = (jnp.arange(tile)[:, None] < jnp.arange(tile)[None, :]).astype(_BF16)

    mixer = pl.pallas_call(
        functools.partial(_mixer_kernel, alpha=alpha, tile=tile, t_pad=t_pad, chunk=chunk,
                          n_chunks=n_chunks, n_tiles=n_tiles, conf_w=conf_w, sc_w_=sc_w_,
                          conf_k=conf_k, sc_k=sc_k, a_halo=a_halo, u_halo=u_halo),
        grid=(bsz, nt),
        in_specs=[
            pl.BlockSpec((1, tile, d), lambda b, j: (b, j, 0)),
            _full((1, d)), _full((1, d)),
            _resident((d, d_in)), _full((1, d_in)),
            _full((conf_k, conf_w)), _full((1, conf_w)), _full((1, conf_w)), _full((1, conf_w)),
            _full((sc_k, sc_w_)), _full((1, sc_w_)),
            _resident((conf_w + sc_w_, d)), _full((1, d)),
            _full((1, d)), _full((1, d)),
            _full((ROUTER_ROWS, d)), _full((ROUTER_ROWS, 1)),
            _resident((tile, tile)),
        ],
        out_specs=[
            pl.BlockSpec((1, tile, d), lambda b, j: (b, j, 0)),
            pl.BlockSpec((1, 1, tile), lambda b, j: (b * nt + j, 0, 0)),
            pl.BlockSpec(memory_space=pl.ANY),
            pl.BlockSpec(memory_space=pltpu.SMEM),
            pl.BlockSpec(memory_space=pltpu.SMEM),
            pl.BlockSpec(memory_space=pltpu.SMEM),
            pl.BlockSpec(memory_space=pltpu.SMEM),
        ],
        out_shape=[
            jax.ShapeDtypeStruct((bsz, seq, d), _F32),
            jax.ShapeDtypeStruct((n_tiles, 1, tile), jnp.int32),
            jax.ShapeDtypeStruct((n_chunks * chunk, d_pay), _F32),
            jax.ShapeDtypeStruct((n_tiles, N_GROUPS * META_FIELDS), jnp.int32),
            jax.ShapeDtypeStruct((n_chunks + 1,), jnp.int32),
            jax.ShapeDtypeStruct((n_chunks,), jnp.int32),
            jax.ShapeDtypeStruct((n_chunks,), jnp.int32),
        ],
        scratch_shapes=[
            pltpu.VMEM((a_halo + tile, conf_w), _F32),
            pltpu.VMEM((SUBLANES - 1, a_halo + tile - SUBLANES, conf_w), _F32),
            pltpu.VMEM((u_halo + tile, sc_w_), _F32),
            pltpu.VMEM((sc_k - 1, tile, sc_w_), _F32),
            pltpu.VMEM((d, d_in), _BF16),
            pltpu.VMEM((conf_w + sc_w_, d), _BF16),
            pltpu.VMEM((2, t_pad, d_pay), _F32),
            pltpu.VMEM((chunk, d_pay), _F32),
            pltpu.SMEM((N_GROUPS,), jnp.int32),
            pltpu.SMEM((N_GROUPS,), jnp.int32),
            pltpu.SMEM((1,), jnp.int32),
            pltpu.SMEM((2, 2 * N_GROUPS), jnp.int32),
            pltpu.SemaphoreType.DMA((2,)),
            pltpu.SemaphoreType.DMA(()),
        ],
        compiler_params=pltpu.CompilerParams(
            dimension_semantics=("arbitrary", "arbitrary"), vmem_limit_bytes=VMEM_LIMIT_BYTES),
        name="mixer",
    )
    x1, pos, xs, meta, info, order, rank = mixer(
        x, _row(ln_in_g), _row(ln_in_b), w_in[0], _row(b_in[0]),
        conf_dw_w[0], _row(conf_dw_b[0]), _row(conf_ln_g[0]), _row(conf_ln_b[0]),
        sc_w[0], _row(sc_b[0]), w_out[0], _row(b_out[0]),
        _row(ln1_g[0]), _row(ln1_b[0]), wr.astype(_BF16), br.reshape(ROUTER_ROWS, 1), tri)

    moe = pl.pallas_call(
        functools.partial(_moe_kernel, d=d, d_expert=d_expert, n_chunks=n_chunks, per_step=MOE_CHUNKS),
        grid_spec=pltpu.PrefetchScalarGridSpec(
            num_scalar_prefetch=2,
            grid=(n_chunks // MOE_CHUNKS,),
            in_specs=[
                pl.BlockSpec((chunk, d_pay), functools.partial(
                    lambda i, order, info, ci: (order[i * MOE_CHUNKS + ci], 0), ci=ci))
                for ci in range(MOE_CHUNKS)
            ] + [
                pl.BlockSpec(memory_space=pl.ANY),
                pl.BlockSpec(memory_space=pl.ANY),
                pl.BlockSpec(memory_space=pl.ANY),
            ],
            out_specs=pl.BlockSpec((MOE_CHUNKS * chunk, d), lambda i, order, info: (i, 0)),
            scratch_shapes=[
                pltpu.VMEM((2, EXPERTS_PER_GROUP, d, d_expert), _F32),
                pltpu.VMEM((2, EXPERTS_PER_GROUP, d, d_expert), _F32),
                pltpu.VMEM((2, EXPERTS_PER_GROUP, d_expert, d), _F32),
                pltpu.SMEM((1,), jnp.int32),
                pltpu.SemaphoreType.DMA(()),
            ],
        ),
        out_shape=jax.ShapeDtypeStruct((n_chunks * chunk, d), _F32),
        compiler_params=pltpu.CompilerParams(
            dimension_semantics=("arbitrary",), vmem_limit_bytes=VMEM_LIMIT_BYTES),
        name="moe",
    )
    ys = moe(order, info, *([xs] * MOE_CHUNKS), w_gate[0], w_up[0], w_down[0])

    ple = pl.pallas_call(
        functools.partial(_ple_kernel, alpha=alpha, tile=tile, chunk=chunk, n_sub=N_SUB, tiles_per_step=PLE_TILES,
                          t_pad=t_pad, n_steps=n_tiles // PLE_TILES),
        grid_spec=pltpu.PrefetchScalarGridSpec(
            num_scalar_prefetch=2,
            grid=(n_tiles // PLE_TILES,),
            in_specs=[
                pl.BlockSpec((PLE_TILES * tile, d), lambda k, meta, rank: (k, 0)),
                pl.BlockSpec((PLE_TILES * tile, d_ple), lambda k, meta, rank: (k, 0)),
                pl.BlockSpec((PLE_TILES, 1, tile), lambda k, meta, rank: (k, 0, 0)),
                pl.BlockSpec(memory_space=pl.ANY),
                _resident((d, d)), _full((1, d)), _resident((d_ple, d)), _full((1, d)), _full((1, d)),
            ],
            out_specs=pl.BlockSpec((PLE_TILES * tile, d), lambda k, meta, rank: (k, 0)),
            scratch_shapes=[
                pltpu.VMEM((2, PLE_TILES, t_pad, d), _F32),
                pltpu.SemaphoreType.DMA((2,)),
            ],
        ),
        out_shape=jax.ShapeDtypeStruct((n, d), _F32),
        compiler_params=pltpu.CompilerParams(
            dimension_semantics=("arbitrary",), vmem_limit_bytes=VMEM_LIMIT_BYTES),
        name="ple",
    )
    out = ple(meta, rank, x1.reshape(n, d), p[0].reshape(n, d_ple), pos, ys,
              w_pg[0], _row(b_pg[0]), w_pp[0], _row(ln2_g[0]), _row(ln2_b[0]))
    return out.reshape(bsz, seq, d)
```

```python
import functools

import jax
import jax.numpy as jnp
from jax import lax
from jax.experimental import pallas as pl
from jax.experimental.pallas import tpu as pltpu

LN_EPS = 1e-5
N_GROUPS = 4
EXPERTS_PER_GROUP = 4
SUBLANES = 8
LANES = 128
ROUTER_ROWS = 32
TILE = 512
MOE_CHUNKS = 2
PLE_TILES = 2
N_SUB = 2
VMEM_LIMIT_BYTES = 56 * 1024 * 1024
NEG_BIG = -1e30
META_FIELDS = 5

_F32 = jnp.float32
_BF16 = jnp.bfloat16
_NT = (((1,), (1,)), ((), ()))
_TN = (((0,), (0,)), ((), ()))


def _layer_norm(x, g, b):
    mu = jnp.mean(x, axis=-1, keepdims=True)
    xc = x - mu
    var = jnp.mean(xc * xc, axis=-1, keepdims=True)
    return xc * lax.rsqrt(var + LN_EPS) * g + b


def _sigmoid(x):
    return 0.5 * jnp.tanh(0.5 * x) + 0.5


def _dot(a, b):
    return jnp.dot(a, b, preferred_element_type=_F32)


def _aligned(v):
    return v if isinstance(v, int) else pl.multiple_of(v, SUBLANES)


def _rows(src, s0, dst, d0, m, sem):
    return pltpu.make_async_copy(src.at[pl.ds(_aligned(s0), _aligned(m))],
                                 dst.at[pl.ds(_aligned(d0), _aligned(m))], sem)


def _route(logits):
    row = lax.broadcasted_iota(jnp.int32, (SUBLANES, logits.shape[1]), 0)
    real = row < N_GROUPS
    gl = jnp.where(real, logits[0:SUBLANES], NEG_BIG)
    gm = jnp.max(gl, axis=0, keepdims=True)
    ge = jnp.exp(gl - gm)
    gp = ge / jnp.sum(ge, axis=0, keepdims=True)
    gp_top = jnp.max(gp, axis=0, keepdims=True)
    gidx = jnp.min(jnp.where(gp == gp_top, row, SUBLANES), axis=0, keepdims=True)

    el = jnp.zeros_like(gl)
    for g in range(N_GROUPS):
        lo = N_GROUPS + g * EXPERTS_PER_GROUP
        blk = logits[lo - lo % SUBLANES:lo - lo % SUBLANES + SUBLANES]
        if lo % SUBLANES:
            blk = pltpu.roll(blk, SUBLANES - lo % SUBLANES, 0)
        el = jnp.where(gidx == g, blk, el)
    el = jnp.where(real, el, NEG_BIG)
    em = jnp.max(el, axis=0, keepdims=True)
    ee = jnp.exp(el - em)
    ep = ee / jnp.sum(ee, axis=0, keepdims=True)
    ep = jnp.where(real, ep, -1.0)
    p1 = jnp.max(ep, axis=0, keepdims=True)
    i1 = jnp.min(jnp.where(ep == p1, row, SUBLANES), axis=0, keepdims=True)
    ep2 = jnp.where(row == i1, -1.0, ep)
    p2 = jnp.max(ep2, axis=0, keepdims=True)
    i2 = jnp.min(jnp.where(ep2 == p2, row, SUBLANES), axis=0, keepdims=True)
    denom = p1 + p2
    w = jnp.where(row == i1, p1 / denom, jnp.where(row == i2, p2 / denom, 0.0))
    return gidx, w * gp_top


def _mixer_kernel(x_ref, lng_ref, lnb_ref, w_in_ref, b_in_ref, cw_ref, cb_ref, clg_ref, clb_ref,
                  sw_ref, sb_ref, w_out_ref, b_out_ref, l1g_ref, l1b_ref, wr_ref, br_ref, tri_ref,
                  x1_ref, pos_ref, xs_ref, meta_ref, info_ref, order_ref, rank_ref,
                  abuf, ashift, ubuf, stage, zbuf, cur_ref, fill_ref, nfree_ref,
                  ring_ref, sem, zsem, *, alpha, tile, t_pad, chunk, n_chunks, n_tiles, conf_w, sc_w_,
                  conf_k, sc_k, a_halo, u_halo):
    j = pl.program_id(1)
    k = pl.program_id(0) * pl.num_programs(1) + j
    slot = k % 2
    d = x1_ref.shape[-1]

    @pl.when(k == 0)
    def _():
        nfree_ref[0] = 0
        for g in range(N_GROUPS):
            cur_ref[g] = 0
            fill_ref[g] = chunk
        for c in range(info_ref.shape[0]):
            info_ref[c] = N_GROUPS - 1

    @pl.when(j == 0)
    def _():
        abuf[0:a_halo, :] = jnp.zeros((a_halo, conf_w), _F32)
        ubuf[0:u_halo, :] = jnp.zeros((u_halo, sc_w_), _F32)

    x0 = _layer_norm(x_ref[0], lng_ref[...], lnb_ref[...])
    h = _dot(x0.astype(_BF16), w_in_ref[...]) + b_in_ref[...]
    c0, c1, c2, c3 = conf_w, 2 * conf_w, 2 * conf_w + sc_w_, 2 * conf_w + 2 * sc_w_

    a = h[:, 0:c0] * _sigmoid(h[:, c0:c1])
    abuf[a_halo:a_halo + tile, :] = a
    for i in range(1, SUBLANES):
        ashift[i - 1] = abuf[i:i + tile + a_halo - SUBLANES, :]
    acc = jnp.broadcast_to(cb_ref[...], (tile, conf_w))
    for t in range(conf_k):
        q, i = divmod(a_halo - (conf_k - 1) + t, SUBLANES)
        if i == 0:
            src = abuf[SUBLANES * q:SUBLANES * q + tile, :]
        else:
            src = ashift[i - 1, SUBLANES * q:SUBLANES * q + tile, :]
        acc = acc + cw_ref[t:t + 1, :] * src
    abuf[0:a_halo, :] = abuf[tile:tile + a_halo, :]
    an = _layer_norm(acc, clg_ref[...], clb_ref[...])
    a2 = an * _sigmoid(an)

    u = h[:, c2:c3] * h[:, c3:]
    ubuf[u_halo:u_halo + tile, :] = u
    sc = jnp.broadcast_to(sb_ref[...], (tile, sc_w_))
    for t in range(sc_k):
        off = u_halo - (sc_k - 1) + t
        sc = sc + sw_ref[t:t + 1, :] * ubuf[off:off + tile, :]
    ubuf[0:u_halo, :] = ubuf[tile:tile + u_halo, :]
    s = h[:, c1:c2] * sc

    mix = (_dot(a2.astype(_BF16), w_out_ref[0:conf_w, :])
           + _dot(s.astype(_BF16), w_out_ref[conf_w:, :]) + b_out_ref[...])
    x1 = _layer_norm(alpha * x0 + mix, l1g_ref[...], l1b_ref[...])
    x1_ref[0] = x1
    x1b = x1.astype(_BF16)

    logits = lax.dot_general(wr_ref[...], x1b, _NT, preferred_element_type=_F32) + br_ref[...]
    gidx, w = _route(logits)

    row8 = lax.broadcasted_iota(jnp.int32, (SUBLANES, tile), 0)
    onehot = jnp.where(gidx == row8, 1.0, 0.0)
    cnt = jnp.sum(onehot, axis=1, keepdims=True)
    sizes, starts = [], []
    start = jnp.int32(0)
    startv = jnp.zeros((SUBLANES, tile), jnp.int32)
    for g in range(N_GROUPS):
        n8 = (cnt[g, 0].astype(jnp.int32) + (SUBLANES - 1)) & (-SUBLANES)
        sizes.append(n8)
        starts.append(start)
        startv = jnp.where(row8 == g, start, startv)
        start = start + n8
    cum = _dot(onehot.astype(_BF16), tri_ref[...])
    pos = jnp.sum(onehot * (startv.astype(_F32) + cum), axis=0, keepdims=True).astype(jnp.int32)
    pos_ref[0] = pos
    perm = jnp.where(lax.broadcasted_iota(jnp.int32, (t_pad, tile), 0) == pos, 1.0, 0.0).astype(_BF16)

    w_hi = w.astype(_BF16).astype(_F32)
    w_lo = (w - w_hi).astype(_BF16).astype(_F32)
    w_hl = jnp.where(row8 < EXPERTS_PER_GROUP, w_hi, pltpu.roll(w_lo, EXPERTS_PER_GROUP, 0))
    w_rows = jnp.concatenate([w_hl, jnp.zeros((LANES - SUBLANES, tile), _F32)], axis=0).astype(_BF16)

    def wait_slot(sl):
        for g in range(N_GROUPS):
            for piece in range(2):
                m = ring_ref[sl, 2 * g + piece]

                @pl.when(m > 0)
                def _():
                    _rows(stage.at[sl], 0, xs_ref, 0, m, sem.at[sl]).wait()

    @pl.when(k >= 2)
    def _():
        wait_slot(slot)

    stage[slot, :, 0:d] = _dot(perm, x1b)
    stage[slot, :, d:] = lax.dot_general(perm, w_rows, _NT, preferred_element_type=_F32)

    for g in range(N_GROUPS):
        n8, sg = sizes[g], starts[g]
        fill, cur, newc = fill_ref[g], cur_ref[g], nfree_ref[0]
        m1 = jnp.minimum(n8, chunk - fill)
        m2 = n8 - m1
        r1 = cur * chunk + fill
        r2 = newc * chunk
        ring_ref[slot, 2 * g] = m1
        ring_ref[slot, 2 * g + 1] = m2
        for f, v in enumerate((r1, m1, r2, m2, sg)):
            meta_ref[k, g * META_FIELDS + f] = v

        @pl.when(m1 > 0)
        def _():
            _rows(stage.at[slot], sg, xs_ref, r1, m1, sem.at[slot]).start()

        @pl.when(m2 > 0)
        def _():
            _rows(stage.at[slot], sg + m1, xs_ref, r2, m2, sem.at[slot]).start()
            info_ref[newc] = g
            cur_ref[g] = newc
            nfree_ref[0] = newc + 1

        fill_ref[g] = jnp.where(m2 > 0, m2, fill + m1)

    @pl.when(k == n_tiles - 1)
    def _():
        zbuf[...] = jnp.zeros_like(zbuf)
        nfree = nfree_ref[0]
        info_ref[n_chunks] = nfree
        tails = []
        for g in range(N_GROUPS):
            rem = chunk - fill_ref[g]
            tails.append((rem, _rows(zbuf, 0, xs_ref, cur_ref[g] * chunk + fill_ref[g], rem, zsem)))
        spare = [(c, pltpu.make_async_copy(zbuf, xs_ref.at[pl.ds(c * chunk, chunk)], zsem))
                 for c in range(n_tiles * tile // chunk, n_chunks)]
        for rem, cp in tails:
            pl.when(rem > 0)(cp.start)
        for c, cp in spare:
            pl.when(c >= nfree)(cp.start)

        slot_out = jnp.int32(0)
        for g in range(N_GROUPS):
            def place(c, nxt, g=g):
                hit = info_ref[c] == g

                @pl.when(hit)
                def _():
                    order_ref[nxt] = c
                    rank_ref[c] = nxt

                return nxt + hit.astype(jnp.int32)

            slot_out = lax.fori_loop(0, n_chunks, place, slot_out)

        @pl.when(k >= 1)
        def _():
            wait_slot(1 - slot)

        wait_slot(slot)
        for rem, cp in tails:
            pl.when(rem > 0)(cp.wait)
        for c, cp in spare:
            pl.when(c >= nfree)(cp.wait)


def _moe_kernel(order_ref, info_ref, *refs, d, d_expert, n_chunks, per_step):
    xs_refs = refs[:per_step]
    wg_hbm, wu_hbm, wd_hbm, o_ref, wg_buf, wu_buf, wd_buf, slot_ref, wsem = refs[per_step:]
    chunk = xs_refs[0].shape[0]

    def weight_copies(g, sl):
        lo = g * EXPERTS_PER_GROUP
        return [pltpu.make_async_copy(src.at[pl.ds(lo, EXPERTS_PER_GROUP)], dst.at[sl], wsem)
                for src, dst in ((wg_hbm, wg_buf), (wu_hbm, wu_buf), (wd_hbm, wd_buf))]

    for ci, xs_ref in enumerate(xs_refs):
        i = pl.program_id(0) * per_step + ci
        rows = slice(ci * chunk, (ci + 1) * chunk)
        c = order_ref[i]
        grp = info_ref[c]
        prev = info_ref[order_ref[jnp.maximum(i - 1, 0)]]

        @pl.when(i == 0)
        def _():
            slot_ref[0] = 0
            for cp in weight_copies(grp, 0):
                cp.start()

        @pl.when((i == 0) | (grp != prev))
        def _():
            @pl.when(i > 0)
            def _():
                slot_ref[0] = 1 - slot_ref[0]

            cur = slot_ref[0]
            for cp in weight_copies(grp, cur):
                cp.wait()

            def later_group(j, found):
                gj = info_ref[order_ref[j]]
                return jnp.where((found < 0) & (j > i) & (gj != grp), gj, found)

            nxt = lax.fori_loop(0, n_chunks, later_group, jnp.int32(-1))

            @pl.when(nxt >= 0)
            def _():
                for cp in weight_copies(nxt, 1 - cur):
                    cp.start()

        used = c < info_ref[n_chunks]

        @pl.when(used)
        def _():
            xb = xs_ref[:, 0:d]
            ws = slot_ref[0]
            aux = xs_ref[:, d:]
            y = jnp.zeros((chunk, o_ref.shape[1]), _F32)
            for e in range(EXPERTS_PER_GROUP):
                hg = _dot(xb, wg_buf[ws, e])
                hu = _dot(xb, wu_buf[ws, e])
                we = aux[:, e:e + 1] + aux[:, EXPERTS_PER_GROUP + e:EXPERTS_PER_GROUP + e + 1]
                hid = hg * _sigmoid(hg) * hu * we
                y = y + _dot(hid, wd_buf[ws, e])
            o_ref[rows, :] = y

        @pl.when(jnp.logical_not(used))
        def _():
            o_ref[rows, :] = jnp.zeros((chunk, o_ref.shape[1]), _F32)


def _ple_kernel(meta_ref, rank_ref, x1_ref, p_ref, pos_ref, ys_ref, wpg_ref, bpg_ref, wpp_ref, l2g_ref, l2b_ref,
                o_ref, ybuf, sem, *, alpha, tile, chunk, n_sub, tiles_per_step, t_pad, n_steps):
    k = pl.program_id(0)
    slot = k % 2

    def moved(r):
        return rank_ref[r // chunk] * chunk + r % chunk

    def pieces(step, sl):
        out = []
        for ti in range(tiles_per_step):
            kk = step * tiles_per_step + ti
            for g in range(N_GROUPS):
                r1, m1, r2, m2, sg = (meta_ref[kk, g * META_FIELDS + f] for f in range(META_FIELDS))
                out.append((m1, _rows(ys_ref, moved(r1), ybuf.at[sl, ti], sg, m1, sem.at[sl])))
                out.append((m2, _rows(ys_ref, moved(r2), ybuf.at[sl, ti], sg + m1, m2, sem.at[sl])))
        return out

    def fetch(kk, sl):
        for m, cp in pieces(kk, sl):
            pl.when(m > 0)(cp.start)

    @pl.when(k == 0)
    def _():
        ybuf[...] = jnp.zeros_like(ybuf)
        fetch(0, 0)

    @pl.when(k + 1 < n_steps)
    def _():
        fetch(k + 1, 1 - slot)

    for m, cp in pieces(k, slot):
        pl.when(m > 0)(cp.wait)

    sub = tile // n_sub
    for ti in range(tiles_per_step):
        yb = ybuf[slot, ti]
        pos = pos_ref[ti]
        for blk in range(n_sub):
            cols = slice(blk * sub, (blk + 1) * sub)
            rows = slice(ti * tile + blk * sub, ti * tile + (blk + 1) * sub)
            perm = jnp.where(lax.broadcasted_iota(jnp.int32, (t_pad, sub), 0) == pos[:, cols], 1.0, 0.0)
            y = lax.dot_general(perm, yb, _TN, preferred_element_type=_F32)
            r = alpha * x1_ref[rows, :] + y
            gate = _sigmoid(_dot(r, wpg_ref[...]) + bpg_ref[...])
            pp = _dot(p_ref[rows, :], wpp_ref[...])
            o_ref[rows, :] = _layer_norm(r + gate * pp, l2g_ref[...], l2b_ref[...])


def _full(shape):
    return pl.BlockSpec(shape, lambda *_: (0,) * len(shape))


def _resident(shape):
    return pl.BlockSpec(shape, lambda *_: (0,) * len(shape), pipeline_mode=pl.Buffered(1))


def _row(v):
    return v.reshape(1, -1).astype(_F32)


def kernel(x, p, ln_in_g, ln_in_b, w_in, b_in, conf_dw_w, conf_dw_b, conf_ln_g, conf_ln_b, sc_w, sc_b, w_out, b_out, ln1_g, ln1_b, w_rg, b_rg, w_re, b_re, w_gate, w_up, w_down, w_pg, b_pg, w_pp, ln2_g, ln2_b):
    depth = w_in.shape[0]
    assert depth == 1, "single-layer block"
    alpha = (2.0 * depth) ** 0.25
    bsz, seq, d = x.shape
    n = bsz * seq
    conf_k, conf_w = conf_dw_w.shape[1:]
    sc_k, sc_w_ = sc_w.shape[1:]
    d_in = w_in.shape[2]
    n_exp, _, d_expert = w_gate.shape[1:]
    assert n_exp == N_GROUPS * EXPERTS_PER_GROUP
    assert d_in == 2 * conf_w + 3 * sc_w_ and w_out.shape[1] == conf_w + sc_w_
    d_ple = p.shape[-1]

    tile = chunk = TILE
    assert seq % tile == 0 and (bsz * (seq // tile)) % PLE_TILES == 0
    nt = seq // tile
    n_tiles = bsz * nt
    a_halo = -(-(conf_k - 1) // SUBLANES) * SUBLANES
    u_halo = -(-(sc_k - 1) // SUBLANES) * SUBLANES
    t_pad = tile + N_GROUPS * SUBLANES
    n_chunks = -(-(n + (SUBLANES - 1) * N_GROUPS * n_tiles) // chunk) + N_GROUPS
    n_chunks = -(-n_chunks // MOE_CHUNKS) * MOE_CHUNKS
    d_pay = d + LANES

    n_logits = N_GROUPS + n_exp
    wr = jnp.pad(jnp.concatenate([w_rg[0], w_re[0]], axis=1).T, ((0, ROUTER_ROWS - n_logits), (0, 0)))
    br = jnp.pad(jnp.concatenate([b_rg[0], b_re[0]]), (0, ROUTER_ROWS - n_logits))
    tri = (jnp.arange(tile)[:, None] < jnp.arange(tile)[None, :]).astype(_BF16)

    mixer = pl.pallas_call(
        functools.partial(_mixer_kernel, alpha=alpha, tile=tile, t_pad=t_pad, chunk=chunk,
                          n_chunks=n_chunks, n_tiles=n_tiles, conf_w=conf_w, sc_w_=sc_w_,
                          conf_k=conf_k, sc_k=sc_k, a_halo=a_halo, u_halo=u_halo),
        grid=(bsz, nt),
        in_specs=[
            pl.BlockSpec((1, tile, d), lambda b, j: (b, j, 0)),
            _full((1, d)), _full((1, d)),
            _resident((d, d_in)), _full((1, d_in)),
            _full((conf_k, conf_w)), _full((1, conf_w)), _full((1, conf_w)), _full((1, conf_w)),
            _full((sc_k, sc_w_)), _full((1, sc_w_)),
            _resident((conf_w + sc_w_, d)), _full((1, d)),
            _full((1, d)), _full((1, d)),
            _full((ROUTER_ROWS, d)), _full((ROUTER_ROWS, 1)),
            _resident((tile, tile)),
        ],
        out_specs=[
            pl.BlockSpec((1, tile, d), lambda b, j: (b, j, 0)),
            pl.BlockSpec((1, 1, tile), lambda b, j: (b * nt + j, 0, 0)),
            pl.BlockSpec(memory_space=pl.ANY),
            pl.BlockSpec(memory_space=pltpu.SMEM),
            pl.BlockSpec(memory_space=pltpu.SMEM),
            pl.BlockSpec(memory_space=pltpu.SMEM),
            pl.BlockSpec(memory_space=pltpu.SMEM),
        ],
        out_shape=[
            jax.ShapeDtypeStruct((bsz, seq, d), _F32),
            jax.ShapeDtypeStruct((n_tiles, 1, tile), jnp.int32),
            jax.ShapeDtypeStruct((n_chunks * chunk, d_pay), _F32),
            jax.ShapeDtypeStruct((n_tiles, N_GROUPS * META_FIELDS), jnp.int32),
            jax.ShapeDtypeStruct((n_chunks + 1,), jnp.int32),
            jax.ShapeDtypeStruct((n_chunks,), jnp.int32),
            jax.ShapeDtypeStruct((n_chunks,), jnp.int32),
        ],
        scratch_shapes=[
            pltpu.VMEM((a_halo + tile, conf_w), _F32),
            pltpu.VMEM((SUBLANES - 1, a_halo + tile - SUBLANES, conf_w), _F32),
            pltpu.VMEM((u_halo + tile, sc_w_), _F32),
            pltpu.VMEM((2, t_pad, d_pay), _F32),
            pltpu.VMEM((chunk, d_pay), _F32),
            pltpu.SMEM((N_GROUPS,), jnp.int32),
            pltpu.SMEM((N_GROUPS,), jnp.int32),
            pltpu.SMEM((1,), jnp.int32),
            pltpu.SMEM((2, 2 * N_GROUPS), jnp.int32),
            pltpu.SemaphoreType.DMA((2,)),
            pltpu.SemaphoreType.DMA(()),
        ],
        compiler_params=pltpu.CompilerParams(
            dimension_semantics=("arbitrary", "arbitrary"), vmem_limit_bytes=VMEM_LIMIT_BYTES),
        name="mixer",
    )
    x1, pos, xs, meta, info, order, rank = mixer(
        x, _row(ln_in_g), _row(ln_in_b), w_in[0].astype(_BF16), _row(b_in[0]),
        conf_dw_w[0], _row(conf_dw_b[0]), _row(conf_ln_g[0]), _row(conf_ln_b[0]),
        sc_w[0], _row(sc_b[0]), w_out[0].astype(_BF16), _row(b_out[0]),
        _row(ln1_g[0]), _row(ln1_b[0]), wr.astype(_BF16), br.reshape(ROUTER_ROWS, 1), tri)

    moe = pl.pallas_call(
        functools.partial(_moe_kernel, d=d, d_expert=d_expert, n_chunks=n_chunks, per_step=MOE_CHUNKS),
        grid_spec=pltpu.PrefetchScalarGridSpec(
            num_scalar_prefetch=2,
            grid=(n_chunks // MOE_CHUNKS,),
            in_specs=[
                pl.BlockSpec((chunk, d_pay), functools.partial(
                    lambda i, order, info, ci: (order[i * MOE_CHUNKS + ci], 0), ci=ci))
                for ci in range(MOE_CHUNKS)
            ] + [
                pl.BlockSpec(memory_space=pl.ANY),
                pl.BlockSpec(memory_space=pl.ANY),
                pl.BlockSpec(memory_space=pl.ANY),
            ],
            out_specs=pl.BlockSpec((MOE_CHUNKS * chunk, d), lambda i, order, info: (i, 0)),
            scratch_shapes=[
                pltpu.VMEM((2, EXPERTS_PER_GROUP, d, d_expert), _F32),
                pltpu.VMEM((2, EXPERTS_PER_GROUP, d, d_expert), _F32),
                pltpu.VMEM((2, EXPERTS_PER_GROUP, d_expert, d), _F32),
                pltpu.SMEM((1,), jnp.int32),
                pltpu.SemaphoreType.DMA(()),
            ],
        ),
        out_shape=jax.ShapeDtypeStruct((n_chunks * chunk, d), _F32),
        compiler_params=pltpu.CompilerParams(
            dimension_semantics=("arbitrary",), vmem_limit_bytes=VMEM_LIMIT_BYTES),
        name="moe",
    )
    ys = moe(order, info, *([xs] * MOE_CHUNKS), w_gate[0], w_up[0], w_down[0])

    ple = pl.pallas_call(
        functools.partial(_ple_kernel, alpha=alpha, tile=tile, chunk=chunk, n_sub=N_SUB, tiles_per_step=PLE_TILES,
                          t_pad=t_pad, n_steps=n_tiles // PLE_TILES),
        grid_spec=pltpu.PrefetchScalarGridSpec(
            num_scalar_prefetch=2,
            grid=(n_tiles // PLE_TILES,),
            in_specs=[
                pl.BlockSpec((PLE_TILES * tile, d), lambda k, meta, rank: (k, 0)),
                pl.BlockSpec((PLE_TILES * tile, d_ple), lambda k, meta, rank: (k, 0)),
                pl.BlockSpec((PLE_TILES, 1, tile), lambda k, meta, rank: (k, 0, 0)),
                pl.BlockSpec(memory_space=pl.ANY),
                _resident((d, d)), _full((1, d)), _resident((d_ple, d)), _full((1, d)), _full((1, d)),
            ],
            out_specs=pl.BlockSpec((PLE_TILES * tile, d), lambda k, meta, rank: (k, 0)),
            scratch_shapes=[
                pltpu.VMEM((2, PLE_TILES, t_pad, d), _F32),
                pltpu.SemaphoreType.DMA((2,)),
            ],
        ),
        out_shape=jax.ShapeDtypeStruct((n, d), _F32),
        compiler_params=pltpu.CompilerParams(
            dimension_semantics=("arbitrary",), vmem_limit_bytes=VMEM_LIMIT_BYTES),
        name="ple",
    )
    out = ple(meta, rank, x1.reshape(n, d), p[0].reshape(n, d_ple), pos, ys,
              w_pg[0], _row(b_pg[0]), w_pp[0], _row(ln2_g[0]), _row(ln2_b[0]))
    return out.reshape(bsz, seq, d)
```

```python
import functools

import jax
import jax.numpy as jnp
from jax import lax
from jax.experimental import pallas as pl
from jax.experimental.pallas import tpu as pltpu

LN_EPS = 1e-5
N_GROUPS = 4
EXPERTS_PER_GROUP = 4
SUBLANES = 8
LANES = 128
ROUTER_ROWS = 32
TILE = 512
MOE_CHUNKS = 2
PLE_TILES = 2
N_SUB = 2
VMEM_LIMIT_BYTES = 56 * 1024 * 1024
NEG_BIG = -1e30
META_FIELDS = 5

_F32 = jnp.float32
_BF16 = jnp.bfloat16
_NT = (((1,), (1,)), ((), ()))
_TN = (((0,), (0,)), ((), ()))


def _layer_norm(x, g, b):
    mu = jnp.mean(x, axis=-1, keepdims=True)
    xc = x - mu
    var = jnp.mean(xc * xc, axis=-1, keepdims=True)
    return xc * lax.rsqrt(var + LN_EPS) * g + b


def _sigmoid(x):
    return 0.5 * jnp.tanh(0.5 * x) + 0.5


def _dot(a, b):
    return jnp.dot(a, b, preferred_element_type=_F32)


def _aligned(v):
    return v if isinstance(v, int) else pl.multiple_of(v, SUBLANES)


def _rows(src, s0, dst, d0, m, sem):
    return pltpu.make_async_copy(src.at[pl.ds(_aligned(s0), _aligned(m))],
                                 dst.at[pl.ds(_aligned(d0), _aligned(m))], sem)


def _route(logits):
    row = lax.broadcasted_iota(jnp.int32, (SUBLANES, logits.shape[1]), 0)
    real = row < N_GROUPS
    gl = jnp.where(real, logits[0:SUBLANES], NEG_BIG)
    gm = jnp.max(gl, axis=0, keepdims=True)
    ge = jnp.exp(gl - gm)
    gp = ge / jnp.sum(ge, axis=0, keepdims=True)
    gp_top = jnp.max(gp, axis=0, keepdims=True)
    gidx = jnp.min(jnp.where(gp == gp_top, row, SUBLANES), axis=0, keepdims=True)

    el = jnp.zeros_like(gl)
    for g in range(N_GROUPS):
        lo = N_GROUPS + g * EXPERTS_PER_GROUP
        blk = logits[lo - lo % SUBLANES:lo - lo % SUBLANES + SUBLANES]
        if lo % SUBLANES:
            blk = pltpu.roll(blk, SUBLANES - lo % SUBLANES, 0)
        el = jnp.where(gidx == g, blk, el)
    el = jnp.where(real, el, NEG_BIG)
    em = jnp.max(el, axis=0, keepdims=True)
    ee = jnp.exp(el - em)
    ep = ee / jnp.sum(ee, axis=0, keepdims=True)
    ep = jnp.where(real, ep, -1.0)
    p1 = jnp.max(ep, axis=0, keepdims=True)
    i1 = jnp.min(jnp.where(ep == p1, row, SUBLANES), axis=0, keepdims=True)
    ep2 = jnp.where(row == i1, -1.0, ep)
    p2 = jnp.max(ep2, axis=0, keepdims=True)
    i2 = jnp.min(jnp.where(ep2 == p2, row, SUBLANES), axis=0, keepdims=True)
    denom = p1 + p2
    w = jnp.where(row == i1, p1 / denom, jnp.where(row == i2, p2 / denom, 0.0))
    return gidx, w * gp_top


def _mixer_kernel(x_ref, lng_ref, lnb_ref, w_in_ref, b_in_ref, cw_ref, cb_ref, clg_ref, clb_ref,
                  sw_ref, sb_ref, w_out_ref, b_out_ref, l1g_ref, l1b_ref, wr_ref, br_ref, tri_ref,
                  x1_ref, pos_ref, xs_ref, meta_ref, info_ref, order_ref, rank_ref,
                  abuf, ashift, ubuf, w_in_bf, w_out_bf, stage, zbuf, cur_ref, fill_ref, nfree_ref,
                  ring_ref, sem, zsem, *, alpha, tile, t_pad, chunk, n_chunks, n_tiles, conf_w, sc_w_,
                  conf_k, sc_k, a_halo, u_halo):
    j = pl.program_id(1)
    k = pl.program_id(0) * pl.num_programs(1) + j
    slot = k % 2
    d = x1_ref.shape[-1]

    @pl.when(k == 0)
    def _():
        w_in_bf[...] = w_in_ref[...].astype(_BF16)
        w_out_bf[...] = w_out_ref[...].astype(_BF16)
        nfree_ref[0] = 0
        for g in range(N_GROUPS):
            cur_ref[g] = 0
            fill_ref[g] = chunk
        for c in range(n_chunks + 1):
            info_ref[c] = N_GROUPS - 1
        for c in range(n_chunks):
            info_ref[n_chunks + 1 + c] = chunk

    @pl.when(j == 0)
    def _():
        abuf[0:a_halo, :] = jnp.zeros((a_halo, conf_w), _F32)
        ubuf[0:u_halo, :] = jnp.zeros((u_halo, sc_w_), _F32)

    x0 = _layer_norm(x_ref[0], lng_ref[...], lnb_ref[...])
    h = _dot(x0.astype(_BF16), w_in_bf[...]) + b_in_ref[...]
    c0, c1, c2, c3 = conf_w, 2 * conf_w, 2 * conf_w + sc_w_, 2 * conf_w + 2 * sc_w_

    a = h[:, 0:c0] * _sigmoid(h[:, c0:c1])
    abuf[a_halo:a_halo + tile, :] = a
    for i in range(1, SUBLANES):
        ashift[i - 1] = abuf[i:i + tile + a_halo - SUBLANES, :]
    acc = jnp.broadcast_to(cb_ref[...], (tile, conf_w))
    for t in range(conf_k):
        q, i = divmod(a_halo - (conf_k - 1) + t, SUBLANES)
        if i == 0:
            src = abuf[SUBLANES * q:SUBLANES * q + tile, :]
        else:
            src = ashift[i - 1, SUBLANES * q:SUBLANES * q + tile, :]
        acc = acc + cw_ref[t:t + 1, :] * src
    abuf[0:a_halo, :] = abuf[tile:tile + a_halo, :]
    an = _layer_norm(acc, clg_ref[...], clb_ref[...])
    a2 = an * _sigmoid(an)

    u = h[:, c2:c3] * h[:, c3:]
    ubuf[u_halo:u_halo + tile, :] = u
    sc = jnp.broadcast_to(sb_ref[...], (tile, sc_w_))
    for t in range(sc_k):
        off = u_halo - (sc_k - 1) + t
        sc = sc + sw_ref[t:t + 1, :] * ubuf[off:off + tile, :]
    ubuf[0:u_halo, :] = ubuf[tile:tile + u_halo, :]
    s = h[:, c1:c2] * sc

    mix = (_dot(a2.astype(_BF16), w_out_bf[0:conf_w, :])
           + _dot(s.astype(_BF16), w_out_bf[conf_w:, :]) + b_out_ref[...])
    x1 = _layer_norm(alpha * x0 + mix, l1g_ref[...], l1b_ref[...])
    x1_ref[0] = x1
    x1b = x1.astype(_BF16)

    logits = lax.dot_general(wr_ref[...], x1b, _NT, preferred_element_type=_F32) + br_ref[...]
    gidx, w = _route(logits)

    row8 = lax.broadcasted_iota(jnp.int32, (SUBLANES, tile), 0)
    onehot = jnp.where(gidx == row8, 1.0, 0.0)
    cnt = jnp.sum(onehot, axis=1, keepdims=True)
    sizes, starts = [], []
    start = jnp.int32(0)
    startv = jnp.zeros((SUBLANES, tile), jnp.int32)
    for g in range(N_GROUPS):
        n8 = (cnt[g, 0].astype(jnp.int32) + (SUBLANES - 1)) & (-SUBLANES)
        sizes.append(n8)
        starts.append(start)
        startv = jnp.where(row8 == g, start, startv)
        start = start + n8
    cum = _dot(onehot.astype(_BF16), tri_ref[...])
    pos = jnp.sum(onehot * (startv.astype(_F32) + cum), axis=0, keepdims=True).astype(jnp.int32)
    pos_ref[0] = pos
    perm = jnp.where(lax.broadcasted_iota(jnp.int32, (t_pad, tile), 0) == pos, 1.0, 0.0).astype(_BF16)

    w_hi = w.astype(_BF16).astype(_F32)
    w_lo = (w - w_hi).astype(_BF16).astype(_F32)
    w_hl = jnp.where(row8 < EXPERTS_PER_GROUP, w_hi, pltpu.roll(w_lo, EXPERTS_PER_GROUP, 0))
    w_rows = jnp.concatenate([w_hl, jnp.zeros((LANES - SUBLANES, tile), _F32)], axis=0).astype(_BF16)

    def wait_slot(sl):
        for g in range(N_GROUPS):
            for piece in range(2):
                m = ring_ref[sl, 2 * g + piece]

                @pl.when(m > 0)
                def _():
                    _rows(stage.at[sl], 0, xs_ref, 0, m, sem.at[sl]).wait()

    @pl.when(k >= 2)
    def _():
        wait_slot(slot)

    stage[slot, :, 0:d] = _dot(perm, x1b)
    stage[slot, :, d:] = lax.dot_general(perm, w_rows, _NT, preferred_element_type=_F32)

    for g in range(N_GROUPS):
        n8, sg = sizes[g], starts[g]
        fill, cur, newc = fill_ref[g], cur_ref[g], nfree_ref[0]
        m1 = jnp.minimum(n8, chunk - fill)
        m2 = n8 - m1
        r1 = cur * chunk + fill
        r2 = newc * chunk
        ring_ref[slot, 2 * g] = m1
        ring_ref[slot, 2 * g + 1] = m2
        for f, v in enumerate((r1, m1, r2, m2, sg)):
            meta_ref[k, g * META_FIELDS + f] = v

        @pl.when(m1 > 0)
        def _():
            _rows(stage.at[slot], sg, xs_ref, r1, m1, sem.at[slot]).start()

        @pl.when(m2 > 0)
        def _():
            _rows(stage.at[slot], sg + m1, xs_ref, r2, m2, sem.at[slot]).start()
            info_ref[newc] = g
            cur_ref[g] = newc
            nfree_ref[0] = newc + 1

        fill_ref[g] = jnp.where(m2 > 0, m2, fill + m1)

    @pl.when(k == n_tiles - 1)
    def _():
        zbuf[...] = jnp.zeros_like(zbuf)
        nfree = nfree_ref[0]
        info_ref[n_chunks] = nfree
        tails = []
        for g in range(N_GROUPS):
            rem = chunk - fill_ref[g]

            @pl.when(rem > 0)
            def _():
                info_ref[n_chunks + 1 + cur_ref[g]] = fill_ref[g]

            tails.append((rem, _rows(zbuf, 0, xs_ref, cur_ref[g] * chunk + fill_ref[g], rem, zsem)))
        spare = [(c, pltpu.make_async_copy(zbuf, xs_ref.at[pl.ds(c * chunk, chunk)], zsem))
                 for c in range(n_tiles * tile // chunk, n_chunks)]
        for rem, cp in tails:
            pl.when(rem > 0)(cp.start)
        for c, cp in spare:
            pl.when(c >= nfree)(cp.start)

        slot_out = jnp.int32(0)
        for g in range(N_GROUPS):
            def place(c, nxt, g=g):
                hit = info_ref[c] == g

                @pl.when(hit)
                def _():
                    order_ref[nxt] = c
                    rank_ref[c] = nxt

                return nxt + hit.astype(jnp.int32)

            slot_out = lax.fori_loop(0, n_chunks, place, slot_out)

        @pl.when(k >= 1)
        def _():
            wait_slot(1 - slot)

        wait_slot(slot)
        for rem, cp in tails:
            pl.when(rem > 0)(cp.wait)
        for c, cp in spare:
            pl.when(c >= nfree)(cp.wait)


def _moe_kernel(order_ref, info_ref, *refs, d, d_expert, n_chunks, per_step):
    xs_refs = refs[:per_step]
    wg_hbm, wu_hbm, wd_hbm, o_ref, wg_buf, wu_buf, wd_buf, slot_ref, wsem = refs[per_step:]
    chunk = xs_refs[0].shape[0]

    def weight_copies(g, sl):
        lo = g * EXPERTS_PER_GROUP
        return [pltpu.make_async_copy(src.at[pl.ds(lo, EXPERTS_PER_GROUP)], dst.at[sl], wsem)
                for src, dst in ((wg_hbm, wg_buf), (wu_hbm, wu_buf), (wd_hbm, wd_buf))]

    for ci, xs_ref in enumerate(xs_refs):
        i = pl.program_id(0) * per_step + ci
        rows = slice(ci * chunk, (ci + 1) * chunk)
        c = order_ref[i]
        grp = info_ref[c]
        prev = info_ref[order_ref[jnp.maximum(i - 1, 0)]]

        @pl.when(i == 0)
        def _():
            slot_ref[0] = 0
            for cp in weight_copies(grp, 0):
                cp.start()

        @pl.when((i == 0) | (grp != prev))
        def _():
            @pl.when(i > 0)
            def _():
                slot_ref[0] = 1 - slot_ref[0]

            cur = slot_ref[0]
            for cp in weight_copies(grp, cur):
                cp.wait()

            def later_group(j, found):
                gj = info_ref[order_ref[j]]
                return jnp.where((found < 0) & (j > i) & (gj != grp), gj, found)

            nxt = lax.fori_loop(0, n_chunks, later_group, jnp.int32(-1))

            @pl.when(nxt >= 0)
            def _():
                for cp in weight_copies(nxt, 1 - cur):
                    cp.start()

        used = c < info_ref[n_chunks]
        half = chunk // 2
        low_only = info_ref[n_chunks + 1 + c] <= half

        def experts(m, xs_ref=xs_ref):
            xb = xs_ref[0:m, 0:d]
            ws = slot_ref[0]
            aux = xs_ref[0:m, d:]
            y = jnp.zeros((m, o_ref.shape[1]), _F32)
            for e in range(EXPERTS_PER_GROUP):
                hg = _dot(xb, wg_buf[ws, e])
                hu = _dot(xb, wu_buf[ws, e])
                we = aux[:, e:e + 1] + aux[:, EXPERTS_PER_GROUP + e:EXPERTS_PER_GROUP + e + 1]
                hid = hg * _sigmoid(hg) * hu * we
                y = y + _dot(hid, wd_buf[ws, e])
            return y

        @pl.when(used & jnp.logical_not(low_only))
        def _():
            o_ref[rows, :] = experts(chunk)

        @pl.when(used & low_only)
        def _():
            o_ref[ci * chunk:ci * chunk + half, :] = experts(half)
            o_ref[ci * chunk + half:(ci + 1) * chunk, :] = jnp.zeros((half, o_ref.shape[1]), _F32)

        @pl.when(jnp.logical_not(used))
        def _():
            o_ref[rows, :] = jnp.zeros((chunk, o_ref.shape[1]), _F32)


def _ple_kernel(meta_ref, rank_ref, x1_ref, p_ref, pos_ref, ys_ref, wpg_ref, bpg_ref, wpp_ref, l2g_ref, l2b_ref,
                o_ref, ybuf, sem, *, alpha, tile, chunk, n_sub, tiles_per_step, t_pad, n_steps):
    k = pl.program_id(0)
    slot = k % 2

    def moved(r):
        return rank_ref[r // chunk] * chunk + r % chunk

    def pieces(step, sl):
        out = []
        for ti in range(tiles_per_step):
            kk = step * tiles_per_step + ti
            for g in range(N_GROUPS):
                r1, m1, r2, m2, sg = (meta_ref[kk, g * META_FIELDS + f] for f in range(META_FIELDS))
                out.append((m1, _rows(ys_ref, moved(r1), ybuf.at[sl, ti], sg, m1, sem.at[sl])))
                out.append((m2, _rows(ys_ref, moved(r2), ybuf.at[sl, ti], sg + m1, m2, sem.at[sl])))
        return out

    def fetch(kk, sl):
        for m, cp in pieces(kk, sl):
            pl.when(m > 0)(cp.start)

    @pl.when(k == 0)
    def _():
        ybuf[...] = jnp.zeros_like(ybuf)
        fetch(0, 0)

    @pl.when(k + 1 < n_steps)
    def _():
        fetch(k + 1, 1 - slot)

    for m, cp in pieces(k, slot):
        pl.when(m > 0)(cp.wait)

    sub = tile // n_sub
    for ti in range(tiles_per_step):
        yb = ybuf[slot, ti]
        pos = pos_ref[ti]
        for blk in range(n_sub):
            cols = slice(blk * sub, (blk + 1) * sub)
            rows = slice(ti * tile + blk * sub, ti * tile + (blk + 1) * sub)
            perm = jnp.where(lax.broadcasted_iota(jnp.int32, (t_pad, sub), 0) == pos[:, cols], 1.0, 0.0)
            y = lax.dot_general(perm, yb, _TN, preferred_element_type=_F32)
            r = alpha * x1_ref[rows, :] + y
            gate = _sigmoid(_dot(r, wpg_ref[...]) + bpg_ref[...])
            pp = _dot(p_ref[rows, :], wpp_ref[...])
            o_ref[rows, :] = _layer_norm(r + gate * pp, l2g_ref[...], l2b_ref[...])


def _full(shape):
    return pl.BlockSpec(shape, lambda *_: (0,) * len(shape))


def _resident(shape):
    return pl.BlockSpec(shape, lambda *_: (0,) * len(shape), pipeline_mode=pl.Buffered(1))


def _row(v):
    return v.reshape(1, -1).astype(_F32)


def kernel(x, p, ln_in_g, ln_in_b, w_in, b_in, conf_dw_w, conf_dw_b, conf_ln_g, conf_ln_b, sc_w, sc_b, w_out, b_out, ln1_g, ln1_b, w_rg, b_rg, w_re, b_re, w_gate, w_up, w_down, w_pg, b_pg, w_pp, ln2_g, ln2_b):
    depth = w_in.shape[0]
    assert depth == 1, "single-layer block"
    alpha = (2.0 * depth) ** 0.25
    bsz, seq, d = x.shape
    n = bsz * seq
    conf_k, conf_w = conf_dw_w.shape[1:]
    sc_k, sc_w_ = sc_w.shape[1:]
    d_in = w_in.shape[2]
    n_exp, _, d_expert = w_gate.shape[1:]
    assert n_exp == N_GROUPS * EXPERTS_PER_GROUP
    assert d_in == 2 * conf_w + 3 * sc_w_ and w_out.shape[1] == conf_w + sc_w_
    d_ple = p.shape[-1]

    tile = chunk = TILE
    assert seq % tile == 0 and (bsz * (seq // tile)) % PLE_TILES == 0
    nt = seq // tile
    n_tiles = bsz * nt
    a_halo = -(-(conf_k - 1) // SUBLANES) * SUBLANES
    u_halo = -(-(sc_k - 1) // SUBLANES) * SUBLANES
    t_pad = tile + N_GROUPS * SUBLANES
    n_chunks = -(-(n + (SUBLANES - 1) * N_GROUPS * n_tiles) // chunk) + N_GROUPS
    n_chunks = -(-n_chunks // MOE_CHUNKS) * MOE_CHUNKS
    d_pay = d + LANES

    n_logits = N_GROUPS + n_exp
    wr = jnp.pad(jnp.concatenate([w_rg[0], w_re[0]], axis=1).T, ((0, ROUTER_ROWS - n_logits), (0, 0)))
    br = jnp.pad(jnp.concatenate([b_rg[0], b_re[0]]), (0, ROUTER_ROWS - n_logits))
    tri = (jnp.arange(tile)[:, None] < jnp.arange(tile)[None, :]).astype(_BF16)

    mixer = pl.pallas_call(
        functools.partial(_mixer_kernel, alpha=alpha, tile=tile, t_pad=t_pad, chunk=chunk,
                          n_chunks=n_chunks, n_tiles=n_tiles, conf_w=conf_w, sc_w_=sc_w_,
                          conf_k=conf_k, sc_k=sc_k, a_halo=a_halo, u_halo=u_halo),
        grid=(bsz, nt),
        in_specs=[
            pl.BlockSpec((1, tile, d), lambda b, j: (b, j, 0)),
            _full((1, d)), _full((1, d)),
            _resident((d, d_in)), _full((1, d_in)),
            _full((conf_k, conf_w)), _full((1, conf_w)), _full((1, conf_w)), _full((1, conf_w)),
            _full((sc_k, sc_w_)), _full((1, sc_w_)),
            _resident((conf_w + sc_w_, d)), _full((1, d)),
            _full((1, d)), _full((1, d)),
            _full((ROUTER_ROWS, d)), _full((ROUTER_ROWS, 1)),
            _resident((tile, tile)),
        ],
        out_specs=[
            pl.BlockSpec((1, tile, d), lambda b, j: (b, j, 0)),
            pl.BlockSpec((1, 1, tile), lambda b, j: (b * nt + j, 0, 0)),
            pl.BlockSpec(memory_space=pl.ANY),
            pl.BlockSpec(memory_space=pltpu.SMEM),
            pl.BlockSpec(memory_space=pltpu.SMEM),
            pl.BlockSpec(memory_space=pltpu.SMEM),
            pl.BlockSpec(memory_space=pltpu.SMEM),
        ],
        out_shape=[
            jax.ShapeDtypeStruct((bsz, seq, d), _F32),
            jax.ShapeDtypeStruct((n_tiles, 1, tile), jnp.int32),
            jax.ShapeDtypeStruct((n_chunks * chunk, d_pay), _F32),
            jax.ShapeDtypeStruct((n_tiles, N_GROUPS * META_FIELDS), jnp.int32),
            jax.ShapeDtypeStruct((2 * n_chunks + 1,), jnp.int32),
            jax.ShapeDtypeStruct((n_chunks,), jnp.int32),
            jax.ShapeDtypeStruct((n_chunks,), jnp.int32),
        ],
        scratch_shapes=[
            pltpu.VMEM((a_halo + tile, conf_w), _F32),
            pltpu.VMEM((SUBLANES - 1, a_halo + tile - SUBLANES, conf_w), _F32),
            pltpu.VMEM((u_halo + tile, sc_w_), _F32),
            pltpu.VMEM((d, d_in), _BF16),
            pltpu.VMEM((conf_w + sc_w_, d), _BF16),
            pltpu.VMEM((2, t_pad, d_pay), _F32),
            pltpu.VMEM((chunk, d_pay), _F32),
            pltpu.SMEM((N_GROUPS,), jnp.int32),
            pltpu.SMEM((N_GROUPS,), jnp.int32),
            pltpu.SMEM((1,), jnp.int32),
            pltpu.SMEM((2, 2 * N_GROUPS), jnp.int32),
            pltpu.SemaphoreType.DMA((2,)),
            pltpu.SemaphoreType.DMA(()),
        ],
        compiler_params=pltpu.CompilerParams(
            dimension_semantics=("arbitrary", "arbitrary"), vmem_limit_bytes=VMEM_LIMIT_BYTES),
        name="mixer",
    )
    x1, pos, xs, meta, info, order, rank = mixer(
        x, _row(ln_in_g), _row(ln_in_b), w_in[0], _row(b_in[0]),
        conf_dw_w[0], _row(conf_dw_b[0]), _row(conf_ln_g[0]), _row(conf_ln_b[0]),
        sc_w[0], _row(sc_b[0]), w_out[0], _row(b_out[0]),
        _row(ln1_g[0]), _row(ln1_b[0]), wr.astype(_BF16), br.reshape(ROUTER_ROWS, 1), tri)

    moe = pl.pallas_call(
        functools.partial(_moe_kernel, d=d, d_expert=d_expert, n_chunks=n_chunks, per_step=MOE_CHUNKS),
        grid_spec=pltpu.PrefetchScalarGridSpec(
            num_scalar_prefetch=2,
            grid=(n_chunks // MOE_CHUNKS,),
            in_specs=[
                pl.BlockSpec((chunk, d_pay), functools.partial(
                    lambda i, order, info, ci: (order[i * MOE_CHUNKS + ci], 0), ci=ci))
                for ci in range(MOE_CHUNKS)
            ] + [
                pl.BlockSpec(memory_space=pl.ANY),
                pl.BlockSpec(memory_space=pl.ANY),
                pl.BlockSpec(memory_space=pl.ANY),
            ],
            out_specs=pl.BlockSpec((MOE_CHUNKS * chunk, d), lambda i, order, info: (i, 0)),
            scratch_shapes=[
                pltpu.VMEM((2, EXPERTS_PER_GROUP, d, d_expert), _F32),
                pltpu.VMEM((2, EXPERTS_PER_GROUP, d, d_expert), _F32),
                pltpu.VMEM((2, EXPERTS_PER_GROUP, d_expert, d), _F32),
                pltpu.SMEM((1,), jnp.int32),
                pltpu.SemaphoreType.DMA(()),
            ],
        ),
        out_shape=jax.ShapeDtypeStruct((n_chunks * chunk, d), _F32),
        compiler_params=pltpu.CompilerParams(
            dimension_semantics=("arbitrary",), vmem_limit_bytes=VMEM_LIMIT_BYTES),
        name="moe",
    )
    ys = moe(order, info, *([xs] * MOE_CHUNKS), w_gate[0], w_up[0], w_down[0])

    ple = pl.pallas_call(
        functools.partial(_ple_kernel, alpha=alpha, tile=tile, chunk=chunk, n_sub=N_SUB, tiles_per_step=PLE_TILES,
                          t_pad=t_pad, n_steps=n_tiles // PLE_TILES),
        grid_spec=pltpu.PrefetchScalarGridSpec(
            num_scalar_prefetch=2,
            grid=(n_tiles // PLE_TILES,),
            in_specs=[
                pl.BlockSpec((PLE_TILES * tile, d), lambda k, meta, rank: (k, 0)),
                pl.BlockSpec((PLE_TILES * tile, d_ple), lambda k, meta, rank: (k, 0)),
                pl.BlockSpec((PLE_TILES, 1, tile), lambda k, meta, rank: (k, 0, 0)),
                pl.BlockSpec(memory_space=pl.ANY),
                _resident((d, d)), _full((1, d)), _resident((d_ple, d)), _full((1, d)), _full((1, d)),
            ],
            out_specs=pl.BlockSpec((PLE_TILES * tile, d), lambda k, meta, rank: (k, 0)),
            scratch_shapes=[
                pltpu.VMEM((2, PLE_TILES, t_pad, d), _F32),
                pltpu.SemaphoreType.DMA((2,)),
            ],
        ),
        out_shape=jax.ShapeDtypeStruct((n, d), _F32),
        compiler_params=pltpu.CompilerParams(
            dimension_semantics=("arbitrary",), vmem_limit_bytes=VMEM_LIMIT_BYTES),
        name="ple",
    )
    out = ple(meta, rank, x1.reshape(n, d), p[0].reshape(n, d_ple), pos, ys,
              w_pg[0], _row(b_pg[0]), w_pp[0], _row(ln2_g[0]), _row(ln2_b[0]))
    return out.reshape(bsz, seq, d)
```

```python
import functools

import jax
import jax.numpy as jnp
from jax import lax
from jax.experimental import pallas as pl
from jax.experimental.pallas import tpu as pltpu

LN_EPS = 1e-5
N_GROUPS = 4
EXPERTS_PER_GROUP = 4
SUBLANES = 8
LANES = 128
ROUTER_ROWS = 32
TILE = 512
MOE_CHUNKS = 2
PLE_TILES = 2
N_SUB = 2
VMEM_LIMIT_BYTES = 56 * 1024 * 1024
NEG_BIG = -1e30
META_FIELDS = 5

_F32 = jnp.float32
_BF16 = jnp.bfloat16
_NT = (((1,), (1,)), ((), ()))
_TN = (((0,), (0,)), ((), ()))


def _layer_norm(x, g, b):
    mu = jnp.mean(x, axis=-1, keepdims=True)
    xc = x - mu
    var = jnp.mean(xc * xc, axis=-1, keepdims=True)
    return xc * lax.rsqrt(var + LN_EPS) * g + b


def _sigmoid(x):
    return 0.5 * jnp.tanh(0.5 * x) + 0.5


def _dot(a, b):
    return jnp.dot(a, b, preferred_element_type=_F32)


def _aligned(v):
    return v if isinstance(v, int) else pl.multiple_of(v, SUBLANES)


def _rows(src, s0, dst, d0, m, sem):
    return pltpu.make_async_copy(src.at[pl.ds(_aligned(s0), _aligned(m))],
                                 dst.at[pl.ds(_aligned(d0), _aligned(m))], sem)


def _route(logits):
    row = lax.broadcasted_iota(jnp.int32, (SUBLANES, logits.shape[1]), 0)
    real = row < N_GROUPS
    gl = jnp.where(real, logits[0:SUBLANES], NEG_BIG)
    gm = jnp.max(gl, axis=0, keepdims=True)
    ge = jnp.exp(gl - gm)
    gp = ge / jnp.sum(ge, axis=0, keepdims=True)
    gp_top = jnp.max(gp, axis=0, keepdims=True)
    gidx = jnp.min(jnp.where(gp == gp_top, row, SUBLANES), axis=0, keepdims=True)

    el = jnp.zeros_like(gl)
    for g in range(N_GROUPS):
        lo = N_GROUPS + g * EXPERTS_PER_GROUP
        blk = logits[lo - lo % SUBLANES:lo - lo % SUBLANES + SUBLANES]
        if lo % SUBLANES:
            blk = pltpu.roll(blk, SUBLANES - lo % SUBLANES, 0)
        el = jnp.where(gidx == g, blk, el)
    el = jnp.where(real, el, NEG_BIG)
    em = jnp.max(el, axis=0, keepdims=True)
    ee = jnp.exp(el - em)
    ep = ee / jnp.sum(ee, axis=0, keepdims=True)
    ep = jnp.where(real, ep, -1.0)
    p1 = jnp.max(ep, axis=0, keepdims=True)
    i1 = jnp.min(jnp.where(ep == p1, row, SUBLANES), axis=0, keepdims=True)
    ep2 = jnp.where(row == i1, -1.0, ep)
    p2 = jnp.max(ep2, axis=0, keepdims=True)
    i2 = jnp.min(jnp.where(ep2 == p2, row, SUBLANES), axis=0, keepdims=True)
    denom = p1 + p2
    w = jnp.where(row == i1, p1 / denom, jnp.where(row == i2, p2 / denom, 0.0))
    return gidx, w * gp_top


def _mixer_kernel(x_ref, lng_ref, lnb_ref, w_in_ref, b_in_ref, cw_ref, cb_ref, clg_ref, clb_ref,
                  sw_ref, sb_ref, w_out_ref, b_out_ref, l1g_ref, l1b_ref, wr_ref, br_ref, tri_ref,
                  x1_ref, pos_ref, xs_ref, meta_ref, info_ref, order_ref, rank_ref,
                  abuf, ashift, ubuf, w_in_bf, w_out_bf, stage, zbuf, cur_ref, fill_ref, nfree_ref,
                  ring_ref, sem, zsem, *, alpha, tile, t_pad, chunk, n_chunks, n_tiles, conf_w, sc_w_,
                  conf_k, sc_k, a_halo, u_halo):
    j = pl.program_id(1)
    k = pl.program_id(0) * pl.num_programs(1) + j
    slot = k % 2
    d = x1_ref.shape[-1]

    @pl.when(k == 0)
    def _():
        w_in_bf[...] = w_in_ref[...].astype(_BF16)
        w_out_bf[...] = w_out_ref[...].astype(_BF16)
        nfree_ref[0] = 0
        for g in range(N_GROUPS):
            cur_ref[g] = 0
            fill_ref[g] = chunk
        for c in range(n_chunks + 1):
            info_ref[c] = N_GROUPS - 1
        for c in range(n_chunks):
            info_ref[n_chunks + 1 + c] = chunk

    @pl.when(j == 0)
    def _():
        abuf[0:a_halo, :] = jnp.zeros((a_halo, conf_w), _F32)
        ubuf[0:u_halo, :] = jnp.zeros((u_halo, sc_w_), _F32)

    x0 = _layer_norm(x_ref[0], lng_ref[...], lnb_ref[...])
    h = _dot(x0.astype(_BF16), w_in_bf[...]) + b_in_ref[...]
    c0, c1, c2, c3 = conf_w, 2 * conf_w, 2 * conf_w + sc_w_, 2 * conf_w + 2 * sc_w_

    a = h[:, 0:c0] * _sigmoid(h[:, c0:c1])
    abuf[a_halo:a_halo + tile, :] = a
    for i in range(1, SUBLANES):
        ashift[i - 1] = abuf[i:i + tile + a_halo - SUBLANES, :]
    acc = jnp.broadcast_to(cb_ref[...], (tile, conf_w))
    for t in range(conf_k):
        q, i = divmod(a_halo - (conf_k - 1) + t, SUBLANES)
        if i == 0:
            src = abuf[SUBLANES * q:SUBLANES * q + tile, :]
        else:
            src = ashift[i - 1, SUBLANES * q:SUBLANES * q + tile, :]
        acc = acc + cw_ref[t:t + 1, :] * src
    abuf[0:a_halo, :] = abuf[tile:tile + a_halo, :]
    an = _layer_norm(acc, clg_ref[...], clb_ref[...])
    a2 = an * _sigmoid(an)

    u = h[:, c2:c3] * h[:, c3:]
    ubuf[u_halo:u_halo + tile, :] = u
    sc = jnp.broadcast_to(sb_ref[...], (tile, sc_w_))
    for t in range(sc_k):
        off = u_halo - (sc_k - 1) + t
        sc = sc + sw_ref[t:t + 1, :] * ubuf[off:off + tile, :]
    ubuf[0:u_halo, :] = ubuf[tile:tile + u_halo, :]
    s = h[:, c1:c2] * sc

    mix = (_dot(a2.astype(_BF16), w_out_bf[0:conf_w, :])
           + _dot(s.astype(_BF16), w_out_bf[conf_w:, :]) + b_out_ref[...])
    x1 = _layer_norm(alpha * x0 + mix, l1g_ref[...], l1b_ref[...])
    x1_ref[0] = x1
    x1b = x1.astype(_BF16)

    logits = lax.dot_general(wr_ref[...], x1b, _NT, preferred_element_type=_F32) + br_ref[...]
    gidx, w = _route(logits)

    row8 = lax.broadcasted_iota(jnp.int32, (SUBLANES, tile), 0)
    onehot = jnp.where(gidx == row8, 1.0, 0.0)
    cnt = jnp.sum(onehot, axis=1, keepdims=True)
    sizes, starts = [], []
    start = jnp.int32(0)
    startv = jnp.zeros((SUBLANES, tile), jnp.int32)
    for g in range(N_GROUPS):
        n8 = (cnt[g, 0].astype(jnp.int32) + (SUBLANES - 1)) & (-SUBLANES)
        sizes.append(n8)
        starts.append(start)
        startv = jnp.where(row8 == g, start, startv)
        start = start + n8
    cum = _dot(onehot.astype(_BF16), tri_ref[...])
    pos = jnp.sum(onehot * (startv.astype(_F32) + cum), axis=0, keepdims=True).astype(jnp.int32)
    pos_ref[0] = pos
    perm = jnp.where(lax.broadcasted_iota(jnp.int32, (t_pad, tile), 0) == pos, 1.0, 0.0).astype(_BF16)

    w_hi = w.astype(_BF16).astype(_F32)
    w_lo = (w - w_hi).astype(_BF16).astype(_F32)
    w_hl = jnp.where(row8 < EXPERTS_PER_GROUP, w_hi, pltpu.roll(w_lo, EXPERTS_PER_GROUP, 0))
    w_rows = jnp.concatenate([w_hl, jnp.zeros((LANES - SUBLANES, tile), _F32)], axis=0).astype(_BF16)

    def wait_slot(sl):
        for g in range(N_GROUPS):
            for piece in range(2):
                m = ring_ref[sl, 2 * g + piece]

                @pl.when(m > 0)
                def _():
                    _rows(stage.at[sl], 0, xs_ref, 0, m, sem.at[sl]).wait()

    @pl.when(k >= 2)
    def _():
        wait_slot(slot)

    stage[slot, :, 0:d] = _dot(perm, x1b)
    stage[slot, :, d:] = lax.dot_general(perm, w_rows, _NT, preferred_element_type=_F32)

    for g in range(N_GROUPS):
        n8, sg = sizes[g], starts[g]
        fill, cur, newc = fill_ref[g], cur_ref[g], nfree_ref[0]
        m1 = jnp.minimum(n8, chunk - fill)
        m2 = n8 - m1
        r1 = cur * chunk + fill
        r2 = newc * chunk
        ring_ref[slot, 2 * g] = m1
        ring_ref[slot, 2 * g + 1] = m2
        for f, v in enumerate((r1, m1, r2, m2, sg)):
            meta_ref[k, g * META_FIELDS + f] = v

        @pl.when(m1 > 0)
        def _():
            _rows(stage.at[slot], sg, xs_ref, r1, m1, sem.at[slot]).start()

        @pl.when(m2 > 0)
        def _():
            _rows(stage.at[slot], sg + m1, xs_ref, r2, m2, sem.at[slot]).start()
            info_ref[newc] = g
            cur_ref[g] = newc
            nfree_ref[0] = newc + 1

        fill_ref[g] = jnp.where(m2 > 0, m2, fill + m1)

    @pl.when(k == n_tiles - 1)
    def _():
        zbuf[...] = jnp.zeros_like(zbuf)
        nfree = nfree_ref[0]
        info_ref[n_chunks] = nfree
        tails = []
        for g in range(N_GROUPS):
            rem = chunk - fill_ref[g]

            @pl.when(rem > 0)
            def _():
                info_ref[n_chunks + 1 + cur_ref[g]] = fill_ref[g]

            tails.append((rem, _rows(zbuf, 0, xs_ref, cur_ref[g] * chunk + fill_ref[g], rem, zsem)))
        spare = [(c, pltpu.make_async_copy(zbuf, xs_ref.at[pl.ds(c * chunk, chunk)], zsem))
                 for c in range(n_tiles * tile // chunk, n_chunks)]
        for rem, cp in tails:
            pl.when(rem > 0)(cp.start)
        for c, cp in spare:
            pl.when(c >= nfree)(cp.start)

        slot_out = jnp.int32(0)
        for g in range(N_GROUPS):
            def place(c, nxt, g=g):
                hit = info_ref[c] == g

                @pl.when(hit)
                def _():
                    order_ref[nxt] = c
                    rank_ref[c] = nxt

                return nxt + hit.astype(jnp.int32)

            slot_out = lax.fori_loop(0, n_chunks, place, slot_out)

        @pl.when(k >= 1)
        def _():
            wait_slot(1 - slot)

        wait_slot(slot)
        for rem, cp in tails:
            pl.when(rem > 0)(cp.wait)
        for c, cp in spare:
            pl.when(c >= nfree)(cp.wait)


def _moe_kernel(order_ref, info_ref, *refs, d, d_expert, n_chunks, per_step):
    xs_refs = refs[:per_step]
    wg_hbm, wu_hbm, wd_hbm, o_ref, wg_buf, wu_buf, wd_buf, slot_ref, wsem = refs[per_step:]
    chunk = xs_refs[0].shape[0]

    def weight_copies(g, sl):
        lo = g * EXPERTS_PER_GROUP
        return [pltpu.make_async_copy(src.at[pl.ds(lo, EXPERTS_PER_GROUP)], dst.at[sl], wsem.at[0])
                for src, dst in ((wg_hbm, wg_buf), (wu_hbm, wu_buf), (wd_hbm, wd_buf))]

    def first_copies(g, e):
        return [pltpu.make_async_copy(src.at[g * EXPERTS_PER_GROUP + e], dst.at[0, e], wsem.at[1 + e])
                for src, dst in ((wg_hbm, wg_buf), (wu_hbm, wu_buf), (wd_hbm, wd_buf))]

    for ci, xs_ref in enumerate(xs_refs):
        i = pl.program_id(0) * per_step + ci
        rows = slice(ci * chunk, (ci + 1) * chunk)
        c = order_ref[i]
        grp = info_ref[c]
        prev = info_ref[order_ref[jnp.maximum(i - 1, 0)]]

        @pl.when(i == 0)
        def _():
            slot_ref[0] = 0
            for e in range(EXPERTS_PER_GROUP):
                for cp in first_copies(grp, e):
                    cp.start()

        @pl.when((i == 0) | (grp != prev))
        def _():
            @pl.when(i > 0)
            def _():
                slot_ref[0] = 1 - slot_ref[0]
                for cp in weight_copies(grp, slot_ref[0]):
                    cp.wait()

            cur = slot_ref[0]

            def later_group(j, found):
                gj = info_ref[order_ref[j]]
                return jnp.where((found < 0) & (j > i) & (gj != grp), gj, found)

            nxt = lax.fori_loop(0, n_chunks, later_group, jnp.int32(-1))

            @pl.when(nxt >= 0)
            def _():
                for cp in weight_copies(nxt, 1 - cur):
                    cp.start()

        used = c < info_ref[n_chunks]
        half = chunk // 2
        low_only = info_ref[n_chunks + 1 + c] <= half

        def experts(m, xs_ref=xs_ref, grp=grp, await_first=False):
            xb = xs_ref[0:m, 0:d]
            ws = slot_ref[0]
            aux = xs_ref[0:m, d:]
            y = jnp.zeros((m, o_ref.shape[1]), _F32)
            for e in range(EXPERTS_PER_GROUP):
                if await_first:
                    for cp in first_copies(grp, e):
                        cp.wait()
                hg = _dot(xb, wg_buf[ws, e])
                hu = _dot(xb, wu_buf[ws, e])
                we = aux[:, e:e + 1] + aux[:, EXPERTS_PER_GROUP + e:EXPERTS_PER_GROUP + e + 1]
                hid = hg * _sigmoid(hg) * hu * we
                y = y + _dot(hid, wd_buf[ws, e])
            return y

        later = used
        if ci == 0:
            later = used & (i > 0)

            @pl.when(i == 0)
            def _():
                o_ref[rows, :] = experts(chunk, await_first=True)

        @pl.when(later & jnp.logical_not(low_only))
        def _():
            o_ref[rows, :] = experts(chunk)

        @pl.when(later & low_only)
        def _():
            o_ref[ci * chunk:ci * chunk + half, :] = experts(half)
            o_ref[ci * chunk + half:(ci + 1) * chunk, :] = jnp.zeros((half, o_ref.shape[1]), _F32)

        @pl.when(jnp.logical_not(used))
        def _():
            o_ref[rows, :] = jnp.zeros((chunk, o_ref.shape[1]), _F32)


def _ple_kernel(meta_ref, rank_ref, x1_ref, p_ref, pos_ref, ys_ref, wpg_ref, bpg_ref, wpp_ref, l2g_ref, l2b_ref,
                o_ref, ybuf, sem, *, alpha, tile, chunk, n_sub, tiles_per_step, t_pad, n_steps):
    k = pl.program_id(0)
    slot = k % 2

    def moved(r):
        return rank_ref[r // chunk] * chunk + r % chunk

    def pieces(step, sl):
        out = []
        for ti in range(tiles_per_step):
            kk = step * tiles_per_step + ti
            for g in range(N_GROUPS):
                r1, m1, r2, m2, sg = (meta_ref[kk, g * META_FIELDS + f] for f in range(META_FIELDS))
                out.append((m1, _rows(ys_ref, moved(r1), ybuf.at[sl, ti], sg, m1, sem.at[sl])))
                out.append((m2, _rows(ys_ref, moved(r2), ybuf.at[sl, ti], sg + m1, m2, sem.at[sl])))
        return out

    def fetch(kk, sl):
        for m, cp in pieces(kk, sl):
            pl.when(m > 0)(cp.start)

    @pl.when(k == 0)
    def _():
        ybuf[...] = jnp.zeros_like(ybuf)
        fetch(0, 0)

    @pl.when(k + 1 < n_steps)
    def _():
        fetch(k + 1, 1 - slot)

    for m, cp in pieces(k, slot):
        pl.when(m > 0)(cp.wait)

    sub = tile // n_sub
    for ti in range(tiles_per_step):
        yb = ybuf[slot, ti]
        pos = pos_ref[ti]
        for blk in range(n_sub):
            cols = slice(blk * sub, (blk + 1) * sub)
            rows = slice(ti * tile + blk * sub, ti * tile + (blk + 1) * sub)
            perm = jnp.where(lax.broadcasted_iota(jnp.int32, (t_pad, sub), 0) == pos[:, cols], 1.0, 0.0)
            y = lax.dot_general(perm, yb, _TN, preferred_element_type=_F32)
            r = alpha * x1_ref[rows, :] + y
            gate = _sigmoid(_dot(r, wpg_ref[...]) + bpg_ref[...])
            pp = _dot(p_ref[rows, :], wpp_ref[...])
            o_ref[rows, :] = _layer_norm(r + gate * pp, l2g_ref[...], l2b_ref[...])


def _full(shape):
    return pl.BlockSpec(shape, lambda *_: (0,) * len(shape))


def _resident(shape):
    return pl.BlockSpec(shape, lambda *_: (0,) * len(shape), pipeline_mode=pl.Buffered(1))


def _row(v):
    return v.reshape(1, -1).astype(_F32)


def kernel(x, p, ln_in_g, ln_in_b, w_in, b_in, conf_dw_w, conf_dw_b, conf_ln_g, conf_ln_b, sc_w, sc_b, w_out, b_out, ln1_g, ln1_b, w_rg, b_rg, w_re, b_re, w_gate, w_up, w_down, w_pg, b_pg, w_pp, ln2_g, ln2_b):
    depth = w_in.shape[0]
    assert depth == 1, "single-layer block"
    alpha = (2.0 * depth) ** 0.25
    bsz, seq, d = x.shape
    n = bsz * seq
    conf_k, conf_w = conf_dw_w.shape[1:]
    sc_k, sc_w_ = sc_w.shape[1:]
    d_in = w_in.shape[2]
    n_exp, _, d_expert = w_gate.shape[1:]
    assert n_exp == N_GROUPS * EXPERTS_PER_GROUP
    assert d_in == 2 * conf_w + 3 * sc_w_ and w_out.shape[1] == conf_w + sc_w_
    d_ple = p.shape[-1]

    tile = chunk = TILE
    assert seq % tile == 0 and (bsz * (seq // tile)) % PLE_TILES == 0
    nt = seq // tile
    n_tiles = bsz * nt
    a_halo = -(-(conf_k - 1) // SUBLANES) * SUBLANES
    u_halo = -(-(sc_k - 1) // SUBLANES) * SUBLANES
    t_pad = tile + N_GROUPS * SUBLANES
    n_chunks = -(-(n + (SUBLANES - 1) * N_GROUPS * n_tiles) // chunk) + N_GROUPS
    n_chunks = -(-n_chunks // MOE_CHUNKS) * MOE_CHUNKS
    d_pay = d + LANES

    n_logits = N_GROUPS + n_exp
    wr = jnp.pad(jnp.concatenate([w_rg[0], w_re[0]], axis=1).T, ((0, ROUTER_ROWS - n_logits), (0, 0)))
    br = jnp.pad(jnp.concatenate([b_rg[0], b_re[0]]), (0, ROUTER_ROWS - n_logits))
    tri = (jnp.arange(tile)[:, None] < jnp.arange(tile)[None, :]).astype(_BF16)

    mixer = pl.pallas_call(
        functools.partial(_mixer_kernel, alpha=alpha, tile=tile, t_pad=t_pad, chunk=chunk,
                          n_chunks=n_chunks, n_tiles=n_tiles, conf_w=conf_w, sc_w_=sc_w_,
                          conf_k=conf_k, sc_k=sc_k, a_halo=a_halo, u_halo=u_halo),
        grid=(bsz, nt),
        in_specs=[
            pl.BlockSpec((1, tile, d), lambda b, j: (b, j, 0)),
            _full((1, d)), _full((1, d)),
            _resident((d, d_in)), _full((1, d_in)),
            _full((conf_k, conf_w)), _full((1, conf_w)), _full((1, conf_w)), _full((1, conf_w)),
            _full((sc_k, sc_w_)), _full((1, sc_w_)),
            _resident((conf_w + sc_w_, d)), _full((1, d)),
            _full((1, d)), _full((1, d)),
            _full((ROUTER_ROWS, d)), _full((ROUTER_ROWS, 1)),
            _resident((tile, tile)),
        ],
        out_specs=[
            pl.BlockSpec((1, tile, d), lambda b, j: (b, j, 0)),
            pl.BlockSpec((1, 1, tile), lambda b, j: (b * nt + j, 0, 0)),
            pl.BlockSpec(memory_space=pl.ANY),
            pl.BlockSpec(memory_space=pltpu.SMEM),
            pl.BlockSpec(memory_space=pltpu.SMEM),
            pl.BlockSpec(memory_space=pltpu.SMEM),
            pl.BlockSpec(memory_space=pltpu.SMEM),
        ],
        out_shape=[
            jax.ShapeDtypeStruct((bsz, seq, d), _F32),
            jax.ShapeDtypeStruct((n_tiles, 1, tile), jnp.int32),
            jax.ShapeDtypeStruct((n_chunks * chunk, d_pay), _F32),
            jax.ShapeDtypeStruct((n_tiles, N_GROUPS * META_FIELDS), jnp.int32),
            jax.ShapeDtypeStruct((2 * n_chunks + 1,), jnp.int32),
            jax.ShapeDtypeStruct((n_chunks,), jnp.int32),
            jax.ShapeDtypeStruct((n_chunks,), jnp.int32),
        ],
        scratch_shapes=[
            pltpu.VMEM((a_halo + tile, conf_w), _F32),
            pltpu.VMEM((SUBLANES - 1, a_halo + tile - SUBLANES, conf_w), _F32),
            pltpu.VMEM((u_halo + tile, sc_w_), _F32),
            pltpu.VMEM((d, d_in), _BF16),
            pltpu.VMEM((conf_w + sc_w_, d), _BF16),
            pltpu.VMEM((2, t_pad, d_pay), _F32),
            pltpu.VMEM((chunk, d_pay), _F32),
            pltpu.SMEM((N_GROUPS,), jnp.int32),
            pltpu.SMEM((N_GROUPS,), jnp.int32),
            pltpu.SMEM((1,), jnp.int32),
            pltpu.SMEM((2, 2 * N_GROUPS), jnp.int32),
            pltpu.SemaphoreType.DMA((2,)),
            pltpu.SemaphoreType.DMA(()),
        ],
        compiler_params=pltpu.CompilerParams(
            dimension_semantics=("arbitrary", "arbitrary"), vmem_limit_bytes=VMEM_LIMIT_BYTES),
        name="mixer",
    )
    x1, pos, xs, meta, info, order, rank = mixer(
        x, _row(ln_in_g), _row(ln_in_b), w_in[0], _row(b_in[0]),
        conf_dw_w[0], _row(conf_dw_b[0]), _row(conf_ln_g[0]), _row(conf_ln_b[0]),
        sc_w[0], _row(sc_b[0]), w_out[0], _row(b_out[0]),
        _row(ln1_g[0]), _row(ln1_b[0]), wr.astype(_BF16), br.reshape(ROUTER_ROWS, 1), tri)

    moe = pl.pallas_call(
        functools.partial(_moe_kernel, d=d, d_expert=d_expert, n_chunks=n_chunks, per_step=MOE_CHUNKS),
        grid_spec=pltpu.PrefetchScalarGridSpec(
            num_scalar_prefetch=2,
            grid=(n_chunks // MOE_CHUNKS,),
            in_specs=[
                pl.BlockSpec((chunk, d_pay), functools.partial(
                    lambda i, order, info, ci: (order[i * MOE_CHUNKS + ci], 0), ci=ci))
                for ci in range(MOE_CHUNKS)
            ] + [
                pl.BlockSpec(memory_space=pl.ANY),
                pl.BlockSpec(memory_space=pl.ANY),
                pl.BlockSpec(memory_space=pl.ANY),
            ],
            out_specs=pl.BlockSpec((MOE_CHUNKS * chunk, d), lambda i, order, info: (i, 0)),
            scratch_shapes=[
                pltpu.VMEM((2, EXPERTS_PER_GROUP, d, d_expert), _F32),
                pltpu.VMEM((2, EXPERTS_PER_GROUP, d, d_expert), _F32),
                pltpu.VMEM((2, EXPERTS_PER_GROUP, d_expert, d), _F32),
                pltpu.SMEM((1,), jnp.int32),
                pltpu.SemaphoreType.DMA((1 + EXPERTS_PER_GROUP,)),
            ],
        ),
        out_shape=jax.ShapeDtypeStruct((n_chunks * chunk, d), _F32),
        compiler_params=pltpu.CompilerParams(
            dimension_semantics=("arbitrary",), vmem_limit_bytes=VMEM_LIMIT_BYTES),
        name="moe",
    )
    ys = moe(order, info, *([xs] * MOE_CHUNKS), w_gate[0], w_up[0], w_down[0])

    ple = pl.pallas_call(
        functools.partial(_ple_kernel, alpha=alpha, tile=tile, chunk=chunk, n_sub=N_SUB, tiles_per_step=PLE_TILES,
                          t_pad=t_pad, n_steps=n_tiles // PLE_TILES),
        grid_spec=pltpu.PrefetchScalarGridSpec(
            num_scalar_prefetch=2,
            grid=(n_tiles // PLE_TILES,),
            in_specs=[
                pl.BlockSpec((PLE_TILES * tile, d), lambda k, meta, rank: (k, 0)),
                pl.BlockSpec((PLE_TILES * tile, d_ple), lambda k, meta, rank: (k, 0)),
                pl.BlockSpec((PLE_TILES, 1, tile), lambda k, meta, rank: (k, 0, 0)),
                pl.BlockSpec(memory_space=pl.ANY),
                _resident((d, d)), _full((1, d)), _resident((d_ple, d)), _full((1, d)), _full((1, d)),
            ],
            out_specs=pl.BlockSpec((PLE_TILES * tile, d), lambda k, meta, rank: (k, 0)),
            scratch_shapes=[
                pltpu.VMEM((2, PLE_TILES, t_pad, d), _F32),
                pltpu.SemaphoreType.DMA((2,)),
            ],
        ),
        out_shape=jax.ShapeDtypeStruct((n, d), _F32),
        compiler_params=pltpu.CompilerParams(
            dimension_semantics=("arbitrary",), vmem_limit_bytes=VMEM_LIMIT_BYTES),
        name="ple",
    )
    out = ple(meta, rank, x1.reshape(n, d), p[0].reshape(n, d_ple), pos, ys,
              w_pg[0], _row(b_pg[0]), w_pp[0], _row(ln2_g[0]), _row(ln2_b[0]))
    return out.reshape(bsz, seq, d)
```

```python
import functools

import jax
import jax.numpy as jnp
from jax import lax
from jax.experimental import pallas as pl
from jax.experimental.pallas import tpu as pltpu

LN_EPS = 1e-5
N_GROUPS = 4
EXPERTS_PER_GROUP = 4
SUBLANES = 8
LANES = 128
ROUTER_ROWS = 32
TILE = 512
MOE_CHUNKS = 2
PLE_TILES = 2
N_SUB = 2
VMEM_LIMIT_BYTES = 56 * 1024 * 1024
NEG_BIG = -1e30
META_FIELDS = 5

_F32 = jnp.float32
_BF16 = jnp.bfloat16
_NT = (((1,), (1,)), ((), ()))
_TN = (((0,), (0,)), ((), ()))


def _layer_norm(x, g, b):
    mu = jnp.mean(x, axis=-1, keepdims=True)
    xc = x - mu
    var = jnp.mean(xc * xc, axis=-1, keepdims=True)
    return xc * lax.rsqrt(var + LN_EPS) * g + b


def _sigmoid(x):
    return 0.5 * jnp.tanh(0.5 * x) + 0.5


def _dot(a, b):
    return jnp.dot(a, b, preferred_element_type=_F32)


def _aligned(v):
    return v if isinstance(v, int) else pl.multiple_of(v, SUBLANES)


def _rows(src, s0, dst, d0, m, sem):
    return pltpu.make_async_copy(src.at[pl.ds(_aligned(s0), _aligned(m))],
                                 dst.at[pl.ds(_aligned(d0), _aligned(m))], sem)


def _route(logits):
    row = lax.broadcasted_iota(jnp.int32, (SUBLANES, logits.shape[1]), 0)
    real = row < N_GROUPS
    gl = jnp.where(real, logits[0:SUBLANES], NEG_BIG)
    gm = jnp.max(gl, axis=0, keepdims=True)
    ge = jnp.exp(gl - gm)
    gp = ge / jnp.sum(ge, axis=0, keepdims=True)
    gp_top = jnp.max(gp, axis=0, keepdims=True)
    gidx = jnp.min(jnp.where(gp == gp_top, row, SUBLANES), axis=0, keepdims=True)

    el = jnp.zeros_like(gl)
    for g in range(N_GROUPS):
        lo = N_GROUPS + g * EXPERTS_PER_GROUP
        blk = logits[lo - lo % SUBLANES:lo - lo % SUBLANES + SUBLANES]
        if lo % SUBLANES:
            blk = pltpu.roll(blk, SUBLANES - lo % SUBLANES, 0)
        el = jnp.where(gidx == g, blk, el)
    el = jnp.where(real, el, NEG_BIG)
    em = jnp.max(el, axis=0, keepdims=True)
    ee = jnp.exp(el - em)
    ep = ee / jnp.sum(ee, axis=0, keepdims=True)
    ep = jnp.where(real, ep, -1.0)
    p1 = jnp.max(ep, axis=0, keepdims=True)
    i1 = jnp.min(jnp.where(ep == p1, row, SUBLANES), axis=0, keepdims=True)
    ep2 = jnp.where(row == i1, -1.0, ep)
    p2 = jnp.max(ep2, axis=0, keepdims=True)
    i2 = jnp.min(jnp.where(ep2 == p2, row, SUBLANES), axis=0, keepdims=True)
    denom = p1 + p2
    w = jnp.where(row == i1, p1 / denom, jnp.where(row == i2, p2 / denom, 0.0))
    return gidx, w * gp_top


def _mixer_kernel(x_ref, lng_ref, lnb_ref, w_in_ref, b_in_ref, cw_ref, cb_ref, clg_ref, clb_ref,
                  sw_ref, sb_ref, w_out_ref, b_out_ref, l1g_ref, l1b_ref, wr_ref, brg_ref, bre_ref,
                  x1_ref, pos_ref, xs_ref, meta_ref, info_ref, order_ref, rank_ref,
                  abuf, ashift, ubuf, w_in_bf, w_out_bf, wr_bf, br_ref, tri_ref, stage, zbuf, cur_ref, fill_ref, nfree_ref,
                  ring_ref, sem, zsem, *, alpha, tile, t_pad, chunk, n_chunks, n_tiles, conf_w, sc_w_,
                  conf_k, sc_k, a_halo, u_halo, n_logits):
    j = pl.program_id(1)
    k = pl.program_id(0) * pl.num_programs(1) + j
    slot = k % 2
    d = x1_ref.shape[-1]

    @pl.when(k == 0)
    def _():
        w_in_bf[...] = w_in_ref[...].astype(_BF16)
        w_out_bf[...] = w_out_ref[...].astype(_BF16)
        wr_bf[...] = jnp.zeros_like(wr_bf)
        wr_bf[0:n_logits, :] = wr_ref[...].astype(_BF16)
        def column(b_ref, first_row):
            shape = (ROUTER_ROWS, b_ref.shape[1])
            on_diag = (lax.broadcasted_iota(jnp.int32, shape, 0)
                       == lax.broadcasted_iota(jnp.int32, shape, 1) + first_row)
            return jnp.sum(jnp.where(on_diag, b_ref[...], 0.0), axis=1, keepdims=True)

        br_ref[...] = column(brg_ref, 0) + column(bre_ref, N_GROUPS)
        earlier = lax.broadcasted_iota(jnp.int32, (tile, tile), 0) < lax.broadcasted_iota(jnp.int32, (tile, tile), 1)
        tri_ref[...] = jnp.where(earlier, 1.0, 0.0).astype(_BF16)
        nfree_ref[0] = 0
        for g in range(N_GROUPS):
            cur_ref[g] = 0
            fill_ref[g] = chunk
        for c in range(n_chunks + 1):
            info_ref[c] = N_GROUPS - 1
        for c in range(n_chunks):
            info_ref[n_chunks + 1 + c] = chunk

    @pl.when(j == 0)
    def _():
        abuf[0:a_halo, :] = jnp.zeros((a_halo, conf_w), _F32)
        ubuf[0:u_halo, :] = jnp.zeros((u_halo, sc_w_), _F32)

    x0 = _layer_norm(x_ref[0], lng_ref[...], lnb_ref[...])
    h = _dot(x0.astype(_BF16), w_in_bf[...]) + b_in_ref[...]
    c0, c1, c2, c3 = conf_w, 2 * conf_w, 2 * conf_w + sc_w_, 2 * conf_w + 2 * sc_w_

    a = h[:, 0:c0] * _sigmoid(h[:, c0:c1])
    abuf[a_halo:a_halo + tile, :] = a
    for i in range(1, SUBLANES):
        ashift[i - 1] = abuf[i:i + tile + a_halo - SUBLANES, :]
    acc = jnp.broadcast_to(cb_ref[...], (tile, conf_w))
    for t in range(conf_k):
        q, i = divmod(a_halo - (conf_k - 1) + t, SUBLANES)
        if i == 0:
            src = abuf[SUBLANES * q:SUBLANES * q + tile, :]
        else:
            src = ashift[i - 1, SUBLANES * q:SUBLANES * q + tile, :]
        acc = acc + cw_ref[t:t + 1, :] * src
    abuf[0:a_halo, :] = abuf[tile:tile + a_halo, :]
    an = _layer_norm(acc, clg_ref[...], clb_ref[...])
    a2 = an * _sigmoid(an)

    u = h[:, c2:c3] * h[:, c3:]
    ubuf[u_halo:u_halo + tile, :] = u
    sc = jnp.broadcast_to(sb_ref[...], (tile, sc_w_))
    for t in range(sc_k):
        off = u_halo - (sc_k - 1) + t
        sc = sc + sw_ref[t:t + 1, :] * ubuf[off:off + tile, :]
    ubuf[0:u_halo, :] = ubuf[tile:tile + u_halo, :]
    s = h[:, c1:c2] * sc

    mix = (_dot(a2.astype(_BF16), w_out_bf[0:conf_w, :])
           + _dot(s.astype(_BF16), w_out_bf[conf_w:, :]) + b_out_ref[...])
    x1 = _layer_norm(alpha * x0 + mix, l1g_ref[...], l1b_ref[...])
    x1_ref[0] = x1
    x1b = x1.astype(_BF16)

    logits = lax.dot_general(wr_bf[...], x1b, _NT, preferred_element_type=_F32) + br_ref[...]
    gidx, w = _route(logits)

    row8 = lax.broadcasted_iota(jnp.int32, (SUBLANES, tile), 0)
    onehot = jnp.where(gidx == row8, 1.0, 0.0)
    cnt = jnp.sum(onehot, axis=1, keepdims=True)
    sizes, starts = [], []
    start = jnp.int32(0)
    startv = jnp.zeros((SUBLANES, tile), jnp.int32)
    for g in range(N_GROUPS):
        n8 = (cnt[g, 0].astype(jnp.int32) + (SUBLANES - 1)) & (-SUBLANES)
        sizes.append(n8)
        starts.append(start)
        startv = jnp.where(row8 == g, start, startv)
        start = start + n8
    cum = _dot(onehot.astype(_BF16), tri_ref[...])
    pos = jnp.sum(onehot * (startv.astype(_F32) + cum), axis=0, keepdims=True).astype(jnp.int32)
    pos_ref[0] = pos
    perm = jnp.where(lax.broadcasted_iota(jnp.int32, (t_pad, tile), 0) == pos, 1.0, 0.0).astype(_BF16)

    w_hi = w.astype(_BF16).astype(_F32)
    w_lo = (w - w_hi).astype(_BF16).astype(_F32)
    w_hl = jnp.where(row8 < EXPERTS_PER_GROUP, w_hi, pltpu.roll(w_lo, EXPERTS_PER_GROUP, 0))
    w_rows = jnp.concatenate([w_hl, jnp.zeros((LANES - SUBLANES, tile), _F32)], axis=0).astype(_BF16)

    def wait_slot(sl):
        for g in range(N_GROUPS):
            for piece in range(2):
                m = ring_ref[sl, 2 * g + piece]

                @pl.when(m > 0)
                def _():
                    _rows(stage.at[sl], 0, xs_ref, 0, m, sem.at[sl]).wait()

    @pl.when(k >= 2)
    def _():
        wait_slot(slot)

    stage[slot, :, 0:d] = _dot(perm, x1b)
    stage[slot, :, d:] = lax.dot_general(perm, w_rows, _NT, preferred_element_type=_F32)

    for g in range(N_GROUPS):
        n8, sg = sizes[g], starts[g]
        fill, cur, newc = fill_ref[g], cur_ref[g], nfree_ref[0]
        m1 = jnp.minimum(n8, chunk - fill)
        m2 = n8 - m1
        r1 = cur * chunk + fill
        r2 = newc * chunk
        ring_ref[slot, 2 * g] = m1
        ring_ref[slot, 2 * g + 1] = m2
        for f, v in enumerate((r1, m1, r2, m2, sg)):
            meta_ref[k, g * META_FIELDS + f] = v

        @pl.when(m1 > 0)
        def _():
            _rows(stage.at[slot], sg, xs_ref, r1, m1, sem.at[slot]).start()

        @pl.when(m2 > 0)
        def _():
            _rows(stage.at[slot], sg + m1, xs_ref, r2, m2, sem.at[slot]).start()
            info_ref[newc] = g
            cur_ref[g] = newc
            nfree_ref[0] = newc + 1

        fill_ref[g] = jnp.where(m2 > 0, m2, fill + m1)

    @pl.when(k == n_tiles - 1)
    def _():
        zbuf[...] = jnp.zeros_like(zbuf)
        nfree = nfree_ref[0]
        info_ref[n_chunks] = nfree
        tails = []
        for g in range(N_GROUPS):
            rem = chunk - fill_ref[g]

            @pl.when(rem > 0)
            def _():
                info_ref[n_chunks + 1 + cur_ref[g]] = fill_ref[g]

            tails.append((rem, _rows(zbuf, 0, xs_ref, cur_ref[g] * chunk + fill_ref[g], rem, zsem)))
        spare = [(c, pltpu.make_async_copy(zbuf, xs_ref.at[pl.ds(c * chunk, chunk)], zsem))
                 for c in range(n_tiles * tile // chunk, n_chunks)]
        for rem, cp in tails:
            pl.when(rem > 0)(cp.start)
        for c, cp in spare:
            pl.when(c >= nfree)(cp.start)

        slot_out = jnp.int32(0)
        for g in range(N_GROUPS):
            def place(c, nxt, g=g):
                hit = info_ref[c] == g

                @pl.when(hit)
                def _():
                    order_ref[nxt] = c
                    rank_ref[c] = nxt

                return nxt + hit.astype(jnp.int32)

            slot_out = lax.fori_loop(0, n_chunks, place, slot_out)

        @pl.when(k >= 1)
        def _():
            wait_slot(1 - slot)

        wait_slot(slot)
        for rem, cp in tails:
            pl.when(rem > 0)(cp.wait)
        for c, cp in spare:
            pl.when(c >= nfree)(cp.wait)


def _moe_kernel(order_ref, info_ref, *refs, d, d_expert, n_chunks, per_step):
    xs_refs = refs[:per_step]
    wg_hbm, wu_hbm, wd_hbm, o_ref, wg_buf, wu_buf, wd_buf, slot_ref, wsem = refs[per_step:]
    chunk = xs_refs[0].shape[0]

    def weight_copies(g, sl):
        lo = g * EXPERTS_PER_GROUP
        return [pltpu.make_async_copy(src.at[pl.ds(lo, EXPERTS_PER_GROUP)], dst.at[sl], wsem.at[0])
                for src, dst in ((wg_hbm, wg_buf), (wu_hbm, wu_buf), (wd_hbm, wd_buf))]

    def first_copies(g, e):
        return [pltpu.make_async_copy(src.at[g * EXPERTS_PER_GROUP + e], dst.at[0, e], wsem.at[1 + e])
                for src, dst in ((wg_hbm, wg_buf), (wu_hbm, wu_buf), (wd_hbm, wd_buf))]

    for ci, xs_ref in enumerate(xs_refs):
        i = pl.program_id(0) * per_step + ci
        rows = slice(ci * chunk, (ci + 1) * chunk)
        c = order_ref[i]
        grp = info_ref[c]
        prev = info_ref[order_ref[jnp.maximum(i - 1, 0)]]

        @pl.when(i == 0)
        def _():
            slot_ref[0] = 0
            for e in range(EXPERTS_PER_GROUP):
                for cp in first_copies(grp, e):
                    cp.start()

        @pl.when((i == 0) | (grp != prev))
        def _():
            @pl.when(i > 0)
            def _():
                slot_ref[0] = 1 - slot_ref[0]
                for cp in weight_copies(grp, slot_ref[0]):
                    cp.wait()

            cur = slot_ref[0]

            def later_group(j, found):
                gj = info_ref[order_ref[j]]
                return jnp.where((found < 0) & (j > i) & (gj != grp), gj, found)

            nxt = lax.fori_loop(0, n_chunks, later_group, jnp.int32(-1))

            @pl.when(nxt >= 0)
            def _():
                for cp in weight_copies(nxt, 1 - cur):
                    cp.start()

        used = c < info_ref[n_chunks]
        half = chunk // 2
        low_only = info_ref[n_chunks + 1 + c] <= half

        def experts(m, xs_ref=xs_ref, grp=grp, await_first=False):
            xb = xs_ref[0:m, 0:d]
            ws = slot_ref[0]
            aux = xs_ref[0:m, d:]
            y = jnp.zeros((m, o_ref.shape[1]), _F32)
            for e in range(EXPERTS_PER_GROUP):
                if await_first:
                    for cp in first_copies(grp, e):
                        cp.wait()
                hg = _dot(xb, wg_buf[ws, e])
                hu = _dot(xb, wu_buf[ws, e])
                we = aux[:, e:e + 1] + aux[:, EXPERTS_PER_GROUP + e:EXPERTS_PER_GROUP + e + 1]
                hid = hg * _sigmoid(hg) * hu * we
                y = y + _dot(hid, wd_buf[ws, e])
            return y

        later = used
        if ci == 0:
            later = used & (i > 0)

            @pl.when(i == 0)
            def _():
                o_ref[rows, :] = experts(chunk, await_first=True)

        @pl.when(later & jnp.logical_not(low_only))
        def _():
            o_ref[rows, :] = experts(chunk)

        @pl.when(later & low_only)
        def _():
            o_ref[ci * chunk:ci * chunk + half, :] = experts(half)
            o_ref[ci * chunk + half:(ci + 1) * chunk, :] = jnp.zeros((half, o_ref.shape[1]), _F32)

        @pl.when(jnp.logical_not(used))
        def _():
            o_ref[rows, :] = jnp.zeros((chunk, o_ref.shape[1]), _F32)


def _ple_kernel(meta_ref, rank_ref, x1_ref, p_ref, pos_ref, ys_ref, wpg_ref, bpg_ref, wpp_ref, l2g_ref, l2b_ref,
                o_ref, ybuf, sem, *, alpha, tile, chunk, n_sub, tiles_per_step, t_pad, n_steps):
    k = pl.program_id(0)
    slot = k % 2

    def moved(r):
        return rank_ref[r // chunk] * chunk + r % chunk

    def pieces(step, sl):
        out = []
        for ti in range(tiles_per_step):
            kk = step * tiles_per_step + ti
            for g in range(N_GROUPS):
                r1, m1, r2, m2, sg = (meta_ref[kk, g * META_FIELDS + f] for f in range(META_FIELDS))
                out.append((m1, _rows(ys_ref, moved(r1), ybuf.at[sl, ti], sg, m1, sem.at[sl])))
                out.append((m2, _rows(ys_ref, moved(r2), ybuf.at[sl, ti], sg + m1, m2, sem.at[sl])))
        return out

    def fetch(kk, sl):
        for m, cp in pieces(kk, sl):
            pl.when(m > 0)(cp.start)

    @pl.when(k == 0)
    def _():
        ybuf[...] = jnp.zeros_like(ybuf)
        fetch(0, 0)

    @pl.when(k + 1 < n_steps)
    def _():
        fetch(k + 1, 1 - slot)

    for m, cp in pieces(k, slot):
        pl.when(m > 0)(cp.wait)

    sub = tile // n_sub
    for ti in range(tiles_per_step):
        yb = ybuf[slot, ti]
        pos = pos_ref[ti]
        for blk in range(n_sub):
            cols = slice(blk * sub, (blk + 1) * sub)
            rows = slice(ti * tile + blk * sub, ti * tile + (blk + 1) * sub)
            perm = jnp.where(lax.broadcasted_iota(jnp.int32, (t_pad, sub), 0) == pos[:, cols], 1.0, 0.0)
            y = lax.dot_general(perm, yb, _TN, preferred_element_type=_F32)
            r = alpha * x1_ref[rows, :] + y
            gate = _sigmoid(_dot(r, wpg_ref[...]) + bpg_ref[...])
            pp = _dot(p_ref[rows, :], wpp_ref[...])
            o_ref[rows, :] = _layer_norm(r + gate * pp, l2g_ref[...], l2b_ref[...])


def _full(shape):
    return pl.BlockSpec(shape, lambda *_: (0,) * len(shape))


def _resident(shape):
    return pl.BlockSpec(shape, lambda *_: (0,) * len(shape), pipeline_mode=pl.Buffered(1))


def _row(v):
    return v.reshape(1, -1).astype(_F32)


def kernel(x, p, ln_in_g, ln_in_b, w_in, b_in, conf_dw_w, conf_dw_b, conf_ln_g, conf_ln_b, sc_w, sc_b, w_out, b_out, ln1_g, ln1_b, w_rg, b_rg, w_re, b_re, w_gate, w_up, w_down, w_pg, b_pg, w_pp, ln2_g, ln2_b):
    depth = w_in.shape[0]
    assert depth == 1, "single-layer block"
    alpha = (2.0 * depth) ** 0.25
    bsz, seq, d = x.shape
    n = bsz * seq
    conf_k, conf_w = conf_dw_w.shape[1:]
    sc_k, sc_w_ = sc_w.shape[1:]
    d_in = w_in.shape[2]
    n_exp, _, d_expert = w_gate.shape[1:]
    assert n_exp == N_GROUPS * EXPERTS_PER_GROUP
    assert d_in == 2 * conf_w + 3 * sc_w_ and w_out.shape[1] == conf_w + sc_w_
    d_ple = p.shape[-1]

    tile = chunk = TILE
    assert seq % tile == 0 and (bsz * (seq // tile)) % PLE_TILES == 0
    nt = seq // tile
    n_tiles = bsz * nt
    a_halo = -(-(conf_k - 1) // SUBLANES) * SUBLANES
    u_halo = -(-(sc_k - 1) // SUBLANES) * SUBLANES
    t_pad = tile + N_GROUPS * SUBLANES
    n_chunks = -(-(n + (SUBLANES - 1) * N_GROUPS * n_tiles) // chunk) + N_GROUPS
    n_chunks = -(-n_chunks // MOE_CHUNKS) * MOE_CHUNKS
    d_pay = d + LANES

    n_logits = N_GROUPS + n_exp
    wr = jnp.concatenate([w_rg[0], w_re[0]], axis=1).T

    mixer = pl.pallas_call(
        functools.partial(_mixer_kernel, alpha=alpha, tile=tile, t_pad=t_pad, chunk=chunk,
                          n_chunks=n_chunks, n_tiles=n_tiles, conf_w=conf_w, sc_w_=sc_w_,
                          conf_k=conf_k, sc_k=sc_k, a_halo=a_halo, u_halo=u_halo, n_logits=n_logits),
        grid=(bsz, nt),
        in_specs=[
            pl.BlockSpec((1, tile, d), lambda b, j: (b, j, 0)),
            _full((1, d)), _full((1, d)),
            _resident((d, d_in)), _full((1, d_in)),
            _full((conf_k, conf_w)), _full((1, conf_w)), _full((1, conf_w)), _full((1, conf_w)),
            _full((sc_k, sc_w_)), _full((1, sc_w_)),
            _resident((conf_w + sc_w_, d)), _full((1, d)),
            _full((1, d)), _full((1, d)),
            _full((n_logits, d)), _full((1, N_GROUPS)), _full((1, n_exp)),
        ],
        out_specs=[
            pl.BlockSpec((1, tile, d), lambda b, j: (b, j, 0)),
            pl.BlockSpec((1, 1, tile), lambda b, j: (b * nt + j, 0, 0)),
            pl.BlockSpec(memory_space=pl.ANY),
            pl.BlockSpec(memory_space=pltpu.SMEM),
            pl.BlockSpec(memory_space=pltpu.SMEM),
            pl.BlockSpec(memory_space=pltpu.SMEM),
            pl.BlockSpec(memory_space=pltpu.SMEM),
        ],
        out_shape=[
            jax.ShapeDtypeStruct((bsz, seq, d), _F32),
            jax.ShapeDtypeStruct((n_tiles, 1, tile), jnp.int32),
            jax.ShapeDtypeStruct((n_chunks * chunk, d_pay), _F32),
            jax.ShapeDtypeStruct((n_tiles, N_GROUPS * META_FIELDS), jnp.int32),
            jax.ShapeDtypeStruct((2 * n_chunks + 1,), jnp.int32),
            jax.ShapeDtypeStruct((n_chunks,), jnp.int32),
            jax.ShapeDtypeStruct((n_chunks,), jnp.int32),
        ],
        scratch_shapes=[
            pltpu.VMEM((a_halo + tile, conf_w), _F32),
            pltpu.VMEM((SUBLANES - 1, a_halo + tile - SUBLANES, conf_w), _F32),
            pltpu.VMEM((u_halo + tile, sc_w_), _F32),
            pltpu.VMEM((d, d_in), _BF16),
            pltpu.VMEM((conf_w + sc_w_, d), _BF16),
            pltpu.VMEM((ROUTER_ROWS, d), _BF16),
            pltpu.VMEM((ROUTER_ROWS, 1), _F32),
            pltpu.VMEM((tile, tile), _BF16),
            pltpu.VMEM((2, t_pad, d_pay), _F32),
            pltpu.VMEM((chunk, d_pay), _F32),
            pltpu.SMEM((N_GROUPS,), jnp.int32),
            pltpu.SMEM((N_GROUPS,), jnp.int32),
            pltpu.SMEM((1,), jnp.int32),
            pltpu.SMEM((2, 2 * N_GROUPS), jnp.int32),
            pltpu.SemaphoreType.DMA((2,)),
            pltpu.SemaphoreType.DMA(()),
        ],
        compiler_params=pltpu.CompilerParams(
            dimension_semantics=("arbitrary", "arbitrary"), vmem_limit_bytes=VMEM_LIMIT_BYTES),
        name="mixer",
    )
    x1, pos, xs, meta, info, order, rank = mixer(
        x, _row(ln_in_g), _row(ln_in_b), w_in[0], _row(b_in[0]),
        conf_dw_w[0], _row(conf_dw_b[0]), _row(conf_ln_g[0]), _row(conf_ln_b[0]),
        sc_w[0], _row(sc_b[0]), w_out[0], _row(b_out[0]),
        _row(ln1_g[0]), _row(ln1_b[0]), wr, _row(b_rg[0]), _row(b_re[0]))

    moe = pl.pallas_call(
        functools.partial(_moe_kernel, d=d, d_expert=d_expert, n_chunks=n_chunks, per_step=MOE_CHUNKS),
        grid_spec=pltpu.PrefetchScalarGridSpec(
            num_scalar_prefetch=2,
            grid=(n_chunks // MOE_CHUNKS,),
            in_specs=[
                pl.BlockSpec((chunk, d_pay), functools.partial(
                    lambda i, order, info, ci: (order[i * MOE_CHUNKS + ci], 0), ci=ci))
                for ci in range(MOE_CHUNKS)
            ] + [
                pl.BlockSpec(memory_space=pl.ANY),
                pl.BlockSpec(memory_space=pl.ANY),
                pl.BlockSpec(memory_space=pl.ANY),
            ],
            out_specs=pl.BlockSpec((MOE_CHUNKS * chunk, d), lambda i, order, info: (i, 0)),
            scratch_shapes=[
                pltpu.VMEM((2, EXPERTS_PER_GROUP, d, d_expert), _F32),
                pltpu.VMEM((2, EXPERTS_PER_GROUP, d, d_expert), _F32),
                pltpu.VMEM((2, EXPERTS_PER_GROUP, d_expert, d), _F32),
                pltpu.SMEM((1,), jnp.int32),
                pltpu.SemaphoreType.DMA((1 + EXPERTS_PER_GROUP,)),
            ],
        ),
        out_shape=jax.ShapeDtypeStruct((n_chunks * chunk, d), _F32),
        compiler_params=pltpu.CompilerParams(
            dimension_semantics=("arbitrary",), vmem_limit_bytes=VMEM_LIMIT_BYTES),
        name="moe",
    )
    ys = moe(order, info, *([xs] * MOE_CHUNKS), w_gate[0], w_up[0], w_down[0])

    ple = pl.pallas_call(
        functools.partial(_ple_kernel, alpha=alpha, tile=tile, chunk=chunk, n_sub=N_SUB, tiles_per_step=PLE_TILES,
                          t_pad=t_pad, n_steps=n_tiles // PLE_TILES),
        grid_spec=pltpu.PrefetchScalarGridSpec(
            num_scalar_prefetch=2,
            grid=(n_tiles // PLE_TILES,),
            in_specs=[
                pl.BlockSpec((PLE_TILES * tile, d), lambda k, meta, rank: (k, 0)),
                pl.BlockSpec((PLE_TILES * tile, d_ple), lambda k, meta, rank: (k, 0)),
                pl.BlockSpec((PLE_TILES, 1, tile), lambda k, meta, rank: (k, 0, 0)),
                pl.BlockSpec(memory_space=pl.ANY),
                _resident((d, d)), _full((1, d)), _resident((d_ple, d)), _full((1, d)), _full((1, d)),
            ],
            out_specs=pl.BlockSpec((PLE_TILES * tile, d), lambda k, meta, rank: (k, 0)),
            scratch_shapes=[
                pltpu.VMEM((2, PLE_TILES, t_pad, d), _F32),
                pltpu.SemaphoreType.DMA((2,)),
            ],
        ),
        out_shape=jax.ShapeDtypeStruct((n, d), _F32),
        compiler_params=pltpu.CompilerParams(
            dimension_semantics=("arbitrary",), vmem_limit_bytes=VMEM_LIMIT_BYTES),
        name="ple",
    )
    out = ple(meta, rank, x1.reshape(n, d), p[0].reshape(n, d_ple), pos, ys,
              w_pg[0], _row(b_pg[0]), w_pp[0], _row(ln2_g[0]), _row(ln2_b[0]))
    return out.reshape(bsz, seq, d)
```

```python
import functools

import jax
import jax.numpy as jnp
from jax import lax
from jax.experimental import pallas as pl
from jax.experimental.pallas import tpu as pltpu

LN_EPS = 1e-5
N_GROUPS = 4
EXPERTS_PER_GROUP = 4
SUBLANES = 8
LANES = 128
ROUTER_ROWS = 32
TILE = 512
MOE_CHUNKS = 2
PLE_TILES = 2
N_SUB = 2
VMEM_LIMIT_BYTES = 56 * 1024 * 1024
NEG_BIG = -1e30
META_FIELDS = 5

_F32 = jnp.float32
_BF16 = jnp.bfloat16
_NT = (((1,), (1,)), ((), ()))
_TN = (((0,), (0,)), ((), ()))


def _layer_norm(x, g, b):
    mu = jnp.mean(x, axis=-1, keepdims=True)
    xc = x - mu
    var = jnp.mean(xc * xc, axis=-1, keepdims=True)
    return xc * lax.rsqrt(var + LN_EPS) * g + b


def _sigmoid(x):
    return 0.5 * jnp.tanh(0.5 * x) + 0.5


def _dot(a, b):
    return jnp.dot(a, b, preferred_element_type=_F32)


def _aligned(v):
    return v if isinstance(v, int) else pl.multiple_of(v, SUBLANES)


def _rows(src, s0, dst, d0, m, sem):
    return pltpu.make_async_copy(src.at[pl.ds(_aligned(s0), _aligned(m))],
                                 dst.at[pl.ds(_aligned(d0), _aligned(m))], sem)


def _route(logits):
    row = lax.broadcasted_iota(jnp.int32, (SUBLANES, logits.shape[1]), 0)
    real = row < N_GROUPS
    gl = jnp.where(real, logits[0:SUBLANES], NEG_BIG)
    gm = jnp.max(gl, axis=0, keepdims=True)
    ge = jnp.exp(gl - gm)
    gp = ge / jnp.sum(ge, axis=0, keepdims=True)
    gp_top = jnp.max(gp, axis=0, keepdims=True)
    gidx = jnp.min(jnp.where(gp == gp_top, row, SUBLANES), axis=0, keepdims=True)

    el = jnp.zeros_like(gl)
    for g in range(N_GROUPS):
        lo = N_GROUPS + g * EXPERTS_PER_GROUP
        blk = logits[lo - lo % SUBLANES:lo - lo % SUBLANES + SUBLANES]
        if lo % SUBLANES:
            blk = pltpu.roll(blk, SUBLANES - lo % SUBLANES, 0)
        el = jnp.where(gidx == g, blk, el)
    el = jnp.where(real, el, NEG_BIG)
    em = jnp.max(el, axis=0, keepdims=True)
    ee = jnp.exp(el - em)
    ep = ee / jnp.sum(ee, axis=0, keepdims=True)
    ep = jnp.where(real, ep, -1.0)
    p1 = jnp.max(ep, axis=0, keepdims=True)
    i1 = jnp.min(jnp.where(ep == p1, row, SUBLANES), axis=0, keepdims=True)
    ep2 = jnp.where(row == i1, -1.0, ep)
    p2 = jnp.max(ep2, axis=0, keepdims=True)
    i2 = jnp.min(jnp.where(ep2 == p2, row, SUBLANES), axis=0, keepdims=True)
    denom = p1 + p2
    w = jnp.where(row == i1, p1 / denom, jnp.where(row == i2, p2 / denom, 0.0))
    return gidx, w * gp_top


def _mixer_kernel(x_ref, lng_ref, lnb_ref, w_in_ref, b_in_ref, cw_ref, cb_ref, clg_ref, clb_ref,
                  sw_ref, sb_ref, w_out_ref, b_out_ref, l1g_ref, l1b_ref, wr_ref, brg_ref, bre_ref,
                  x1_ref, pos_ref, xs_ref, meta_ref, info_ref, order_ref, rank_ref,
                  abuf, ashift, ubuf, w_in_st, w_out_st, w_in_bf, w_out_bf, wr_bf, br_ref, tri_ref, stage, zbuf,
                  cur_ref, fill_ref, nfree_ref, ring_ref, sem, zsem, wsem,
                  *, alpha, tile, t_pad, chunk, n_chunks, n_tiles, conf_w, sc_w_,
                  conf_k, sc_k, a_halo, u_halo, n_logits):
    j = pl.program_id(1)
    k = pl.program_id(0) * pl.num_programs(1) + j
    slot = k % 2
    d = x1_ref.shape[-1]
    c0, c1, c2, c3 = conf_w, 2 * conf_w, 2 * conf_w + sc_w_, 2 * conf_w + 2 * sc_w_
    w_parts = (
        (w_in_ref.at[:, 0:c1], w_in_st.at[:, 0:c1], w_in_bf.at[:, 0:c1]),
        (w_in_ref.at[:, c1:], w_in_st.at[:, c1:], w_in_bf.at[:, c1:]),
        (w_out_ref, w_out_st, w_out_bf),
    )
    w_copies = [pltpu.make_async_copy(src, st, wsem.at[n]) for n, (src, st, _) in enumerate(w_parts)]

    def weights_ready(n):
        w_copies[n].wait()
        w_parts[n][2][...] = w_parts[n][1][...].astype(_BF16)

    @pl.when(k == 0)
    def _():
        for cp in w_copies:
            cp.start()
        weights_ready(0)
        wr_bf[...] = jnp.zeros_like(wr_bf)
        wr_bf[0:n_logits, :] = wr_ref[...].astype(_BF16)
        def column(b_ref, first_row):
            shape = (ROUTER_ROWS, b_ref.shape[1])
            on_diag = (lax.broadcasted_iota(jnp.int32, shape, 0)
                       == lax.broadcasted_iota(jnp.int32, shape, 1) + first_row)
            return jnp.sum(jnp.where(on_diag, b_ref[...], 0.0), axis=1, keepdims=True)

        br_ref[...] = column(brg_ref, 0) + column(bre_ref, N_GROUPS)
        earlier = lax.broadcasted_iota(jnp.int32, (tile, tile), 0) < lax.broadcasted_iota(jnp.int32, (tile, tile), 1)
        tri_ref[...] = jnp.where(earlier, 1.0, 0.0).astype(_BF16)
        nfree_ref[0] = 0
        for g in range(N_GROUPS):
            cur_ref[g] = 0
            fill_ref[g] = chunk
        for c in range(n_chunks + 1):
            info_ref[c] = N_GROUPS - 1
        for c in range(n_chunks):
            info_ref[n_chunks + 1 + c] = chunk

    @pl.when(j == 0)
    def _():
        abuf[0:a_halo, :] = jnp.zeros((a_halo, conf_w), _F32)
        ubuf[0:u_halo, :] = jnp.zeros((u_halo, sc_w_), _F32)

    x0 = _layer_norm(x_ref[0], lng_ref[...], lnb_ref[...])
    x0b = x0.astype(_BF16)
    h = _dot(x0b, w_in_bf[:, 0:c1]) + b_in_ref[:, 0:c1]

    a = h[:, 0:c0] * _sigmoid(h[:, c0:c1])
    abuf[a_halo:a_halo + tile, :] = a
    for i in range(1, SUBLANES):
        ashift[i - 1] = abuf[i:i + tile + a_halo - SUBLANES, :]
    acc = jnp.broadcast_to(cb_ref[...], (tile, conf_w))
    for t in range(conf_k):
        q, i = divmod(a_halo - (conf_k - 1) + t, SUBLANES)
        if i == 0:
            src = abuf[SUBLANES * q:SUBLANES * q + tile, :]
        else:
            src = ashift[i - 1, SUBLANES * q:SUBLANES * q + tile, :]
        acc = acc + cw_ref[t:t + 1, :] * src
    abuf[0:a_halo, :] = abuf[tile:tile + a_halo, :]

    @pl.when(k == 0)
    def _():
        weights_ready(1)
        weights_ready(2)

    h2 = _dot(x0b, w_in_bf[:, c1:]) + b_in_ref[:, c1:]
    an = _layer_norm(acc, clg_ref[...], clb_ref[...])
    a2 = an * _sigmoid(an)

    u = h2[:, c2 - c1:c3 - c1] * h2[:, c3 - c1:]
    ubuf[u_halo:u_halo + tile, :] = u
    sc = jnp.broadcast_to(sb_ref[...], (tile, sc_w_))
    for t in range(sc_k):
        off = u_halo - (sc_k - 1) + t
        sc = sc + sw_ref[t:t + 1, :] * ubuf[off:off + tile, :]
    ubuf[0:u_halo, :] = ubuf[tile:tile + u_halo, :]
    s = h2[:, 0:c2 - c1] * sc

    mix = (_dot(a2.astype(_BF16), w_out_bf[0:conf_w, :])
           + _dot(s.astype(_BF16), w_out_bf[conf_w:, :]) + b_out_ref[...])
    x1 = _layer_norm(alpha * x0 + mix, l1g_ref[...], l1b_ref[...])
    x1_ref[0] = x1
    x1b = x1.astype(_BF16)

    logits = lax.dot_general(wr_bf[...], x1b, _NT, preferred_element_type=_F32) + br_ref[...]
    gidx, w = _route(logits)

    row8 = lax.broadcasted_iota(jnp.int32, (SUBLANES, tile), 0)
    onehot = jnp.where(gidx == row8, 1.0, 0.0)
    cnt = jnp.sum(onehot, axis=1, keepdims=True)
    sizes, starts = [], []
    start = jnp.int32(0)
    startv = jnp.zeros((SUBLANES, tile), jnp.int32)
    for g in range(N_GROUPS):
        n8 = (cnt[g, 0].astype(jnp.int32) + (SUBLANES - 1)) & (-SUBLANES)
        sizes.append(n8)
        starts.append(start)
        startv = jnp.where(row8 == g, start, startv)
        start = start + n8
    cum = _dot(onehot.astype(_BF16), tri_ref[...])
    pos = jnp.sum(onehot * (startv.astype(_F32) + cum), axis=0, keepdims=True).astype(jnp.int32)
    pos_ref[0] = pos
    perm = jnp.where(lax.broadcasted_iota(jnp.int32, (t_pad, tile), 0) == pos, 1.0, 0.0).astype(_BF16)

    w_hi = w.astype(_BF16).astype(_F32)
    w_lo = (w - w_hi).astype(_BF16).astype(_F32)
    w_hl = jnp.where(row8 < EXPERTS_PER_GROUP, w_hi, pltpu.roll(w_lo, EXPERTS_PER_GROUP, 0))
    w_rows = jnp.concatenate([w_hl, jnp.zeros((LANES - SUBLANES, tile), _F32)], axis=0).astype(_BF16)

    def wait_slot(sl):
        for g in range(N_GROUPS):
            for piece in range(2):
                m = ring_ref[sl, 2 * g + piece]

                @pl.when(m > 0)
                def _():
                    _rows(stage.at[sl], 0, xs_ref, 0, m, sem.at[sl]).wait()

    @pl.when(k >= 2)
    def _():
        wait_slot(slot)

    stage[slot, :, 0:d] = _dot(perm, x1b)
    stage[slot, :, d:] = lax.dot_general(perm, w_rows, _NT, preferred_element_type=_F32)

    for g in range(N_GROUPS):
        n8, sg = sizes[g], starts[g]
        fill, cur, newc = fill_ref[g], cur_ref[g], nfree_ref[0]
        m1 = jnp.minimum(n8, chunk - fill)
        m2 = n8 - m1
        r1 = cur * chunk + fill
        r2 = newc * chunk
        ring_ref[slot, 2 * g] = m1
        ring_ref[slot, 2 * g + 1] = m2
        for f, v in enumerate((r1, m1, r2, m2, sg)):
            meta_ref[k, g * META_FIELDS + f] = v

        @pl.when(m1 > 0)
        def _():
            _rows(stage.at[slot], sg, xs_ref, r1, m1, sem.at[slot]).start()

        @pl.when(m2 > 0)
        def _():
            _rows(stage.at[slot], sg + m1, xs_ref, r2, m2, sem.at[slot]).start()
            info_ref[newc] = g
            cur_ref[g] = newc
            nfree_ref[0] = newc + 1

        fill_ref[g] = jnp.where(m2 > 0, m2, fill + m1)

    @pl.when(k == n_tiles - 1)
    def _():
        zbuf[...] = jnp.zeros_like(zbuf)
        nfree = nfree_ref[0]
        info_ref[n_chunks] = nfree
        tails = []
        for g in range(N_GROUPS):
            rem = chunk - fill_ref[g]

            @pl.when(rem > 0)
            def _():
                info_ref[n_chunks + 1 + cur_ref[g]] = fill_ref[g]

            tails.append((rem, _rows(zbuf, 0, xs_ref, cur_ref[g] * chunk + fill_ref[g], rem, zsem)))
        spare = [(c, pltpu.make_async_copy(zbuf, xs_ref.at[pl.ds(c * chunk, chunk)], zsem))
                 for c in range(n_tiles * tile // chunk, n_chunks)]
        for rem, cp in tails:
            pl.when(rem > 0)(cp.start)
        for c, cp in spare:
            pl.when(c >= nfree)(cp.start)

        slot_out = jnp.int32(0)
        for g in range(N_GROUPS):
            def place(c, nxt, g=g):
                hit = info_ref[c] == g

                @pl.when(hit)
                def _():
                    order_ref[nxt] = c
                    rank_ref[c] = nxt

                return nxt + hit.astype(jnp.int32)

            slot_out = lax.fori_loop(0, n_chunks, place, slot_out)

        @pl.when(k >= 1)
        def _():
            wait_slot(1 - slot)

        wait_slot(slot)
        for rem, cp in tails:
            pl.when(rem > 0)(cp.wait)
        for c, cp in spare:
            pl.when(c >= nfree)(cp.wait)


def _moe_kernel(order_ref, info_ref, *refs, d, d_expert, n_chunks, per_step):
    xs_refs = refs[:per_step]
    wg_hbm, wu_hbm, wd_hbm, o_ref, wg_buf, wu_buf, wd_buf, slot_ref, wsem = refs[per_step:]
    chunk = xs_refs[0].shape[0]

    def weight_copies(g, sl):
        lo = g * EXPERTS_PER_GROUP
        return [pltpu.make_async_copy(src.at[pl.ds(lo, EXPERTS_PER_GROUP)], dst.at[sl], wsem.at[0])
                for src, dst in ((wg_hbm, wg_buf), (wu_hbm, wu_buf), (wd_hbm, wd_buf))]

    def first_copies(g, e):
        return [pltpu.make_async_copy(src.at[g * EXPERTS_PER_GROUP + e], dst.at[0, e], wsem.at[1 + e])
                for src, dst in ((wg_hbm, wg_buf), (wu_hbm, wu_buf), (wd_hbm, wd_buf))]

    for ci, xs_ref in enumerate(xs_refs):
        i = pl.program_id(0) * per_step + ci
        rows = slice(ci * chunk, (ci + 1) * chunk)
        c = order_ref[i]
        grp = info_ref[c]
        prev = info_ref[order_ref[jnp.maximum(i - 1, 0)]]

        @pl.when(i == 0)
        def _():
            slot_ref[0] = 0
            for e in range(EXPERTS_PER_GROUP):
                for cp in first_copies(grp, e):
                    cp.start()

        @pl.when((i == 0) | (grp != prev))
        def _():
            @pl.when(i > 0)
            def _():
                slot_ref[0] = 1 - slot_ref[0]
                for cp in weight_copies(grp, slot_ref[0]):
                    cp.wait()

            cur = slot_ref[0]

            def later_group(j, found):
                gj = info_ref[order_ref[j]]
                return jnp.where((found < 0) & (j > i) & (gj != grp), gj, found)

            nxt = lax.fori_loop(0, n_chunks, later_group, jnp.int32(-1))

            @pl.when(nxt >= 0)
            def _():
                for cp in weight_copies(nxt, 1 - cur):
                    cp.start()

        used = c < info_ref[n_chunks]
        half = chunk // 2
        low_only = info_ref[n_chunks + 1 + c] <= half

        def experts(m, xs_ref=xs_ref, grp=grp, await_first=False):
            xb = xs_ref[0:m, 0:d]
            ws = slot_ref[0]
            aux = xs_ref[0:m, d:]
            y = jnp.zeros((m, o_ref.shape[1]), _F32)
            for e in range(EXPERTS_PER_GROUP):
                if await_first:
                    for cp in first_copies(grp, e):
                        cp.wait()
                hg = _dot(xb, wg_buf[ws, e])
                hu = _dot(xb, wu_buf[ws, e])
                we = aux[:, e:e + 1] + aux[:, EXPERTS_PER_GROUP + e:EXPERTS_PER_GROUP + e + 1]
                hid = hg * _sigmoid(hg) * hu * we
                y = y + _dot(hid, wd_buf[ws, e])
            return y

        later = used
        if ci == 0:
            later = used & (i > 0)

            @pl.when(i == 0)
            def _():
                o_ref[rows, :] = experts(chunk, await_first=True)

        @pl.when(later & jnp.logical_not(low_only))
        def _():
            o_ref[rows, :] = experts(chunk)

        @pl.when(later & low_only)
        def _():
            o_ref[ci * chunk:ci * chunk + half, :] = experts(half)
            o_ref[ci * chunk + half:(ci + 1) * chunk, :] = jnp.zeros((half, o_ref.shape[1]), _F32)

        @pl.when(jnp.logical_not(used))
        def _():
            o_ref[rows, :] = jnp.zeros((chunk, o_ref.shape[1]), _F32)


def _ple_kernel(meta_ref, rank_ref, x1_ref, p_ref, pos_ref, ys_ref, wpg_ref, bpg_ref, wpp_ref, l2g_ref, l2b_ref,
                o_ref, ybuf, sem, *, alpha, tile, chunk, n_sub, tiles_per_step, t_pad, n_steps):
    k = pl.program_id(0)
    slot = k % 2

    def moved(r):
        return rank_ref[r // chunk] * chunk + r % chunk

    def pieces(step, sl):
        out = []
        for ti in range(tiles_per_step):
            kk = step * tiles_per_step + ti
            for g in range(N_GROUPS):
                r1, m1, r2, m2, sg = (meta_ref[kk, g * META_FIELDS + f] for f in range(META_FIELDS))
                out.append((m1, _rows(ys_ref, moved(r1), ybuf.at[sl, ti], sg, m1, sem.at[sl])))
                out.append((m2, _rows(ys_ref, moved(r2), ybuf.at[sl, ti], sg + m1, m2, sem.at[sl])))
        return out

    def fetch(kk, sl):
        for m, cp in pieces(kk, sl):
            pl.when(m > 0)(cp.start)

    @pl.when(k == 0)
    def _():
        ybuf[:, :, tile:, :] = jnp.zeros((2, tiles_per_step, t_pad - tile, ybuf.shape[-1]), _F32)
        fetch(0, 0)

    @pl.when(k + 1 < n_steps)
    def _():
        fetch(k + 1, 1 - slot)

    for m, cp in pieces(k, slot):
        pl.when(m > 0)(cp.wait)

    sub = tile // n_sub
    for ti in range(tiles_per_step):
        yb = ybuf[slot, ti]
        pos = pos_ref[ti]
        for blk in range(n_sub):
            cols = slice(blk * sub, (blk + 1) * sub)
            rows = slice(ti * tile + blk * sub, ti * tile + (blk + 1) * sub)
            perm = jnp.where(lax.broadcasted_iota(jnp.int32, (t_pad, sub), 0) == pos[:, cols], 1.0, 0.0)
            y = lax.dot_general(perm, yb, _TN, preferred_element_type=_F32)
            r = alpha * x1_ref[rows, :] + y
            gate = _sigmoid(_dot(r, wpg_ref[...]) + bpg_ref[...])
            pp = _dot(p_ref[rows, :], wpp_ref[...])
            o_ref[rows, :] = _layer_norm(r + gate * pp, l2g_ref[...], l2b_ref[...])


def _full(shape):
    return pl.BlockSpec(shape, lambda *_: (0,) * len(shape))


def _resident(shape):
    return pl.BlockSpec(shape, lambda *_: (0,) * len(shape), pipeline_mode=pl.Buffered(1))


def _row(v):
    return v.reshape(1, -1).astype(_F32)


def kernel(x, p, ln_in_g, ln_in_b, w_in, b_in, conf_dw_w, conf_dw_b, conf_ln_g, conf_ln_b, sc_w, sc_b, w_out, b_out, ln1_g, ln1_b, w_rg, b_rg, w_re, b_re, w_gate, w_up, w_down, w_pg, b_pg, w_pp, ln2_g, ln2_b):
    depth = w_in.shape[0]
    assert depth == 1, "single-layer block"
    alpha = (2.0 * depth) ** 0.25
    bsz, seq, d = x.shape
    n = bsz * seq
    conf_k, conf_w = conf_dw_w.shape[1:]
    sc_k, sc_w_ = sc_w.shape[1:]
    d_in = w_in.shape[2]
    n_exp, _, d_expert = w_gate.shape[1:]
    assert n_exp == N_GROUPS * EXPERTS_PER_GROUP
    assert d_in == 2 * conf_w + 3 * sc_w_ and w_out.shape[1] == conf_w + sc_w_
    d_ple = p.shape[-1]

    tile = chunk = TILE
    assert seq % tile == 0 and (bsz * (seq // tile)) % PLE_TILES == 0
    nt = seq // tile
    n_tiles = bsz * nt
    a_halo = -(-(conf_k - 1) // SUBLANES) * SUBLANES
    u_halo = -(-(sc_k - 1) // SUBLANES) * SUBLANES
    t_pad = tile + N_GROUPS * SUBLANES
    n_chunks = -(-(n + (SUBLANES - 1) * N_GROUPS * n_tiles) // chunk) + N_GROUPS
    n_chunks = -(-n_chunks // MOE_CHUNKS) * MOE_CHUNKS
    d_pay = d + LANES

    n_logits = N_GROUPS + n_exp
    wr = jnp.concatenate([w_rg[0], w_re[0]], axis=1).T

    mixer = pl.pallas_call(
        functools.partial(_mixer_kernel, alpha=alpha, tile=tile, t_pad=t_pad, chunk=chunk,
                          n_chunks=n_chunks, n_tiles=n_tiles, conf_w=conf_w, sc_w_=sc_w_,
                          conf_k=conf_k, sc_k=sc_k, a_halo=a_halo, u_halo=u_halo, n_logits=n_logits),
        grid=(bsz, nt),
        in_specs=[
            pl.BlockSpec((1, tile, d), lambda b, j: (b, j, 0)),
            _full((1, d)), _full((1, d)),
            pl.BlockSpec(memory_space=pl.ANY), _full((1, d_in)),
            _full((conf_k, conf_w)), _full((1, conf_w)), _full((1, conf_w)), _full((1, conf_w)),
            _full((sc_k, sc_w_)), _full((1, sc_w_)),
            pl.BlockSpec(memory_space=pl.ANY), _full((1, d)),
            _full((1, d)), _full((1, d)),
            _full((n_logits, d)), _full((1, N_GROUPS)), _full((1, n_exp)),
        ],
        out_specs=[
            pl.BlockSpec((1, tile, d), lambda b, j: (b, j, 0)),
            pl.BlockSpec((1, 1, tile), lambda b, j: (b * nt + j, 0, 0)),
            pl.BlockSpec(memory_space=pl.ANY),
            pl.BlockSpec(memory_space=pltpu.SMEM),
            pl.BlockSpec(memory_space=pltpu.SMEM),
            pl.BlockSpec(memory_space=pltpu.SMEM),
            pl.BlockSpec(memory_space=pltpu.SMEM),
        ],
        out_shape=[
            jax.ShapeDtypeStruct((bsz, seq, d), _F32),
            jax.ShapeDtypeStruct((n_tiles, 1, tile), jnp.int32),
            jax.ShapeDtypeStruct((n_chunks * chunk, d_pay), _F32),
            jax.ShapeDtypeStruct((n_tiles, N_GROUPS * META_FIELDS), jnp.int32),
            jax.ShapeDtypeStruct((2 * n_chunks + 1,), jnp.int32),
            jax.ShapeDtypeStruct((n_chunks,), jnp.int32),
            jax.ShapeDtypeStruct((n_chunks,), jnp.int32),
        ],
        scratch_shapes=[
            pltpu.VMEM((a_halo + tile, conf_w), _F32),
            pltpu.VMEM((SUBLANES - 1, a_halo + tile - SUBLANES, conf_w), _F32),
            pltpu.VMEM((u_halo + tile, sc_w_), _F32),
            pltpu.VMEM((d, d_in), _F32),
            pltpu.VMEM((conf_w + sc_w_, d), _F32),
            pltpu.VMEM((d, d_in), _BF16),
            pltpu.VMEM((conf_w + sc_w_, d), _BF16),
            pltpu.VMEM((ROUTER_ROWS, d), _BF16),
            pltpu.VMEM((ROUTER_ROWS, 1), _F32),
            pltpu.VMEM((tile, tile), _BF16),
            pltpu.VMEM((2, t_pad, d_pay), _F32),
            pltpu.VMEM((chunk, d_pay), _F32),
            pltpu.SMEM((N_GROUPS,), jnp.int32),
            pltpu.SMEM((N_GROUPS,), jnp.int32),
            pltpu.SMEM((1,), jnp.int32),
            pltpu.SMEM((2, 2 * N_GROUPS), jnp.int32),
            pltpu.SemaphoreType.DMA((2,)),
            pltpu.SemaphoreType.DMA(()),
            pltpu.SemaphoreType.DMA((3,)),
        ],
        compiler_params=pltpu.CompilerParams(
            dimension_semantics=("arbitrary", "arbitrary"), vmem_limit_bytes=VMEM_LIMIT_BYTES),
        name="mixer",
    )
    x1, pos, xs, meta, info, order, rank = mixer(
        x, _row(ln_in_g), _row(ln_in_b), w_in[0], _row(b_in[0]),
        conf_dw_w[0], _row(conf_dw_b[0]), _row(conf_ln_g[0]), _row(conf_ln_b[0]),
        sc_w[0], _row(sc_b[0]), w_out[0], _row(b_out[0]),
        _row(ln1_g[0]), _row(ln1_b[0]), wr, _row(b_rg[0]), _row(b_re[0]))

    moe = pl.pallas_call(
        functools.partial(_moe_kernel, d=d, d_expert=d_expert, n_chunks=n_chunks, per_step=MOE_CHUNKS),
        grid_spec=pltpu.PrefetchScalarGridSpec(
            num_scalar_prefetch=2,
            grid=(n_chunks // MOE_CHUNKS,),
            in_specs=[
                pl.BlockSpec((chunk, d_pay), functools.partial(
                    lambda i, order, info, ci: (order[i * MOE_CHUNKS + ci], 0), ci=ci))
                for ci in range(MOE_CHUNKS)
            ] + [
                pl.BlockSpec(memory_space=pl.ANY),
                pl.BlockSpec(memory_space=pl.ANY),
                pl.BlockSpec(memory_space=pl.ANY),
            ],
            out_specs=pl.BlockSpec((MOE_CHUNKS * chunk, d), lambda i, order, info: (i, 0)),
            scratch_shapes=[
                pltpu.VMEM((2, EXPERTS_PER_GROUP, d, d_expert), _F32),
                pltpu.VMEM((2, EXPERTS_PER_GROUP, d, d_expert), _F32),
                pltpu.VMEM((2, EXPERTS_PER_GROUP, d_expert, d), _F32),
                pltpu.SMEM((1,), jnp.int32),
                pltpu.SemaphoreType.DMA((1 + EXPERTS_PER_GROUP,)),
            ],
        ),
        out_shape=jax.ShapeDtypeStruct((n_chunks * chunk, d), _F32),
        compiler_params=pltpu.CompilerParams(
            dimension_semantics=("arbitrary",), vmem_limit_bytes=VMEM_LIMIT_BYTES),
        name="moe",
    )
    ys = moe(order, info, *([xs] * MOE_CHUNKS), w_gate[0], w_up[0], w_down[0])

    ple = pl.pallas_call(
        functools.partial(_ple_kernel, alpha=alpha, tile=tile, chunk=chunk, n_sub=N_SUB, tiles_per_step=PLE_TILES,
                          t_pad=t_pad, n_steps=n_tiles // PLE_TILES),
        grid_spec=pltpu.PrefetchScalarGridSpec(
            num_scalar_prefetch=2,
            grid=(n_tiles // PLE_TILES,),
            in_specs=[
                pl.BlockSpec((PLE_TILES * tile, d), lambda k, meta, rank: (k, 0)),
                pl.BlockSpec((PLE_TILES * tile, d_ple), lambda k, meta, rank: (k, 0)),
                pl.BlockSpec((PLE_TILES, 1, tile), lambda k, meta, rank: (k, 0, 0)),
                pl.BlockSpec(memory_space=pl.ANY),
                _resident((d, d)), _full((1, d)), _resident((d_ple, d)), _full((1, d)), _full((1, d)),
            ],
            out_specs=pl.BlockSpec((PLE_TILES * tile, d), lambda k, meta, rank: (k, 0)),
            scratch_shapes=[
                pltpu.VMEM((2, PLE_TILES, t_pad, d), _F32),
                pltpu.SemaphoreType.DMA((2,)),
            ],
        ),
        out_shape=jax.ShapeDtypeStruct((n, d), _F32),
        compiler_params=pltpu.CompilerParams(
            dimension_semantics=("arbitrary",), vmem_limit_bytes=VMEM_LIMIT_BYTES),
        name="ple",
    )
    out = ple(meta, rank, x1.reshape(n, d), p[0].reshape(n, d_ple), pos, ys,
              w_pg[0], _row(b_pg[0]), w_pp[0], _row(ln2_g[0]), _row(ln2_b[0]))
    return out.reshape(bsz, seq, d)
```

```python
import functools

import jax
import jax.numpy as jnp
from jax import lax
from jax.experimental import pallas as pl
from jax.experimental.pallas import tpu as pltpu

LN_EPS = 1e-5
N_GROUPS = 4
EXPERTS_PER_GROUP = 4
SUBLANES = 8
LANES = 128
ROUTER_ROWS = 32
TILE = 512
MOE_CHUNKS = 2
MOE_ROW_STEPS = 4
PLE_TILES = 2
N_SUB = 2
VMEM_LIMIT_BYTES = 56 * 1024 * 1024
NEG_BIG = -1e30
META_FIELDS = 5

_F32 = jnp.float32
_BF16 = jnp.bfloat16
_NT = (((1,), (1,)), ((), ()))
_TN = (((0,), (0,)), ((), ()))


def _layer_norm(x, g, b):
    mu = jnp.mean(x, axis=-1, keepdims=True)
    xc = x - mu
    var = jnp.mean(xc * xc, axis=-1, keepdims=True)
    return xc * lax.rsqrt(var + LN_EPS) * g + b


def _sigmoid(x):
    return 0.5 * jnp.tanh(0.5 * x) + 0.5


def _dot(a, b):
    return jnp.dot(a, b, preferred_element_type=_F32)


def _aligned(v):
    return v if isinstance(v, int) else pl.multiple_of(v, SUBLANES)


def _rows(src, s0, dst, d0, m, sem):
    return pltpu.make_async_copy(src.at[pl.ds(_aligned(s0), _aligned(m))],
                                 dst.at[pl.ds(_aligned(d0), _aligned(m))], sem)


def _route(logits):
    row = lax.broadcasted_iota(jnp.int32, (SUBLANES, logits.shape[1]), 0)
    real = row < N_GROUPS
    gl = jnp.where(real, logits[0:SUBLANES], NEG_BIG)
    gm = jnp.max(gl, axis=0, keepdims=True)
    ge = jnp.exp(gl - gm)
    gp = ge / jnp.sum(ge, axis=0, keepdims=True)
    gp_top = jnp.max(gp, axis=0, keepdims=True)
    gidx = jnp.min(jnp.where(gp == gp_top, row, SUBLANES), axis=0, keepdims=True)

    el = jnp.zeros_like(gl)
    for g in range(N_GROUPS):
        lo = N_GROUPS + g * EXPERTS_PER_GROUP
        blk = logits[lo - lo % SUBLANES:lo - lo % SUBLANES + SUBLANES]
        if lo % SUBLANES:
            blk = pltpu.roll(blk, SUBLANES - lo % SUBLANES, 0)
        el = jnp.where(gidx == g, blk, el)
    el = jnp.where(real, el, NEG_BIG)
    em = jnp.max(el, axis=0, keepdims=True)
    ee = jnp.exp(el - em)
    ep = ee / jnp.sum(ee, axis=0, keepdims=True)
    ep = jnp.where(real, ep, -1.0)
    p1 = jnp.max(ep, axis=0, keepdims=True)
    i1 = jnp.min(jnp.where(ep == p1, row, SUBLANES), axis=0, keepdims=True)
    ep2 = jnp.where(row == i1, -1.0, ep)
    p2 = jnp.max(ep2, axis=0, keepdims=True)
    i2 = jnp.min(jnp.where(ep2 == p2, row, SUBLANES), axis=0, keepdims=True)
    denom = p1 + p2
    w = jnp.where(row == i1, p1 / denom, jnp.where(row == i2, p2 / denom, 0.0))
    return gidx, w * gp_top


def _mixer_kernel(x_ref, lng_ref, lnb_ref, w_in_ref, b_in_ref, cw_ref, cb_ref, clg_ref, clb_ref,
                  sw_ref, sb_ref, w_out_ref, b_out_ref, l1g_ref, l1b_ref, wr_ref, brg_ref, bre_ref,
                  x1_ref, pos_ref, xs_ref, meta_ref, info_ref, order_ref, rank_ref,
                  abuf, ashift, ubuf, w_in_st, w_out_st, w_in_bf, w_out_bf, wr_bf, br_ref, tri_ref, stage, zbuf,
                  cur_ref, fill_ref, nfree_ref, ring_ref, sem, zsem, wsem,
                  *, alpha, tile, t_pad, chunk, n_chunks, n_tiles, conf_w, sc_w_,
                  conf_k, sc_k, a_halo, u_halo, n_logits):
    j = pl.program_id(1)
    k = pl.program_id(0) * pl.num_programs(1) + j
    slot = k % 2
    d = x1_ref.shape[-1]
    c0, c1, c2, c3 = conf_w, 2 * conf_w, 2 * conf_w + sc_w_, 2 * conf_w + 2 * sc_w_
    w_parts = (
        (w_in_ref.at[:, 0:c1], w_in_st.at[:, 0:c1], w_in_bf.at[:, 0:c1]),
        (w_in_ref.at[:, c1:], w_in_st.at[:, c1:], w_in_bf.at[:, c1:]),
        (w_out_ref, w_out_st, w_out_bf),
    )
    w_copies = [pltpu.make_async_copy(src, st, wsem.at[n]) for n, (src, st, _) in enumerate(w_parts)]

    def weights_ready(n):
        w_copies[n].wait()
        w_parts[n][2][...] = w_parts[n][1][...].astype(_BF16)

    @pl.when(k == 0)
    def _():
        for cp in w_copies:
            cp.start()
        weights_ready(0)
        wr_bf[...] = jnp.zeros_like(wr_bf)
        wr_bf[0:n_logits, :] = wr_ref[...].astype(_BF16)
        def column(b_ref, first_row):
            shape = (ROUTER_ROWS, b_ref.shape[1])
            on_diag = (lax.broadcasted_iota(jnp.int32, shape, 0)
                       == lax.broadcasted_iota(jnp.int32, shape, 1) + first_row)
            return jnp.sum(jnp.where(on_diag, b_ref[...], 0.0), axis=1, keepdims=True)

        br_ref[...] = column(brg_ref, 0) + column(bre_ref, N_GROUPS)
        earlier = lax.broadcasted_iota(jnp.int32, (tile, tile), 0) < lax.broadcasted_iota(jnp.int32, (tile, tile), 1)
        tri_ref[...] = jnp.where(earlier, 1.0, 0.0).astype(_BF16)
        nfree_ref[0] = 0
        for g in range(N_GROUPS):
            cur_ref[g] = 0
            fill_ref[g] = chunk
        for c in range(n_chunks + 1):
            info_ref[c] = N_GROUPS - 1
        for c in range(n_chunks):
            info_ref[n_chunks + 1 + c] = chunk

    @pl.when(j == 0)
    def _():
        abuf[0:a_halo, :] = jnp.zeros((a_halo, conf_w), _F32)
        ubuf[0:u_halo, :] = jnp.zeros((u_halo, sc_w_), _F32)

    x0 = _layer_norm(x_ref[0], lng_ref[...], lnb_ref[...])
    x0b = x0.astype(_BF16)
    h = _dot(x0b, w_in_bf[:, 0:c1]) + b_in_ref[:, 0:c1]

    a = h[:, 0:c0] * _sigmoid(h[:, c0:c1])
    abuf[a_halo:a_halo + tile, :] = a
    for i in range(1, SUBLANES):
        ashift[i - 1] = abuf[i:i + tile + a_halo - SUBLANES, :]
    acc = jnp.broadcast_to(cb_ref[...], (tile, conf_w))
    for t in range(conf_k):
        q, i = divmod(a_halo - (conf_k - 1) + t, SUBLANES)
        if i == 0:
            src = abuf[SUBLANES * q:SUBLANES * q + tile, :]
        else:
            src = ashift[i - 1, SUBLANES * q:SUBLANES * q + tile, :]
        acc = acc + cw_ref[t:t + 1, :] * src
    abuf[0:a_halo, :] = abuf[tile:tile + a_halo, :]

    @pl.when(k == 0)
    def _():
        weights_ready(1)
        weights_ready(2)

    h2 = _dot(x0b, w_in_bf[:, c1:]) + b_in_ref[:, c1:]
    an = _layer_norm(acc, clg_ref[...], clb_ref[...])
    a2 = an * _sigmoid(an)

    u = h2[:, c2 - c1:c3 - c1] * h2[:, c3 - c1:]
    ubuf[u_halo:u_halo + tile, :] = u
    sc = jnp.broadcast_to(sb_ref[...], (tile, sc_w_))
    for t in range(sc_k):
        off = u_halo - (sc_k - 1) + t
        sc = sc + sw_ref[t:t + 1, :] * ubuf[off:off + tile, :]
    ubuf[0:u_halo, :] = ubuf[tile:tile + u_halo, :]
    s = h2[:, 0:c2 - c1] * sc

    mix = (_dot(a2.astype(_BF16), w_out_bf[0:conf_w, :])
           + _dot(s.astype(_BF16), w_out_bf[conf_w:, :]) + b_out_ref[...])
    x1 = _layer_norm(alpha * x0 + mix, l1g_ref[...], l1b_ref[...])
    x1_ref[0] = x1
    x1b = x1.astype(_BF16)

    logits = lax.dot_general(wr_bf[...], x1b, _NT, preferred_element_type=_F32) + br_ref[...]
    gidx, w = _route(logits)

    row8 = lax.broadcasted_iota(jnp.int32, (SUBLANES, tile), 0)
    onehot = jnp.where(gidx == row8, 1.0, 0.0)
    cnt = jnp.sum(onehot, axis=1, keepdims=True)
    sizes, starts = [], []
    start = jnp.int32(0)
    startv = jnp.zeros((SUBLANES, tile), jnp.int32)
    for g in range(N_GROUPS):
        n8 = (cnt[g, 0].astype(jnp.int32) + (SUBLANES - 1)) & (-SUBLANES)
        sizes.append(n8)
        starts.append(start)
        startv = jnp.where(row8 == g, start, startv)
        start = start + n8
    cum = _dot(onehot.astype(_BF16), tri_ref[...])
    pos = jnp.sum(onehot * (startv.astype(_F32) + cum), axis=0, keepdims=True).astype(jnp.int32)
    pos_ref[0] = pos
    perm = jnp.where(lax.broadcasted_iota(jnp.int32, (t_pad, tile), 0) == pos, 1.0, 0.0).astype(_BF16)

    w_hi = w.astype(_BF16).astype(_F32)
    w_lo = (w - w_hi).astype(_BF16).astype(_F32)
    w_hl = jnp.where(row8 < EXPERTS_PER_GROUP, w_hi, pltpu.roll(w_lo, EXPERTS_PER_GROUP, 0))
    w_rows = jnp.concatenate([w_hl, jnp.zeros((LANES - SUBLANES, tile), _F32)], axis=0).astype(_BF16)

    def wait_slot(sl):
        for g in range(N_GROUPS):
            for piece in range(2):
                m = ring_ref[sl, 2 * g + piece]

                @pl.when(m > 0)
                def _():
                    _rows(stage.at[sl], 0, xs_ref, 0, m, sem.at[sl]).wait()

    @pl.when(k >= 2)
    def _():
        wait_slot(slot)

    stage[slot, :, 0:d] = _dot(perm, x1b)
    stage[slot, :, d:] = lax.dot_general(perm, w_rows, _NT, preferred_element_type=_F32)

    for g in range(N_GROUPS):
        n8, sg = sizes[g], starts[g]
        fill, cur, newc = fill_ref[g], cur_ref[g], nfree_ref[0]
        m1 = jnp.minimum(n8, chunk - fill)
        m2 = n8 - m1
        r1 = cur * chunk + fill
        r2 = newc * chunk
        ring_ref[slot, 2 * g] = m1
        ring_ref[slot, 2 * g + 1] = m2
        for f, v in enumerate((r1, m1, r2, m2, sg)):
            meta_ref[k, g * META_FIELDS + f] = v

        @pl.when(m1 > 0)
        def _():
            _rows(stage.at[slot], sg, xs_ref, r1, m1, sem.at[slot]).start()

        @pl.when(m2 > 0)
        def _():
            _rows(stage.at[slot], sg + m1, xs_ref, r2, m2, sem.at[slot]).start()
            info_ref[newc] = g
            cur_ref[g] = newc
            nfree_ref[0] = newc + 1

        fill_ref[g] = jnp.where(m2 > 0, m2, fill + m1)

    @pl.when(k == n_tiles - 1)
    def _():
        zbuf[...] = jnp.zeros_like(zbuf)
        nfree = nfree_ref[0]
        info_ref[n_chunks] = nfree
        tails = []
        for g in range(N_GROUPS):
            rem = chunk - fill_ref[g]

            @pl.when(rem > 0)
            def _():
                info_ref[n_chunks + 1 + cur_ref[g]] = fill_ref[g]

            tails.append((rem, _rows(zbuf, 0, xs_ref, cur_ref[g] * chunk + fill_ref[g], rem, zsem)))
        spare = [(c, pltpu.make_async_copy(zbuf, xs_ref.at[pl.ds(c * chunk, chunk)], zsem))
                 for c in range(n_tiles * tile // chunk, n_chunks)]
        for rem, cp in tails:
            pl.when(rem > 0)(cp.start)
        for c, cp in spare:
            pl.when(c >= nfree)(cp.start)

        slot_out = jnp.int32(0)
        for g in range(N_GROUPS):
            def place(c, nxt, g=g):
                hit = info_ref[c] == g

                @pl.when(hit)
                def _():
                    order_ref[nxt] = c
                    rank_ref[c] = nxt

                return nxt + hit.astype(jnp.int32)

            slot_out = lax.fori_loop(0, n_chunks, place, slot_out)

        @pl.when(k >= 1)
        def _():
            wait_slot(1 - slot)

        wait_slot(slot)
        for rem, cp in tails:
            pl.when(rem > 0)(cp.wait)
        for c, cp in spare:
            pl.when(c >= nfree)(cp.wait)


def _moe_kernel(order_ref, info_ref, *refs, d, d_expert, n_chunks, per_step):
    xs_refs = refs[:per_step]
    wg_hbm, wu_hbm, wd_hbm, o_ref, wg_buf, wu_buf, wd_buf, slot_ref, wsem = refs[per_step:]
    chunk = xs_refs[0].shape[0]

    def weight_copies(g, sl):
        lo = g * EXPERTS_PER_GROUP
        return [pltpu.make_async_copy(src.at[pl.ds(lo, EXPERTS_PER_GROUP)], dst.at[sl], wsem.at[0])
                for src, dst in ((wg_hbm, wg_buf), (wu_hbm, wu_buf), (wd_hbm, wd_buf))]

    def first_copies(g, e):
        return [pltpu.make_async_copy(src.at[g * EXPERTS_PER_GROUP + e], dst.at[0, e], wsem.at[1 + e])
                for src, dst in ((wg_hbm, wg_buf), (wu_hbm, wu_buf), (wd_hbm, wd_buf))]

    for ci, xs_ref in enumerate(xs_refs):
        i = pl.program_id(0) * per_step + ci
        rows = slice(ci * chunk, (ci + 1) * chunk)
        c = order_ref[i]
        grp = info_ref[c]
        prev = info_ref[order_ref[jnp.maximum(i - 1, 0)]]

        @pl.when(i == 0)
        def _():
            slot_ref[0] = 0
            for e in range(EXPERTS_PER_GROUP):
                for cp in first_copies(grp, e):
                    cp.start()

        @pl.when((i == 0) | (grp != prev))
        def _():
            @pl.when(i > 0)
            def _():
                slot_ref[0] = 1 - slot_ref[0]
                for cp in weight_copies(grp, slot_ref[0]):
                    cp.wait()

            cur = slot_ref[0]

            def later_group(j, found):
                gj = info_ref[order_ref[j]]
                return jnp.where((found < 0) & (j > i) & (gj != grp), gj, found)

            nxt = lax.fori_loop(0, n_chunks, later_group, jnp.int32(-1))

            @pl.when(nxt >= 0)
            def _():
                for cp in weight_copies(nxt, 1 - cur):
                    cp.start()

        used = c < info_ref[n_chunks]
        in_use = info_ref[n_chunks + 1 + c]
        piece = chunk // MOE_ROW_STEPS

        def experts(m, xs_ref=xs_ref, grp=grp, await_first=False):
            xb = xs_ref[0:m, 0:d]
            ws = slot_ref[0]
            aux = xs_ref[0:m, d:]
            y = jnp.zeros((m, o_ref.shape[1]), _F32)
            for e in range(EXPERTS_PER_GROUP):
                if await_first:
                    for cp in first_copies(grp, e):
                        cp.wait()
                hg = _dot(xb, wg_buf[ws, e])
                hu = _dot(xb, wu_buf[ws, e])
                we = aux[:, e:e + 1] + aux[:, EXPERTS_PER_GROUP + e:EXPERTS_PER_GROUP + e + 1]
                hid = hg * _sigmoid(hg) * hu * we
                y = y + _dot(hid, wd_buf[ws, e])
            return y

        later = used
        if ci == 0:
            later = used & (i > 0)

            @pl.when(i == 0)
            def _():
                o_ref[rows, :] = experts(chunk, await_first=True)

        for q in range(1, MOE_ROW_STEPS + 1):
            m = q * piece
            fits = in_use <= m
            if q > 1:
                fits = (in_use > m - piece) & fits if q < MOE_ROW_STEPS else in_use > m - piece

            @pl.when(later & fits)
            def _(m=m):
                o_ref[ci * chunk:ci * chunk + m, :] = experts(m)
                if m < chunk:
                    o_ref[ci * chunk + m:(ci + 1) * chunk, :] = jnp.zeros((chunk - m, o_ref.shape[1]), _F32)

        @pl.when(jnp.logical_not(used))
        def _():
            o_ref[rows, :] = jnp.zeros((chunk, o_ref.shape[1]), _F32)


def _ple_kernel(meta_ref, rank_ref, x1_ref, p_ref, pos_ref, ys_ref, wpg_ref, bpg_ref, wpp_ref, l2g_ref, l2b_ref,
                o_ref, ybuf, sem, *, alpha, tile, chunk, n_sub, tiles_per_step, t_pad, n_steps):
    k = pl.program_id(0)
    slot = k % 2

    def moved(r):
        return rank_ref[r // chunk] * chunk + r % chunk

    def pieces(step, sl):
        out = []
        for ti in range(tiles_per_step):
            kk = step * tiles_per_step + ti
            for g in range(N_GROUPS):
                r1, m1, r2, m2, sg = (meta_ref[kk, g * META_FIELDS + f] for f in range(META_FIELDS))
                out.append((m1, _rows(ys_ref, moved(r1), ybuf.at[sl, ti], sg, m1, sem.at[sl])))
                out.append((m2, _rows(ys_ref, moved(r2), ybuf.at[sl, ti], sg + m1, m2, sem.at[sl])))
        return out

    def fetch(kk, sl):
        for m, cp in pieces(kk, sl):
            pl.when(m > 0)(cp.start)

    @pl.when(k == 0)
    def _():
        ybuf[:, :, tile:, :] = jnp.zeros((2, tiles_per_step, t_pad - tile, ybuf.shape[-1]), _F32)
        fetch(0, 0)

    @pl.when(k + 1 < n_steps)
    def _():
        fetch(k + 1, 1 - slot)

    for m, cp in pieces(k, slot):
        pl.when(m > 0)(cp.wait)

    sub = tile // n_sub
    for ti in range(tiles_per_step):
        yb = ybuf[slot, ti]
        pos = pos_ref[ti]
        for blk in range(n_sub):
            cols = slice(blk * sub, (blk + 1) * sub)
            rows = slice(ti * tile + blk * sub, ti * tile + (blk + 1) * sub)
            perm = jnp.where(lax.broadcasted_iota(jnp.int32, (t_pad, sub), 0) == pos[:, cols], 1.0, 0.0)
            y = lax.dot_general(perm, yb, _TN, preferred_element_type=_F32)
            r = alpha * x1_ref[rows, :] + y
            gate = _sigmoid(_dot(r, wpg_ref[...]) + bpg_ref[...])
            pp = _dot(p_ref[rows, :], wpp_ref[...])
            o_ref[rows, :] = _layer_norm(r + gate * pp, l2g_ref[...], l2b_ref[...])


def _full(shape):
    return pl.BlockSpec(shape, lambda *_: (0,) * len(shape))


def _resident(shape):
    return pl.BlockSpec(shape, lambda *_: (0,) * len(shape), pipeline_mode=pl.Buffered(1))


def _row(v):
    return v.reshape(1, -1).astype(_F32)


def kernel(x, p, ln_in_g, ln_in_b, w_in, b_in, conf_dw_w, conf_dw_b, conf_ln_g, conf_ln_b, sc_w, sc_b, w_out, b_out, ln1_g, ln1_b, w_rg, b_rg, w_re, b_re, w_gate, w_up, w_down, w_pg, b_pg, w_pp, ln2_g, ln2_b):
    depth = w_in.shape[0]
    assert depth == 1, "single-layer block"
    alpha = (2.0 * depth) ** 0.25
    bsz, seq, d = x.shape
    n = bsz * seq
    conf_k, conf_w = conf_dw_w.shape[1:]
    sc_k, sc_w_ = sc_w.shape[1:]
    d_in = w_in.shape[2]
    n_exp, _, d_expert = w_gate.shape[1:]
    assert n_exp == N_GROUPS * EXPERTS_PER_GROUP
    assert d_in == 2 * conf_w + 3 * sc_w_ and w_out.shape[1] == conf_w + sc_w_
    d_ple = p.shape[-1]

    tile = chunk = TILE
    assert seq % tile == 0 and (bsz * (seq // tile)) % PLE_TILES == 0
    nt = seq // tile
    n_tiles = bsz * nt
    a_halo = -(-(conf_k - 1) // SUBLANES) * SUBLANES
    u_halo = -(-(sc_k - 1) // SUBLANES) * SUBLANES
    t_pad = tile + N_GROUPS * SUBLANES
    n_chunks = -(-(n + (SUBLANES - 1) * N_GROUPS * n_tiles) // chunk) + N_GROUPS
    n_chunks = -(-n_chunks // MOE_CHUNKS) * MOE_CHUNKS
    d_pay = d + LANES

    n_logits = N_GROUPS + n_exp
    wr = jnp.concatenate([w_rg[0], w_re[0]], axis=1).T

    mixer = pl.pallas_call(
        functools.partial(_mixer_kernel, alpha=alpha, tile=tile, t_pad=t_pad, chunk=chunk,
                          n_chunks=n_chunks, n_tiles=n_tiles, conf_w=conf_w, sc_w_=sc_w_,
                          conf_k=conf_k, sc_k=sc_k, a_halo=a_halo, u_halo=u_halo, n_logits=n_logits),
        grid=(bsz, nt),
        in_specs=[
            pl.BlockSpec((1, tile, d), lambda b, j: (b, j, 0)),
            _full((1, d)), _full((1, d)),
            pl.BlockSpec(memory_space=pl.ANY), _full((1, d_in)),
            _full((conf_k, conf_w)), _full((1, conf_w)), _full((1, conf_w)), _full((1, conf_w)),
            _full((sc_k, sc_w_)), _full((1, sc_w_)),
            pl.BlockSpec(memory_space=pl.ANY), _full((1, d)),
            _full((1, d)), _full((1, d)),
            _full((n_logits, d)), _full((1, N_GROUPS)), _full((1, n_exp)),
        ],
        out_specs=[
            pl.BlockSpec((1, tile, d), lambda b, j: (b, j, 0)),
            pl.BlockSpec((1, 1, tile), lambda b, j: (b * nt + j, 0, 0)),
            pl.BlockSpec(memory_space=pl.ANY),
            pl.BlockSpec(memory_space=pltpu.SMEM),
            pl.BlockSpec(memory_space=pltpu.SMEM),
            pl.BlockSpec(memory_space=pltpu.SMEM),
            pl.BlockSpec(memory_space=pltpu.SMEM),
        ],
        out_shape=[
            jax.ShapeDtypeStruct((bsz, seq, d), _F32),
            jax.ShapeDtypeStruct((n_tiles, 1, tile), jnp.int32),
            jax.ShapeDtypeStruct((n_chunks * chunk, d_pay), _F32),
            jax.ShapeDtypeStruct((n_tiles, N_GROUPS * META_FIELDS), jnp.int32),
            jax.ShapeDtypeStruct((2 * n_chunks + 1,), jnp.int32),
            jax.ShapeDtypeStruct((n_chunks,), jnp.int32),
            jax.ShapeDtypeStruct((n_chunks,), jnp.int32),
        ],
        scratch_shapes=[
            pltpu.VMEM((a_halo + tile, conf_w), _F32),
            pltpu.VMEM((SUBLANES - 1, a_halo + tile - SUBLANES, conf_w), _F32),
            pltpu.VMEM((u_halo + tile, sc_w_), _F32),
            pltpu.VMEM((d, d_in), _F32),
            pltpu.VMEM((conf_w + sc_w_, d), _F32),
            pltpu.VMEM((d, d_in), _BF16),
            pltpu.VMEM((conf_w + sc_w_, d), _BF16),
            pltpu.VMEM((ROUTER_ROWS, d), _BF16),
            pltpu.VMEM((ROUTER_ROWS, 1), _F32),
            pltpu.VMEM((tile, tile), _BF16),
            pltpu.VMEM((2, t_pad, d_pay), _F32),
            pltpu.VMEM((chunk, d_pay), _F32),
            pltpu.SMEM((N_GROUPS,), jnp.int32),
            pltpu.SMEM((N_GROUPS,), jnp.int32),
            pltpu.SMEM((1,), jnp.int32),
            pltpu.SMEM((2, 2 * N_GROUPS), jnp.int32),
            pltpu.SemaphoreType.DMA((2,)),
            pltpu.SemaphoreType.DMA(()),
            pltpu.SemaphoreType.DMA((3,)),
        ],
        compiler_params=pltpu.CompilerParams(
            dimension_semantics=("arbitrary", "arbitrary"), vmem_limit_bytes=VMEM_LIMIT_BYTES),
        name="mixer",
    )
    x1, pos, xs, meta, info, order, rank = mixer(
        x, _row(ln_in_g), _row(ln_in_b), w_in[0], _row(b_in[0]),
        conf_dw_w[0], _row(conf_dw_b[0]), _row(conf_ln_g[0]), _row(conf_ln_b[0]),
        sc_w[0], _row(sc_b[0]), w_out[0], _row(b_out[0]),
        _row(ln1_g[0]), _row(ln1_b[0]), wr, _row(b_rg[0]), _row(b_re[0]))

    moe = pl.pallas_call(
        functools.partial(_moe_kernel, d=d, d_expert=d_expert, n_chunks=n_chunks, per_step=MOE_CHUNKS),
        grid_spec=pltpu.PrefetchScalarGridSpec(
            num_scalar_prefetch=2,
            grid=(n_chunks // MOE_CHUNKS,),
            in_specs=[
                pl.BlockSpec((chunk, d_pay), functools.partial(
                    lambda i, order, info, ci: (order[i * MOE_CHUNKS + ci], 0), ci=ci))
                for ci in range(MOE_CHUNKS)
            ] + [
                pl.BlockSpec(memory_space=pl.ANY),
                pl.BlockSpec(memory_space=pl.ANY),
                pl.BlockSpec(memory_space=pl.ANY),
            ],
            out_specs=pl.BlockSpec((MOE_CHUNKS * chunk, d), lambda i, order, info: (i, 0)),
            scratch_shapes=[
                pltpu.VMEM((2, EXPERTS_PER_GROUP, d, d_expert), _F32),
                pltpu.VMEM((2, EXPERTS_PER_GROUP, d, d_expert), _F32),
                pltpu.VMEM((2, EXPERTS_PER_GROUP, d_expert, d), _F32),
                pltpu.SMEM((1,), jnp.int32),
                pltpu.SemaphoreType.DMA((1 + EXPERTS_PER_GROUP,)),
            ],
        ),
        out_shape=jax.ShapeDtypeStruct((n_chunks * chunk, d), _F32),
        compiler_params=pltpu.CompilerParams(
            dimension_semantics=("arbitrary",), vmem_limit_bytes=VMEM_LIMIT_BYTES),
        name="moe",
    )
    ys = moe(order, info, *([xs] * MOE_CHUNKS), w_gate[0], w_up[0], w_down[0])

    ple = pl.pallas_call(
        functools.partial(_ple_kernel, alpha=alpha, tile=tile, chunk=chunk, n_sub=N_SUB, tiles_per_step=PLE_TILES,
                          t_pad=t_pad, n_steps=n_tiles // PLE_TILES),
        grid_spec=pltpu.PrefetchScalarGridSpec(
            num_scalar_prefetch=2,
            grid=(n_tiles // PLE_TILES,),
            in_specs=[
                pl.BlockSpec((PLE_TILES * tile, d), lambda k, meta, rank: (k, 0)),
                pl.BlockSpec((PLE_TILES * tile, d_ple), lambda k, meta, rank: (k, 0)),
                pl.BlockSpec((PLE_TILES, 1, tile), lambda k, meta, rank: (k, 0, 0)),
                pl.BlockSpec(memory_space=pl.ANY),
                _resident((d, d)), _full((1, d)), _resident((d_ple, d)), _full((1, d)), _full((1, d)),
            ],
            out_specs=pl.BlockSpec((PLE_TILES * tile, d), lambda k, meta, rank: (k, 0)),
            scratch_shapes=[
                pltpu.VMEM((2, PLE_TILES, t_pad, d), _F32),
                pltpu.SemaphoreType.DMA((2,)),
            ],
        ),
        out_shape=jax.ShapeDtypeStruct((n, d), _F32),
        compiler_params=pltpu.CompilerParams(
            dimension_semantics=("arbitrary",), vmem_limit_bytes=VMEM_LIMIT_BYTES),
        name="ple",
    )
    out = ple(meta, rank, x1.reshape(n, d), p[0].reshape(n, d_ple), pos, ys,
              w_pg[0], _row(b_pg[0]), w_pp[0], _row(ln2_g[0]), _row(ln2_b[0]))
    return out.reshape(bsz, seq, d)
```

```python
import functools

import jax
import jax.numpy as jnp
from jax import lax
from jax.experimental import pallas as pl
from jax.experimental.pallas import tpu as pltpu

LN_EPS = 1e-5
N_GROUPS = 4
EXPERTS_PER_GROUP = 4
SUBLANES = 8
LANES = 128
ROUTER_ROWS = 32
TILE = 512
MOE_CHUNKS = 2
PLE_TILES = 2
N_SUB = 2
VMEM_LIMIT_BYTES = 56 * 1024 * 1024
NEG_BIG = -1e30
META_FIELDS = 5

_F32 = jnp.float32
_BF16 = jnp.bfloat16
_NT = (((1,), (1,)), ((), ()))
_TN = (((0,), (0,)), ((), ()))


def _layer_norm(x, g, b):
    mu = jnp.mean(x, axis=-1, keepdims=True)
    xc = x - mu
    var = jnp.mean(xc * xc, axis=-1, keepdims=True)
    return xc * lax.rsqrt(var + LN_EPS) * g + b


def _sigmoid(x):
    return 0.5 * jnp.tanh(0.5 * x) + 0.5


def _dot(a, b):
    return jnp.dot(a, b, preferred_element_type=_F32)


def _aligned(v):
    return v if isinstance(v, int) else pl.multiple_of(v, SUBLANES)


def _rows(src, s0, dst, d0, m, sem):
    return pltpu.make_async_copy(src.at[pl.ds(_aligned(s0), _aligned(m))],
                                 dst.at[pl.ds(_aligned(d0), _aligned(m))], sem)


def _route(logits):
    row = lax.broadcasted_iota(jnp.int32, (SUBLANES, logits.shape[1]), 0)
    real = row < N_GROUPS
    gl = jnp.where(real, logits[0:SUBLANES], NEG_BIG)
    gm = jnp.max(gl, axis=0, keepdims=True)
    ge = jnp.exp(gl - gm)
    gp = ge / jnp.sum(ge, axis=0, keepdims=True)
    gp_top = jnp.max(gp, axis=0, keepdims=True)
    gidx = jnp.min(jnp.where(gp == gp_top, row, SUBLANES), axis=0, keepdims=True)

    el = jnp.zeros_like(gl)
    for g in range(N_GROUPS):
        lo = N_GROUPS + g * EXPERTS_PER_GROUP
        blk = logits[lo - lo % SUBLANES:lo - lo % SUBLANES + SUBLANES]
        if lo % SUBLANES:
            blk = pltpu.roll(blk, SUBLANES - lo % SUBLANES, 0)
        el = jnp.where(gidx == g, blk, el)
    el = jnp.where(real, el, NEG_BIG)
    em = jnp.max(el, axis=0, keepdims=True)
    ee = jnp.exp(el - em)
    ep = ee / jnp.sum(ee, axis=0, keepdims=True)
    ep = jnp.where(real, ep, -1.0)
    p1 = jnp.max(ep, axis=0, keepdims=True)
    i1 = jnp.min(jnp.where(ep == p1, row, SUBLANES), axis=0, keepdims=True)
    ep2 = jnp.where(row == i1, -1.0, ep)
    p2 = jnp.max(ep2, axis=0, keepdims=True)
    i2 = jnp.min(jnp.where(ep2 == p2, row, SUBLANES), axis=0, keepdims=True)
    denom = p1 + p2
    w = jnp.where(row == i1, p1 / denom, jnp.where(row == i2, p2 / denom, 0.0))
    return gidx, w * gp_top


def _mixer_kernel(x_ref, lng_ref, lnb_ref, w_in_ref, b_in_ref, cw_ref, cb_ref, clg_ref, clb_ref,
                  sw_ref, sb_ref, w_out_ref, b_out_ref, l1g_ref, l1b_ref, wr_ref, brg_ref, bre_ref,
                  x1_ref, pos_ref, xs_ref, meta_ref, info_ref, order_ref, rank_ref,
                  abuf, ashift, ubuf, w_in_st, w_out_st, w_in_bf, w_out_bf, wr_bf, br_ref, tri_ref, stage, zbuf,
                  cur_ref, fill_ref, nfree_ref, ring_ref, sem, zsem, wsem,
                  *, alpha, tile, t_pad, chunk, n_chunks, n_tiles, conf_w, sc_w_,
                  conf_k, sc_k, a_halo, u_halo, n_logits):
    j = pl.program_id(1)
    k = pl.program_id(0) * pl.num_programs(1) + j
    slot = k % 2
    d = x1_ref.shape[-1]
    c0, c1, c2, c3 = conf_w, 2 * conf_w, 2 * conf_w + sc_w_, 2 * conf_w + 2 * sc_w_
    w_parts = (
        (w_in_ref.at[:, 0:c1], w_in_st.at[:, 0:c1], w_in_bf.at[:, 0:c1]),
        (w_in_ref.at[:, c1:], w_in_st.at[:, c1:], w_in_bf.at[:, c1:]),
        (w_out_ref, w_out_st, w_out_bf),
    )
    w_copies = [pltpu.make_async_copy(src, st, wsem.at[n]) for n, (src, st, _) in enumerate(w_parts)]

    def weights_ready(n):
        w_copies[n].wait()
        w_parts[n][2][...] = w_parts[n][1][...].astype(_BF16)

    spare = [pltpu.make_async_copy(zbuf, xs_ref.at[pl.ds(c * chunk, chunk)], zsem)
             for c in range(n_tiles * tile // chunk, n_chunks)]

    @pl.when(k == 0)
    def _():
        for cp in w_copies:
            cp.start()
        zbuf[...] = jnp.zeros_like(zbuf)
        for cp in spare:
            cp.start()
        weights_ready(0)
        wr_bf[...] = jnp.zeros_like(wr_bf)
        wr_bf[0:n_logits, :] = wr_ref[...].astype(_BF16)
        def column(b_ref, first_row):
            shape = (ROUTER_ROWS, b_ref.shape[1])
            on_diag = (lax.broadcasted_iota(jnp.int32, shape, 0)
                       == lax.broadcasted_iota(jnp.int32, shape, 1) + first_row)
            return jnp.sum(jnp.where(on_diag, b_ref[...], 0.0), axis=1, keepdims=True)

        br_ref[...] = column(brg_ref, 0) + column(bre_ref, N_GROUPS)
        earlier = lax.broadcasted_iota(jnp.int32, (tile, tile), 0) < lax.broadcasted_iota(jnp.int32, (tile, tile), 1)
        tri_ref[...] = jnp.where(earlier, 1.0, 0.0).astype(_BF16)
        nfree_ref[0] = 0
        for g in range(N_GROUPS):
            cur_ref[g] = 0
            fill_ref[g] = chunk
        for c in range(n_chunks + 1):
            info_ref[c] = N_GROUPS - 1
        for c in range(n_chunks):
            info_ref[n_chunks + 1 + c] = chunk

    @pl.when(k == 1)
    def _():
        for cp in spare:
            cp.wait()

    @pl.when(j == 0)
    def _():
        abuf[0:a_halo, :] = jnp.zeros((a_halo, conf_w), _F32)
        ubuf[0:u_halo, :] = jnp.zeros((u_halo, sc_w_), _F32)

    x0 = _layer_norm(x_ref[0], lng_ref[...], lnb_ref[...])
    x0b = x0.astype(_BF16)
    h = _dot(x0b, w_in_bf[:, 0:c1]) + b_in_ref[:, 0:c1]

    a = h[:, 0:c0] * _sigmoid(h[:, c0:c1])
    abuf[a_halo:a_halo + tile, :] = a
    for i in range(1, SUBLANES):
        ashift[i - 1] = abuf[i:i + tile + a_halo - SUBLANES, :]
    acc = jnp.broadcast_to(cb_ref[...], (tile, conf_w))
    for t in range(conf_k):
        q, i = divmod(a_halo - (conf_k - 1) + t, SUBLANES)
        if i == 0:
            src = abuf[SUBLANES * q:SUBLANES * q + tile, :]
        else:
            src = ashift[i - 1, SUBLANES * q:SUBLANES * q + tile, :]
        acc = acc + cw_ref[t:t + 1, :] * src
    abuf[0:a_halo, :] = abuf[tile:tile + a_halo, :]

    @pl.when(k == 0)
    def _():
        weights_ready(1)
        weights_ready(2)

    h2 = _dot(x0b, w_in_bf[:, c1:]) + b_in_ref[:, c1:]
    an = _layer_norm(acc, clg_ref[...], clb_ref[...])
    a2 = an * _sigmoid(an)

    u = h2[:, c2 - c1:c3 - c1] * h2[:, c3 - c1:]
    ubuf[u_halo:u_halo + tile, :] = u
    sc = jnp.broadcast_to(sb_ref[...], (tile, sc_w_))
    for t in range(sc_k):
        off = u_halo - (sc_k - 1) + t
        sc = sc + sw_ref[t:t + 1, :] * ubuf[off:off + tile, :]
    ubuf[0:u_halo, :] = ubuf[tile:tile + u_halo, :]
    s = h2[:, 0:c2 - c1] * sc

    mix = (_dot(a2.astype(_BF16), w_out_bf[0:conf_w, :])
           + _dot(s.astype(_BF16), w_out_bf[conf_w:, :]) + b_out_ref[...])
    x1 = _layer_norm(alpha * x0 + mix, l1g_ref[...], l1b_ref[...])
    x1_ref[0] = x1
    x1b = x1.astype(_BF16)

    logits = lax.dot_general(wr_bf[...], x1b, _NT, preferred_element_type=_F32) + br_ref[...]
    gidx, w = _route(logits)

    row8 = lax.broadcasted_iota(jnp.int32, (SUBLANES, tile), 0)
    onehot = jnp.where(gidx == row8, 1.0, 0.0)
    cnt = jnp.sum(onehot, axis=1, keepdims=True)
    sizes, starts = [], []
    start = jnp.int32(0)
    startv = jnp.zeros((SUBLANES, tile), jnp.int32)
    for g in range(N_GROUPS):
        n8 = (cnt[g, 0].astype(jnp.int32) + (SUBLANES - 1)) & (-SUBLANES)
        sizes.append(n8)
        starts.append(start)
        startv = jnp.where(row8 == g, start, startv)
        start = start + n8
    cum = _dot(onehot.astype(_BF16), tri_ref[...])
    pos = jnp.sum(onehot * (startv.astype(_F32) + cum), axis=0, keepdims=True).astype(jnp.int32)
    pos_ref[0] = pos
    perm = jnp.where(lax.broadcasted_iota(jnp.int32, (t_pad, tile), 0) == pos, 1.0, 0.0).astype(_BF16)

    w_hi = w.astype(_BF16).astype(_F32)
    w_lo = (w - w_hi).astype(_BF16).astype(_F32)
    w_hl = jnp.where(row8 < EXPERTS_PER_GROUP, w_hi, pltpu.roll(w_lo, EXPERTS_PER_GROUP, 0))
    w_rows = jnp.concatenate([w_hl, jnp.zeros((LANES - SUBLANES, tile), _F32)], axis=0).astype(_BF16)

    def wait_slot(sl):
        for g in range(N_GROUPS):
            for piece in range(2):
                m = ring_ref[sl, 2 * g + piece]

                @pl.when(m > 0)
                def _():
                    _rows(stage.at[sl], 0, xs_ref, 0, m, sem.at[sl]).wait()

    @pl.when(k >= 2)
    def _():
        wait_slot(slot)

    stage[slot, :, 0:d] = _dot(perm, x1b)
    stage[slot, :, d:] = lax.dot_general(perm, w_rows, _NT, preferred_element_type=_F32)

    for g in range(N_GROUPS):
        n8, sg = sizes[g], starts[g]
        fill, cur, newc = fill_ref[g], cur_ref[g], nfree_ref[0]
        m1 = jnp.minimum(n8, chunk - fill)
        m2 = n8 - m1
        r1 = cur * chunk + fill
        r2 = newc * chunk
        ring_ref[slot, 2 * g] = m1
        ring_ref[slot, 2 * g + 1] = m2
        for f, v in enumerate((r1, m1, r2, m2, sg)):
            meta_ref[k, g * META_FIELDS + f] = v

        @pl.when(m1 > 0)
        def _():
            _rows(stage.at[slot], sg, xs_ref, r1, m1, sem.at[slot]).start()

        @pl.when(m2 > 0)
        def _():
            _rows(stage.at[slot], sg + m1, xs_ref, r2, m2, sem.at[slot]).start()
            info_ref[newc] = g
            cur_ref[g] = newc
            nfree_ref[0] = newc + 1

        fill_ref[g] = jnp.where(m2 > 0, m2, fill + m1)

    @pl.when(k == n_tiles - 1)
    def _():
        nfree = nfree_ref[0]
        info_ref[n_chunks] = nfree
        tails = []
        for g in range(N_GROUPS):
            rem = chunk - fill_ref[g]

            @pl.when(rem > 0)
            def _():
                info_ref[n_chunks + 1 + cur_ref[g]] = fill_ref[g]

            tails.append((rem, _rows(zbuf, 0, xs_ref, cur_ref[g] * chunk + fill_ref[g], rem, zsem)))
        for rem, cp in tails:
            pl.when(rem > 0)(cp.start)

        slot_out = jnp.int32(0)
        for g in range(N_GROUPS):
            def place(c, nxt, g=g):
                hit = info_ref[c] == g

                @pl.when(hit)
                def _():
                    order_ref[nxt] = c
                    rank_ref[c] = nxt

                return nxt + hit.astype(jnp.int32)

            slot_out = lax.fori_loop(0, n_chunks, place, slot_out)

        @pl.when(k >= 1)
        def _():
            wait_slot(1 - slot)

        wait_slot(slot)
        for rem, cp in tails:
            pl.when(rem > 0)(cp.wait)


def _moe_kernel(order_ref, info_ref, *refs, d, d_expert, n_chunks, per_step):
    xs_refs = refs[:per_step]
    wg_hbm, wu_hbm, wd_hbm, o_ref, wg_buf, wu_buf, wd_buf, slot_ref, wsem = refs[per_step:]
    chunk = xs_refs[0].shape[0]

    def weight_copies(g, sl):
        lo = g * EXPERTS_PER_GROUP
        return [pltpu.make_async_copy(src.at[pl.ds(lo, EXPERTS_PER_GROUP)], dst.at[sl], wsem.at[0])
                for src, dst in ((wg_hbm, wg_buf), (wu_hbm, wu_buf), (wd_hbm, wd_buf))]

    def first_copies(g, e):
        return [pltpu.make_async_copy(src.at[g * EXPERTS_PER_GROUP + e], dst.at[0, e], wsem.at[1 + e])
                for src, dst in ((wg_hbm, wg_buf), (wu_hbm, wu_buf), (wd_hbm, wd_buf))]

    for ci, xs_ref in enumerate(xs_refs):
        i = pl.program_id(0) * per_step + ci
        rows = slice(ci * chunk, (ci + 1) * chunk)
        c = order_ref[i]
        grp = info_ref[c]
        prev = info_ref[order_ref[jnp.maximum(i - 1, 0)]]

        @pl.when(i == 0)
        def _():
            slot_ref[0] = 0
            for e in range(EXPERTS_PER_GROUP):
                for cp in first_copies(grp, e):
                    cp.start()

        @pl.when((i == 0) | (grp != prev))
        def _():
            @pl.when(i > 0)
            def _():
                slot_ref[0] = 1 - slot_ref[0]
                for cp in weight_copies(grp, slot_ref[0]):
                    cp.wait()

            cur = slot_ref[0]

            def later_group(j, found):
                gj = info_ref[order_ref[j]]
                return jnp.where((found < 0) & (j > i) & (gj != grp), gj, found)

            nxt = lax.fori_loop(0, n_chunks, later_group, jnp.int32(-1))

            @pl.when(nxt >= 0)
            def _():
                for cp in weight_copies(nxt, 1 - cur):
                    cp.start()

        used = c < info_ref[n_chunks]
        half = chunk // 2
        low_only = info_ref[n_chunks + 1 + c] <= half

        def experts(m, xs_ref=xs_ref, grp=grp, await_first=False):
            xb = xs_ref[0:m, 0:d]
            ws = slot_ref[0]
            aux = xs_ref[0:m, d:]
            y = jnp.zeros((m, o_ref.shape[1]), _F32)
            for e in range(EXPERTS_PER_GROUP):
                if await_first:
                    for cp in first_copies(grp, e):
                        cp.wait()
                hg = _dot(xb, wg_buf[ws, e])
                hu = _dot(xb, wu_buf[ws, e])
                we = aux[:, e:e + 1] + aux[:, EXPERTS_PER_GROUP + e:EXPERTS_PER_GROUP + e + 1]
                hid = hg * _sigmoid(hg) * hu * we
                y = y + _dot(hid, wd_buf[ws, e])
            return y

        later = used
        if ci == 0:
            later = used & (i > 0)

            @pl.when(i == 0)
            def _():
                o_ref[rows, :] = experts(chunk, await_first=True)

        @pl.when(later & jnp.logical_not(low_only))
        def _():
            o_ref[rows, :] = experts(chunk)

        @pl.when(later & low_only)
        def _():
            o_ref[ci * chunk:ci * chunk + half, :] = experts(half)
            o_ref[ci * chunk + half:(ci + 1) * chunk, :] = jnp.zeros((half, o_ref.shape[1]), _F32)

        @pl.when(jnp.logical_not(used))
        def _():
            o_ref[rows, :] = jnp.zeros((chunk, o_ref.shape[1]), _F32)


def _ple_kernel(meta_ref, rank_ref, x1_ref, p_ref, pos_ref, ys_ref, wpg_ref, bpg_ref, wpp_ref, l2g_ref, l2b_ref,
                o_ref, ybuf, sem, *, alpha, tile, chunk, n_sub, tiles_per_step, t_pad, n_steps):
    k = pl.program_id(0)
    slot = k % 2

    def moved(r):
        return rank_ref[r // chunk] * chunk + r % chunk

    def pieces(step, sl):
        out = []
        for ti in range(tiles_per_step):
            kk = step * tiles_per_step + ti
            for g in range(N_GROUPS):
                r1, m1, r2, m2, sg = (meta_ref[kk, g * META_FIELDS + f] for f in range(META_FIELDS))
                out.append((m1, _rows(ys_ref, moved(r1), ybuf.at[sl, ti], sg, m1, sem.at[sl])))
                out.append((m2, _rows(ys_ref, moved(r2), ybuf.at[sl, ti], sg + m1, m2, sem.at[sl])))
        return out

    def fetch(kk, sl):
        for m, cp in pieces(kk, sl):
            pl.when(m > 0)(cp.start)

    @pl.when(k == 0)
    def _():
        ybuf[:, :, tile:, :] = jnp.zeros((2, tiles_per_step, t_pad - tile, ybuf.shape[-1]), _F32)
        fetch(0, 0)

    @pl.when(k + 1 < n_steps)
    def _():
        fetch(k + 1, 1 - slot)

    for m, cp in pieces(k, slot):
        pl.when(m > 0)(cp.wait)

    sub = tile // n_sub
    for ti in range(tiles_per_step):
        yb = ybuf[slot, ti]
        pos = pos_ref[ti]
        for blk in range(n_sub):
            cols = slice(blk * sub, (blk + 1) * sub)
            rows = slice(ti * tile + blk * sub, ti * tile + (blk + 1) * sub)
            perm = jnp.where(lax.broadcasted_iota(jnp.int32, (t_pad, sub), 0) == pos[:, cols], 1.0, 0.0)
            y = lax.dot_general(perm, yb, _TN, preferred_element_type=_F32)
            r = alpha * x1_ref[rows, :] + y
            gate = _sigmoid(_dot(r, wpg_ref[...]) + bpg_ref[...])
            pp = _dot(p_ref[rows, :], wpp_ref[...])
            o_ref[rows, :] = _layer_norm(r + gate * pp, l2g_ref[...], l2b_ref[...])


def _full(shape):
    return pl.BlockSpec(shape, lambda *_: (0,) * len(shape))


def _resident(shape):
    return pl.BlockSpec(shape, lambda *_: (0,) * len(shape), pipeline_mode=pl.Buffered(1))


def _row(v):
    return v.reshape(1, -1).astype(_F32)


def kernel(x, p, ln_in_g, ln_in_b, w_in, b_in, conf_dw_w, conf_dw_b, conf_ln_g, conf_ln_b, sc_w, sc_b, w_out, b_out, ln1_g, ln1_b, w_rg, b_rg, w_re, b_re, w_gate, w_up, w_down, w_pg, b_pg, w_pp, ln2_g, ln2_b):
    depth = w_in.shape[0]
    assert depth == 1, "single-layer block"
    alpha = (2.0 * depth) ** 0.25
    bsz, seq, d = x.shape
    n = bsz * seq
    conf_k, conf_w = conf_dw_w.shape[1:]
    sc_k, sc_w_ = sc_w.shape[1:]
    d_in = w_in.shape[2]
    n_exp, _, d_expert = w_gate.shape[1:]
    assert n_exp == N_GROUPS * EXPERTS_PER_GROUP
    assert d_in == 2 * conf_w + 3 * sc_w_ and w_out.shape[1] == conf_w + sc_w_
    d_ple = p.shape[-1]

    tile = chunk = TILE
    assert seq % tile == 0 and (bsz * (seq // tile)) % PLE_TILES == 0
    assert bsz * (seq // tile) >= 2 * N_GROUPS, "spare chunks are zero-filled during the first two steps"
    nt = seq // tile
    n_tiles = bsz * nt
    a_halo = -(-(conf_k - 1) // SUBLANES) * SUBLANES
    u_halo = -(-(sc_k - 1) // SUBLANES) * SUBLANES
    t_pad = tile + N_GROUPS * SUBLANES
    n_chunks = -(-(n + (SUBLANES - 1) * N_GROUPS * n_tiles) // chunk) + N_GROUPS
    n_chunks = -(-n_chunks // MOE_CHUNKS) * MOE_CHUNKS
    d_pay = d + LANES

    n_logits = N_GROUPS + n_exp
    wr = jnp.concatenate([w_rg[0], w_re[0]], axis=1).T

    mixer = pl.pallas_call(
        functools.partial(_mixer_kernel, alpha=alpha, tile=tile, t_pad=t_pad, chunk=chunk,
                          n_chunks=n_chunks, n_tiles=n_tiles, conf_w=conf_w, sc_w_=sc_w_,
                          conf_k=conf_k, sc_k=sc_k, a_halo=a_halo, u_halo=u_halo, n_logits=n_logits),
        grid=(bsz, nt),
        in_specs=[
            pl.BlockSpec((1, tile, d), lambda b, j: (b, j, 0)),
            _full((1, d)), _full((1, d)),
            pl.BlockSpec(memory_space=pl.ANY), _full((1, d_in)),
            _full((conf_k, conf_w)), _full((1, conf_w)), _full((1, conf_w)), _full((1, conf_w)),
            _full((sc_k, sc_w_)), _full((1, sc_w_)),
            pl.BlockSpec(memory_space=pl.ANY), _full((1, d)),
            _full((1, d)), _full((1, d)),
            _full((n_logits, d)), _full((1, N_GROUPS)), _full((1, n_exp)),
        ],
        out_specs=[
            pl.BlockSpec((1, tile, d), lambda b, j: (b, j, 0)),
            pl.BlockSpec((1, 1, tile), lambda b, j: (b * nt + j, 0, 0)),
            pl.BlockSpec(memory_space=pl.ANY),
            pl.BlockSpec(memory_space=pltpu.SMEM),
            pl.BlockSpec(memory_space=pltpu.SMEM),
            pl.BlockSpec(memory_space=pltpu.SMEM),
            pl.BlockSpec(memory_space=pltpu.SMEM),
        ],
        out_shape=[
            jax.ShapeDtypeStruct((bsz, seq, d), _F32),
            jax.ShapeDtypeStruct((n_tiles, 1, tile), jnp.int32),
            jax.ShapeDtypeStruct((n_chunks * chunk, d_pay), _F32),
            jax.ShapeDtypeStruct((n_tiles, N_GROUPS * META_FIELDS), jnp.int32),
            jax.ShapeDtypeStruct((2 * n_chunks + 1,), jnp.int32),
            jax.ShapeDtypeStruct((n_chunks,), jnp.int32),
            jax.ShapeDtypeStruct((n_chunks,), jnp.int32),
        ],
        scratch_shapes=[
            pltpu.VMEM((a_halo + tile, conf_w), _F32),
            pltpu.VMEM((SUBLANES - 1, a_halo + tile - SUBLANES, conf_w), _F32),
            pltpu.VMEM((u_halo + tile, sc_w_), _F32),
            pltpu.VMEM((d, d_in), _F32),
            pltpu.VMEM((conf_w + sc_w_, d), _F32),
            pltpu.VMEM((d, d_in), _BF16),
            pltpu.VMEM((conf_w + sc_w_, d), _BF16),
            pltpu.VMEM((ROUTER_ROWS, d), _BF16),
            pltpu.VMEM((ROUTER_ROWS, 1), _F32),
            pltpu.VMEM((tile, tile), _BF16),
            pltpu.VMEM((2, t_pad, d_pay), _F32),
            pltpu.VMEM((chunk, d_pay), _F32),
            pltpu.SMEM((N_GROUPS,), jnp.int32),
            pltpu.SMEM((N_GROUPS,), jnp.int32),
            pltpu.SMEM((1,), jnp.int32),
            pltpu.SMEM((2, 2 * N_GROUPS), jnp.int32),
            pltpu.SemaphoreType.DMA((2,)),
            pltpu.SemaphoreType.DMA(()),
            pltpu.SemaphoreType.DMA((3,)),
        ],
        compiler_params=pltpu.CompilerParams(
            dimension_semantics=("arbitrary", "arbitrary"), vmem_limit_bytes=VMEM_LIMIT_BYTES),
        name="mixer",
    )
    x1, pos, xs, meta, info, order, rank = mixer(
        x, _row(ln_in_g), _row(ln_in_b), w_in[0], _row(b_in[0]),
        conf_dw_w[0], _row(conf_dw_b[0]), _row(conf_ln_g[0]), _row(conf_ln_b[0]),
        sc_w[0], _row(sc_b[0]), w_out[0], _row(b_out[0]),
        _row(ln1_g[0]), _row(ln1_b[0]), wr, _row(b_rg[0]), _row(b_re[0]))

    moe = pl.pallas_call(
        functools.partial(_moe_kernel, d=d, d_expert=d_expert, n_chunks=n_chunks, per_step=MOE_CHUNKS),
        grid_spec=pltpu.PrefetchScalarGridSpec(
            num_scalar_prefetch=2,
            grid=(n_chunks // MOE_CHUNKS,),
            in_specs=[
                pl.BlockSpec((chunk, d_pay), functools.partial(
                    lambda i, order, info, ci: (order[i * MOE_CHUNKS + ci], 0), ci=ci))
                for ci in range(MOE_CHUNKS)
            ] + [
                pl.BlockSpec(memory_space=pl.ANY),
                pl.BlockSpec(memory_space=pl.ANY),
                pl.BlockSpec(memory_space=pl.ANY),
            ],
            out_specs=pl.BlockSpec((MOE_CHUNKS * chunk, d), lambda i, order, info: (i, 0)),
            scratch_shapes=[
                pltpu.VMEM((2, EXPERTS_PER_GROUP, d, d_expert), _F32),
                pltpu.VMEM((2, EXPERTS_PER_GROUP, d, d_expert), _F32),
                pltpu.VMEM((2, EXPERTS_PER_GROUP, d_expert, d), _F32),
                pltpu.SMEM((1,), jnp.int32),
                pltpu.SemaphoreType.DMA((1 + EXPERTS_PER_GROUP,)),
            ],
        ),
        out_shape=jax.ShapeDtypeStruct((n_chunks * chunk, d), _F32),
        compiler_params=pltpu.CompilerParams(
            dimension_semantics=("arbitrary",), vmem_limit_bytes=VMEM_LIMIT_BYTES),
        name="moe",
    )
    ys = moe(order, info, *([xs] * MOE_CHUNKS), w_gate[0], w_up[0], w_down[0])

    ple = pl.pallas_call(
        functools.partial(_ple_kernel, alpha=alpha, tile=tile, chunk=chunk, n_sub=N_SUB, tiles_per_step=PLE_TILES,
                          t_pad=t_pad, n_steps=n_tiles // PLE_TILES),
        grid_spec=pltpu.PrefetchScalarGridSpec(
            num_scalar_prefetch=2,
            grid=(n_tiles // PLE_TILES,),
            in_specs=[
                pl.BlockSpec((PLE_TILES * tile, d), lambda k, meta, rank: (k, 0)),
                pl.BlockSpec((PLE_TILES * tile, d_ple), lambda k, meta, rank: (k, 0)),
                pl.BlockSpec((PLE_TILES, 1, tile), lambda k, meta, rank: (k, 0, 0)),
                pl.BlockSpec(memory_space=pl.ANY),
                _resident((d, d)), _full((1, d)), _resident((d_ple, d)), _full((1, d)), _full((1, d)),
            ],
            out_specs=pl.BlockSpec((PLE_TILES * tile, d), lambda k, meta, rank: (k, 0)),
            scratch_shapes=[
                pltpu.VMEM((2, PLE_TILES, t_pad, d), _F32),
                pltpu.SemaphoreType.DMA((2,)),
            ],
        ),
        out_shape=jax.ShapeDtypeStruct((n, d), _F32),
        compiler_params=pltpu.CompilerParams(
            dimension_semantics=("arbitrary",), vmem_limit_bytes=VMEM_LIMIT_BYTES),
        name="ple",
    )
    out = ple(meta, rank, x1.reshape(n, d), p[0].reshape(n, d_ple), pos, ys,
              w_pg[0], _row(b_pg[0]), w_pp[0], _row(ln2_g[0]), _row(ln2_b[0]))
    return out.reshape(bsz, seq, d)
```

```python
import functools

import jax
import jax.numpy as jnp
from jax import lax
from jax.experimental import pallas as pl
from jax.experimental.pallas import tpu as pltpu

LN_EPS = 1e-5
N_GROUPS = 4
EXPERTS_PER_GROUP = 4
SUBLANES = 8
LANES = 128
ROUTER_ROWS = 32
TILE = 512
MOE_CHUNKS = 2
PLE_TILES = 2
N_SUB = 2
VMEM_LIMIT_BYTES = 56 * 1024 * 1024
NEG_BIG = -1e30
META_FIELDS = 5

_F32 = jnp.float32
_BF16 = jnp.bfloat16
_NT = (((1,), (1,)), ((), ()))
_TN = (((0,), (0,)), ((), ()))


def _layer_norm(x, g, b):
    mu = jnp.mean(x, axis=-1, keepdims=True)
    xc = x - mu
    var = jnp.mean(xc * xc, axis=-1, keepdims=True)
    return xc * lax.rsqrt(var + LN_EPS) * g + b


def _sigmoid(x):
    return 0.5 * jnp.tanh(0.5 * x) + 0.5


def _dot(a, b):
    return jnp.dot(a, b, preferred_element_type=_F32)


def _aligned(v):
    return v if isinstance(v, int) else pl.multiple_of(v, SUBLANES)


def _rows(src, s0, dst, d0, m, sem):
    return pltpu.make_async_copy(src.at[pl.ds(_aligned(s0), _aligned(m))],
                                 dst.at[pl.ds(_aligned(d0), _aligned(m))], sem)


def _route(logits):
    row = lax.broadcasted_iota(jnp.int32, (SUBLANES, logits.shape[1]), 0)
    real = row < N_GROUPS
    gl = jnp.where(real, logits[0:SUBLANES], NEG_BIG)
    gm = jnp.max(gl, axis=0, keepdims=True)
    ge = jnp.exp(gl - gm)
    gp = ge / jnp.sum(ge, axis=0, keepdims=True)
    gp_top = jnp.max(gp, axis=0, keepdims=True)
    gidx = jnp.min(jnp.where(gp == gp_top, row, SUBLANES), axis=0, keepdims=True)

    el = jnp.zeros_like(gl)
    for g in range(N_GROUPS):
        lo = N_GROUPS + g * EXPERTS_PER_GROUP
        blk = logits[lo - lo % SUBLANES:lo - lo % SUBLANES + SUBLANES]
        if lo % SUBLANES:
            blk = pltpu.roll(blk, SUBLANES - lo % SUBLANES, 0)
        el = jnp.where(gidx == g, blk, el)
    el = jnp.where(real, el, NEG_BIG)
    em = jnp.max(el, axis=0, keepdims=True)
    ee = jnp.exp(el - em)
    ep = ee / jnp.sum(ee, axis=0, keepdims=True)
    ep = jnp.where(real, ep, -1.0)
    p1 = jnp.max(ep, axis=0, keepdims=True)
    i1 = jnp.min(jnp.where(ep == p1, row, SUBLANES), axis=0, keepdims=True)
    ep2 = jnp.where(row == i1, -1.0, ep)
    p2 = jnp.max(ep2, axis=0, keepdims=True)
    i2 = jnp.min(jnp.where(ep2 == p2, row, SUBLANES), axis=0, keepdims=True)
    denom = p1 + p2
    w = jnp.where(row == i1, p1 / denom, jnp.where(row == i2, p2 / denom, 0.0))
    return gidx, w * gp_top


def _mixer_kernel(x_ref, lng_ref, lnb_ref, w_in_ref, b_in_ref, cw_ref, cb_ref, clg_ref, clb_ref,
                  sw_ref, sb_ref, w_out_ref, b_out_ref, l1g_ref, l1b_ref, wr_ref, brg_ref, bre_ref,
                  x1_ref, pos_ref, xs_ref, meta_ref, info_ref, order_ref, rank_ref,
                  abuf, ashift, ubuf, w_in_st, w_out_st, w_in_bf, w_out_bf, wr_bf, br_ref, tri_ref, stage, zbuf,
                  cur_ref, fill_ref, nfree_ref, ring_ref, sem, zsem, wsem,
                  *, alpha, tile, t_pad, chunk, n_chunks, n_tiles, conf_w, sc_w_,
                  conf_k, sc_k, a_halo, u_halo, n_logits):
    j = pl.program_id(1)
    k = pl.program_id(0) * pl.num_programs(1) + j
    slot = k % 2
    d = x1_ref.shape[-1]
    c0, c1, c2, c3 = conf_w, 2 * conf_w, 2 * conf_w + sc_w_, 2 * conf_w + 2 * sc_w_
    w_parts = (
        (w_in_ref.at[:, 0:c1], w_in_st.at[:, 0:c1], w_in_bf.at[:, 0:c1]),
        (w_in_ref.at[:, c1:], w_in_st.at[:, c1:], w_in_bf.at[:, c1:]),
        (w_out_ref, w_out_st, w_out_bf),
    )
    w_copies = [pltpu.make_async_copy(src, st, wsem.at[n]) for n, (src, st, _) in enumerate(w_parts)]

    def weights_ready(n):
        w_copies[n].wait()
        w_parts[n][2][...] = w_parts[n][1][...].astype(_BF16)

    spare = [pltpu.make_async_copy(zbuf, xs_ref.at[pl.ds(c * chunk, chunk)], zsem)
             for c in range(n_tiles * tile // chunk, n_chunks)]

    @pl.when(k == 0)
    def _():
        for cp in w_copies:
            cp.start()
        zbuf[...] = jnp.zeros_like(zbuf)
        weights_ready(0)
        wr_bf[...] = jnp.zeros_like(wr_bf)
        wr_bf[0:n_logits, :] = wr_ref[...].astype(_BF16)
        def column(b_ref, first_row):
            shape = (ROUTER_ROWS, b_ref.shape[1])
            on_diag = (lax.broadcasted_iota(jnp.int32, shape, 0)
                       == lax.broadcasted_iota(jnp.int32, shape, 1) + first_row)
            return jnp.sum(jnp.where(on_diag, b_ref[...], 0.0), axis=1, keepdims=True)

        br_ref[...] = column(brg_ref, 0) + column(bre_ref, N_GROUPS)
        earlier = lax.broadcasted_iota(jnp.int32, (tile, tile), 0) < lax.broadcasted_iota(jnp.int32, (tile, tile), 1)
        tri_ref[...] = jnp.where(earlier, 1.0, 0.0).astype(_BF16)
        nfree_ref[0] = 0
        for g in range(N_GROUPS):
            cur_ref[g] = 0
            fill_ref[g] = chunk
        for c in range(n_chunks + 1):
            info_ref[c] = N_GROUPS - 1
        for c in range(n_chunks):
            info_ref[n_chunks + 1 + c] = chunk

    @pl.when(k == 1)
    def _():
        for cp in spare:
            cp.start()

    @pl.when(k == 2)
    def _():
        for cp in spare:
            cp.wait()

    @pl.when(j == 0)
    def _():
        abuf[0:a_halo, :] = jnp.zeros((a_halo, conf_w), _F32)
        ubuf[0:u_halo, :] = jnp.zeros((u_halo, sc_w_), _F32)

    x0 = _layer_norm(x_ref[0], lng_ref[...], lnb_ref[...])
    x0b = x0.astype(_BF16)
    h = _dot(x0b, w_in_bf[:, 0:c1]) + b_in_ref[:, 0:c1]

    a = h[:, 0:c0] * _sigmoid(h[:, c0:c1])
    abuf[a_halo:a_halo + tile, :] = a
    for i in range(1, SUBLANES):
        ashift[i - 1] = abuf[i:i + tile + a_halo - SUBLANES, :]
    acc = jnp.broadcast_to(cb_ref[...], (tile, conf_w))
    for t in range(conf_k):
        q, i = divmod(a_halo - (conf_k - 1) + t, SUBLANES)
        if i == 0:
            src = abuf[SUBLANES * q:SUBLANES * q + tile, :]
        else:
            src = ashift[i - 1, SUBLANES * q:SUBLANES * q + tile, :]
        acc = acc + cw_ref[t:t + 1, :] * src
    abuf[0:a_halo, :] = abuf[tile:tile + a_halo, :]

    @pl.when(k == 0)
    def _():
        weights_ready(1)
        weights_ready(2)

    h2 = _dot(x0b, w_in_bf[:, c1:]) + b_in_ref[:, c1:]
    an = _layer_norm(acc, clg_ref[...], clb_ref[...])
    a2 = an * _sigmoid(an)

    u = h2[:, c2 - c1:c3 - c1] * h2[:, c3 - c1:]
    ubuf[u_halo:u_halo + tile, :] = u
    sc = jnp.broadcast_to(sb_ref[...], (tile, sc_w_))
    for t in range(sc_k):
        off = u_halo - (sc_k - 1) + t
        sc = sc + sw_ref[t:t + 1, :] * ubuf[off:off + tile, :]
    ubuf[0:u_halo, :] = ubuf[tile:tile + u_halo, :]
    s = h2[:, 0:c2 - c1] * sc

    mix = (_dot(a2.astype(_BF16), w_out_bf[0:conf_w, :])
           + _dot(s.astype(_BF16), w_out_bf[conf_w:, :]) + b_out_ref[...])
    x1 = _layer_norm(alpha * x0 + mix, l1g_ref[...], l1b_ref[...])
    x1_ref[0] = x1
    x1b = x1.astype(_BF16)

    logits = lax.dot_general(wr_bf[...], x1b, _NT, preferred_element_type=_F32) + br_ref[...]
    gidx, w = _route(logits)

    row8 = lax.broadcasted_iota(jnp.int32, (SUBLANES, tile), 0)
    onehot = jnp.where(gidx == row8, 1.0, 0.0)
    cnt = jnp.sum(onehot, axis=1, keepdims=True)
    sizes, starts = [], []
    start = jnp.int32(0)
    startv = jnp.zeros((SUBLANES, tile), jnp.int32)
    for g in range(N_GROUPS):
        n8 = (cnt[g, 0].astype(jnp.int32) + (SUBLANES - 1)) & (-SUBLANES)
        sizes.append(n8)
        starts.append(start)
        startv = jnp.where(row8 == g, start, startv)
        start = start + n8
    cum = _dot(onehot.astype(_BF16), tri_ref[...])
    pos = jnp.sum(onehot * (startv.astype(_F32) + cum), axis=0, keepdims=True).astype(jnp.int32)
    pos_ref[0] = pos
    perm = jnp.where(lax.broadcasted_iota(jnp.int32, (t_pad, tile), 0) == pos, 1.0, 0.0).astype(_BF16)

    w_hi = w.astype(_BF16).astype(_F32)
    w_lo = (w - w_hi).astype(_BF16).astype(_F32)
    w_hl = jnp.where(row8 < EXPERTS_PER_GROUP, w_hi, pltpu.roll(w_lo, EXPERTS_PER_GROUP, 0))
    w_rows = jnp.concatenate([w_hl, jnp.zeros((LANES - SUBLANES, tile), _F32)], axis=0).astype(_BF16)

    def wait_slot(sl):
        for g in range(N_GROUPS):
            for piece in range(2):
                m = ring_ref[sl, 2 * g + piece]

                @pl.when(m > 0)
                def _():
                    _rows(stage.at[sl], 0, xs_ref, 0, m, sem.at[sl]).wait()

    @pl.when(k >= 2)
    def _():
        wait_slot(slot)

    stage[slot, :, 0:d] = _dot(perm, x1b)
    stage[slot, :, d:] = lax.dot_general(perm, w_rows, _NT, preferred_element_type=_F32)

    for g in range(N_GROUPS):
        n8, sg = sizes[g], starts[g]
        fill, cur, newc = fill_ref[g], cur_ref[g], nfree_ref[0]
        m1 = jnp.minimum(n8, chunk - fill)
        m2 = n8 - m1
        r1 = cur * chunk + fill
        r2 = newc * chunk
        ring_ref[slot, 2 * g] = m1
        ring_ref[slot, 2 * g + 1] = m2
        for f, v in enumerate((r1, m1, r2, m2, sg)):
            meta_ref[k, g * META_FIELDS + f] = v

        @pl.when(m1 > 0)
        def _():
            _rows(stage.at[slot], sg, xs_ref, r1, m1, sem.at[slot]).start()

        @pl.when(m2 > 0)
        def _():
            _rows(stage.at[slot], sg + m1, xs_ref, r2, m2, sem.at[slot]).start()
            info_ref[newc] = g
            cur_ref[g] = newc
            nfree_ref[0] = newc + 1

        fill_ref[g] = jnp.where(m2 > 0, m2, fill + m1)

    @pl.when(k == n_tiles - 1)
    def _():
        nfree = nfree_ref[0]
        info_ref[n_chunks] = nfree
        tails = []
        for g in range(N_GROUPS):
            rem = chunk - fill_ref[g]

            @pl.when(rem > 0)
            def _():
                info_ref[n_chunks + 1 + cur_ref[g]] = fill_ref[g]

            tails.append((rem, _rows(zbuf, 0, xs_ref, cur_ref[g] * chunk + fill_ref[g], rem, zsem)))
        for rem, cp in tails:
            pl.when(rem > 0)(cp.start)

        slot_out = jnp.int32(0)
        for g in range(N_GROUPS):
            def place(c, nxt, g=g):
                hit = info_ref[c] == g

                @pl.when(hit)
                def _():
                    order_ref[nxt] = c
                    rank_ref[c] = nxt

                return nxt + hit.astype(jnp.int32)

            slot_out = lax.fori_loop(0, n_chunks, place, slot_out)

        @pl.when(k >= 1)
        def _():
            wait_slot(1 - slot)

        wait_slot(slot)
        for rem, cp in tails:
            pl.when(rem > 0)(cp.wait)


def _moe_kernel(order_ref, info_ref, *refs, d, d_expert, n_chunks, per_step):
    xs_refs = refs[:per_step]
    wg_hbm, wu_hbm, wd_hbm, o_ref, wg_buf, wu_buf, wd_buf, slot_ref, wsem = refs[per_step:]
    chunk = xs_refs[0].shape[0]

    def weight_copies(g, sl):
        lo = g * EXPERTS_PER_GROUP
        return [pltpu.make_async_copy(src.at[pl.ds(lo, EXPERTS_PER_GROUP)], dst.at[sl], wsem.at[0])
                for src, dst in ((wg_hbm, wg_buf), (wu_hbm, wu_buf), (wd_hbm, wd_buf))]

    def first_copies(g, e):
        return [pltpu.make_async_copy(src.at[g * EXPERTS_PER_GROUP + e], dst.at[0, e], wsem.at[1 + e])
                for src, dst in ((wg_hbm, wg_buf), (wu_hbm, wu_buf), (wd_hbm, wd_buf))]

    for ci, xs_ref in enumerate(xs_refs):
        i = pl.program_id(0) * per_step + ci
        rows = slice(ci * chunk, (ci + 1) * chunk)
        c = order_ref[i]
        grp = info_ref[c]
        prev = info_ref[order_ref[jnp.maximum(i - 1, 0)]]

        @pl.when(i == 0)
        def _():
            slot_ref[0] = 0
            for e in range(EXPERTS_PER_GROUP):
                for cp in first_copies(grp, e):
                    cp.start()

        @pl.when((i == 0) | (grp != prev))
        def _():
            @pl.when(i > 0)
            def _():
                slot_ref[0] = 1 - slot_ref[0]
                for cp in weight_copies(grp, slot_ref[0]):
                    cp.wait()

            cur = slot_ref[0]

            def later_group(j, found):
                gj = info_ref[order_ref[j]]
                return jnp.where((found < 0) & (j > i) & (gj != grp), gj, found)

            nxt = lax.fori_loop(0, n_chunks, later_group, jnp.int32(-1))

            @pl.when(nxt >= 0)
            def _():
                for cp in weight_copies(nxt, 1 - cur):
                    cp.start()

        used = c < info_ref[n_chunks]
        half = chunk // 2
        low_only = info_ref[n_chunks + 1 + c] <= half

        def experts(m, xs_ref=xs_ref, grp=grp, await_first=False):
            xb = xs_ref[0:m, 0:d]
            ws = slot_ref[0]
            aux = xs_ref[0:m, d:]
            y = jnp.zeros((m, o_ref.shape[1]), _F32)
            for e in range(EXPERTS_PER_GROUP):
                if await_first:
                    for cp in first_copies(grp, e):
                        cp.wait()
                hg = _dot(xb, wg_buf[ws, e])
                hu = _dot(xb, wu_buf[ws, e])
                we = aux[:, e:e + 1] + aux[:, EXPERTS_PER_GROUP + e:EXPERTS_PER_GROUP + e + 1]
                hid = hg * _sigmoid(hg) * hu * we
                y = y + _dot(hid, wd_buf[ws, e])
            return y

        later = used
        if ci == 0:
            later = used & (i > 0)

            @pl.when(i == 0)
            def _():
                o_ref[rows, :] = experts(chunk, await_first=True)

        @pl.when(later & jnp.logical_not(low_only))
        def _():
            o_ref[rows, :] = experts(chunk)

        @pl.when(later & low_only)
        def _():
            o_ref[ci * chunk:ci * chunk + half, :] = experts(half)
            o_ref[ci * chunk + half:(ci + 1) * chunk, :] = jnp.zeros((half, o_ref.shape[1]), _F32)

        @pl.when(jnp.logical_not(used))
        def _():
            o_ref[rows, :] = jnp.zeros((chunk, o_ref.shape[1]), _F32)


def _ple_kernel(meta_ref, rank_ref, x1_ref, p_ref, pos_ref, ys_ref, wpg_ref, bpg_ref, wpp_ref, l2g_ref, l2b_ref,
                o_ref, ybuf, sem, *, alpha, tile, chunk, n_sub, tiles_per_step, t_pad, n_steps):
    k = pl.program_id(0)
    slot = k % 2

    def moved(r):
        return rank_ref[r // chunk] * chunk + r % chunk

    def pieces(step, sl):
        out = []
        for ti in range(tiles_per_step):
            kk = step * tiles_per_step + ti
            for g in range(N_GROUPS):
                r1, m1, r2, m2, sg = (meta_ref[kk, g * META_FIELDS + f] for f in range(META_FIELDS))
                out.append((m1, _rows(ys_ref, moved(r1), ybuf.at[sl, ti], sg, m1, sem.at[sl])))
                out.append((m2, _rows(ys_ref, moved(r2), ybuf.at[sl, ti], sg + m1, m2, sem.at[sl])))
        return out

    def fetch(kk, sl):
        for m, cp in pieces(kk, sl):
            pl.when(m > 0)(cp.start)

    @pl.when(k == 0)
    def _():
        ybuf[:, :, tile:, :] = jnp.zeros((2, tiles_per_step, t_pad - tile, ybuf.shape[-1]), _F32)
        fetch(0, 0)

    @pl.when(k + 1 < n_steps)
    def _():
        fetch(k + 1, 1 - slot)

    for m, cp in pieces(k, slot):
        pl.when(m > 0)(cp.wait)

    sub = tile // n_sub
    for ti in range(tiles_per_step):
        yb = ybuf[slot, ti]
        pos = pos_ref[ti]
        for blk in range(n_sub):
            cols = slice(blk * sub, (blk + 1) * sub)
            rows = slice(ti * tile + blk * sub, ti * tile + (blk + 1) * sub)
            perm = jnp.where(lax.broadcasted_iota(jnp.int32, (t_pad, sub), 0) == pos[:, cols], 1.0, 0.0)
            y = lax.dot_general(perm, yb, _TN, preferred_element_type=_F32)
            r = alpha * x1_ref[rows, :] + y
            gate = _sigmoid(_dot(r, wpg_ref[...]) + bpg_ref[...])
            pp = _dot(p_ref[rows, :], wpp_ref[...])
            o_ref[rows, :] = _layer_norm(r + gate * pp, l2g_ref[...], l2b_ref[...])


def _full(shape):
    return pl.BlockSpec(shape, lambda *_: (0,) * len(shape))


def _resident(shape):
    return pl.BlockSpec(shape, lambda *_: (0,) * len(shape), pipeline_mode=pl.Buffered(1))


def _row(v):
    return v.reshape(1, -1).astype(_F32)


def kernel(x, p, ln_in_g, ln_in_b, w_in, b_in, conf_dw_w, conf_dw_b, conf_ln_g, conf_ln_b, sc_w, sc_b, w_out, b_out, ln1_g, ln1_b, w_rg, b_rg, w_re, b_re, w_gate, w_up, w_down, w_pg, b_pg, w_pp, ln2_g, ln2_b):
    depth = w_in.shape[0]
    assert depth == 1, "single-layer block"
    alpha = (2.0 * depth) ** 0.25
    bsz, seq, d = x.shape
    n = bsz * seq
    conf_k, conf_w = conf_dw_w.shape[1:]
    sc_k, sc_w_ = sc_w.shape[1:]
    d_in = w_in.shape[2]
    n_exp, _, d_expert = w_gate.shape[1:]
    assert n_exp == N_GROUPS * EXPERTS_PER_GROUP
    assert d_in == 2 * conf_w + 3 * sc_w_ and w_out.shape[1] == conf_w + sc_w_
    d_ple = p.shape[-1]

    tile = chunk = TILE
    assert seq % tile == 0 and (bsz * (seq // tile)) % PLE_TILES == 0
    assert bsz * (seq // tile) >= 3 * N_GROUPS, "spare chunks are zero-filled during the first three steps"
    nt = seq // tile
    n_tiles = bsz * nt
    a_halo = -(-(conf_k - 1) // SUBLANES) * SUBLANES
    u_halo = -(-(sc_k - 1) // SUBLANES) * SUBLANES
    t_pad = tile + N_GROUPS * SUBLANES
    n_chunks = -(-(n + (SUBLANES - 1) * N_GROUPS * n_tiles) // chunk) + N_GROUPS
    n_chunks = -(-n_chunks // MOE_CHUNKS) * MOE_CHUNKS
    d_pay = d + LANES

    n_logits = N_GROUPS + n_exp
    wr = jnp.concatenate([w_rg[0], w_re[0]], axis=1).T

    mixer = pl.pallas_call(
        functools.partial(_mixer_kernel, alpha=alpha, tile=tile, t_pad=t_pad, chunk=chunk,
                          n_chunks=n_chunks, n_tiles=n_tiles, conf_w=conf_w, sc_w_=sc_w_,
                          conf_k=conf_k, sc_k=sc_k, a_halo=a_halo, u_halo=u_halo, n_logits=n_logits),
        grid=(bsz, nt),
        in_specs=[
            pl.BlockSpec((1, tile, d), lambda b, j: (b, j, 0)),
            _full((1, d)), _full((1, d)),
            pl.BlockSpec(memory_space=pl.ANY), _full((1, d_in)),
            _full((conf_k, conf_w)), _full((1, conf_w)), _full((1, conf_w)), _full((1, conf_w)),
            _full((sc_k, sc_w_)), _full((1, sc_w_)),
            pl.BlockSpec(memory_space=pl.ANY), _full((1, d)),
            _full((1, d)), _full((1, d)),
            _full((n_logits, d)), _full((1, N_GROUPS)), _full((1, n_exp)),
        ],
        out_specs=[
            pl.BlockSpec((1, tile, d), lambda b, j: (b, j, 0)),
            pl.BlockSpec((1, 1, tile), lambda b, j: (b * nt + j, 0, 0)),
            pl.BlockSpec(memory_space=pl.ANY),
            pl.BlockSpec(memory_space=pltpu.SMEM),
            pl.BlockSpec(memory_space=pltpu.SMEM),
            pl.BlockSpec(memory_space=pltpu.SMEM),
            pl.BlockSpec(memory_space=pltpu.SMEM),
        ],
        out_shape=[
            jax.ShapeDtypeStruct((bsz, seq, d), _F32),
            jax.ShapeDtypeStruct((n_tiles, 1, tile), jnp.int32),
            jax.ShapeDtypeStruct((n_chunks * chunk, d_pay), _F32),
            jax.ShapeDtypeStruct((n_tiles, N_GROUPS * META_FIELDS), jnp.int32),
            jax.ShapeDtypeStruct((2 * n_chunks + 1,), jnp.int32),
            jax.ShapeDtypeStruct((n_chunks,), jnp.int32),
            jax.ShapeDtypeStruct((n_chunks,), jnp.int32),
        ],
        scratch_shapes=[
            pltpu.VMEM((a_halo + tile, conf_w), _F32),
            pltpu.VMEM((SUBLANES - 1, a_halo + tile - SUBLANES, conf_w), _F32),
            pltpu.VMEM((u_halo + tile, sc_w_), _F32),
            pltpu.VMEM((d, d_in), _F32),
            pltpu.VMEM((conf_w + sc_w_, d), _F32),
            pltpu.VMEM((d, d_in), _BF16),
            pltpu.VMEM((conf_w + sc_w_, d), _BF16),
            pltpu.VMEM((ROUTER_ROWS, d), _BF16),
            pltpu.VMEM((ROUTER_ROWS, 1), _F32),
            pltpu.VMEM((tile, tile), _BF16),
            pltpu.VMEM((2, t_pad, d_pay), _F32),
            pltpu.VMEM((chunk, d_pay), _F32),
            pltpu.SMEM((N_GROUPS,), jnp.int32),
            pltpu.SMEM((N_GROUPS,), jnp.int32),
            pltpu.SMEM((1,), jnp.int32),
            pltpu.SMEM((2, 2 * N_GROUPS), jnp.int32),
            pltpu.SemaphoreType.DMA((2,)),
            pltpu.SemaphoreType.DMA(()),
            pltpu.SemaphoreType.DMA((3,)),
        ],
        compiler_params=pltpu.CompilerParams(
            dimension_semantics=("arbitrary", "arbitrary"), vmem_limit_bytes=VMEM_LIMIT_BYTES),
        name="mixer",
    )
    x1, pos, xs, meta, info, order, rank = mixer(
        x, _row(ln_in_g), _row(ln_in_b), w_in[0], _row(b_in[0]),
        conf_dw_w[0], _row(conf_dw_b[0]), _row(conf_ln_g[0]), _row(conf_ln_b[0]),
        sc_w[0], _row(sc_b[0]), w_out[0], _row(b_out[0]),
        _row(ln1_g[0]), _row(ln1_b[0]), wr, _row(b_rg[0]), _row(b_re[0]))

    moe = pl.pallas_call(
        functools.partial(_moe_kernel, d=d, d_expert=d_expert, n_chunks=n_chunks, per_step=MOE_CHUNKS),
        grid_spec=pltpu.PrefetchScalarGridSpec(
            num_scalar_prefetch=2,
            grid=(n_chunks // MOE_CHUNKS,),
            in_specs=[
                pl.BlockSpec((chunk, d_pay), functools.partial(
                    lambda i, order, info, ci: (order[i * MOE_CHUNKS + ci], 0), ci=ci))
                for ci in range(MOE_CHUNKS)
            ] + [
                pl.BlockSpec(memory_space=pl.ANY),
                pl.BlockSpec(memory_space=pl.ANY),
                pl.BlockSpec(memory_space=pl.ANY),
            ],
            out_specs=pl.BlockSpec((MOE_CHUNKS * chunk, d), lambda i, order, info: (i, 0)),
            scratch_shapes=[
                pltpu.VMEM((2, EXPERTS_PER_GROUP, d, d_expert), _F32),
                pltpu.VMEM((2, EXPERTS_PER_GROUP, d, d_expert), _F32),
                pltpu.VMEM((2, EXPERTS_PER_GROUP, d_expert, d), _F32),
                pltpu.SMEM((1,), jnp.int32),
                pltpu.SemaphoreType.DMA((1 + EXPERTS_PER_GROUP,)),
            ],
        ),
        out_shape=jax.ShapeDtypeStruct((n_chunks * chunk, d), _F32),
        compiler_params=pltpu.CompilerParams(
            dimension_semantics=("arbitrary",), vmem_limit_bytes=VMEM_LIMIT_BYTES),
        name="moe",
    )
    ys = moe(order, info, *([xs] * MOE_CHUNKS), w_gate[0], w_up[0], w_down[0])

    ple = pl.pallas_call(
        functools.partial(_ple_kernel, alpha=alpha, tile=tile, chunk=chunk, n_sub=N_SUB, tiles_per_step=PLE_TILES,
                          t_pad=t_pad, n_steps=n_tiles // PLE_TILES),
        grid_spec=pltpu.PrefetchScalarGridSpec(
            num_scalar_prefetch=2,
            grid=(n_tiles // PLE_TILES,),
            in_specs=[
                pl.BlockSpec((PLE_TILES * tile, d), lambda k, meta, rank: (k, 0)),
                pl.BlockSpec((PLE_TILES * tile, d_ple), lambda k, meta, rank: (k, 0)),
                pl.BlockSpec((PLE_TILES, 1, tile), lambda k, meta, rank: (k, 0, 0)),
                pl.BlockSpec(memory_space=pl.ANY),
                _resident((d, d)), _full((1, d)), _resident((d_ple, d)), _full((1, d)), _full((1, d)),
            ],
            out_specs=pl.BlockSpec((PLE_TILES * tile, d), lambda k, meta, rank: (k, 0)),
            scratch_shapes=[
                pltpu.VMEM((2, PLE_TILES, t_pad, d), _F32),
                pltpu.SemaphoreType.DMA((2,)),
            ],
        ),
        out_shape=jax.ShapeDtypeStruct((n, d), _F32),
        compiler_params=pltpu.CompilerParams(
            dimension_semantics=("arbitrary",), vmem_limit_bytes=VMEM_LIMIT_BYTES),
        name="ple",
    )
    out = ple(meta, rank, x1.reshape(n, d), p[0].reshape(n, d_ple), pos, ys,
              w_pg[0], _row(b_pg[0]), w_pp[0], _row(ln2_g[0]), _row(ln2_b[0]))
    return out.reshape(bsz, seq, d)
```

```python
import functools

import jax
import jax.numpy as jnp
from jax import lax
from jax.experimental import pallas as pl
from jax.experimental.pallas import tpu as pltpu

LN_EPS = 1e-5
N_GROUPS = 4
EXPERTS_PER_GROUP = 4
SUBLANES = 8
LANES = 128
ROUTER_ROWS = 32
TILE = 512
MOE_CHUNKS = 2
PLE_TILES = 2
PLE_RING = 3
N_SUB = 2
VMEM_LIMIT_BYTES = 56 * 1024 * 1024
NEG_BIG = -1e30
META_FIELDS = 5

_F32 = jnp.float32
_BF16 = jnp.bfloat16
_NT = (((1,), (1,)), ((), ()))
_TN = (((0,), (0,)), ((), ()))


def _layer_norm(x, g, b):
    mu = jnp.mean(x, axis=-1, keepdims=True)
    xc = x - mu
    var = jnp.mean(xc * xc, axis=-1, keepdims=True)
    return xc * lax.rsqrt(var + LN_EPS) * g + b


def _sigmoid(x):
    return 0.5 * jnp.tanh(0.5 * x) + 0.5


def _dot(a, b):
    return jnp.dot(a, b, preferred_element_type=_F32)


def _aligned(v):
    return v if isinstance(v, int) else pl.multiple_of(v, SUBLANES)


def _rows(src, s0, dst, d0, m, sem):
    return pltpu.make_async_copy(src.at[pl.ds(_aligned(s0), _aligned(m))],
                                 dst.at[pl.ds(_aligned(d0), _aligned(m))], sem)


def _route(logits):
    row = lax.broadcasted_iota(jnp.int32, (SUBLANES, logits.shape[1]), 0)
    real = row < N_GROUPS
    gl = jnp.where(real, logits[0:SUBLANES], NEG_BIG)
    gm = jnp.max(gl, axis=0, keepdims=True)
    ge = jnp.exp(gl - gm)
    gp = ge / jnp.sum(ge, axis=0, keepdims=True)
    gp_top = jnp.max(gp, axis=0, keepdims=True)
    gidx = jnp.min(jnp.where(gp == gp_top, row, SUBLANES), axis=0, keepdims=True)

    el = jnp.zeros_like(gl)
    for g in range(N_GROUPS):
        lo = N_GROUPS + g * EXPERTS_PER_GROUP
        blk = logits[lo - lo % SUBLANES:lo - lo % SUBLANES + SUBLANES]
        if lo % SUBLANES:
            blk = pltpu.roll(blk, SUBLANES - lo % SUBLANES, 0)
        el = jnp.where(gidx == g, blk, el)
    el = jnp.where(real, el, NEG_BIG)
    em = jnp.max(el, axis=0, keepdims=True)
    ee = jnp.exp(el - em)
    ep = ee / jnp.sum(ee, axis=0, keepdims=True)
    ep = jnp.where(real, ep, -1.0)
    p1 = jnp.max(ep, axis=0, keepdims=True)
    i1 = jnp.min(jnp.where(ep == p1, row, SUBLANES), axis=0, keepdims=True)
    ep2 = jnp.where(row == i1, -1.0, ep)
    p2 = jnp.max(ep2, axis=0, keepdims=True)
    i2 = jnp.min(jnp.where(ep2 == p2, row, SUBLANES), axis=0, keepdims=True)
    denom = p1 + p2
    w = jnp.where(row == i1, p1 / denom, jnp.where(row == i2, p2 / denom, 0.0))
    return gidx, w * gp_top


def _mixer_kernel(x_ref, lng_ref, lnb_ref, w_in_ref, b_in_ref, cw_ref, cb_ref, clg_ref, clb_ref,
                  sw_ref, sb_ref, w_out_ref, b_out_ref, l1g_ref, l1b_ref, wr_ref, brg_ref, bre_ref,
                  x1_ref, pos_ref, xs_ref, meta_ref, info_ref, order_ref, rank_ref,
                  abuf, ashift, ubuf, w_in_st, w_out_st, w_in_bf, w_out_bf, wr_bf, br_ref, tri_ref, stage, zbuf,
                  cur_ref, fill_ref, nfree_ref, ring_ref, sem, zsem, wsem,
                  *, alpha, tile, t_pad, chunk, n_chunks, n_tiles, conf_w, sc_w_,
                  conf_k, sc_k, a_halo, u_halo, n_logits):
    j = pl.program_id(1)
    k = pl.program_id(0) * pl.num_programs(1) + j
    slot = k % 2
    d = x1_ref.shape[-1]
    c0, c1, c2, c3 = conf_w, 2 * conf_w, 2 * conf_w + sc_w_, 2 * conf_w + 2 * sc_w_
    w_parts = (
        (w_in_ref.at[:, 0:c1], w_in_st.at[:, 0:c1], w_in_bf.at[:, 0:c1]),
        (w_in_ref.at[:, c1:], w_in_st.at[:, c1:], w_in_bf.at[:, c1:]),
        (w_out_ref, w_out_st, w_out_bf),
    )
    w_copies = [pltpu.make_async_copy(src, st, wsem.at[n]) for n, (src, st, _) in enumerate(w_parts)]

    def weights_ready(n):
        w_copies[n].wait()
        w_parts[n][2][...] = w_parts[n][1][...].astype(_BF16)

    spare = [pltpu.make_async_copy(zbuf, xs_ref.at[pl.ds(c * chunk, chunk)], zsem)
             for c in range(n_tiles * tile // chunk, n_chunks)]

    @pl.when(k == 0)
    def _():
        for cp in w_copies:
            cp.start()
        zbuf[...] = jnp.zeros_like(zbuf)
        weights_ready(0)
        wr_bf[...] = jnp.zeros_like(wr_bf)
        wr_bf[0:n_logits, :] = wr_ref[...].astype(_BF16)
        def column(b_ref, first_row):
            shape = (ROUTER_ROWS, b_ref.shape[1])
            on_diag = (lax.broadcasted_iota(jnp.int32, shape, 0)
                       == lax.broadcasted_iota(jnp.int32, shape, 1) + first_row)
            return jnp.sum(jnp.where(on_diag, b_ref[...], 0.0), axis=1, keepdims=True)

        br_ref[...] = column(brg_ref, 0) + column(bre_ref, N_GROUPS)
        earlier = lax.broadcasted_iota(jnp.int32, (tile, tile), 0) < lax.broadcasted_iota(jnp.int32, (tile, tile), 1)
        tri_ref[...] = jnp.where(earlier, 1.0, 0.0).astype(_BF16)
        nfree_ref[0] = 0
        for g in range(N_GROUPS):
            cur_ref[g] = 0
            fill_ref[g] = chunk
        for c in range(n_chunks + 1):
            info_ref[c] = N_GROUPS - 1
        for c in range(n_chunks):
            info_ref[n_chunks + 1 + c] = chunk

    @pl.when(k == 1)
    def _():
        for cp in spare:
            cp.start()

    @pl.when(k == 2)
    def _():
        for cp in spare:
            cp.wait()

    @pl.when(j == 0)
    def _():
        abuf[0:a_halo, :] = jnp.zeros((a_halo, conf_w), _F32)
        ubuf[0:u_halo, :] = jnp.zeros((u_halo, sc_w_), _F32)

    x0 = _layer_norm(x_ref[0], lng_ref[...], lnb_ref[...])
    x0b = x0.astype(_BF16)
    h = _dot(x0b, w_in_bf[:, 0:c1]) + b_in_ref[:, 0:c1]

    a = h[:, 0:c0] * _sigmoid(h[:, c0:c1])
    abuf[a_halo:a_halo + tile, :] = a
    for i in range(1, SUBLANES):
        ashift[i - 1] = abuf[i:i + tile + a_halo - SUBLANES, :]
    acc = jnp.broadcast_to(cb_ref[...], (tile, conf_w))
    for t in range(conf_k):
        q, i = divmod(a_halo - (conf_k - 1) + t, SUBLANES)
        if i == 0:
            src = abuf[SUBLANES * q:SUBLANES * q + tile, :]
        else:
            src = ashift[i - 1, SUBLANES * q:SUBLANES * q + tile, :]
        acc = acc + cw_ref[t:t + 1, :] * src
    abuf[0:a_halo, :] = abuf[tile:tile + a_halo, :]

    @pl.when(k == 0)
    def _():
        weights_ready(1)
        weights_ready(2)

    h2 = _dot(x0b, w_in_bf[:, c1:]) + b_in_ref[:, c1:]
    an = _layer_norm(acc, clg_ref[...], clb_ref[...])
    a2 = an * _sigmoid(an)

    u = h2[:, c2 - c1:c3 - c1] * h2[:, c3 - c1:]
    ubuf[u_halo:u_halo + tile, :] = u
    sc = jnp.broadcast_to(sb_ref[...], (tile, sc_w_))
    for t in range(sc_k):
        off = u_halo - (sc_k - 1) + t
        sc = sc + sw_ref[t:t + 1, :] * ubuf[off:off + tile, :]
    ubuf[0:u_halo, :] = ubuf[tile:tile + u_halo, :]
    s = h2[:, 0:c2 - c1] * sc

    mix = (_dot(a2.astype(_BF16), w_out_bf[0:conf_w, :])
           + _dot(s.astype(_BF16), w_out_bf[conf_w:, :]) + b_out_ref[...])
    x1 = _layer_norm(alpha * x0 + mix, l1g_ref[...], l1b_ref[...])
    x1_ref[0] = x1
    x1b = x1.astype(_BF16)

    logits = lax.dot_general(wr_bf[...], x1b, _NT, preferred_element_type=_F32) + br_ref[...]
    gidx, w = _route(logits)

    row8 = lax.broadcasted_iota(jnp.int32, (SUBLANES, tile), 0)
    onehot = jnp.where(gidx == row8, 1.0, 0.0)
    cnt = jnp.sum(onehot, axis=1, keepdims=True)
    sizes, starts = [], []
    start = jnp.int32(0)
    startv = jnp.zeros((SUBLANES, tile), jnp.int32)
    for g in range(N_GROUPS):
        n8 = (cnt[g, 0].astype(jnp.int32) + (SUBLANES - 1)) & (-SUBLANES)
        sizes.append(n8)
        starts.append(start)
        startv = jnp.where(row8 == g, start, startv)
        start = start + n8
    cum = _dot(onehot.astype(_BF16), tri_ref[...])
    pos = jnp.sum(onehot * (startv.astype(_F32) + cum), axis=0, keepdims=True).astype(jnp.int32)
    pos_ref[0] = pos
    perm = jnp.where(lax.broadcasted_iota(jnp.int32, (t_pad, tile), 0) == pos, 1.0, 0.0).astype(_BF16)

    w_hi = w.astype(_BF16).astype(_F32)
    w_lo = (w - w_hi).astype(_BF16).astype(_F32)
    w_hl = jnp.where(row8 < EXPERTS_PER_GROUP, w_hi, pltpu.roll(w_lo, EXPERTS_PER_GROUP, 0))
    w_rows = jnp.concatenate([w_hl, jnp.zeros((LANES - SUBLANES, tile), _F32)], axis=0).astype(_BF16)

    def wait_slot(sl):
        for g in range(N_GROUPS):
            for piece in range(2):
                m = ring_ref[sl, 2 * g + piece]

                @pl.when(m > 0)
                def _():
                    _rows(stage.at[sl], 0, xs_ref, 0, m, sem.at[sl]).wait()

    @pl.when(k >= 2)
    def _():
        wait_slot(slot)

    stage[slot, :, 0:d] = _dot(perm, x1b)
    stage[slot, :, d:] = lax.dot_general(perm, w_rows, _NT, preferred_element_type=_F32)

    for g in range(N_GROUPS):
        n8, sg = sizes[g], starts[g]
        fill, cur, newc = fill_ref[g], cur_ref[g], nfree_ref[0]
        m1 = jnp.minimum(n8, chunk - fill)
        m2 = n8 - m1
        r1 = cur * chunk + fill
        r2 = newc * chunk
        ring_ref[slot, 2 * g] = m1
        ring_ref[slot, 2 * g + 1] = m2
        for f, v in enumerate((r1, m1, r2, m2, sg)):
            meta_ref[k, g * META_FIELDS + f] = v

        @pl.when(m1 > 0)
        def _():
            _rows(stage.at[slot], sg, xs_ref, r1, m1, sem.at[slot]).start()

        @pl.when(m2 > 0)
        def _():
            _rows(stage.at[slot], sg + m1, xs_ref, r2, m2, sem.at[slot]).start()
            info_ref[newc] = g
            cur_ref[g] = newc
            nfree_ref[0] = newc + 1

        fill_ref[g] = jnp.where(m2 > 0, m2, fill + m1)

    @pl.when(k == n_tiles - 1)
    def _():
        nfree = nfree_ref[0]
        info_ref[n_chunks] = nfree
        tails = []
        for g in range(N_GROUPS):
            rem = chunk - fill_ref[g]

            @pl.when(rem > 0)
            def _():
                info_ref[n_chunks + 1 + cur_ref[g]] = fill_ref[g]

            tails.append((rem, _rows(zbuf, 0, xs_ref, cur_ref[g] * chunk + fill_ref[g], rem, zsem)))
        for rem, cp in tails:
            pl.when(rem > 0)(cp.start)

        slot_out = jnp.int32(0)
        for g in range(N_GROUPS):
            def place(c, nxt, g=g):
                hit = info_ref[c] == g

                @pl.when(hit)
                def _():
                    order_ref[nxt] = c
                    rank_ref[c] = nxt

                return nxt + hit.astype(jnp.int32)

            slot_out = lax.fori_loop(0, n_chunks, place, slot_out)

        @pl.when(k >= 1)
        def _():
            wait_slot(1 - slot)

        wait_slot(slot)
        for rem, cp in tails:
            pl.when(rem > 0)(cp.wait)


def _moe_kernel(order_ref, info_ref, *refs, d, d_expert, n_chunks, per_step):
    xs_refs = refs[:per_step]
    wg_hbm, wu_hbm, wd_hbm, o_ref, wg_buf, wu_buf, wd_buf, slot_ref, wsem = refs[per_step:]
    chunk = xs_refs[0].shape[0]

    def weight_copies(g, sl):
        lo = g * EXPERTS_PER_GROUP
        return [pltpu.make_async_copy(src.at[pl.ds(lo, EXPERTS_PER_GROUP)], dst.at[sl], wsem.at[0])
                for src, dst in ((wg_hbm, wg_buf), (wu_hbm, wu_buf), (wd_hbm, wd_buf))]

    def first_copies(g, e):
        return [pltpu.make_async_copy(src.at[g * EXPERTS_PER_GROUP + e], dst.at[0, e], wsem.at[1 + e])
                for src, dst in ((wg_hbm, wg_buf), (wu_hbm, wu_buf), (wd_hbm, wd_buf))]

    for ci, xs_ref in enumerate(xs_refs):
        i = pl.program_id(0) * per_step + ci
        rows = slice(ci * chunk, (ci + 1) * chunk)
        c = order_ref[i]
        grp = info_ref[c]
        prev = info_ref[order_ref[jnp.maximum(i - 1, 0)]]

        @pl.when(i == 0)
        def _():
            slot_ref[0] = 0
            for e in range(EXPERTS_PER_GROUP):
                for cp in first_copies(grp, e):
                    cp.start()

        @pl.when((i == 0) | (grp != prev))
        def _():
            @pl.when(i > 0)
            def _():
                slot_ref[0] = 1 - slot_ref[0]
                for cp in weight_copies(grp, slot_ref[0]):
                    cp.wait()

            cur = slot_ref[0]

            def later_group(j, found):
                gj = info_ref[order_ref[j]]
                return jnp.where((found < 0) & (j > i) & (gj != grp), gj, found)

            nxt = lax.fori_loop(0, n_chunks, later_group, jnp.int32(-1))

            @pl.when(nxt >= 0)
            def _():
                for cp in weight_copies(nxt, 1 - cur):
                    cp.start()

        used = c < info_ref[n_chunks]
        half = chunk // 2
        low_only = info_ref[n_chunks + 1 + c] <= half

        def experts(m, xs_ref=xs_ref, grp=grp, await_first=False):
            xb = xs_ref[0:m, 0:d]
            ws = slot_ref[0]
            aux = xs_ref[0:m, d:]
            y = jnp.zeros((m, o_ref.shape[1]), _F32)
            for e in range(EXPERTS_PER_GROUP):
                if await_first:
                    for cp in first_copies(grp, e):
                        cp.wait()
                hg = _dot(xb, wg_buf[ws, e])
                hu = _dot(xb, wu_buf[ws, e])
                we = aux[:, e:e + 1] + aux[:, EXPERTS_PER_GROUP + e:EXPERTS_PER_GROUP + e + 1]
                hid = hg * _sigmoid(hg) * hu * we
                y = y + _dot(hid, wd_buf[ws, e])
            return y

        later = used
        if ci == 0:
            later = used & (i > 0)

            @pl.when(i == 0)
            def _():
                o_ref[rows, :] = experts(chunk, await_first=True)

        @pl.when(later & jnp.logical_not(low_only))
        def _():
            o_ref[rows, :] = experts(chunk)

        @pl.when(later & low_only)
        def _():
            o_ref[ci * chunk:ci * chunk + half, :] = experts(half)
            o_ref[ci * chunk + half:(ci + 1) * chunk, :] = jnp.zeros((half, o_ref.shape[1]), _F32)

        @pl.when(jnp.logical_not(used))
        def _():
            o_ref[rows, :] = jnp.zeros((chunk, o_ref.shape[1]), _F32)


def _ple_kernel(meta_ref, rank_ref, x1_hbm, p_hbm, pos_ref, ys_ref, wpg_ref, bpg_ref, wpp_ref, l2g_ref, l2b_ref,
                o_ref, ybuf, xbuf, pbuf, sem, isem, *, alpha, tile, chunk, n_sub, tiles_per_step, t_pad, n_steps):
    k = pl.program_id(0)
    slot = k % 2
    ring = k % PLE_RING
    step_rows = tiles_per_step * tile

    def inputs(step, sl):
        src = pl.ds(pl.multiple_of(step * step_rows, step_rows), step_rows)
        return [pltpu.make_async_copy(x1_hbm.at[src], xbuf.at[sl], isem.at[sl, 0]),
                pltpu.make_async_copy(p_hbm.at[src], pbuf.at[sl], isem.at[sl, 1])]

    def moved(r):
        return rank_ref[r // chunk] * chunk + r % chunk

    def pieces(step, sl):
        out = []
        for ti in range(tiles_per_step):
            kk = step * tiles_per_step + ti
            for g in range(N_GROUPS):
                r1, m1, r2, m2, sg = (meta_ref[kk, g * META_FIELDS + f] for f in range(META_FIELDS))
                out.append((m1, _rows(ys_ref, moved(r1), ybuf.at[sl, ti], sg, m1, sem.at[sl])))
                out.append((m2, _rows(ys_ref, moved(r2), ybuf.at[sl, ti], sg + m1, m2, sem.at[sl])))
        return out

    def fetch(kk, sl):
        for m, cp in pieces(kk, sl):
            pl.when(m > 0)(cp.start)

    @pl.when(k == 0)
    def _():
        ybuf[:, :, tile:, :] = jnp.zeros((2, tiles_per_step, t_pad - tile, ybuf.shape[-1]), _F32)
        fetch(0, 0)
        for ahead in range(min(PLE_RING - 1, n_steps)):
            for cp in inputs(ahead, ahead):
                cp.start()

    @pl.when(k + 1 < n_steps)
    def _():
        fetch(k + 1, 1 - slot)

    @pl.when(k + PLE_RING - 1 < n_steps)
    def _():
        for cp in inputs(k + PLE_RING - 1, (k + PLE_RING - 1) % PLE_RING):
            cp.start()

    for m, cp in pieces(k, slot):
        pl.when(m > 0)(cp.wait)
    for cp in inputs(k, ring):
        cp.wait()

    sub = tile // n_sub
    for ti in range(tiles_per_step):
        yb = ybuf[slot, ti]
        pos = pos_ref[ti]
        for blk in range(n_sub):
            cols = slice(blk * sub, (blk + 1) * sub)
            rows = slice(ti * tile + blk * sub, ti * tile + (blk + 1) * sub)
            perm = jnp.where(lax.broadcasted_iota(jnp.int32, (t_pad, sub), 0) == pos[:, cols], 1.0, 0.0)
            y = lax.dot_general(perm, yb, _TN, preferred_element_type=_F32)
            r = alpha * xbuf[ring, rows, :] + y
            gate = _sigmoid(_dot(r, wpg_ref[...]) + bpg_ref[...])
            pp = _dot(pbuf[ring, rows, :], wpp_ref[...])
            o_ref[rows, :] = _layer_norm(r + gate * pp, l2g_ref[...], l2b_ref[...])


def _full(shape):
    return pl.BlockSpec(shape, lambda *_: (0,) * len(shape))


def _resident(shape):
    return pl.BlockSpec(shape, lambda *_: (0,) * len(shape), pipeline_mode=pl.Buffered(1))


def _row(v):
    return v.reshape(1, -1).astype(_F32)


def kernel(x, p, ln_in_g, ln_in_b, w_in, b_in, conf_dw_w, conf_dw_b, conf_ln_g, conf_ln_b, sc_w, sc_b, w_out, b_out, ln1_g, ln1_b, w_rg, b_rg, w_re, b_re, w_gate, w_up, w_down, w_pg, b_pg, w_pp, ln2_g, ln2_b):
    depth = w_in.shape[0]
    assert depth == 1, "single-layer block"
    alpha = (2.0 * depth) ** 0.25
    bsz, seq, d = x.shape
    n = bsz * seq
    conf_k, conf_w = conf_dw_w.shape[1:]
    sc_k, sc_w_ = sc_w.shape[1:]
    d_in = w_in.shape[2]
    n_exp, _, d_expert = w_gate.shape[1:]
    assert n_exp == N_GROUPS * EXPERTS_PER_GROUP
    assert d_in == 2 * conf_w + 3 * sc_w_ and w_out.shape[1] == conf_w + sc_w_
    d_ple = p.shape[-1]

    tile = chunk = TILE
    assert seq % tile == 0 and (bsz * (seq // tile)) % PLE_TILES == 0
    assert bsz * (seq // tile) >= 3 * N_GROUPS, "spare chunks are zero-filled during the first three steps"
    nt = seq // tile
    n_tiles = bsz * nt
    a_halo = -(-(conf_k - 1) // SUBLANES) * SUBLANES
    u_halo = -(-(sc_k - 1) // SUBLANES) * SUBLANES
    t_pad = tile + N_GROUPS * SUBLANES
    n_chunks = -(-(n + (SUBLANES - 1) * N_GROUPS * n_tiles) // chunk) + N_GROUPS
    n_chunks = -(-n_chunks // MOE_CHUNKS) * MOE_CHUNKS
    d_pay = d + LANES

    n_logits = N_GROUPS + n_exp
    wr = jnp.concatenate([w_rg[0], w_re[0]], axis=1).T

    mixer = pl.pallas_call(
        functools.partial(_mixer_kernel, alpha=alpha, tile=tile, t_pad=t_pad, chunk=chunk,
                          n_chunks=n_chunks, n_tiles=n_tiles, conf_w=conf_w, sc_w_=sc_w_,
                          conf_k=conf_k, sc_k=sc_k, a_halo=a_halo, u_halo=u_halo, n_logits=n_logits),
        grid=(bsz, nt),
        in_specs=[
            pl.BlockSpec((1, tile, d), lambda b, j: (b, j, 0)),
            _full((1, d)), _full((1, d)),
            pl.BlockSpec(memory_space=pl.ANY), _full((1, d_in)),
            _full((conf_k, conf_w)), _full((1, conf_w)), _full((1, conf_w)), _full((1, conf_w)),
            _full((sc_k, sc_w_)), _full((1, sc_w_)),
            pl.BlockSpec(memory_space=pl.ANY), _full((1, d)),
            _full((1, d)), _full((1, d)),
            _full((n_logits, d)), _full((1, N_GROUPS)), _full((1, n_exp)),
        ],
        out_specs=[
            pl.BlockSpec((1, tile, d), lambda b, j: (b, j, 0)),
            pl.BlockSpec((1, 1, tile), lambda b, j: (b * nt + j, 0, 0)),
            pl.BlockSpec(memory_space=pl.ANY),
            pl.BlockSpec(memory_space=pltpu.SMEM),
            pl.BlockSpec(memory_space=pltpu.SMEM),
            pl.BlockSpec(memory_space=pltpu.SMEM),
            pl.BlockSpec(memory_space=pltpu.SMEM),
        ],
        out_shape=[
            jax.ShapeDtypeStruct((bsz, seq, d), _F32),
            jax.ShapeDtypeStruct((n_tiles, 1, tile), jnp.int32),
            jax.ShapeDtypeStruct((n_chunks * chunk, d_pay), _F32),
            jax.ShapeDtypeStruct((n_tiles, N_GROUPS * META_FIELDS), jnp.int32),
            jax.ShapeDtypeStruct((2 * n_chunks + 1,), jnp.int32),
            jax.ShapeDtypeStruct((n_chunks,), jnp.int32),
            jax.ShapeDtypeStruct((n_chunks,), jnp.int32),
        ],
        scratch_shapes=[
            pltpu.VMEM((a_halo + tile, conf_w), _F32),
            pltpu.VMEM((SUBLANES - 1, a_halo + tile - SUBLANES, conf_w), _F32),
            pltpu.VMEM((u_halo + tile, sc_w_), _F32),
            pltpu.VMEM((d, d_in), _F32),
            pltpu.VMEM((conf_w + sc_w_, d), _F32),
            pltpu.VMEM((d, d_in), _BF16),
            pltpu.VMEM((conf_w + sc_w_, d), _BF16),
            pltpu.VMEM((ROUTER_ROWS, d), _BF16),
            pltpu.VMEM((ROUTER_ROWS, 1), _F32),
            pltpu.VMEM((tile, tile), _BF16),
            pltpu.VMEM((2, t_pad, d_pay), _F32),
            pltpu.VMEM((chunk, d_pay), _F32),
            pltpu.SMEM((N_GROUPS,), jnp.int32),
            pltpu.SMEM((N_GROUPS,), jnp.int32),
            pltpu.SMEM((1,), jnp.int32),
            pltpu.SMEM((2, 2 * N_GROUPS), jnp.int32),
            pltpu.SemaphoreType.DMA((2,)),
            pltpu.SemaphoreType.DMA(()),
            pltpu.SemaphoreType.DMA((3,)),
        ],
        compiler_params=pltpu.CompilerParams(
            dimension_semantics=("arbitrary", "arbitrary"), vmem_limit_bytes=VMEM_LIMIT_BYTES),
        name="mixer",
    )
    x1, pos, xs, meta, info, order, rank = mixer(
        x, _row(ln_in_g), _row(ln_in_b), w_in[0], _row(b_in[0]),
        conf_dw_w[0], _row(conf_dw_b[0]), _row(conf_ln_g[0]), _row(conf_ln_b[0]),
        sc_w[0], _row(sc_b[0]), w_out[0], _row(b_out[0]),
        _row(ln1_g[0]), _row(ln1_b[0]), wr, _row(b_rg[0]), _row(b_re[0]))

    moe = pl.pallas_call(
        functools.partial(_moe_kernel, d=d, d_expert=d_expert, n_chunks=n_chunks, per_step=MOE_CHUNKS),
        grid_spec=pltpu.PrefetchScalarGridSpec(
            num_scalar_prefetch=2,
            grid=(n_chunks // MOE_CHUNKS,),
            in_specs=[
                pl.BlockSpec((chunk, d_pay), functools.partial(
                    lambda i, order, info, ci: (order[i * MOE_CHUNKS + ci], 0), ci=ci))
                for ci in range(MOE_CHUNKS)
            ] + [
                pl.BlockSpec(memory_space=pl.ANY),
                pl.BlockSpec(memory_space=pl.ANY),
                pl.BlockSpec(memory_space=pl.ANY),
            ],
            out_specs=pl.BlockSpec((MOE_CHUNKS * chunk, d), lambda i, order, info: (i, 0)),
            scratch_shapes=[
                pltpu.VMEM((2, EXPERTS_PER_GROUP, d, d_expert), _F32),
                pltpu.VMEM((2, EXPERTS_PER_GROUP, d, d_expert), _F32),
                pltpu.VMEM((2, EXPERTS_PER_GROUP, d_expert, d), _F32),
                pltpu.SMEM((1,), jnp.int32),
                pltpu.SemaphoreType.DMA((1 + EXPERTS_PER_GROUP,)),
            ],
        ),
        out_shape=jax.ShapeDtypeStruct((n_chunks * chunk, d), _F32),
        compiler_params=pltpu.CompilerParams(
            dimension_semantics=("arbitrary",), vmem_limit_bytes=VMEM_LIMIT_BYTES),
        name="moe",
    )
    ys = moe(order, info, *([xs] * MOE_CHUNKS), w_gate[0], w_up[0], w_down[0])

    ple = pl.pallas_call(
        functools.partial(_ple_kernel, alpha=alpha, tile=tile, chunk=chunk, n_sub=N_SUB, tiles_per_step=PLE_TILES,
                          t_pad=t_pad, n_steps=n_tiles // PLE_TILES),
        grid_spec=pltpu.PrefetchScalarGridSpec(
            num_scalar_prefetch=2,
            grid=(n_tiles // PLE_TILES,),
            in_specs=[
                pl.BlockSpec(memory_space=pl.ANY),
                pl.BlockSpec(memory_space=pl.ANY),
                pl.BlockSpec((PLE_TILES, 1, tile), lambda k, meta, rank: (k, 0, 0)),
                pl.BlockSpec(memory_space=pl.ANY),
                _resident((d, d)), _full((1, d)), _resident((d_ple, d)), _full((1, d)), _full((1, d)),
            ],
            out_specs=pl.BlockSpec((PLE_TILES * tile, d), lambda k, meta, rank: (k, 0)),
            scratch_shapes=[
                pltpu.VMEM((2, PLE_TILES, t_pad, d), _F32),
                pltpu.VMEM((PLE_RING, PLE_TILES * tile, d), _F32),
                pltpu.VMEM((PLE_RING, PLE_TILES * tile, d_ple), _F32),
                pltpu.SemaphoreType.DMA((2,)),
                pltpu.SemaphoreType.DMA((PLE_RING, 2)),
            ],
        ),
        out_shape=jax.ShapeDtypeStruct((n, d), _F32),
        compiler_params=pltpu.CompilerParams(
            dimension_semantics=("arbitrary",), vmem_limit_bytes=VMEM_LIMIT_BYTES),
        name="ple",
    )
    out = ple(meta, rank, x1.reshape(n, d), p[0].reshape(n, d_ple), pos, ys,
              w_pg[0], _row(b_pg[0]), w_pp[0], _row(ln2_g[0]), _row(ln2_b[0]))
    return out.reshape(bsz, seq, d)
```

```python
import functools

import jax
import jax.numpy as jnp
from jax import lax
from jax.experimental import pallas as pl
from jax.experimental.pallas import tpu as pltpu

LN_EPS = 1e-5
N_GROUPS = 4
EXPERTS_PER_GROUP = 4
SUBLANES = 8
LANES = 128
ROUTER_ROWS = 32
TILE = 512
MOE_CHUNKS = 2
PLE_TILES = 2
N_SUB = 2
VMEM_LIMIT_BYTES = 56 * 1024 * 1024
NEG_BIG = -1e30
META_FIELDS = 5

_F32 = jnp.float32
_BF16 = jnp.bfloat16
_NT = (((1,), (1,)), ((), ()))
_TN = (((0,), (0,)), ((), ()))


def _layer_norm(x, g, b):
    mu = jnp.mean(x, axis=-1, keepdims=True)
    xc = x - mu
    var = jnp.mean(xc * xc, axis=-1, keepdims=True)
    return xc * lax.rsqrt(var + LN_EPS) * g + b


def _sigmoid(x):
    return 0.5 * jnp.tanh(0.5 * x) + 0.5


def _dot(a, b):
    return jnp.dot(a, b, preferred_element_type=_F32)


def _aligned(v):
    return v if isinstance(v, int) else pl.multiple_of(v, SUBLANES)


def _rows(src, s0, dst, d0, m, sem):
    return pltpu.make_async_copy(src.at[pl.ds(_aligned(s0), _aligned(m))],
                                 dst.at[pl.ds(_aligned(d0), _aligned(m))], sem)


def _route(logits):
    row = lax.broadcasted_iota(jnp.int32, (SUBLANES, logits.shape[1]), 0)
    real = row < N_GROUPS
    gl = jnp.where(real, logits[0:SUBLANES], NEG_BIG)
    gm = jnp.max(gl, axis=0, keepdims=True)
    ge = jnp.exp(gl - gm)
    gp = ge / jnp.sum(ge, axis=0, keepdims=True)
    gp_top = jnp.max(gp, axis=0, keepdims=True)
    gidx = jnp.min(jnp.where(gp == gp_top, row, SUBLANES), axis=0, keepdims=True)

    el = jnp.zeros_like(gl)
    for g in range(N_GROUPS):
        lo = N_GROUPS + g * EXPERTS_PER_GROUP
        blk = logits[lo - lo % SUBLANES:lo - lo % SUBLANES + SUBLANES]
        if lo % SUBLANES:
            blk = pltpu.roll(blk, SUBLANES - lo % SUBLANES, 0)
        el = jnp.where(gidx == g, blk, el)
    el = jnp.where(real, el, NEG_BIG)
    em = jnp.max(el, axis=0, keepdims=True)
    ee = jnp.exp(el - em)
    ep = ee / jnp.sum(ee, axis=0, keepdims=True)
    ep = jnp.where(real, ep, -1.0)
    p1 = jnp.max(ep, axis=0, keepdims=True)
    i1 = jnp.min(jnp.where(ep == p1, row, SUBLANES), axis=0, keepdims=True)
    ep2 = jnp.where(row == i1, -1.0, ep)
    p2 = jnp.max(ep2, axis=0, keepdims=True)
    i2 = jnp.min(jnp.where(ep2 == p2, row, SUBLANES), axis=0, keepdims=True)
    denom = p1 + p2
    w = jnp.where(row == i1, p1 / denom, jnp.where(row == i2, p2 / denom, 0.0))
    return gidx, w * gp_top


def _mixer_kernel(x_ref, lng_ref, lnb_ref, w_in_ref, b_in_ref, cw_ref, cb_ref, clg_ref, clb_ref,
                  sw_ref, sb_ref, w_out_ref, b_out_ref, l1g_ref, l1b_ref, wr_ref, brg_ref, bre_ref,
                  x1_ref, pos_ref, xs_ref, meta_ref, info_ref, order_ref, rank_ref,
                  abuf, ashift, ubuf, w_in_st, w_out_st, w_in_bf, w_out_bf, wr_bf, br_ref, tri_ref, stage, zbuf,
                  cur_ref, fill_ref, nfree_ref, ring_ref, sem, zsem, wsem,
                  *, alpha, tile, t_pad, chunk, n_chunks, n_tiles, conf_w, sc_w_,
                  conf_k, sc_k, a_halo, u_halo, n_logits):
    j = pl.program_id(1)
    k = pl.program_id(0) * pl.num_programs(1) + j
    slot = k % 2
    d = x1_ref.shape[-1]
    c0, c1, c2, c3 = conf_w, 2 * conf_w, 2 * conf_w + sc_w_, 2 * conf_w + 2 * sc_w_
    w_parts = (
        (w_in_ref.at[:, 0:c1], w_in_st.at[:, 0:c1], w_in_bf.at[:, 0:c1]),
        (w_in_ref.at[:, c1:], w_in_st.at[:, c1:], w_in_bf.at[:, c1:]),
        (w_out_ref, w_out_st, w_out_bf),
    )
    w_copies = [pltpu.make_async_copy(src, st, wsem.at[n]) for n, (src, st, _) in enumerate(w_parts)]

    def weights_ready(n):
        w_copies[n].wait()
        w_parts[n][2][...] = w_parts[n][1][...].astype(_BF16)

    spare = [pltpu.make_async_copy(zbuf, xs_ref.at[pl.ds(c * chunk, chunk)], zsem)
             for c in range(n_tiles * tile // chunk, n_chunks)]

    @pl.when(k == 0)
    def _():
        for cp in w_copies:
            cp.start()
        zbuf[...] = jnp.zeros_like(zbuf)
        wr_bf[...] = jnp.zeros_like(wr_bf)
        wr_bf[0:n_logits, :] = wr_ref[...].astype(_BF16)
        def column(b_ref, first_row):
            shape = (ROUTER_ROWS, b_ref.shape[1])
            on_diag = (lax.broadcasted_iota(jnp.int32, shape, 0)
                       == lax.broadcasted_iota(jnp.int32, shape, 1) + first_row)
            return jnp.sum(jnp.where(on_diag, b_ref[...], 0.0), axis=1, keepdims=True)

        br_ref[...] = column(brg_ref, 0) + column(bre_ref, N_GROUPS)
        earlier = lax.broadcasted_iota(jnp.int32, (tile, tile), 0) < lax.broadcasted_iota(jnp.int32, (tile, tile), 1)
        tri_ref[...] = jnp.where(earlier, 1.0, 0.0).astype(_BF16)
        nfree_ref[0] = 0
        for g in range(N_GROUPS):
            cur_ref[g] = 0
            fill_ref[g] = chunk
        for c in range(n_chunks + 1):
            info_ref[c] = N_GROUPS - 1
        for c in range(n_chunks):
            info_ref[n_chunks + 1 + c] = chunk
        weights_ready(0)

    @pl.when(k == 1)
    def _():
        for cp in spare:
            cp.start()

    @pl.when(k == 2)
    def _():
        for cp in spare:
            cp.wait()

    @pl.when(j == 0)
    def _():
        abuf[0:a_halo, :] = jnp.zeros((a_halo, conf_w), _F32)
        ubuf[0:u_halo, :] = jnp.zeros((u_halo, sc_w_), _F32)

    x0 = _layer_norm(x_ref[0], lng_ref[...], lnb_ref[...])
    x0b = x0.astype(_BF16)
    h = _dot(x0b, w_in_bf[:, 0:c1]) + b_in_ref[:, 0:c1]

    a = h[:, 0:c0] * _sigmoid(h[:, c0:c1])
    abuf[a_halo:a_halo + tile, :] = a
    for i in range(1, SUBLANES):
        ashift[i - 1] = abuf[i:i + tile + a_halo - SUBLANES, :]
    acc = jnp.broadcast_to(cb_ref[...], (tile, conf_w))
    for t in range(conf_k):
        q, i = divmod(a_halo - (conf_k - 1) + t, SUBLANES)
        if i == 0:
            src = abuf[SUBLANES * q:SUBLANES * q + tile, :]
        else:
            src = ashift[i - 1, SUBLANES * q:SUBLANES * q + tile, :]
        acc = acc + cw_ref[t:t + 1, :] * src
    abuf[0:a_halo, :] = abuf[tile:tile + a_halo, :]

    @pl.when(k == 0)
    def _():
        weights_ready(1)
        weights_ready(2)

    h2 = _dot(x0b, w_in_bf[:, c1:]) + b_in_ref[:, c1:]
    an = _layer_norm(acc, clg_ref[...], clb_ref[...])
    a2 = an * _sigmoid(an)

    u = h2[:, c2 - c1:c3 - c1] * h2[:, c3 - c1:]
    ubuf[u_halo:u_halo + tile, :] = u
    sc = jnp.broadcast_to(sb_ref[...], (tile, sc_w_))
    for t in range(sc_k):
        off = u_halo - (sc_k - 1) + t
        sc = sc + sw_ref[t:t + 1, :] * ubuf[off:off + tile, :]
    ubuf[0:u_halo, :] = ubuf[tile:tile + u_halo, :]
    s = h2[:, 0:c2 - c1] * sc

    mix = (_dot(a2.astype(_BF16), w_out_bf[0:conf_w, :])
           + _dot(s.astype(_BF16), w_out_bf[conf_w:, :]) + b_out_ref[...])
    x1 = _layer_norm(alpha * x0 + mix, l1g_ref[...], l1b_ref[...])
    x1_ref[0] = x1
    x1b = x1.astype(_BF16)

    logits = lax.dot_general(wr_bf[...], x1b, _NT, preferred_element_type=_F32) + br_ref[...]
    gidx, w = _route(logits)

    row8 = lax.broadcasted_iota(jnp.int32, (SUBLANES, tile), 0)
    onehot = jnp.where(gidx == row8, 1.0, 0.0)
    cnt = jnp.sum(onehot, axis=1, keepdims=True)
    sizes, starts = [], []
    start = jnp.int32(0)
    startv = jnp.zeros((SUBLANES, tile), jnp.int32)
    for g in range(N_GROUPS):
        n8 = (cnt[g, 0].astype(jnp.int32) + (SUBLANES - 1)) & (-SUBLANES)
        sizes.append(n8)
        starts.append(start)
        startv = jnp.where(row8 == g, start, startv)
        start = start + n8
    cum = _dot(onehot.astype(_BF16), tri_ref[...])
    pos = jnp.sum(onehot * (startv.astype(_F32) + cum), axis=0, keepdims=True).astype(jnp.int32)
    pos_ref[0] = pos
    perm = jnp.where(lax.broadcasted_iota(jnp.int32, (t_pad, tile), 0) == pos, 1.0, 0.0).astype(_BF16)

    w_hi = w.astype(_BF16).astype(_F32)
    w_lo = (w - w_hi).astype(_BF16).astype(_F32)
    w_hl = jnp.where(row8 < EXPERTS_PER_GROUP, w_hi, pltpu.roll(w_lo, EXPERTS_PER_GROUP, 0))
    w_rows = jnp.concatenate([w_hl, jnp.zeros((LANES - SUBLANES, tile), _F32)], axis=0).astype(_BF16)

    def wait_slot(sl):
        for g in range(N_GROUPS):
            for piece in range(2):
                m = ring_ref[sl, 2 * g + piece]

                @pl.when(m > 0)
                def _():
                    _rows(stage.at[sl], 0, xs_ref, 0, m, sem.at[sl]).wait()

    @pl.when(k >= 2)
    def _():
        wait_slot(slot)

    stage[slot, :, 0:d] = _dot(perm, x1b)
    stage[slot, :, d:] = lax.dot_general(perm, w_rows, _NT, preferred_element_type=_F32)

    for g in range(N_GROUPS):
        n8, sg = sizes[g], starts[g]
        fill, cur, newc = fill_ref[g], cur_ref[g], nfree_ref[0]
        m1 = jnp.minimum(n8, chunk - fill)
        m2 = n8 - m1
        r1 = cur * chunk + fill
        r2 = newc * chunk
        ring_ref[slot, 2 * g] = m1
        ring_ref[slot, 2 * g + 1] = m2
        for f, v in enumerate((r1, m1, r2, m2, sg)):
            meta_ref[k, g * META_FIELDS + f] = v

        @pl.when(m1 > 0)
        def _():
            _rows(stage.at[slot], sg, xs_ref, r1, m1, sem.at[slot]).start()

        @pl.when(m2 > 0)
        def _():
            _rows(stage.at[slot], sg + m1, xs_ref, r2, m2, sem.at[slot]).start()
            info_ref[newc] = g
            cur_ref[g] = newc
            nfree_ref[0] = newc + 1

        fill_ref[g] = jnp.where(m2 > 0, m2, fill + m1)

    @pl.when(k == n_tiles - 1)
    def _():
        nfree = nfree_ref[0]
        info_ref[n_chunks] = nfree
        tails = []
        for g in range(N_GROUPS):
            rem = chunk - fill_ref[g]

            @pl.when(rem > 0)
            def _():
                info_ref[n_chunks + 1 + cur_ref[g]] = fill_ref[g]

            tails.append((rem, _rows(zbuf, 0, xs_ref, cur_ref[g] * chunk + fill_ref[g], rem, zsem)))
        for rem, cp in tails:
            pl.when(rem > 0)(cp.start)

        slot_out = jnp.int32(0)
        for g in range(N_GROUPS):
            def place(c, nxt, g=g):
                hit = info_ref[c] == g

                @pl.when(hit)
                def _():
                    order_ref[nxt] = c
                    rank_ref[c] = nxt

                return nxt + hit.astype(jnp.int32)

            slot_out = lax.fori_loop(0, n_chunks, place, slot_out)

        @pl.when(k >= 1)
        def _():
            wait_slot(1 - slot)

        wait_slot(slot)
        for rem, cp in tails:
            pl.when(rem > 0)(cp.wait)


def _moe_kernel(order_ref, info_ref, *refs, d, d_expert, n_chunks, per_step):
    xs_refs = refs[:per_step]
    wg_hbm, wu_hbm, wd_hbm, o_ref, wg_buf, wu_buf, wd_buf, slot_ref, wsem = refs[per_step:]
    chunk = xs_refs[0].shape[0]

    def weight_copies(g, sl):
        lo = g * EXPERTS_PER_GROUP
        return [pltpu.make_async_copy(src.at[pl.ds(lo, EXPERTS_PER_GROUP)], dst.at[sl], wsem.at[0])
                for src, dst in ((wg_hbm, wg_buf), (wu_hbm, wu_buf), (wd_hbm, wd_buf))]

    def first_copies(g, e):
        return [pltpu.make_async_copy(src.at[g * EXPERTS_PER_GROUP + e], dst.at[0, e], wsem.at[1 + 3 * e + n])
                for n, (src, dst) in enumerate(((wg_hbm, wg_buf), (wu_hbm, wu_buf), (wd_hbm, wd_buf)))]

    for ci, xs_ref in enumerate(xs_refs):
        i = pl.program_id(0) * per_step + ci
        rows = slice(ci * chunk, (ci + 1) * chunk)
        c = order_ref[i]
        grp = info_ref[c]
        prev = info_ref[order_ref[jnp.maximum(i - 1, 0)]]

        @pl.when(i == 0)
        def _():
            slot_ref[0] = 0
            for e in range(EXPERTS_PER_GROUP):
                for cp in first_copies(grp, e):
                    cp.start()

        @pl.when((i == 0) | (grp != prev))
        def _():
            @pl.when(i > 0)
            def _():
                slot_ref[0] = 1 - slot_ref[0]
                for cp in weight_copies(grp, slot_ref[0]):
                    cp.wait()

            cur = slot_ref[0]

            def later_group(j, found):
                gj = info_ref[order_ref[j]]
                return jnp.where((found < 0) & (j > i) & (gj != grp), gj, found)

            nxt = lax.fori_loop(0, n_chunks, later_group, jnp.int32(-1))

            @pl.when(nxt >= 0)
            def _():
                for cp in weight_copies(nxt, 1 - cur):
                    cp.start()

        used = c < info_ref[n_chunks]
        half = chunk // 2
        low_only = info_ref[n_chunks + 1 + c] <= half

        def experts(m, xs_ref=xs_ref, grp=grp, await_first=False):
            xb = xs_ref[0:m, 0:d]
            ws = slot_ref[0]
            aux = xs_ref[0:m, d:]
            y = jnp.zeros((m, o_ref.shape[1]), _F32)
            for e in range(EXPERTS_PER_GROUP):
                arrived = [cp.wait for cp in first_copies(grp, e)] if await_first else [lambda: None] * 3
                arrived[0]()
                hg = _dot(xb, wg_buf[ws, e])
                arrived[1]()
                hu = _dot(xb, wu_buf[ws, e])
                we = aux[:, e:e + 1] + aux[:, EXPERTS_PER_GROUP + e:EXPERTS_PER_GROUP + e + 1]
                hid = hg * _sigmoid(hg) * hu * we
                arrived[2]()
                y = y + _dot(hid, wd_buf[ws, e])
            return y

        later = used
        if ci == 0:
            later = used & (i > 0)

            @pl.when(i == 0)
            def _():
                o_ref[rows, :] = experts(chunk, await_first=True)

        @pl.when(later & jnp.logical_not(low_only))
        def _():
            o_ref[rows, :] = experts(chunk)

        @pl.when(later & low_only)
        def _():
            o_ref[ci * chunk:ci * chunk + half, :] = experts(half)
            o_ref[ci * chunk + half:(ci + 1) * chunk, :] = jnp.zeros((half, o_ref.shape[1]), _F32)

        @pl.when(jnp.logical_not(used))
        def _():
            o_ref[rows, :] = jnp.zeros((chunk, o_ref.shape[1]), _F32)


def _ple_kernel(meta_ref, rank_ref, x1_ref, p_ref, pos_ref, ys_ref, wpg_ref, bpg_ref, wpp_ref, l2g_ref, l2b_ref,
                o_ref, ybuf, sem, *, alpha, tile, chunk, n_sub, tiles_per_step, t_pad, n_steps):
    k = pl.program_id(0)
    slot = k % 2

    def moved(r):
        return rank_ref[r // chunk] * chunk + r % chunk

    def pieces(step, sl):
        out = []
        for ti in range(tiles_per_step):
            kk = step * tiles_per_step + ti
            for g in range(N_GROUPS):
                r1, m1, r2, m2, sg = (meta_ref[kk, g * META_FIELDS + f] for f in range(META_FIELDS))
                out.append((m1, _rows(ys_ref, moved(r1), ybuf.at[sl, ti], sg, m1, sem.at[sl])))
                out.append((m2, _rows(ys_ref, moved(r2), ybuf.at[sl, ti], sg + m1, m2, sem.at[sl])))
        return out

    def fetch(kk, sl):
        for m, cp in pieces(kk, sl):
            pl.when(m > 0)(cp.start)

    @pl.when(k == 0)
    def _():
        ybuf[:, :, tile:, :] = jnp.zeros((2, tiles_per_step, t_pad - tile, ybuf.shape[-1]), _F32)
        fetch(0, 0)

    @pl.when(k + 1 < n_steps)
    def _():
        fetch(k + 1, 1 - slot)

    for m, cp in pieces(k, slot):
        pl.when(m > 0)(cp.wait)

    sub = tile // n_sub
    for ti in range(tiles_per_step):
        yb = ybuf[slot, ti]
        pos = pos_ref[ti]
        for blk in range(n_sub):
            cols = slice(blk * sub, (blk + 1) * sub)
            rows = slice(ti * tile + blk * sub, ti * tile + (blk + 1) * sub)
            perm = jnp.where(lax.broadcasted_iota(jnp.int32, (t_pad, sub), 0) == pos[:, cols], 1.0, 0.0)
            y = lax.dot_general(perm, yb, _TN, preferred_element_type=_F32)
            r = alpha * x1_ref[rows, :] + y
            gate = _sigmoid(_dot(r, wpg_ref[...]) + bpg_ref[...])
            pp = _dot(p_ref[rows, :], wpp_ref[...])
            o_ref[rows, :] = _layer_norm(r + gate * pp, l2g_ref[...], l2b_ref[...])


def _full(shape):
    return pl.BlockSpec(shape, lambda *_: (0,) * len(shape))


def _resident(shape):
    return pl.BlockSpec(shape, lambda *_: (0,) * len(shape), pipeline_mode=pl.Buffered(1))


def _row(v):
    return v.reshape(1, -1).astype(_F32)


def kernel(x, p, ln_in_g, ln_in_b, w_in, b_in, conf_dw_w, conf_dw_b, conf_ln_g, conf_ln_b, sc_w, sc_b, w_out, b_out, ln1_g, ln1_b, w_rg, b_rg, w_re, b_re, w_gate, w_up, w_down, w_pg, b_pg, w_pp, ln2_g, ln2_b):
    depth = w_in.shape[0]
    assert depth == 1, "single-layer block"
    alpha = (2.0 * depth) ** 0.25
    bsz, seq, d = x.shape
    n = bsz * seq
    conf_k, conf_w = conf_dw_w.shape[1:]
    sc_k, sc_w_ = sc_w.shape[1:]
    d_in = w_in.shape[2]
    n_exp, _, d_expert = w_gate.shape[1:]
    assert n_exp == N_GROUPS * EXPERTS_PER_GROUP
    assert d_in == 2 * conf_w + 3 * sc_w_ and w_out.shape[1] == conf_w + sc_w_
    d_ple = p.shape[-1]

    tile = chunk = TILE
    assert seq % tile == 0 and (bsz * (seq // tile)) % PLE_TILES == 0
    assert bsz * (seq // tile) >= 3 * N_GROUPS, "spare chunks are zero-filled during the first three steps"
    nt = seq // tile
    n_tiles = bsz * nt
    a_halo = -(-(conf_k - 1) // SUBLANES) * SUBLANES
    u_halo = -(-(sc_k - 1) // SUBLANES) * SUBLANES
    t_pad = tile + N_GROUPS * SUBLANES
    n_chunks = -(-(n + (SUBLANES - 1) * N_GROUPS * n_tiles) // chunk) + N_GROUPS
    n_chunks = -(-n_chunks // MOE_CHUNKS) * MOE_CHUNKS
    d_pay = d + LANES

    n_logits = N_GROUPS + n_exp
    wr = jnp.concatenate([w_rg[0], w_re[0]], axis=1).T

    mixer = pl.pallas_call(
        functools.partial(_mixer_kernel, alpha=alpha, tile=tile, t_pad=t_pad, chunk=chunk,
                          n_chunks=n_chunks, n_tiles=n_tiles, conf_w=conf_w, sc_w_=sc_w_,
                          conf_k=conf_k, sc_k=sc_k, a_halo=a_halo, u_halo=u_halo, n_logits=n_logits),
        grid=(bsz, nt),
        in_specs=[
            pl.BlockSpec((1, tile, d), lambda b, j: (b, j, 0)),
            _full((1, d)), _full((1, d)),
            pl.BlockSpec(memory_space=pl.ANY), _full((1, d_in)),
            _full((conf_k, conf_w)), _full((1, conf_w)), _full((1, conf_w)), _full((1, conf_w)),
            _full((sc_k, sc_w_)), _full((1, sc_w_)),
            pl.BlockSpec(memory_space=pl.ANY), _full((1, d)),
            _full((1, d)), _full((1, d)),
            _full((n_logits, d)), _full((1, N_GROUPS)), _full((1, n_exp)),
        ],
        out_specs=[
            pl.BlockSpec((1, tile, d), lambda b, j: (b, j, 0)),
            pl.BlockSpec((1, 1, tile), lambda b, j: (b * nt + j, 0, 0)),
            pl.BlockSpec(memory_space=pl.ANY),
            pl.BlockSpec(memory_space=pltpu.SMEM),
            pl.BlockSpec(memory_space=pltpu.SMEM),
            pl.BlockSpec(memory_space=pltpu.SMEM),
            pl.BlockSpec(memory_space=pltpu.SMEM),
        ],
        out_shape=[
            jax.ShapeDtypeStruct((bsz, seq, d), _F32),
            jax.ShapeDtypeStruct((n_tiles, 1, tile), jnp.int32),
            jax.ShapeDtypeStruct((n_chunks * chunk, d_pay), _F32),
            jax.ShapeDtypeStruct((n_tiles, N_GROUPS * META_FIELDS), jnp.int32),
            jax.ShapeDtypeStruct((2 * n_chunks + 1,), jnp.int32),
            jax.ShapeDtypeStruct((n_chunks,), jnp.int32),
            jax.ShapeDtypeStruct((n_chunks,), jnp.int32),
        ],
        scratch_shapes=[
            pltpu.VMEM((a_halo + tile, conf_w), _F32),
            pltpu.VMEM((SUBLANES - 1, a_halo + tile - SUBLANES, conf_w), _F32),
            pltpu.VMEM((u_halo + tile, sc_w_), _F32),
            pltpu.VMEM((d, d_in), _F32),
            pltpu.VMEM((conf_w + sc_w_, d), _F32),
            pltpu.VMEM((d, d_in), _BF16),
            pltpu.VMEM((conf_w + sc_w_, d), _BF16),
            pltpu.VMEM((ROUTER_ROWS, d), _BF16),
            pltpu.VMEM((ROUTER_ROWS, 1), _F32),
            pltpu.VMEM((tile, tile), _BF16),
            pltpu.VMEM((2, t_pad, d_pay), _F32),
            pltpu.VMEM((chunk, d_pay), _F32),
            pltpu.SMEM((N_GROUPS,), jnp.int32),
            pltpu.SMEM((N_GROUPS,), jnp.int32),
            pltpu.SMEM((1,), jnp.int32),
            pltpu.SMEM((2, 2 * N_GROUPS), jnp.int32),
            pltpu.SemaphoreType.DMA((2,)),
            pltpu.SemaphoreType.DMA(()),
            pltpu.SemaphoreType.DMA((3,)),
        ],
        compiler_params=pltpu.CompilerParams(
            dimension_semantics=("arbitrary", "arbitrary"), vmem_limit_bytes=VMEM_LIMIT_BYTES),
        name="mixer",
    )
    x1, pos, xs, meta, info, order, rank = mixer(
        x, _row(ln_in_g), _row(ln_in_b), w_in[0], _row(b_in[0]),
        conf_dw_w[0], _row(conf_dw_b[0]), _row(conf_ln_g[0]), _row(conf_ln_b[0]),
        sc_w[0], _row(sc_b[0]), w_out[0], _row(b_out[0]),
        _row(ln1_g[0]), _row(ln1_b[0]), wr, _row(b_rg[0]), _row(b_re[0]))

    moe = pl.pallas_call(
        functools.partial(_moe_kernel, d=d, d_expert=d_expert, n_chunks=n_chunks, per_step=MOE_CHUNKS),
        grid_spec=pltpu.PrefetchScalarGridSpec(
            num_scalar_prefetch=2,
            grid=(n_chunks // MOE_CHUNKS,),
            in_specs=[
                pl.BlockSpec((chunk, d_pay), functools.partial(
                    lambda i, order, info, ci: (order[i * MOE_CHUNKS + ci], 0), ci=ci))
                for ci in range(MOE_CHUNKS)
            ] + [
                pl.BlockSpec(memory_space=pl.ANY),
                pl.BlockSpec(memory_space=pl.ANY),
                pl.BlockSpec(memory_space=pl.ANY),
            ],
            out_specs=pl.BlockSpec((MOE_CHUNKS * chunk, d), lambda i, order, info: (i, 0)),
            scratch_shapes=[
                pltpu.VMEM((2, EXPERTS_PER_GROUP, d, d_expert), _F32),
                pltpu.VMEM((2, EXPERTS_PER_GROUP, d, d_expert), _F32),
                pltpu.VMEM((2, EXPERTS_PER_GROUP, d_expert, d), _F32),
                pltpu.SMEM((1,), jnp.int32),
                pltpu.SemaphoreType.DMA((1 + 3 * EXPERTS_PER_GROUP,)),
            ],
        ),
        out_shape=jax.ShapeDtypeStruct((n_chunks * chunk, d), _F32),
        compiler_params=pltpu.CompilerParams(
            dimension_semantics=("arbitrary",), vmem_limit_bytes=VMEM_LIMIT_BYTES),
        name="moe",
    )
    ys = moe(order, info, *([xs] * MOE_CHUNKS), w_gate[0], w_up[0], w_down[0])

    ple = pl.pallas_call(
        functools.partial(_ple_kernel, alpha=alpha, tile=tile, chunk=chunk, n_sub=N_SUB, tiles_per_step=PLE_TILES,
                          t_pad=t_pad, n_steps=n_tiles // PLE_TILES),
        grid_spec=pltpu.PrefetchScalarGridSpec(
            num_scalar_prefetch=2,
            grid=(n_tiles // PLE_TILES,),
            in_specs=[
                pl.BlockSpec((PLE_TILES * tile, d), lambda k, meta, rank: (k, 0)),
                pl.BlockSpec((PLE_TILES * tile, d_ple), lambda k, meta, rank: (k, 0)),
                pl.BlockSpec((PLE_TILES, 1, tile), lambda k, meta, rank: (k, 0, 0)),
                pl.BlockSpec(memory_space=pl.ANY),
                _resident((d, d)), _full((1, d)), _resident((d_ple, d)), _full((1, d)), _full((1, d)),
            ],
            out_specs=pl.BlockSpec((PLE_TILES * tile, d), lambda k, meta, rank: (k, 0)),
            scratch_shapes=[
                pltpu.VMEM((2, PLE_TILES, t_pad, d), _F32),
                pltpu.SemaphoreType.DMA((2,)),
            ],
        ),
        out_shape=jax.ShapeDtypeStruct((n, d), _F32),
        compiler_params=pltpu.CompilerParams(
            dimension_semantics=("arbitrary",), vmem_limit_bytes=VMEM_LIMIT_BYTES),
        name="ple",
    )
    out = ple(meta, rank, x1.reshape(n, d), p[0].reshape(n, d_ple), pos, ys,
              w_pg[0], _row(b_pg[0]), w_pp[0], _row(ln2_g[0]), _row(ln2_b[0]))
    return out.reshape(bsz, seq, d)
```

```python
import functools

import jax
import jax.numpy as jnp
from jax import lax
from jax.experimental import pallas as pl
from jax.experimental.pallas import tpu as pltpu

LN_EPS = 1e-5
N_GROUPS = 4
EXPERTS_PER_GROUP = 4
SUBLANES = 8
LANES = 128
ROUTER_ROWS = 32
TILE = 512
MOE_CHUNKS = 2
PLE_TILES = 2
N_SUB = 2
VMEM_LIMIT_BYTES = 56 * 1024 * 1024
NEG_BIG = -1e30
META_FIELDS = 5

_F32 = jnp.float32
_BF16 = jnp.bfloat16
_NT = (((1,), (1,)), ((), ()))
_TN = (((0,), (0,)), ((), ()))


def _layer_norm(x, g, b):
    mu = jnp.mean(x, axis=-1, keepdims=True)
    xc = x - mu
    var = jnp.mean(xc * xc, axis=-1, keepdims=True)
    return xc * lax.rsqrt(var + LN_EPS) * g + b


def _sigmoid(x):
    return 0.5 * jnp.tanh(0.5 * x) + 0.5


def _dot(a, b):
    return jnp.dot(a, b, preferred_element_type=_F32)


def _aligned(v):
    return v if isinstance(v, int) else pl.multiple_of(v, SUBLANES)


def _rows(src, s0, dst, d0, m, sem):
    return pltpu.make_async_copy(src.at[pl.ds(_aligned(s0), _aligned(m))],
                                 dst.at[pl.ds(_aligned(d0), _aligned(m))], sem)


def _route(logits):
    row = lax.broadcasted_iota(jnp.int32, (SUBLANES, logits.shape[1]), 0)
    real = row < N_GROUPS
    gl = jnp.where(real, logits[0:SUBLANES], NEG_BIG)
    gm = jnp.max(gl, axis=0, keepdims=True)
    ge = jnp.exp(gl - gm)
    gp = ge / jnp.sum(ge, axis=0, keepdims=True)
    gp_top = jnp.max(gp, axis=0, keepdims=True)
    gidx = jnp.min(jnp.where(gp == gp_top, row, SUBLANES), axis=0, keepdims=True)

    el = jnp.zeros_like(gl)
    for g in range(N_GROUPS):
        lo = N_GROUPS + g * EXPERTS_PER_GROUP
        blk = logits[lo - lo % SUBLANES:lo - lo % SUBLANES + SUBLANES]
        if lo % SUBLANES:
            blk = pltpu.roll(blk, SUBLANES - lo % SUBLANES, 0)
        el = jnp.where(gidx == g, blk, el)
    el = jnp.where(real, el, NEG_BIG)
    em = jnp.max(el, axis=0, keepdims=True)
    ee = jnp.exp(el - em)
    ep = ee / jnp.sum(ee, axis=0, keepdims=True)
    ep = jnp.where(real, ep, -1.0)
    p1 = jnp.max(ep, axis=0, keepdims=True)
    i1 = jnp.min(jnp.where(ep == p1, row, SUBLANES), axis=0, keepdims=True)
    ep2 = jnp.where(row == i1, -1.0, ep)
    p2 = jnp.max(ep2, axis=0, keepdims=True)
    i2 = jnp.min(jnp.where(ep2 == p2, row, SUBLANES), axis=0, keepdims=True)
    denom = p1 + p2
    w = jnp.where(row == i1, p1 / denom, jnp.where(row == i2, p2 / denom, 0.0))
    return gidx, w * gp_top


def _mixer_kernel(x_ref, lng_ref, lnb_ref, w_in_ref, b_in_ref, cw_ref, cb_ref, clg_ref, clb_ref,
                  sw_ref, sb_ref, w_out_ref, b_out_ref, l1g_ref, l1b_ref, wr_ref, brg_ref, bre_ref,
                  x1_ref, pos_ref, xs_ref, meta_ref, info_ref, order_ref, rank_ref,
                  abuf, ashift, ubuf, w_in_st, w_out_st, w_in_bf, w_out_bf, wr_bf, br_ref, tri_ref, stage, zbuf,
                  cur_ref, fill_ref, nfree_ref, ring_ref, sem, zsem, wsem,
                  *, alpha, tile, t_pad, chunk, n_chunks, n_tiles, conf_w, sc_w_,
                  conf_k, sc_k, a_halo, u_halo, n_logits):
    j = pl.program_id(1)
    k = pl.program_id(0) * pl.num_programs(1) + j
    slot = k % 2
    d = x1_ref.shape[-1]
    c0, c1, c2, c3 = conf_w, 2 * conf_w, 2 * conf_w + sc_w_, 2 * conf_w + 2 * sc_w_
    w_parts = (
        (w_in_ref.at[:, 0:c1], w_in_st.at[:, 0:c1], w_in_bf.at[:, 0:c1]),
        (w_in_ref.at[:, c1:], w_in_st.at[:, c1:], w_in_bf.at[:, c1:]),
        (w_out_ref, w_out_st, w_out_bf),
    )
    w_copies = [pltpu.make_async_copy(src, st, wsem.at[n]) for n, (src, st, _) in enumerate(w_parts)]

    def weights_ready(n):
        w_copies[n].wait()
        w_parts[n][2][...] = w_parts[n][1][...].astype(_BF16)

    spare = [pltpu.make_async_copy(zbuf, xs_ref.at[pl.ds(c * chunk, chunk)], zsem)
             for c in range(n_tiles * tile // chunk, n_chunks)]

    @pl.when(k == 0)
    def _():
        for cp in w_copies:
            cp.start()
        zbuf[...] = jnp.zeros_like(zbuf)
        weights_ready(0)
        wr_bf[...] = jnp.zeros_like(wr_bf)
        wr_bf[0:n_logits, :] = wr_ref[...].astype(_BF16)
        def column(b_ref, first_row):
            shape = (ROUTER_ROWS, b_ref.shape[1])
            on_diag = (lax.broadcasted_iota(jnp.int32, shape, 0)
                       == lax.broadcasted_iota(jnp.int32, shape, 1) + first_row)
            return jnp.sum(jnp.where(on_diag, b_ref[...], 0.0), axis=1, keepdims=True)

        br_ref[...] = column(brg_ref, 0) + column(bre_ref, N_GROUPS)
        earlier = lax.broadcasted_iota(jnp.int32, (tile, tile), 0) < lax.broadcasted_iota(jnp.int32, (tile, tile), 1)
        tri_ref[...] = jnp.where(earlier, 1.0, 0.0).astype(_BF16)
        nfree_ref[0] = 0
        for g in range(N_GROUPS):
            cur_ref[g] = 0
            fill_ref[g] = chunk
        for c in range(n_chunks + 1):
            info_ref[c] = N_GROUPS - 1
        for c in range(n_chunks):
            info_ref[n_chunks + 1 + c] = chunk

    @pl.when(k == 1)
    def _():
        for cp in spare:
            cp.start()

    @pl.when(k == 2)
    def _():
        for cp in spare:
            cp.wait()

    @pl.when(j == 0)
    def _():
        abuf[0:a_halo, :] = jnp.zeros((a_halo, conf_w), _F32)
        ubuf[0:u_halo, :] = jnp.zeros((u_halo, sc_w_), _F32)

    x0 = _layer_norm(x_ref[0], lng_ref[...], lnb_ref[...])
    x0b = x0.astype(_BF16)
    h = _dot(x0b, w_in_bf[:, 0:c1]) + b_in_ref[:, 0:c1]

    a = h[:, 0:c0] * _sigmoid(h[:, c0:c1])
    abuf[a_halo:a_halo + tile, :] = a
    for i in range(1, SUBLANES):
        ashift[i - 1] = abuf[i:i + tile + a_halo - SUBLANES, :]
    acc = jnp.broadcast_to(cb_ref[...], (tile, conf_w))
    for t in range(conf_k):
        q, i = divmod(a_halo - (conf_k - 1) + t, SUBLANES)
        if i == 0:
            src = abuf[SUBLANES * q:SUBLANES * q + tile, :]
        else:
            src = ashift[i - 1, SUBLANES * q:SUBLANES * q + tile, :]
        acc = acc + cw_ref[t:t + 1, :] * src
    abuf[0:a_halo, :] = abuf[tile:tile + a_halo, :]

    @pl.when(k == 0)
    def _():
        weights_ready(1)
        weights_ready(2)

    h2 = _dot(x0b, w_in_bf[:, c1:]) + b_in_ref[:, c1:]
    an = _layer_norm(acc, clg_ref[...], clb_ref[...])
    a2 = an * _sigmoid(an)

    u = h2[:, c2 - c1:c3 - c1] * h2[:, c3 - c1:]
    ubuf[u_halo:u_halo + tile, :] = u
    sc = jnp.broadcast_to(sb_ref[...], (tile, sc_w_))
    for t in range(sc_k):
        off = u_halo - (sc_k - 1) + t
        sc = sc + sw_ref[t:t + 1, :] * ubuf[off:off + tile, :]
    ubuf[0:u_halo, :] = ubuf[tile:tile + u_halo, :]
    s = h2[:, 0:c2 - c1] * sc

    mix = (_dot(a2.astype(_BF16), w_out_bf[0:conf_w, :])
           + _dot(s.astype(_BF16), w_out_bf[conf_w:, :]) + b_out_ref[...])
    x1 = _layer_norm(alpha * x0 + mix, l1g_ref[...], l1b_ref[...])
    x1_ref[0] = x1
    x1b = x1.astype(_BF16)

    logits = lax.dot_general(wr_bf[...], x1b, _NT, preferred_element_type=_F32) + br_ref[...]
    gidx, w = _route(logits)

    row8 = lax.broadcasted_iota(jnp.int32, (SUBLANES, tile), 0)
    onehot = jnp.where(gidx == row8, 1.0, 0.0)
    cnt = jnp.sum(onehot, axis=1, keepdims=True)
    sizes, starts = [], []
    start = jnp.int32(0)
    startv = jnp.zeros((SUBLANES, tile), jnp.int32)
    for g in range(N_GROUPS):
        n8 = (cnt[g, 0].astype(jnp.int32) + (SUBLANES - 1)) & (-SUBLANES)
        sizes.append(n8)
        starts.append(start)
        startv = jnp.where(row8 == g, start, startv)
        start = start + n8
    cum = _dot(onehot.astype(_BF16), tri_ref[...])
    pos = jnp.sum(onehot * (startv.astype(_F32) + cum), axis=0, keepdims=True).astype(jnp.int32)
    pos_ref[0] = pos
    perm = jnp.where(lax.broadcasted_iota(jnp.int32, (t_pad, tile), 0) == pos, 1.0, 0.0).astype(_BF16)

    w_hi = w.astype(_BF16).astype(_F32)
    w_lo = (w - w_hi).astype(_BF16).astype(_F32)
    w_hl = jnp.where(row8 < EXPERTS_PER_GROUP, w_hi, pltpu.roll(w_lo, EXPERTS_PER_GROUP, 0))
    w_rows = jnp.concatenate([w_hl, jnp.zeros((LANES - SUBLANES, tile), _F32)], axis=0).astype(_BF16)

    def wait_slot(sl):
        for g in range(N_GROUPS):
            for piece in range(2):
                m = ring_ref[sl, 2 * g + piece]

                @pl.when(m > 0)
                def _():
                    _rows(stage.at[sl], 0, xs_ref, 0, m, sem.at[sl]).wait()

    @pl.when(k >= 2)
    def _():
        wait_slot(slot)

    stage[slot, :, 0:d] = _dot(perm, x1b)
    stage[slot, :, d:] = lax.dot_general(perm, w_rows, _NT, preferred_element_type=_F32)

    for g in range(N_GROUPS):
        n8, sg = sizes[g], starts[g]
        fill, cur, newc = fill_ref[g], cur_ref[g], nfree_ref[0]
        m1 = jnp.minimum(n8, chunk - fill)
        m2 = n8 - m1
        r1 = cur * chunk + fill
        r2 = newc * chunk
        ring_ref[slot, 2 * g] = m1
        ring_ref[slot, 2 * g + 1] = m2
        for f, v in enumerate((r1, m1, r2, m2, sg)):
            meta_ref[k, g * META_FIELDS + f] = v

        @pl.when(m1 > 0)
        def _():
            _rows(stage.at[slot], sg, xs_ref, r1, m1, sem.at[slot]).start()

        @pl.when(m2 > 0)
        def _():
            _rows(stage.at[slot], sg + m1, xs_ref, r2, m2, sem.at[slot]).start()
            info_ref[newc] = g
            cur_ref[g] = newc
            nfree_ref[0] = newc + 1

        fill_ref[g] = jnp.where(m2 > 0, m2, fill + m1)

    @pl.when(k == n_tiles - 1)
    def _():
        nfree = nfree_ref[0]
        info_ref[n_chunks] = nfree
        tails = []
        for g in range(N_GROUPS):
            rem = chunk - fill_ref[g]

            @pl.when(rem > 0)
            def _():
                info_ref[n_chunks + 1 + cur_ref[g]] = fill_ref[g]

            tails.append((rem, _rows(zbuf, 0, xs_ref, cur_ref[g] * chunk + fill_ref[g], rem, zsem)))
        for rem, cp in tails:
            pl.when(rem > 0)(cp.start)

        slot_out = jnp.int32(0)
        for g in range(N_GROUPS):
            def place(c, nxt, g=g):
                hit = info_ref[c] == g

                @pl.when(hit)
                def _():
                    order_ref[nxt] = c
                    rank_ref[c] = nxt

                return nxt + hit.astype(jnp.int32)

            slot_out = lax.fori_loop(0, n_chunks, place, slot_out)

        @pl.when(k >= 1)
        def _():
            wait_slot(1 - slot)

        wait_slot(slot)
        for rem, cp in tails:
            pl.when(rem > 0)(cp.wait)


def _moe_kernel(order_ref, info_ref, *refs, d, d_expert, n_chunks, per_step):
    xs_refs = refs[:per_step]
    wg_hbm, wu_hbm, wd_hbm, o_ref, wg_buf, wu_buf, wd_buf, slot_ref, wsem = refs[per_step:]
    chunk = xs_refs[0].shape[0]

    def weight_copies(g, sl):
        lo = g * EXPERTS_PER_GROUP
        return [pltpu.make_async_copy(src.at[pl.ds(lo, EXPERTS_PER_GROUP)], dst.at[sl], wsem.at[0])
                for src, dst in ((wg_hbm, wg_buf), (wu_hbm, wu_buf), (wd_hbm, wd_buf))]

    def first_copies(g, e):
        return [pltpu.make_async_copy(src.at[g * EXPERTS_PER_GROUP + e], dst.at[0, e], wsem.at[1 + e])
                for src, dst in ((wg_hbm, wg_buf), (wu_hbm, wu_buf), (wd_hbm, wd_buf))]

    for ci, xs_ref in enumerate(xs_refs):
        i = pl.program_id(0) * per_step + ci
        rows = slice(ci * chunk, (ci + 1) * chunk)
        c = order_ref[i]
        grp = info_ref[c]
        prev = info_ref[order_ref[jnp.maximum(i - 1, 0)]]

        @pl.when(i == 0)
        def _():
            slot_ref[0] = 0
            for e in range(EXPERTS_PER_GROUP):
                for cp in first_copies(grp, e):
                    cp.start()

        @pl.when((i == 0) | (grp != prev))
        def _():
            @pl.when(i > 0)
            def _():
                slot_ref[0] = 1 - slot_ref[0]
                for cp in weight_copies(grp, slot_ref[0]):
                    cp.wait()

            cur = slot_ref[0]

            def later_group(j, found):
                gj = info_ref[order_ref[j]]
                return jnp.where((found < 0) & (j > i) & (gj != grp), gj, found)

            nxt = lax.fori_loop(0, n_chunks, later_group, jnp.int32(-1))

            @pl.when(nxt >= 0)
            def _():
                for cp in weight_copies(nxt, 1 - cur):
                    cp.start()

        used = c < info_ref[n_chunks]
        half = chunk // 2
        low_only = info_ref[n_chunks + 1 + c] <= half

        def experts(m, xs_ref=xs_ref, grp=grp, await_first=False):
            xb = xs_ref[0:m, 0:d]
            ws = slot_ref[0]
            aux = xs_ref[0:m, d:]
            y = jnp.zeros((m, o_ref.shape[1]), _F32)
            for e in range(EXPERTS_PER_GROUP):
                if await_first:
                    for cp in first_copies(grp, e):
                        cp.wait()
                hg = _dot(xb, wg_buf[ws, e])
                hu = _dot(xb, wu_buf[ws, e])
                we = aux[:, e:e + 1] + aux[:, EXPERTS_PER_GROUP + e:EXPERTS_PER_GROUP + e + 1]
                hid = hg * _sigmoid(hg) * hu * we
                y = y + _dot(hid, wd_buf[ws, e])
            return y

        later = used
        if ci == 0:
            later = used & (i > 0)

            @pl.when(i == 0)
            def _():
                o_ref[rows, :] = experts(chunk, await_first=True)

        @pl.when(later & jnp.logical_not(low_only))
        def _():
            o_ref[rows, :] = experts(chunk)

        @pl.when(later & low_only)
        def _():
            o_ref[ci * chunk:ci * chunk + half, :] = experts(half)
            o_ref[ci * chunk + half:(ci + 1) * chunk, :] = jnp.zeros((half, o_ref.shape[1]), _F32)

        @pl.when(jnp.logical_not(used))
        def _():
            o_ref[rows, :] = jnp.zeros((chunk, o_ref.shape[1]), _F32)


def _ple_kernel(meta_ref, rank_ref, x1_ref, p_ref, pos_ref, ys_ref, wpg_ref, bpg_ref, wpp_ref, l2g_ref, l2b_ref,
                o_ref, ybuf, wpg_bf, wpp_bf, sem, *, alpha, tile, chunk, n_sub, tiles_per_step, t_pad, n_steps):
    k = pl.program_id(0)
    slot = k % 2

    def moved(r):
        return rank_ref[r // chunk] * chunk + r % chunk

    def pieces(step, sl):
        out = []
        for ti in range(tiles_per_step):
            kk = step * tiles_per_step + ti
            for g in range(N_GROUPS):
                r1, m1, r2, m2, sg = (meta_ref[kk, g * META_FIELDS + f] for f in range(META_FIELDS))
                out.append((m1, _rows(ys_ref, moved(r1), ybuf.at[sl, ti], sg, m1, sem.at[sl])))
                out.append((m2, _rows(ys_ref, moved(r2), ybuf.at[sl, ti], sg + m1, m2, sem.at[sl])))
        return out

    def fetch(kk, sl):
        for m, cp in pieces(kk, sl):
            pl.when(m > 0)(cp.start)

    @pl.when(k == 0)
    def _():
        ybuf[:, :, tile:, :] = jnp.zeros((2, tiles_per_step, t_pad - tile, ybuf.shape[-1]), _F32)
        fetch(0, 0)
        wpg_bf[...] = wpg_ref[...].astype(_BF16)
        wpp_bf[...] = wpp_ref[...].astype(_BF16)

    @pl.when(k + 1 < n_steps)
    def _():
        fetch(k + 1, 1 - slot)

    for m, cp in pieces(k, slot):
        pl.when(m > 0)(cp.wait)

    sub = tile // n_sub
    for ti in range(tiles_per_step):
        yb = ybuf[slot, ti]
        pos = pos_ref[ti]
        for blk in range(n_sub):
            cols = slice(blk * sub, (blk + 1) * sub)
            rows = slice(ti * tile + blk * sub, ti * tile + (blk + 1) * sub)
            perm = jnp.where(lax.broadcasted_iota(jnp.int32, (t_pad, sub), 0) == pos[:, cols], 1.0, 0.0)
            y = lax.dot_general(perm, yb, _TN, preferred_element_type=_F32)
            r = alpha * x1_ref[rows, :] + y
            gate = _sigmoid(_dot(r, wpg_bf[...]) + bpg_ref[...])
            pp = _dot(p_ref[rows, :], wpp_bf[...])
            o_ref[rows, :] = _layer_norm(r + gate * pp, l2g_ref[...], l2b_ref[...])


def _full(shape):
    return pl.BlockSpec(shape, lambda *_: (0,) * len(shape))


def _resident(shape):
    return pl.BlockSpec(shape, lambda *_: (0,) * len(shape), pipeline_mode=pl.Buffered(1))


def _row(v):
    return v.reshape(1, -1).astype(_F32)


def kernel(x, p, ln_in_g, ln_in_b, w_in, b_in, conf_dw_w, conf_dw_b, conf_ln_g, conf_ln_b, sc_w, sc_b, w_out, b_out, ln1_g, ln1_b, w_rg, b_rg, w_re, b_re, w_gate, w_up, w_down, w_pg, b_pg, w_pp, ln2_g, ln2_b):
    depth = w_in.shape[0]
    assert depth == 1, "single-layer block"
    alpha = (2.0 * depth) ** 0.25
    bsz, seq, d = x.shape
    n = bsz * seq
    conf_k, conf_w = conf_dw_w.shape[1:]
    sc_k, sc_w_ = sc_w.shape[1:]
    d_in = w_in.shape[2]
    n_exp, _, d_expert = w_gate.shape[1:]
    assert n_exp == N_GROUPS * EXPERTS_PER_GROUP
    assert d_in == 2 * conf_w + 3 * sc_w_ and w_out.shape[1] == conf_w + sc_w_
    d_ple = p.shape[-1]

    tile = chunk = TILE
    assert seq % tile == 0 and (bsz * (seq // tile)) % PLE_TILES == 0
    assert bsz * (seq // tile) >= 3 * N_GROUPS, "spare chunks are zero-filled during the first three steps"
    nt = seq // tile
    n_tiles = bsz * nt
    a_halo = -(-(conf_k - 1) // SUBLANES) * SUBLANES
    u_halo = -(-(sc_k - 1) // SUBLANES) * SUBLANES
    t_pad = tile + N_GROUPS * SUBLANES
    n_chunks = -(-(n + (SUBLANES - 1) * N_GROUPS * n_tiles) // chunk) + N_GROUPS
    n_chunks = -(-n_chunks // MOE_CHUNKS) * MOE_CHUNKS
    d_pay = d + LANES

    n_logits = N_GROUPS + n_exp
    wr = jnp.concatenate([w_rg[0], w_re[0]], axis=1).T

    mixer = pl.pallas_call(
        functools.partial(_mixer_kernel, alpha=alpha, tile=tile, t_pad=t_pad, chunk=chunk,
                          n_chunks=n_chunks, n_tiles=n_tiles, conf_w=conf_w, sc_w_=sc_w_,
                          conf_k=conf_k, sc_k=sc_k, a_halo=a_halo, u_halo=u_halo, n_logits=n_logits),
        grid=(bsz, nt),
        in_specs=[
            pl.BlockSpec((1, tile, d), lambda b, j: (b, j, 0)),
            _full((1, d)), _full((1, d)),
            pl.BlockSpec(memory_space=pl.ANY), _full((1, d_in)),
            _full((conf_k, conf_w)), _full((1, conf_w)), _full((1, conf_w)), _full((1, conf_w)),
            _full((sc_k, sc_w_)), _full((1, sc_w_)),
            pl.BlockSpec(memory_space=pl.ANY), _full((1, d)),
            _full((1, d)), _full((1, d)),
            _full((n_logits, d)), _full((1, N_GROUPS)), _full((1, n_exp)),
        ],
        out_specs=[
            pl.BlockSpec((1, tile, d), lambda b, j: (b, j, 0)),
            pl.BlockSpec((1, 1, tile), lambda b, j: (b * nt + j, 0, 0)),
            pl.BlockSpec(memory_space=pl.ANY),
            pl.BlockSpec(memory_space=pltpu.SMEM),
            pl.BlockSpec(memory_space=pltpu.SMEM),
            pl.BlockSpec(memory_space=pltpu.SMEM),
            pl.BlockSpec(memory_space=pltpu.SMEM),
        ],
        out_shape=[
            jax.ShapeDtypeStruct((bsz, seq, d), _F32),
            jax.ShapeDtypeStruct((n_tiles, 1, tile), jnp.int32),
            jax.ShapeDtypeStruct((n_chunks * chunk, d_pay), _F32),
            jax.ShapeDtypeStruct((n_tiles, N_GROUPS * META_FIELDS), jnp.int32),
            jax.ShapeDtypeStruct((2 * n_chunks + 1,), jnp.int32),
            jax.ShapeDtypeStruct((n_chunks,), jnp.int32),
            jax.ShapeDtypeStruct((n_chunks,), jnp.int32),
        ],
        scratch_shapes=[
            pltpu.VMEM((a_halo + tile, conf_w), _F32),
            pltpu.VMEM((SUBLANES - 1, a_halo + tile - SUBLANES, conf_w), _F32),
            pltpu.VMEM((u_halo + tile, sc_w_), _F32),
            pltpu.VMEM((d, d_in), _F32),
            pltpu.VMEM((conf_w + sc_w_, d), _F32),
            pltpu.VMEM((d, d_in), _BF16),
            pltpu.VMEM((conf_w + sc_w_, d), _BF16),
            pltpu.VMEM((ROUTER_ROWS, d), _BF16),
            pltpu.VMEM((ROUTER_ROWS, 1), _F32),
            pltpu.VMEM((tile, tile), _BF16),
            pltpu.VMEM((2, t_pad, d_pay), _F32),
            pltpu.VMEM((chunk, d_pay), _F32),
            pltpu.SMEM((N_GROUPS,), jnp.int32),
            pltpu.SMEM((N_GROUPS,), jnp.int32),
            pltpu.SMEM((1,), jnp.int32),
            pltpu.SMEM((2, 2 * N_GROUPS), jnp.int32),
            pltpu.SemaphoreType.DMA((2,)),
            pltpu.SemaphoreType.DMA(()),
            pltpu.SemaphoreType.DMA((3,)),
        ],
        compiler_params=pltpu.CompilerParams(
            dimension_semantics=("arbitrary", "arbitrary"), vmem_limit_bytes=VMEM_LIMIT_BYTES),
        name="mixer",
    )
    x1, pos, xs, meta, info, order, rank = mixer(
        x, _row(ln_in_g), _row(ln_in_b), w_in[0], _row(b_in[0]),
        conf_dw_w[0], _row(conf_dw_b[0]), _row(conf_ln_g[0]), _row(conf_ln_b[0]),
        sc_w[0], _row(sc_b[0]), w_out[0], _row(b_out[0]),
        _row(ln1_g[0]), _row(ln1_b[0]), wr, _row(b_rg[0]), _row(b_re[0]))

    moe = pl.pallas_call(
        functools.partial(_moe_kernel, d=d, d_expert=d_expert, n_chunks=n_chunks, per_step=MOE_CHUNKS),
        grid_spec=pltpu.PrefetchScalarGridSpec(
            num_scalar_prefetch=2,
            grid=(n_chunks // MOE_CHUNKS,),
            in_specs=[
                pl.BlockSpec((chunk, d_pay), functools.partial(
                    lambda i, order, info, ci: (order[i * MOE_CHUNKS + ci], 0), ci=ci))
                for ci in range(MOE_CHUNKS)
            ] + [
                pl.BlockSpec(memory_space=pl.ANY),
                pl.BlockSpec(memory_space=pl.ANY),
                pl.BlockSpec(memory_space=pl.ANY),
            ],
            out_specs=pl.BlockSpec((MOE_CHUNKS * chunk, d), lambda i, order, info: (i, 0)),
            scratch_shapes=[
                pltpu.VMEM((2, EXPERTS_PER_GROUP, d, d_expert), _F32),
                pltpu.VMEM((2, EXPERTS_PER_GROUP, d, d_expert), _F32),
                pltpu.VMEM((2, EXPERTS_PER_GROUP, d_expert, d), _F32),
                pltpu.SMEM((1,), jnp.int32),
                pltpu.SemaphoreType.DMA((1 + EXPERTS_PER_GROUP,)),
            ],
        ),
        out_shape=jax.ShapeDtypeStruct((n_chunks * chunk, d), _F32),
        compiler_params=pltpu.CompilerParams(
            dimension_semantics=("arbitrary",), vmem_limit_bytes=VMEM_LIMIT_BYTES),
        name="moe",
    )
    ys = moe(order, info, *([xs] * MOE_CHUNKS), w_gate[0], w_up[0], w_down[0])

    ple = pl.pallas_call(
        functools.partial(_ple_kernel, alpha=alpha, tile=tile, chunk=chunk, n_sub=N_SUB, tiles_per_step=PLE_TILES,
                          t_pad=t_pad, n_steps=n_tiles // PLE_TILES),
        grid_spec=pltpu.PrefetchScalarGridSpec(
            num_scalar_prefetch=2,
            grid=(n_tiles // PLE_TILES,),
            in_specs=[
                pl.BlockSpec((PLE_TILES * tile, d), lambda k, meta, rank: (k, 0)),
                pl.BlockSpec((PLE_TILES * tile, d_ple), lambda k, meta, rank: (k, 0)),
                pl.BlockSpec((PLE_TILES, 1, tile), lambda k, meta, rank: (k, 0, 0)),
                pl.BlockSpec(memory_space=pl.ANY),
                _resident((d, d)), _full((1, d)), _resident((d_ple, d)), _full((1, d)), _full((1, d)),
            ],
            out_specs=pl.BlockSpec((PLE_TILES * tile, d), lambda k, meta, rank: (k, 0)),
            scratch_shapes=[
                pltpu.VMEM((2, PLE_TILES, t_pad, d), _F32),
                pltpu.VMEM((d, d), _BF16),
                pltpu.VMEM((d_ple, d), _BF16),
                pltpu.SemaphoreType.DMA((2,)),
            ],
        ),
        out_shape=jax.ShapeDtypeStruct((n, d), _F32),
        compiler_params=pltpu.CompilerParams(
            dimension_semantics=("arbitrary",), vmem_limit_bytes=VMEM_LIMIT_BYTES),
        name="ple",
    )
    out = ple(meta, rank, x1.reshape(n, d), p[0].reshape(n, d_ple), pos, ys,
              w_pg[0], _row(b_pg[0]), w_pp[0], _row(ln2_g[0]), _row(ln2_b[0]))
    return out.reshape(bsz, seq, d)
```

```python
import functools

import jax
import jax.numpy as jnp
from jax import lax
from jax.experimental import pallas as pl
from jax.experimental.pallas import tpu as pltpu

LN_EPS = 1e-5
N_GROUPS = 4
EXPERTS_PER_GROUP = 4
SUBLANES = 8
LANES = 128
ROUTER_ROWS = 32
TILE = 512
MOE_CHUNKS = 2
PLE_TILES = 2
N_SUB = 2
VMEM_LIMIT_BYTES = 56 * 1024 * 1024
NEG_BIG = -1e30
META_FIELDS = 5

_F32 = jnp.float32
_BF16 = jnp.bfloat16
_NT = (((1,), (1,)), ((), ()))
_TN = (((0,), (0,)), ((), ()))


def _layer_norm(x, g, b):
    mu = jnp.mean(x, axis=-1, keepdims=True)
    xc = x - mu
    var = jnp.mean(xc * xc, axis=-1, keepdims=True)
    return xc * lax.rsqrt(var + LN_EPS) * g + b


def _sigmoid(x):
    return 0.5 * jnp.tanh(0.5 * x) + 0.5


def _dot(a, b):
    return jnp.dot(a, b, preferred_element_type=_F32)


def _aligned(v):
    return v if isinstance(v, int) else pl.multiple_of(v, SUBLANES)


def _rows(src, s0, dst, d0, m, sem):
    return pltpu.make_async_copy(src.at[pl.ds(_aligned(s0), _aligned(m))],
                                 dst.at[pl.ds(_aligned(d0), _aligned(m))], sem)


def _route(logits):
    row = lax.broadcasted_iota(jnp.int32, (SUBLANES, logits.shape[1]), 0)
    real = row < N_GROUPS
    gl = jnp.where(real, logits[0:SUBLANES], NEG_BIG)
    gm = jnp.max(gl, axis=0, keepdims=True)
    ge = jnp.exp(gl - gm)
    gp = ge / jnp.sum(ge, axis=0, keepdims=True)
    gp_top = jnp.max(gp, axis=0, keepdims=True)
    gidx = jnp.min(jnp.where(gp == gp_top, row, SUBLANES), axis=0, keepdims=True)

    el = jnp.zeros_like(gl)
    for g in range(N_GROUPS):
        lo = N_GROUPS + g * EXPERTS_PER_GROUP
        blk = logits[lo - lo % SUBLANES:lo - lo % SUBLANES + SUBLANES]
        if lo % SUBLANES:
            blk = pltpu.roll(blk, SUBLANES - lo % SUBLANES, 0)
        el = jnp.where(gidx == g, blk, el)
    el = jnp.where(real, el, NEG_BIG)
    em = jnp.max(el, axis=0, keepdims=True)
    ee = jnp.exp(el - em)
    ep = ee / jnp.sum(ee, axis=0, keepdims=True)
    ep = jnp.where(real, ep, -1.0)
    p1 = jnp.max(ep, axis=0, keepdims=True)
    i1 = jnp.min(jnp.where(ep == p1, row, SUBLANES), axis=0, keepdims=True)
    ep2 = jnp.where(row == i1, -1.0, ep)
    p2 = jnp.max(ep2, axis=0, keepdims=True)
    i2 = jnp.min(jnp.where(ep2 == p2, row, SUBLANES), axis=0, keepdims=True)
    denom = p1 + p2
    w = jnp.where(row == i1, p1 / denom, jnp.where(row == i2, p2 / denom, 0.0))
    return gidx, w * gp_top


def _mixer_kernel(x_ref, lng_ref, lnb_ref, w_in_ref, b_in_ref, cw_ref, cb_ref, clg_ref, clb_ref,
                  sw_ref, sb_ref, w_out_ref, b_out_ref, l1g_ref, l1b_ref, wr_ref, brg_ref, bre_ref,
                  x1_ref, pos_ref, xs_ref, meta_ref, info_ref, order_ref, rank_ref,
                  abuf, ashift, ubuf, w_in_st, w_out_st, w_in_bf, w_out_bf, wr_bf, br_ref, tri_ref, stage, zbuf,
                  cur_ref, fill_ref, nfree_ref, ring_ref, sem, zsem, wsem,
                  *, alpha, tile, t_pad, chunk, n_chunks, n_tiles, conf_w, sc_w_,
                  conf_k, sc_k, a_halo, u_halo, n_logits):
    j = pl.program_id(1)
    k = pl.program_id(0) * pl.num_programs(1) + j
    slot = k % 2
    d = x1_ref.shape[-1]
    c0, c1, c2, c3 = conf_w, 2 * conf_w, 2 * conf_w + sc_w_, 2 * conf_w + 2 * sc_w_
    w_parts = (
        (w_in_ref.at[:, 0:c1], w_in_st.at[:, 0:c1], w_in_bf.at[:, 0:c1]),
        (w_in_ref.at[:, c1:], w_in_st.at[:, c1:], w_in_bf.at[:, c1:]),
        (w_out_ref, w_out_st, w_out_bf),
    )
    w_copies = [pltpu.make_async_copy(src, st, wsem.at[n]) for n, (src, st, _) in enumerate(w_parts)]

    def weights_ready(n):
        w_copies[n].wait()
        w_parts[n][2][...] = w_parts[n][1][...].astype(_BF16)

    spare = [pltpu.make_async_copy(zbuf, xs_ref.at[pl.ds(c * chunk, chunk)], zsem)
             for c in range(n_tiles * tile // chunk, n_chunks)]

    @pl.when(k == 0)
    def _():
        for cp in w_copies:
            cp.start()
        zbuf[...] = jnp.zeros_like(zbuf)
        weights_ready(0)
        wr_bf[...] = jnp.zeros_like(wr_bf)
        wr_bf[0:n_logits, :] = wr_ref[...].astype(_BF16)
        def column(b_ref, first_row):
            shape = (ROUTER_ROWS, b_ref.shape[1])
            on_diag = (lax.broadcasted_iota(jnp.int32, shape, 0)
                       == lax.broadcasted_iota(jnp.int32, shape, 1) + first_row)
            return jnp.sum(jnp.where(on_diag, b_ref[...], 0.0), axis=1, keepdims=True)

        br_ref[...] = column(brg_ref, 0) + column(bre_ref, N_GROUPS)
        earlier = lax.broadcasted_iota(jnp.int32, (tile, tile), 0) < lax.broadcasted_iota(jnp.int32, (tile, tile), 1)
        tri_ref[...] = jnp.where(earlier, 1.0, 0.0).astype(_BF16)
        nfree_ref[0] = 0
        for g in range(N_GROUPS):
            cur_ref[g] = 0
            fill_ref[g] = chunk
        for c in range(n_chunks + 1):
            info_ref[c] = N_GROUPS - 1
        for c in range(n_chunks):
            info_ref[n_chunks + 1 + c] = chunk

    @pl.when(k == 1)
    def _():
        for cp in spare:
            cp.start()

    @pl.when(k == 2)
    def _():
        for cp in spare:
            cp.wait()

    @pl.when(j == 0)
    def _():
        abuf[0:a_halo, :] = jnp.zeros((a_halo, conf_w), _F32)
        ubuf[0:u_halo, :] = jnp.zeros((u_halo, sc_w_), _F32)

    x0 = _layer_norm(x_ref[0], lng_ref[...], lnb_ref[...])
    x0b = x0.astype(_BF16)
    h = _dot(x0b, w_in_bf[:, 0:c1]) + b_in_ref[:, 0:c1]

    a = h[:, 0:c0] * _sigmoid(h[:, c0:c1])
    abuf[a_halo:a_halo + tile, :] = a
    for i in range(1, SUBLANES):
        ashift[i - 1] = abuf[i:i + tile + a_halo - SUBLANES, :]
    acc = jnp.broadcast_to(cb_ref[...], (tile, conf_w))
    for t in range(conf_k):
        q, i = divmod(a_halo - (conf_k - 1) + t, SUBLANES)
        if i == 0:
            src = abuf[SUBLANES * q:SUBLANES * q + tile, :]
        else:
            src = ashift[i - 1, SUBLANES * q:SUBLANES * q + tile, :]
        acc = acc + cw_ref[t:t + 1, :] * src
    abuf[0:a_halo, :] = abuf[tile:tile + a_halo, :]

    @pl.when(k == 0)
    def _():
        weights_ready(1)
        weights_ready(2)

    h2 = _dot(x0b, w_in_bf[:, c1:]) + b_in_ref[:, c1:]
    an = _layer_norm(acc, clg_ref[...], clb_ref[...])
    a2 = an * _sigmoid(an)

    u = h2[:, c2 - c1:c3 - c1] * h2[:, c3 - c1:]
    ubuf[u_halo:u_halo + tile, :] = u
    sc = jnp.broadcast_to(sb_ref[...], (tile, sc_w_))
    for t in range(sc_k):
        off = u_halo - (sc_k - 1) + t
        sc = sc + sw_ref[t:t + 1, :] * ubuf[off:off + tile, :]
    ubuf[0:u_halo, :] = ubuf[tile:tile + u_halo, :]
    s = h2[:, 0:c2 - c1] * sc

    mix = _dot(jnp.concatenate([a2.astype(_BF16), s.astype(_BF16)], axis=1), w_out_bf[...]) + b_out_ref[...]
    x1 = _layer_norm(alpha * x0 + mix, l1g_ref[...], l1b_ref[...])
    x1_ref[0] = x1
    x1b = x1.astype(_BF16)

    logits = lax.dot_general(wr_bf[...], x1b, _NT, preferred_element_type=_F32) + br_ref[...]
    gidx, w = _route(logits)

    row8 = lax.broadcasted_iota(jnp.int32, (SUBLANES, tile), 0)
    onehot = jnp.where(gidx == row8, 1.0, 0.0)
    cnt = jnp.sum(onehot, axis=1, keepdims=True)
    sizes, starts = [], []
    start = jnp.int32(0)
    startv = jnp.zeros((SUBLANES, tile), jnp.int32)
    for g in range(N_GROUPS):
        n8 = (cnt[g, 0].astype(jnp.int32) + (SUBLANES - 1)) & (-SUBLANES)
        sizes.append(n8)
        starts.append(start)
        startv = jnp.where(row8 == g, start, startv)
        start = start + n8
    cum = _dot(onehot.astype(_BF16), tri_ref[...])
    pos = jnp.sum(onehot * (startv.astype(_F32) + cum), axis=0, keepdims=True).astype(jnp.int32)
    pos_ref[0] = pos
    perm = jnp.where(lax.broadcasted_iota(jnp.int32, (t_pad, tile), 0) == pos, 1.0, 0.0).astype(_BF16)

    w_hi = w.astype(_BF16).astype(_F32)
    w_lo = (w - w_hi).astype(_BF16).astype(_F32)
    w_hl = jnp.where(row8 < EXPERTS_PER_GROUP, w_hi, pltpu.roll(w_lo, EXPERTS_PER_GROUP, 0))
    w_rows = jnp.concatenate([w_hl, jnp.zeros((LANES - SUBLANES, tile), _F32)], axis=0).astype(_BF16)

    def wait_slot(sl):
        for g in range(N_GROUPS):
            for piece in range(2):
                m = ring_ref[sl, 2 * g + piece]

                @pl.when(m > 0)
                def _():
                    _rows(stage.at[sl], 0, xs_ref, 0, m, sem.at[sl]).wait()

    @pl.when(k >= 2)
    def _():
        wait_slot(slot)

    stage[slot, :, 0:d] = _dot(perm, x1b)
    stage[slot, :, d:] = lax.dot_general(perm, w_rows, _NT, preferred_element_type=_F32)

    for g in range(N_GROUPS):
        n8, sg = sizes[g], starts[g]
        fill, cur, newc = fill_ref[g], cur_ref[g], nfree_ref[0]
        m1 = jnp.minimum(n8, chunk - fill)
        m2 = n8 - m1
        r1 = cur * chunk + fill
        r2 = newc * chunk
        ring_ref[slot, 2 * g] = m1
        ring_ref[slot, 2 * g + 1] = m2
        for f, v in enumerate((r1, m1, r2, m2, sg)):
            meta_ref[k, g * META_FIELDS + f] = v

        @pl.when(m1 > 0)
        def _():
            _rows(stage.at[slot], sg, xs_ref, r1, m1, sem.at[slot]).start()

        @pl.when(m2 > 0)
        def _():
            _rows(stage.at[slot], sg + m1, xs_ref, r2, m2, sem.at[slot]).start()
            info_ref[newc] = g
            cur_ref[g] = newc
            nfree_ref[0] = newc + 1

        fill_ref[g] = jnp.where(m2 > 0, m2, fill + m1)

    @pl.when(k == n_tiles - 1)
    def _():
        nfree = nfree_ref[0]
        info_ref[n_chunks] = nfree
        tails = []
        for g in range(N_GROUPS):
            rem = chunk - fill_ref[g]

            @pl.when(rem > 0)
            def _():
                info_ref[n_chunks + 1 + cur_ref[g]] = fill_ref[g]

            tails.append((rem, _rows(zbuf, 0, xs_ref, cur_ref[g] * chunk + fill_ref[g], rem, zsem)))
        for rem, cp in tails:
            pl.when(rem > 0)(cp.start)

        slot_out = jnp.int32(0)
        for g in range(N_GROUPS):
            def place(c, nxt, g=g):
                hit = info_ref[c] == g

                @pl.when(hit)
                def _():
                    order_ref[nxt] = c
                    rank_ref[c] = nxt

                return nxt + hit.astype(jnp.int32)

            slot_out = lax.fori_loop(0, n_chunks, place, slot_out)

        @pl.when(k >= 1)
        def _():
            wait_slot(1 - slot)

        wait_slot(slot)
        for rem, cp in tails:
            pl.when(rem > 0)(cp.wait)


def _moe_kernel(order_ref, info_ref, *refs, d, d_expert, n_chunks, per_step):
    xs_refs = refs[:per_step]
    wg_hbm, wu_hbm, wd_hbm, o_ref, wg_buf, wu_buf, wd_buf, slot_ref, wsem = refs[per_step:]
    chunk = xs_refs[0].shape[0]

    def weight_copies(g, sl):
        lo = g * EXPERTS_PER_GROUP
        return [pltpu.make_async_copy(src.at[pl.ds(lo, EXPERTS_PER_GROUP)], dst.at[sl], wsem.at[0])
                for src, dst in ((wg_hbm, wg_buf), (wu_hbm, wu_buf), (wd_hbm, wd_buf))]

    def first_copies(g, e):
        return [pltpu.make_async_copy(src.at[g * EXPERTS_PER_GROUP + e], dst.at[0, e], wsem.at[1 + e])
                for src, dst in ((wg_hbm, wg_buf), (wu_hbm, wu_buf), (wd_hbm, wd_buf))]

    for ci, xs_ref in enumerate(xs_refs):
        i = pl.program_id(0) * per_step + ci
        rows = slice(ci * chunk, (ci + 1) * chunk)
        c = order_ref[i]
        grp = info_ref[c]
        prev = info_ref[order_ref[jnp.maximum(i - 1, 0)]]

        @pl.when(i == 0)
        def _():
            slot_ref[0] = 0
            for e in range(EXPERTS_PER_GROUP):
                for cp in first_copies(grp, e):
                    cp.start()

        @pl.when((i == 0) | (grp != prev))
        def _():
            @pl.when(i > 0)
            def _():
                slot_ref[0] = 1 - slot_ref[0]
                for cp in weight_copies(grp, slot_ref[0]):
                    cp.wait()

            cur = slot_ref[0]

            def later_group(j, found):
                gj = info_ref[order_ref[j]]
                return jnp.where((found < 0) & (j > i) & (gj != grp), gj, found)

            nxt = lax.fori_loop(0, n_chunks, later_group, jnp.int32(-1))

            @pl.when(nxt >= 0)
            def _():
                for cp in weight_copies(nxt, 1 - cur):
                    cp.start()

        used = c < info_ref[n_chunks]
        half = chunk // 2
        low_only = info_ref[n_chunks + 1 + c] <= half

        def experts(m, xs_ref=xs_ref, grp=grp, await_first=False):
            xb = xs_ref[0:m, 0:d]
            ws = slot_ref[0]
            aux = xs_ref[0:m, d:]
            y = jnp.zeros((m, o_ref.shape[1]), _F32)
            for e in range(EXPERTS_PER_GROUP):
                if await_first:
                    for cp in first_copies(grp, e):
                        cp.wait()
                hg = _dot(xb, wg_buf[ws, e])
                hu = _dot(xb, wu_buf[ws, e])
                we = aux[:, e:e + 1] + aux[:, EXPERTS_PER_GROUP + e:EXPERTS_PER_GROUP + e + 1]
                hid = hg * _sigmoid(hg) * hu * we
                y = y + _dot(hid, wd_buf[ws, e])
            return y

        later = used
        if ci == 0:
            later = used & (i > 0)

            @pl.when(i == 0)
            def _():
                o_ref[rows, :] = experts(chunk, await_first=True)

        @pl.when(later & jnp.logical_not(low_only))
        def _():
            o_ref[rows, :] = experts(chunk)

        @pl.when(later & low_only)
        def _():
            o_ref[ci * chunk:ci * chunk + half, :] = experts(half)
            o_ref[ci * chunk + half:(ci + 1) * chunk, :] = jnp.zeros((half, o_ref.shape[1]), _F32)

        @pl.when(jnp.logical_not(used))
        def _():
            o_ref[rows, :] = jnp.zeros((chunk, o_ref.shape[1]), _F32)


def _ple_kernel(meta_ref, rank_ref, x1_ref, p_ref, pos_ref, ys_ref, wpg_ref, bpg_ref, wpp_ref, l2g_ref, l2b_ref,
                o_ref, ybuf, wpg_bf, wpp_bf, sem, *, alpha, tile, chunk, n_sub, tiles_per_step, t_pad, n_steps):
    k = pl.program_id(0)
    slot = k % 2

    def moved(r):
        return rank_ref[r // chunk] * chunk + r % chunk

    def pieces(step, sl):
        out = []
        for ti in range(tiles_per_step):
            kk = step * tiles_per_step + ti
            for g in range(N_GROUPS):
                r1, m1, r2, m2, sg = (meta_ref[kk, g * META_FIELDS + f] for f in range(META_FIELDS))
                out.append((m1, _rows(ys_ref, moved(r1), ybuf.at[sl, ti], sg, m1, sem.at[sl])))
                out.append((m2, _rows(ys_ref, moved(r2), ybuf.at[sl, ti], sg + m1, m2, sem.at[sl])))
        return out

    def fetch(kk, sl):
        for m, cp in pieces(kk, sl):
            pl.when(m > 0)(cp.start)

    @pl.when(k == 0)
    def _():
        ybuf[:, :, tile:, :] = jnp.zeros((2, tiles_per_step, t_pad - tile, ybuf.shape[-1]), _F32)
        fetch(0, 0)
        wpg_bf[...] = wpg_ref[...].astype(_BF16)
        wpp_bf[...] = wpp_ref[...].astype(_BF16)

    @pl.when(k + 1 < n_steps)
    def _():
        fetch(k + 1, 1 - slot)

    for m, cp in pieces(k, slot):
        pl.when(m > 0)(cp.wait)

    sub = tile // n_sub
    for ti in range(tiles_per_step):
        yb = ybuf[slot, ti]
        pos = pos_ref[ti]
        for blk in range(n_sub):
            cols = slice(blk * sub, (blk + 1) * sub)
            rows = slice(ti * tile + blk * sub, ti * tile + (blk + 1) * sub)
            perm = jnp.where(lax.broadcasted_iota(jnp.int32, (t_pad, sub), 0) == pos[:, cols], 1.0, 0.0)
            y = lax.dot_general(perm, yb, _TN, preferred_element_type=_F32)
            r = alpha * x1_ref[rows, :] + y
            gate = _sigmoid(_dot(r, wpg_bf[...]) + bpg_ref[...])
            pp = _dot(p_ref[rows, :], wpp_bf[...])
            o_ref[rows, :] = _layer_norm(r + gate * pp, l2g_ref[...], l2b_ref[...])


def _full(shape):
    return pl.BlockSpec(shape, lambda *_: (0,) * len(shape))


def _resident(shape):
    return pl.BlockSpec(shape, lambda *_: (0,) * len(shape), pipeline_mode=pl.Buffered(1))


def _row(v):
    return v.reshape(1, -1).astype(_F32)


def kernel(x, p, ln_in_g, ln_in_b, w_in, b_in, conf_dw_w, conf_dw_b, conf_ln_g, conf_ln_b, sc_w, sc_b, w_out, b_out, ln1_g, ln1_b, w_rg, b_rg, w_re, b_re, w_gate, w_up, w_down, w_pg, b_pg, w_pp, ln2_g, ln2_b):
    depth = w_in.shape[0]
    assert depth == 1, "single-layer block"
    alpha = (2.0 * depth) ** 0.25
    bsz, seq, d = x.shape
    n = bsz * seq
    conf_k, conf_w = conf_dw_w.shape[1:]
    sc_k, sc_w_ = sc_w.shape[1:]
    d_in = w_in.shape[2]
    n_exp, _, d_expert = w_gate.shape[1:]
    assert n_exp == N_GROUPS * EXPERTS_PER_GROUP
    assert d_in == 2 * conf_w + 3 * sc_w_ and w_out.shape[1] == conf_w + sc_w_
    d_ple = p.shape[-1]

    tile = chunk = TILE
    assert seq % tile == 0 and (bsz * (seq // tile)) % PLE_TILES == 0
    assert bsz * (seq // tile) >= 3 * N_GROUPS, "spare chunks are zero-filled during the first three steps"
    nt = seq // tile
    n_tiles = bsz * nt
    a_halo = -(-(conf_k - 1) // SUBLANES) * SUBLANES
    u_halo = -(-(sc_k - 1) // SUBLANES) * SUBLANES
    t_pad = tile + N_GROUPS * SUBLANES
    n_chunks = -(-(n + (SUBLANES - 1) * N_GROUPS * n_tiles) // chunk) + N_GROUPS
    n_chunks = -(-n_chunks // MOE_CHUNKS) * MOE_CHUNKS
    d_pay = d + LANES

    n_logits = N_GROUPS + n_exp
    wr = jnp.concatenate([w_rg[0], w_re[0]], axis=1).T

    mixer = pl.pallas_call(
        functools.partial(_mixer_kernel, alpha=alpha, tile=tile, t_pad=t_pad, chunk=chunk,
                          n_chunks=n_chunks, n_tiles=n_tiles, conf_w=conf_w, sc_w_=sc_w_,
                          conf_k=conf_k, sc_k=sc_k, a_halo=a_halo, u_halo=u_halo, n_logits=n_logits),
        grid=(bsz, nt),
        in_specs=[
            pl.BlockSpec((1, tile, d), lambda b, j: (b, j, 0)),
            _full((1, d)), _full((1, d)),
            pl.BlockSpec(memory_space=pl.ANY), _full((1, d_in)),
            _full((conf_k, conf_w)), _full((1, conf_w)), _full((1, conf_w)), _full((1, conf_w)),
            _full((sc_k, sc_w_)), _full((1, sc_w_)),
            pl.BlockSpec(memory_space=pl.ANY), _full((1, d)),
            _full((1, d)), _full((1, d)),
            _full((n_logits, d)), _full((1, N_GROUPS)), _full((1, n_exp)),
        ],
        out_specs=[
            pl.BlockSpec((1, tile, d), lambda b, j: (b, j, 0)),
            pl.BlockSpec((1, 1, tile), lambda b, j: (b * nt + j, 0, 0)),
            pl.BlockSpec(memory_space=pl.ANY),
            pl.BlockSpec(memory_space=pltpu.SMEM),
            pl.BlockSpec(memory_space=pltpu.SMEM),
            pl.BlockSpec(memory_space=pltpu.SMEM),
            pl.BlockSpec(memory_space=pltpu.SMEM),
        ],
        out_shape=[
            jax.ShapeDtypeStruct((bsz, seq, d), _F32),
            jax.ShapeDtypeStruct((n_tiles, 1, tile), jnp.int32),
            jax.ShapeDtypeStruct((n_chunks * chunk, d_pay), _F32),
            jax.ShapeDtypeStruct((n_tiles, N_GROUPS * META_FIELDS), jnp.int32),
            jax.ShapeDtypeStruct((2 * n_chunks + 1,), jnp.int32),
            jax.ShapeDtypeStruct((n_chunks,), jnp.int32),
            jax.ShapeDtypeStruct((n_chunks,), jnp.int32),
        ],
        scratch_shapes=[
            pltpu.VMEM((a_halo + tile, conf_w), _F32),
            pltpu.VMEM((SUBLANES - 1, a_halo + tile - SUBLANES, conf_w), _F32),
            pltpu.VMEM((u_halo + tile, sc_w_), _F32),
            pltpu.VMEM((d, d_in), _F32),
            pltpu.VMEM((conf_w + sc_w_, d), _F32),
            pltpu.VMEM((d, d_in), _BF16),
            pltpu.VMEM((conf_w + sc_w_, d), _BF16),
            pltpu.VMEM((ROUTER_ROWS, d), _BF16),
            pltpu.VMEM((ROUTER_ROWS, 1), _F32),
            pltpu.VMEM((tile, tile), _BF16),
            pltpu.VMEM((2, t_pad, d_pay), _F32),
            pltpu.VMEM((chunk, d_pay), _F32),
            pltpu.SMEM((N_GROUPS,), jnp.int32),
            pltpu.SMEM((N_GROUPS,), jnp.int32),
            pltpu.SMEM((1,), jnp.int32),
            pltpu.SMEM((2, 2 * N_GROUPS), jnp.int32),
            pltpu.SemaphoreType.DMA((2,)),
            pltpu.SemaphoreType.DMA(()),
            pltpu.SemaphoreType.DMA((3,)),
        ],
        compiler_params=pltpu.CompilerParams(
            dimension_semantics=("arbitrary", "arbitrary"), vmem_limit_bytes=VMEM_LIMIT_BYTES),
        name="mixer",
    )
    x1, pos, xs, meta, info, order, rank = mixer(
        x, _row(ln_in_g), _row(ln_in_b), w_in[0], _row(b_in[0]),
        conf_dw_w[0], _row(conf_dw_b[0]), _row(conf_ln_g[0]), _row(conf_ln_b[0]),
        sc_w[0], _row(sc_b[0]), w_out[0], _row(b_out[0]),
        _row(ln1_g[0]), _row(ln1_b[0]), wr, _row(b_rg[0]), _row(b_re[0]))

    moe = pl.pallas_call(
        functools.partial(_moe_kernel, d=d, d_expert=d_expert, n_chunks=n_chunks, per_step=MOE_CHUNKS),
        grid_spec=pltpu.PrefetchScalarGridSpec(
            num_scalar_prefetch=2,
            grid=(n_chunks // MOE_CHUNKS,),
            in_specs=[
                pl.BlockSpec((chunk, d_pay), functools.partial(
                    lambda i, order, info, ci: (order[i * MOE_CHUNKS + ci], 0), ci=ci))
                for ci in range(MOE_CHUNKS)
            ] + [
                pl.BlockSpec(memory_space=pl.ANY),
                pl.BlockSpec(memory_space=pl.ANY),
                pl.BlockSpec(memory_space=pl.ANY),
            ],
            out_specs=pl.BlockSpec((MOE_CHUNKS * chunk, d), lambda i, order, info: (i, 0)),
            scratch_shapes=[
                pltpu.VMEM((2, EXPERTS_PER_GROUP, d, d_expert), _F32),
                pltpu.VMEM((2, EXPERTS_PER_GROUP, d, d_expert), _F32),
                pltpu.VMEM((2, EXPERTS_PER_GROUP, d_expert, d), _F32),
                pltpu.SMEM((1,), jnp.int32),
                pltpu.SemaphoreType.DMA((1 + EXPERTS_PER_GROUP,)),
            ],
        ),
        out_shape=jax.ShapeDtypeStruct((n_chunks * chunk, d), _F32),
        compiler_params=pltpu.CompilerParams(
            dimension_semantics=("arbitrary",), vmem_limit_bytes=VMEM_LIMIT_BYTES),
        name="moe",
    )
    ys = moe(order, info, *([xs] * MOE_CHUNKS), w_gate[0], w_up[0], w_down[0])

    ple = pl.pallas_call(
        functools.partial(_ple_kernel, alpha=alpha, tile=tile, chunk=chunk, n_sub=N_SUB, tiles_per_step=PLE_TILES,
                          t_pad=t_pad, n_steps=n_tiles // PLE_TILES),
        grid_spec=pltpu.PrefetchScalarGridSpec(
            num_scalar_prefetch=2,
            grid=(n_tiles // PLE_TILES,),
            in_specs=[
                pl.BlockSpec((PLE_TILES * tile, d), lambda k, meta, rank: (k, 0)),
                pl.BlockSpec((PLE_TILES * tile, d_ple), lambda k, meta, rank: (k, 0)),
                pl.BlockSpec((PLE_TILES, 1, tile), lambda k, meta, rank: (k, 0, 0)),
                pl.BlockSpec(memory_space=pl.ANY),
                _resident((d, d)), _full((1, d)), _resident((d_ple, d)), _full((1, d)), _full((1, d)),
            ],
            out_specs=pl.BlockSpec((PLE_TILES * tile, d), lambda k, meta, rank: (k, 0)),
            scratch_shapes=[
                pltpu.VMEM((2, PLE_TILES, t_pad, d), _F32),
                pltpu.VMEM((d, d), _BF16),
                pltpu.VMEM((d_ple, d), _BF16),
                pltpu.SemaphoreType.DMA((2,)),
            ],
        ),
        out_shape=jax.ShapeDtypeStruct((n, d), _F32),
        compiler_params=pltpu.CompilerParams(
            dimension_semantics=("arbitrary",), vmem_limit_bytes=VMEM_LIMIT_BYTES),
        name="ple",
    )
    out = ple(meta, rank, x1.reshape(n, d), p[0].reshape(n, d_ple), pos, ys,
              w_pg[0], _row(b_pg[0]), w_pp[0], _row(ln2_g[0]), _row(ln2_b[0]))
    return out.reshape(bsz, seq, d)
```

```python
import functools

import jax
import jax.numpy as jnp
from jax import lax
from jax.experimental import pallas as pl
from jax.experimental.pallas import tpu as pltpu

LN_EPS = 1e-5
N_GROUPS = 4
EXPERTS_PER_GROUP = 4
SUBLANES = 8
LANES = 128
ROUTER_ROWS = 32
TILE = 512
MOE_CHUNKS = 2
PLE_TILES = 2
N_SUB = 2
VMEM_LIMIT_BYTES = 56 * 1024 * 1024
NEG_BIG = -1e30
META_FIELDS = 5

_F32 = jnp.float32
_BF16 = jnp.bfloat16
_NT = (((1,), (1,)), ((), ()))
_TN = (((0,), (0,)), ((), ()))


def _layer_norm(x, g, b):
    mu = jnp.mean(x, axis=-1, keepdims=True)
    xc = x - mu
    var = jnp.mean(xc * xc, axis=-1, keepdims=True)
    return xc * lax.rsqrt(var + LN_EPS) * g + b


def _sigmoid(x):
    return 0.5 * jnp.tanh(0.5 * x) + 0.5


def _dot(a, b):
    return jnp.dot(a, b, preferred_element_type=_F32)


def _aligned(v):
    return v if isinstance(v, int) else pl.multiple_of(v, SUBLANES)


def _rows(src, s0, dst, d0, m, sem):
    return pltpu.make_async_copy(src.at[pl.ds(_aligned(s0), _aligned(m))],
                                 dst.at[pl.ds(_aligned(d0), _aligned(m))], sem)


def _route(logits):
    row = lax.broadcasted_iota(jnp.int32, (SUBLANES, logits.shape[1]), 0)
    real = row < N_GROUPS
    gl = jnp.where(real, logits[0:SUBLANES], NEG_BIG)
    gm = jnp.max(gl, axis=0, keepdims=True)
    ge = jnp.exp(gl - gm)
    gp = ge / jnp.sum(ge, axis=0, keepdims=True)
    gp_top = jnp.max(gp, axis=0, keepdims=True)
    gidx = jnp.min(jnp.where(gp == gp_top, row, SUBLANES), axis=0, keepdims=True)

    el = jnp.zeros_like(gl)
    for g in range(N_GROUPS):
        lo = N_GROUPS + g * EXPERTS_PER_GROUP
        blk = logits[lo - lo % SUBLANES:lo - lo % SUBLANES + SUBLANES]
        if lo % SUBLANES:
            blk = pltpu.roll(blk, SUBLANES - lo % SUBLANES, 0)
        el = jnp.where(gidx == g, blk, el)
    el = jnp.where(real, el, NEG_BIG)
    em = jnp.max(el, axis=0, keepdims=True)
    ee = jnp.exp(el - em)
    ep = ee / jnp.sum(ee, axis=0, keepdims=True)
    ep = jnp.where(real, ep, -1.0)
    p1 = jnp.max(ep, axis=0, keepdims=True)
    i1 = jnp.min(jnp.where(ep == p1, row, SUBLANES), axis=0, keepdims=True)
    ep2 = jnp.where(row == i1, -1.0, ep)
    p2 = jnp.max(ep2, axis=0, keepdims=True)
    i2 = jnp.min(jnp.where(ep2 == p2, row, SUBLANES), axis=0, keepdims=True)
    denom = p1 + p2
    w = jnp.where(row == i1, p1 / denom, jnp.where(row == i2, p2 / denom, 0.0))
    return gidx, w * gp_top


def _mixer_kernel(x_ref, lng_ref, lnb_ref, w_in_ref, b_in_ref, cw_ref, cb_ref, clg_ref, clb_ref,
                  sw_ref, sb_ref, w_out_ref, b_out_ref, l1g_ref, l1b_ref, wr_ref, brg_ref, bre_ref,
                  x1_ref, pos_ref, xs_ref, meta_ref, info_ref, order_ref, rank_ref,
                  abuf, ashift, ubuf, w_in_st, w_out_st, w_in_bf, w_out_bf, wr_bf, br_ref, tri_ref, stage, zbuf,
                  cur_ref, fill_ref, nfree_ref, ring_ref, sem, zsem, wsem,
                  *, alpha, tile, t_pad, chunk, n_chunks, n_tiles, conf_w, sc_w_,
                  conf_k, sc_k, a_halo, u_halo, n_logits):
    j = pl.program_id(1)
    k = pl.program_id(0) * pl.num_programs(1) + j
    slot = k % 2
    d = x1_ref.shape[-1]
    c0, c1, c2, c3 = conf_w, 2 * conf_w, 2 * conf_w + sc_w_, 2 * conf_w + 2 * sc_w_
    w_parts = (
        (w_in_ref.at[:, 0:c1], w_in_st.at[:, 0:c1], w_in_bf.at[:, 0:c1]),
        (w_in_ref.at[:, c1:], w_in_st.at[:, c1:], w_in_bf.at[:, c1:]),
        (w_out_ref, w_out_st, w_out_bf),
    )
    w_copies = [pltpu.make_async_copy(src, st, wsem.at[n]) for n, (src, st, _) in enumerate(w_parts)]

    def weights_ready(n):
        w_copies[n].wait()
        w_parts[n][2][...] = w_parts[n][1][...].astype(_BF16)

    spare = [pltpu.make_async_copy(zbuf, xs_ref.at[pl.ds(c * chunk, chunk)], zsem)
             for c in range(n_tiles * tile // chunk, n_chunks)]

    @pl.when(k == 0)
    def _():
        for cp in w_copies:
            cp.start()
        zbuf[...] = jnp.zeros_like(zbuf)
        weights_ready(0)
        wr_bf[...] = jnp.zeros_like(wr_bf)
        wr_bf[0:n_logits, :] = wr_ref[...].astype(_BF16)
        def column(b_ref, first_row):
            shape = (ROUTER_ROWS, b_ref.shape[1])
            on_diag = (lax.broadcasted_iota(jnp.int32, shape, 0)
                       == lax.broadcasted_iota(jnp.int32, shape, 1) + first_row)
            return jnp.sum(jnp.where(on_diag, b_ref[...], 0.0), axis=1, keepdims=True)

        br_ref[...] = column(brg_ref, 0) + column(bre_ref, N_GROUPS)
        earlier = lax.broadcasted_iota(jnp.int32, (tile, tile), 0) < lax.broadcasted_iota(jnp.int32, (tile, tile), 1)
        tri_ref[...] = jnp.where(earlier, 1.0, 0.0).astype(_BF16)
        nfree_ref[0] = 0
        for g in range(N_GROUPS):
            cur_ref[g] = 0
            fill_ref[g] = chunk
        for c in range(n_chunks + 1):
            info_ref[c] = N_GROUPS - 1
        for c in range(n_chunks):
            info_ref[n_chunks + 1 + c] = chunk

    @pl.when(k == 1)
    def _():
        for cp in spare:
            cp.start()

    @pl.when(k == 2)
    def _():
        for cp in spare:
            cp.wait()

    @pl.when(j == 0)
    def _():
        abuf[0:a_halo, :] = jnp.zeros((a_halo, conf_w), _F32)
        ubuf[0:u_halo, :] = jnp.zeros((u_halo, sc_w_), _F32)

    x0 = _layer_norm(x_ref[0], lng_ref[...], lnb_ref[...])
    x0b = x0.astype(_BF16)
    h = _dot(x0b, w_in_bf[:, 0:c1]) + b_in_ref[:, 0:c1]

    a = h[:, 0:c0] * _sigmoid(h[:, c0:c1])
    abuf[a_halo:a_halo + tile, :] = a
    for i in range(1, SUBLANES):
        ashift[i - 1] = abuf[i:i + tile + a_halo - SUBLANES, :]
    acc = jnp.broadcast_to(cb_ref[...], (tile, conf_w))
    for t in range(conf_k):
        q, i = divmod(a_halo - (conf_k - 1) + t, SUBLANES)
        if i == 0:
            src = abuf[SUBLANES * q:SUBLANES * q + tile, :]
        else:
            src = ashift[i - 1, SUBLANES * q:SUBLANES * q + tile, :]
        acc = acc + cw_ref[t:t + 1, :] * src
    abuf[0:a_halo, :] = abuf[tile:tile + a_halo, :]

    @pl.when(k == 0)
    def _():
        weights_ready(1)
        weights_ready(2)

    h2 = _dot(x0b, w_in_bf[:, c1:]) + b_in_ref[:, c1:]
    an = _layer_norm(acc, clg_ref[...], clb_ref[...])
    a2 = an * _sigmoid(an)

    u = h2[:, c2 - c1:c3 - c1] * h2[:, c3 - c1:]
    ubuf[u_halo:u_halo + tile, :] = u
    sc = jnp.broadcast_to(sb_ref[...], (tile, sc_w_))
    for t in range(sc_k):
        off = u_halo - (sc_k - 1) + t
        sc = sc + sw_ref[t:t + 1, :] * ubuf[off:off + tile, :]
    ubuf[0:u_halo, :] = ubuf[tile:tile + u_halo, :]
    s = h2[:, 0:c2 - c1] * sc

    mix = _dot(jnp.concatenate([a2.astype(_BF16), s.astype(_BF16)], axis=1), w_out_bf[...]) + b_out_ref[...]
    x1 = _layer_norm(alpha * x0 + mix, l1g_ref[...], l1b_ref[...])
    x1_ref[0] = x1
    x1b = x1.astype(_BF16)

    logits = lax.dot_general(wr_bf[...], x1b, _NT, preferred_element_type=_F32) + br_ref[...]
    gidx, w = _route(logits)

    row8 = lax.broadcasted_iota(jnp.int32, (SUBLANES, tile), 0)
    onehot = jnp.where(gidx == row8, 1.0, 0.0)
    cnt = jnp.sum(onehot, axis=1, keepdims=True)
    sizes, starts = [], []
    start = jnp.int32(0)
    startv = jnp.zeros((SUBLANES, tile), jnp.int32)
    for g in range(N_GROUPS):
        n8 = (cnt[g, 0].astype(jnp.int32) + (SUBLANES - 1)) & (-SUBLANES)
        sizes.append(n8)
        starts.append(start)
        startv = jnp.where(row8 == g, start, startv)
        start = start + n8
    cum = _dot(onehot.astype(_BF16), tri_ref[...])
    pos = jnp.sum(onehot * (startv.astype(_F32) + cum), axis=0, keepdims=True).astype(jnp.int32)
    pos_ref[0] = pos
    perm = jnp.where(lax.broadcasted_iota(jnp.int32, (t_pad, tile), 0) == pos, 1.0, 0.0).astype(_BF16)

    w_hi = w.astype(_BF16).astype(_F32)
    w_lo = (w - w_hi).astype(_BF16).astype(_F32)
    w_hl = jnp.where(row8 < EXPERTS_PER_GROUP, w_hi, pltpu.roll(w_lo, EXPERTS_PER_GROUP, 0))
    w_rows = jnp.concatenate([w_hl, jnp.zeros((LANES - SUBLANES, tile), _F32)], axis=0).astype(_BF16)

    def wait_slot(sl):
        for g in range(N_GROUPS):
            for piece in range(2):
                m = ring_ref[sl, 2 * g + piece]

                @pl.when(m > 0)
                def _():
                    _rows(stage.at[sl], 0, xs_ref, 0, m, sem.at[sl]).wait()

    @pl.when(k >= 2)
    def _():
        wait_slot(slot)

    stage[slot, :, 0:d] = _dot(perm, x1b)
    stage[slot, :, d:] = lax.dot_general(perm, w_rows, _NT, preferred_element_type=_F32)

    for g in range(N_GROUPS):
        n8, sg = sizes[g], starts[g]
        fill, cur, newc = fill_ref[g], cur_ref[g], nfree_ref[0]
        m1 = jnp.minimum(n8, chunk - fill)
        m2 = n8 - m1
        r1 = cur * chunk + fill
        r2 = newc * chunk
        ring_ref[slot, 2 * g] = m1
        ring_ref[slot, 2 * g + 1] = m2
        for f, v in enumerate((r1, m1, r2, m2, sg)):
            meta_ref[k, g * META_FIELDS + f] = v

        @pl.when(m1 > 0)
        def _():
            _rows(stage.at[slot], sg, xs_ref, r1, m1, sem.at[slot]).start()

        @pl.when(m2 > 0)
        def _():
            _rows(stage.at[slot], sg + m1, xs_ref, r2, m2, sem.at[slot]).start()
            info_ref[newc] = g
            cur_ref[g] = newc
            nfree_ref[0] = newc + 1

        fill_ref[g] = jnp.where(m2 > 0, m2, fill + m1)

    @pl.when(k == n_tiles - 1)
    def _():
        nfree = nfree_ref[0]
        info_ref[n_chunks] = nfree
        tails = []
        for g in range(N_GROUPS):
            rem = chunk - fill_ref[g]

            @pl.when(rem > 0)
            def _():
                info_ref[n_chunks + 1 + cur_ref[g]] = fill_ref[g]

            tails.append((rem, _rows(zbuf, 0, xs_ref, cur_ref[g] * chunk + fill_ref[g], rem, zsem)))
        for rem, cp in tails:
            pl.when(rem > 0)(cp.start)

        slot_out = jnp.int32(0)
        for g in range(N_GROUPS):
            def place(c, nxt, g=g):
                hit = info_ref[c] == g

                @pl.when(hit)
                def _():
                    order_ref[nxt] = c
                    rank_ref[c] = nxt

                return nxt + hit.astype(jnp.int32)

            slot_out = lax.fori_loop(0, n_chunks, place, slot_out)

        @pl.when(k >= 1)
        def _():
            wait_slot(1 - slot)

        wait_slot(slot)
        for rem, cp in tails:
            pl.when(rem > 0)(cp.wait)


def _moe_kernel(order_ref, info_ref, *refs, d, d_expert, n_chunks, per_step):
    xs_refs = refs[:per_step]
    wg_hbm, wu_hbm, wd_hbm, o_ref, wg_buf, wu_buf, wd_buf, slot_ref, wsem = refs[per_step:]
    chunk = xs_refs[0].shape[0]

    def weight_copies(g, sl):
        lo = g * EXPERTS_PER_GROUP
        return [pltpu.make_async_copy(src.at[pl.ds(lo, EXPERTS_PER_GROUP)], dst.at[sl], wsem.at[0])
                for src, dst in ((wg_hbm, wg_buf), (wu_hbm, wu_buf), (wd_hbm, wd_buf))]

    def first_copies(g, e):
        return [pltpu.make_async_copy(src.at[g * EXPERTS_PER_GROUP + e], dst.at[0, e], wsem.at[1 + e])
                for src, dst in ((wg_hbm, wg_buf), (wu_hbm, wu_buf), (wd_hbm, wd_buf))]

    for ci, xs_ref in enumerate(xs_refs):
        i = pl.program_id(0) * per_step + ci
        rows = slice(ci * chunk, (ci + 1) * chunk)
        c = order_ref[i]
        grp = info_ref[c]
        prev = info_ref[order_ref[jnp.maximum(i - 1, 0)]]

        @pl.when(i == 0)
        def _():
            slot_ref[0] = 0
            for e in range(EXPERTS_PER_GROUP):
                for cp in first_copies(grp, e):
                    cp.start()

        @pl.when((i == 0) | (grp != prev))
        def _():
            @pl.when(i > 0)
            def _():
                slot_ref[0] = 1 - slot_ref[0]
                for cp in weight_copies(grp, slot_ref[0]):
                    cp.wait()

            cur = slot_ref[0]

            def later_group(j, found):
                gj = info_ref[order_ref[j]]
                return jnp.where((found < 0) & (j > i) & (gj != grp), gj, found)

            nxt = lax.fori_loop(0, n_chunks, later_group, jnp.int32(-1))

            @pl.when(nxt >= 0)
            def _():
                for cp in weight_copies(nxt, 1 - cur):
                    cp.start()

        used = c < info_ref[n_chunks]
        half = chunk // 2
        low_only = info_ref[n_chunks + 1 + c] <= half

        def experts(m, xs_ref=xs_ref, grp=grp, await_first=False):
            xb = xs_ref[0:m, 0:d]
            ws = slot_ref[0]
            aux = xs_ref[0:m, d:]
            hid = []
            for e in range(EXPERTS_PER_GROUP):
                if await_first:
                    for cp in first_copies(grp, e):
                        cp.wait()
                hg = _dot(xb, wg_buf[ws, e])
                hu = _dot(xb, wu_buf[ws, e])
                we = aux[:, e:e + 1] + aux[:, EXPERTS_PER_GROUP + e:EXPERTS_PER_GROUP + e + 1]
                hid.append(hg * _sigmoid(hg) * hu * we)
            return _dot(jnp.concatenate(hid, axis=1), wd_buf[ws].reshape(EXPERTS_PER_GROUP * d_expert, o_ref.shape[1]))

        later = used
        if ci == 0:
            later = used & (i > 0)

            @pl.when(i == 0)
            def _():
                o_ref[rows, :] = experts(chunk, await_first=True)

        @pl.when(later & jnp.logical_not(low_only))
        def _():
            o_ref[rows, :] = experts(chunk)

        @pl.when(later & low_only)
        def _():
            o_ref[ci * chunk:ci * chunk + half, :] = experts(half)
            o_ref[ci * chunk + half:(ci + 1) * chunk, :] = jnp.zeros((half, o_ref.shape[1]), _F32)

        @pl.when(jnp.logical_not(used))
        def _():
            o_ref[rows, :] = jnp.zeros((chunk, o_ref.shape[1]), _F32)


def _ple_kernel(meta_ref, rank_ref, x1_ref, p_ref, pos_ref, ys_ref, wpg_ref, bpg_ref, wpp_ref, l2g_ref, l2b_ref,
                o_ref, ybuf, wpg_bf, wpp_bf, sem, *, alpha, tile, chunk, n_sub, tiles_per_step, t_pad, n_steps):
    k = pl.program_id(0)
    slot = k % 2

    def moved(r):
        return rank_ref[r // chunk] * chunk + r % chunk

    def pieces(step, sl):
        out = []
        for ti in range(tiles_per_step):
            kk = step * tiles_per_step + ti
            for g in range(N_GROUPS):
                r1, m1, r2, m2, sg = (meta_ref[kk, g * META_FIELDS + f] for f in range(META_FIELDS))
                out.append((m1, _rows(ys_ref, moved(r1), ybuf.at[sl, ti], sg, m1, sem.at[sl])))
                out.append((m2, _rows(ys_ref, moved(r2), ybuf.at[sl, ti], sg + m1, m2, sem.at[sl])))
        return out

    def fetch(kk, sl):
        for m, cp in pieces(kk, sl):
            pl.when(m > 0)(cp.start)

    @pl.when(k == 0)
    def _():
        ybuf[:, :, tile:, :] = jnp.zeros((2, tiles_per_step, t_pad - tile, ybuf.shape[-1]), _F32)
        fetch(0, 0)
        wpg_bf[...] = wpg_ref[...].astype(_BF16)
        wpp_bf[...] = wpp_ref[...].astype(_BF16)

    @pl.when(k + 1 < n_steps)
    def _():
        fetch(k + 1, 1 - slot)

    for m, cp in pieces(k, slot):
        pl.when(m > 0)(cp.wait)

    sub = tile // n_sub
    for ti in range(tiles_per_step):
        yb = ybuf[slot, ti]
        pos = pos_ref[ti]
        for blk in range(n_sub):
            cols = slice(blk * sub, (blk + 1) * sub)
            rows = slice(ti * tile + blk * sub, ti * tile + (blk + 1) * sub)
            perm = jnp.where(lax.broadcasted_iota(jnp.int32, (t_pad, sub), 0) == pos[:, cols], 1.0, 0.0)
            y = lax.dot_general(perm, yb, _TN, preferred_element_type=_F32)
            r = alpha * x1_ref[rows, :] + y
            gate = _sigmoid(_dot(r, wpg_bf[...]) + bpg_ref[...])
            pp = _dot(p_ref[rows, :], wpp_bf[...])
            o_ref[rows, :] = _layer_norm(r + gate * pp, l2g_ref[...], l2b_ref[...])


def _full(shape):
    return pl.BlockSpec(shape, lambda *_: (0,) * len(shape))


def _resident(shape):
    return pl.BlockSpec(shape, lambda *_: (0,) * len(shape), pipeline_mode=pl.Buffered(1))


def _row(v):
    return v.reshape(1, -1).astype(_F32)


def kernel(x, p, ln_in_g, ln_in_b, w_in, b_in, conf_dw_w, conf_dw_b, conf_ln_g, conf_ln_b, sc_w, sc_b, w_out, b_out, ln1_g, ln1_b, w_rg, b_rg, w_re, b_re, w_gate, w_up, w_down, w_pg, b_pg, w_pp, ln2_g, ln2_b):
    depth = w_in.shape[0]
    assert depth == 1, "single-layer block"
    alpha = (2.0 * depth) ** 0.25
    bsz, seq, d = x.shape
    n = bsz * seq
    conf_k, conf_w = conf_dw_w.shape[1:]
    sc_k, sc_w_ = sc_w.shape[1:]
    d_in = w_in.shape[2]
    n_exp, _, d_expert = w_gate.shape[1:]
    assert n_exp == N_GROUPS * EXPERTS_PER_GROUP
    assert d_in == 2 * conf_w + 3 * sc_w_ and w_out.shape[1] == conf_w + sc_w_
    d_ple = p.shape[-1]

    tile = chunk = TILE
    assert seq % tile == 0 and (bsz * (seq // tile)) % PLE_TILES == 0
    assert bsz * (seq // tile) >= 3 * N_GROUPS, "spare chunks are zero-filled during the first three steps"
    nt = seq // tile
    n_tiles = bsz * nt
    a_halo = -(-(conf_k - 1) // SUBLANES) * SUBLANES
    u_halo = -(-(sc_k - 1) // SUBLANES) * SUBLANES
    t_pad = tile + N_GROUPS * SUBLANES
    n_chunks = -(-(n + (SUBLANES - 1) * N_GROUPS * n_tiles) // chunk) + N_GROUPS
    n_chunks = -(-n_chunks // MOE_CHUNKS) * MOE_CHUNKS
    d_pay = d + LANES

    n_logits = N_GROUPS + n_exp
    wr = jnp.concatenate([w_rg[0], w_re[0]], axis=1).T

    mixer = pl.pallas_call(
        functools.partial(_mixer_kernel, alpha=alpha, tile=tile, t_pad=t_pad, chunk=chunk,
                          n_chunks=n_chunks, n_tiles=n_tiles, conf_w=conf_w, sc_w_=sc_w_,
                          conf_k=conf_k, sc_k=sc_k, a_halo=a_halo, u_halo=u_halo, n_logits=n_logits),
        grid=(bsz, nt),
        in_specs=[
            pl.BlockSpec((1, tile, d), lambda b, j: (b, j, 0)),
            _full((1, d)), _full((1, d)),
            pl.BlockSpec(memory_space=pl.ANY), _full((1, d_in)),
            _full((conf_k, conf_w)), _full((1, conf_w)), _full((1, conf_w)), _full((1, conf_w)),
            _full((sc_k, sc_w_)), _full((1, sc_w_)),
            pl.BlockSpec(memory_space=pl.ANY), _full((1, d)),
            _full((1, d)), _full((1, d)),
            _full((n_logits, d)), _full((1, N_GROUPS)), _full((1, n_exp)),
        ],
        out_specs=[
            pl.BlockSpec((1, tile, d), lambda b, j: (b, j, 0)),
            pl.BlockSpec((1, 1, tile), lambda b, j: (b * nt + j, 0, 0)),
            pl.BlockSpec(memory_space=pl.ANY),
            pl.BlockSpec(memory_space=pltpu.SMEM),
            pl.BlockSpec(memory_space=pltpu.SMEM),
            pl.BlockSpec(memory_space=pltpu.SMEM),
            pl.BlockSpec(memory_space=pltpu.SMEM),
        ],
        out_shape=[
            jax.ShapeDtypeStruct((bsz, seq, d), _F32),
            jax.ShapeDtypeStruct((n_tiles, 1, tile), jnp.int32),
            jax.ShapeDtypeStruct((n_chunks * chunk, d_pay), _F32),
            jax.ShapeDtypeStruct((n_tiles, N_GROUPS * META_FIELDS), jnp.int32),
            jax.ShapeDtypeStruct((2 * n_chunks + 1,), jnp.int32),
            jax.ShapeDtypeStruct((n_chunks,), jnp.int32),
            jax.ShapeDtypeStruct((n_chunks,), jnp.int32),
        ],
        scratch_shapes=[
            pltpu.VMEM((a_halo + tile, conf_w), _F32),
            pltpu.VMEM((SUBLANES - 1, a_halo + tile - SUBLANES, conf_w), _F32),
            pltpu.VMEM((u_halo + tile, sc_w_), _F32),
            pltpu.VMEM((d, d_in), _F32),
            pltpu.VMEM((conf_w + sc_w_, d), _F32),
            pltpu.VMEM((d, d_in), _BF16),
            pltpu.VMEM((conf_w + sc_w_, d), _BF16),
            pltpu.VMEM((ROUTER_ROWS, d), _BF16),
            pltpu.VMEM((ROUTER_ROWS, 1), _F32),
            pltpu.VMEM((tile, tile), _BF16),
            pltpu.VMEM((2, t_pad, d_pay), _F32),
            pltpu.VMEM((chunk, d_pay), _F32),
            pltpu.SMEM((N_GROUPS,), jnp.int32),
            pltpu.SMEM((N_GROUPS,), jnp.int32),
            pltpu.SMEM((1,), jnp.int32),
            pltpu.SMEM((2, 2 * N_GROUPS), jnp.int32),
            pltpu.SemaphoreType.DMA((2,)),
            pltpu.SemaphoreType.DMA(()),
            pltpu.SemaphoreType.DMA((3,)),
        ],
        compiler_params=pltpu.CompilerParams(
            dimension_semantics=("arbitrary", "arbitrary"), vmem_limit_bytes=VMEM_LIMIT_BYTES),
        name="mixer",
    )
    x1, pos, xs, meta, info, order, rank = mixer(
        x, _row(ln_in_g), _row(ln_in_b), w_in[0], _row(b_in[0]),
        conf_dw_w[0], _row(conf_dw_b[0]), _row(conf_ln_g[0]), _row(conf_ln_b[0]),
        sc_w[0], _row(sc_b[0]), w_out[0], _row(b_out[0]),
        _row(ln1_g[0]), _row(ln1_b[0]), wr, _row(b_rg[0]), _row(b_re[0]))

    moe = pl.pallas_call(
        functools.partial(_moe_kernel, d=d, d_expert=d_expert, n_chunks=n_chunks, per_step=MOE_CHUNKS),
        grid_spec=pltpu.PrefetchScalarGridSpec(
            num_scalar_prefetch=2,
            grid=(n_chunks // MOE_CHUNKS,),
            in_specs=[
                pl.BlockSpec((chunk, d_pay), functools.partial(
                    lambda i, order, info, ci: (order[i * MOE_CHUNKS + ci], 0), ci=ci))
                for ci in range(MOE_CHUNKS)
            ] + [
                pl.BlockSpec(memory_space=pl.ANY),
                pl.BlockSpec(memory_space=pl.ANY),
                pl.BlockSpec(memory_space=pl.ANY),
            ],
            out_specs=pl.BlockSpec((MOE_CHUNKS * chunk, d), lambda i, order, info: (i, 0)),
            scratch_shapes=[
                pltpu.VMEM((2, EXPERTS_PER_GROUP, d, d_expert), _F32),
                pltpu.VMEM((2, EXPERTS_PER_GROUP, d, d_expert), _F32),
                pltpu.VMEM((2, EXPERTS_PER_GROUP, d_expert, d), _F32),
                pltpu.SMEM((1,), jnp.int32),
                pltpu.SemaphoreType.DMA((1 + EXPERTS_PER_GROUP,)),
            ],
        ),
        out_shape=jax.ShapeDtypeStruct((n_chunks * chunk, d), _F32),
        compiler_params=pltpu.CompilerParams(
            dimension_semantics=("arbitrary",), vmem_limit_bytes=VMEM_LIMIT_BYTES),
        name="moe",
    )
    ys = moe(order, info, *([xs] * MOE_CHUNKS), w_gate[0], w_up[0], w_down[0])

    ple = pl.pallas_call(
        functools.partial(_ple_kernel, alpha=alpha, tile=tile, chunk=chunk, n_sub=N_SUB, tiles_per_step=PLE_TILES,
                          t_pad=t_pad, n_steps=n_tiles // PLE_TILES),
        grid_spec=pltpu.PrefetchScalarGridSpec(
            num_scalar_prefetch=2,
            grid=(n_tiles // PLE_TILES,),
            in_specs=[
                pl.BlockSpec((PLE_TILES * tile, d), lambda k, meta, rank: (k, 0)),
                pl.BlockSpec((PLE_TILES * tile, d_ple), lambda k, meta, rank: (k, 0)),
                pl.BlockSpec((PLE_TILES, 1, tile), lambda k, meta, rank: (k, 0, 0)),
                pl.BlockSpec(memory_space=pl.ANY),
                _resident((d, d)), _full((1, d)), _resident((d_ple, d)), _full((1, d)), _full((1, d)),
            ],
            out_specs=pl.BlockSpec((PLE_TILES * tile, d), lambda k, meta, rank: (k, 0)),
            scratch_shapes=[
                pltpu.VMEM((2, PLE_TILES, t_pad, d), _F32),
                pltpu.VMEM((d, d), _BF16),
                pltpu.VMEM((d_ple, d), _BF16),
                pltpu.SemaphoreType.DMA((2,)),
            ],
        ),
        out_shape=jax.ShapeDtypeStruct((n, d), _F32),
        compiler_params=pltpu.CompilerParams(
            dimension_semantics=("arbitrary",), vmem_limit_bytes=VMEM_LIMIT_BYTES),
        name="ple",
    )
    out = ple(meta, rank, x1.reshape(n, d), p[0].reshape(n, d_ple), pos, ys,
              w_pg[0], _row(b_pg[0]), w_pp[0], _row(ln2_g[0]), _row(ln2_b[0]))
    return out.reshape(bsz, seq, d)
```
